```python
import jax, jax.numpy as jnp
from jax import lax
import numpy as np

D_MODEL = 1024
BATCH = 2
SEQ = 8192
DEPTH = 1
DEC_BATCH = 16
DEC_SEQ = 16
PAST_LEN = 4096

CHUNK = 64
HEAD_DIM = 64
R_WIDTH = D_MODEL // 2
R_HEADS = R_WIDTH // HEAD_DIM
DECAY_RANK = 64
ICLR_RANK = 64
SHIFT_WIDTH = 3 * R_WIDTH + DECAY_RANK + ICLR_RANK
B_WIDTH = D_MODEL // 4
B_HEADS = B_WIDTH // HEAD_DIM
BAND_CHUNKS = 8
BAND_WINDOW = BAND_CHUNKS * CHUNK
BAND_LEN = (BAND_CHUNKS + 1) * CHUNK
REL_CLIP = 128
M_WIDTH = D_MODEL // 4
M_HEADS = 4
M_HEAD_DIM = M_WIDTH // M_HEADS
N_MEM = 256
MIX_WIDTH = R_WIDTH + B_WIDTH + M_WIDTH
IN_WIDTH = SHIFT_WIDTH + R_WIDTH + 4 * B_WIDTH + 2 * M_WIDTH
LN_EPS = 1e-5
GN_EPS = 64e-5
DEEPNORM_ALPHA = (2 * DEPTH) ** 0.25
DEEPNORM_BETA = (8 * DEPTH) ** -0.25

kernel_name = "hymba_rwkv7_chunkband_mem_stream_step"


def _layer_norm(x, g, b):
    xf = x.astype(jnp.float32)
    m = jnp.mean(xf, -1, keepdims=True)
    var = jnp.mean(jnp.square(xf - m), -1, keepdims=True)
    return ((xf - m) * lax.rsqrt(var + LN_EPS) * g.astype(jnp.float32) + b.astype(jnp.float32)).astype(x.dtype)


def _split_in(z):
    sizes = [SHIFT_WIDTH, R_WIDTH, B_WIDTH, B_WIDTH, B_WIDTH, B_WIDTH, M_WIDTH]
    offs = list(np.cumsum(sizes))
    return jnp.split(z, offs, axis=-1)


def _wkv_scan(r, decay, k, v, aa, bb, s0):
    def step(S, inp):
        r_t, w_t, k_t, v_t, a_t, b_t = inp
        sa = jnp.einsum('bhij,bhj->bhi', S, a_t)
        S = S * w_t[:, :, None, :] + sa[..., None] * b_t[:, :, None, :] + v_t[..., None] * k_t[:, :, None, :]
        y = jnp.einsum('bhij,bhj->bhi', S, r_t)
        return S, y
    xs = tuple(jnp.moveaxis(t, 1, 0) for t in (r, decay, k, v, aa, bb))
    S, ys = lax.scan(step, s0, xs)
    return jnp.moveaxis(ys, 0, 1), S


def _rwkv7(zs, prev_row, s0, mu, w0, w2, a0, a2, k_k, k_a, r_k, gn_g, gn_b):
    B, T, _ = zs.shape
    f32 = jnp.float32
    zprev = jnp.concatenate([prev_row[:, None, :].astype(zs.dtype), zs[:, :-1]], axis=1)
    xs = zs + (zprev - zs) * mu
    r, k, v, wd, ad = jnp.split(xs, [R_WIDTH, 2 * R_WIDTH, 3 * R_WIDTH, 3 * R_WIDTH + DECAY_RANK], axis=-1)
    logw = -jax.nn.softplus(-(w0 + jnp.tanh(wd) @ w2).astype(f32)) - 0.5
    decay = jnp.exp(-jnp.exp(logw))
    a = jax.nn.sigmoid((a0 + ad @ a2).astype(f32))
    hd = lambda t: t.reshape(B, T, R_HEADS, HEAD_DIM).astype(f32)
    r, k, v, a, decay = hd(r), hd(k), hd(v), hd(a), hd(decay)
    kk = k * k_k.reshape(R_HEADS, HEAD_DIM).astype(f32)
    kk = kk / jnp.maximum(jnp.sqrt(jnp.sum(kk * kk, -1, keepdims=True)), 1e-12)
    k = k * (1.0 + (a - 1.0) * k_a.reshape(R_HEADS, HEAD_DIM).astype(f32))
    y, S = _wkv_scan(r, decay, k, v, -kk, kk * a, s0.astype(f32))
    m = jnp.mean(y, -1, keepdims=True)
    var = jnp.mean(jnp.square(y - m), -1, keepdims=True)
    yn = (y - m) * lax.rsqrt(var + GN_EPS)
    yn = yn * gn_g.reshape(R_HEADS, HEAD_DIM).astype(f32) + gn_b.reshape(R_HEADS, HEAD_DIM).astype(f32)
    bonus = jnp.sum(r * k * r_k.astype(f32), -1, keepdims=True) * v
    out = (yn + bonus).reshape(B, T, R_WIDTH).astype(zs.dtype)
    return out, zs[:, -1], S


def _rel_bias(table, rel):
    return table[:, jnp.clip(rel, -REL_CLIP, REL_CLIP) + REL_CLIP]


def _band_prompt(q, k, v, table):
    B, T, H, D = q.shape
    nc = T // CHUNK
    qc = q.reshape(B, nc, CHUNK, H, D)
    pad = ((0, 0), (BAND_WINDOW, 0), (0, 0), (0, 0))
    kp = jnp.pad(k, pad).reshape(B, nc + BAND_CHUNKS, CHUNK, H, D)
    vp = jnp.pad(v, pad).reshape(B, nc + BAND_CHUNKS, CHUNK, H, D)
    idx = jnp.arange(nc)[:, None] + jnp.arange(BAND_CHUNKS + 1)[None, :]
    kb = kp[:, idx].reshape(B, nc, BAND_LEN, H, D)
    vb = vp[:, idx].reshape(B, nc, BAND_LEN, H, D)
    s = jnp.einsum('bcqhd,bckhd->bchqk', qc, kb, preferred_element_type=jnp.float32) * (D ** -0.5)
    rel = jnp.arange(CHUNK)[:, None] + BAND_WINDOW - jnp.arange(BAND_LEN)[None, :]
    s = s + _rel_bias(table.astype(jnp.float32), rel)[None, None]
    key_chunk = jnp.arange(nc)[:, None] - BAND_CHUNKS + jnp.arange(BAND_LEN)[None, :] // CHUNK
    s = jnp.where((key_chunk >= 0)[None, :, None, None, :], s, -jnp.inf)
    p = jax.nn.softmax(s, axis=-1).astype(v.dtype)
    o = jnp.einsum('bchqk,bckhd->bcqhd', p, vb)
    return o.reshape(B, T, H * D)


def _band_sample(q, k_new, v_new, k_cache, v_cache, table):
    B, S, H, D = q.shape
    R = k_cache.shape[1]
    kall = jnp.concatenate([k_cache.astype(k_new.dtype), k_new], axis=1)
    vall = jnp.concatenate([v_cache.astype(v_new.dtype), v_new], axis=1)
    s = jnp.einsum('bqhd,bkhd->bhqk', q, kall, preferred_element_type=jnp.float32) * (D ** -0.5)
    rel = (R + jnp.arange(S))[:, None] - jnp.arange(R + S)[None, :]
    s = s + _rel_bias(table.astype(jnp.float32), rel)[None]
    p = jax.nn.softmax(s, axis=-1).astype(vall.dtype)
    o = jnp.einsum('bhqk,bkhd->bqhd', p, vall)
    return o.reshape(B, S, H * D)


def _mem_attn(q, mk, mv):
    B, T = q.shape[:2]
    qh = q.reshape(B, T, M_HEADS, M_HEAD_DIM)
    s = jnp.einsum('bqhd,bmhd->bhqm', qh, mk.astype(q.dtype), preferred_element_type=jnp.float32) * (M_HEAD_DIM ** -0.5)
    p = jax.nn.softmax(s, axis=-1).astype(q.dtype)
    return jnp.einsum('bhqm,bmhd->bqhd', p, mv.astype(q.dtype)).reshape(B, T, M_WIDTH)


def _finish(x, o_r, o_b, o_m, g_r, g_b, g_m, w_out, ln_g, ln_b):
    mix = jnp.concatenate([o_r * jax.nn.silu(g_r), o_b * jax.nn.silu(g_b), o_m * jax.nn.silu(g_m)], axis=-1)
    return _layer_norm(DEEPNORM_ALPHA * x + mix @ w_out, ln_g, ln_b)


def setup_inputs(seed: int = 0) -> dict:
    key = jax.random.key(seed)
    ks = jax.random.split(key, 26)
    nrm = lambda k, shape, s: jax.random.normal(k, shape, jnp.float32) * s
    band_rows = min(BAND_WINDOW, PAST_LEN)
    ramp = jnp.linspace(0.0, 1.0, R_WIDTH, dtype=jnp.float32)
    return {
        "x_prompt": nrm(ks[0], (BATCH, SEQ, D_MODEL), 1.0),
        "x_sample": nrm(ks[1], (DEC_BATCH, DEC_SEQ, D_MODEL), 1.0),
        "mem_prompt": nrm(ks[2], (BATCH, N_MEM, D_MODEL), 1.0),
        "state_shift": nrm(ks[3], (DEPTH, DEC_BATCH, SHIFT_WIDTH), 1.0),
        "state_wkv": nrm(ks[4], (DEPTH, DEC_BATCH, R_HEADS, HEAD_DIM, HEAD_DIM), 0.1),
        "cache_band_k": nrm(ks[5], (DEPTH, DEC_BATCH, band_rows, B_HEADS, HEAD_DIM), 1.0),
        "cache_band_v": nrm(ks[6], (DEPTH, DEC_BATCH, band_rows, B_HEADS, HEAD_DIM), 1.0),
        "cache_mem_k": nrm(ks[7], (DEPTH, DEC_BATCH, N_MEM, M_HEADS, M_HEAD_DIM), 1.0),
        "cache_mem_v": nrm(ks[8], (DEPTH, DEC_BATCH, N_MEM, M_HEADS, M_HEAD_DIM), 1.0),
        "w_in": nrm(ks[9], (DEPTH, D_MODEL, IN_WIDTH), D_MODEL ** -0.5),
        "mu_shift": jax.random.uniform(ks[10], (DEPTH, SHIFT_WIDTH), jnp.float32),
        "w0": (-6.0 + 5.0 * ramp ** 1.5)[None] + nrm(ks[11], (DEPTH, R_WIDTH), 0.1),
        "w2": nrm(ks[12], (DEPTH, DECAY_RANK, R_WIDTH), 0.1 * DECAY_RANK ** -0.5),
        "a0": nrm(ks[13], (DEPTH, R_WIDTH), 0.1),
        "a2": nrm(ks[14], (DEPTH, ICLR_RANK, R_WIDTH), 0.5 * ICLR_RANK ** -0.5),
        "k_k": 0.85 + nrm(ks[15], (DEPTH, R_WIDTH), 0.02),
        "k_a": 1.0 + nrm(ks[16], (DEPTH, R_WIDTH), 0.02),
        "r_k": nrm(ks[17], (DEPTH, R_HEADS, HEAD_DIM), 0.1),
        "gn_g": 1.0 + nrm(ks[18], (DEPTH, R_WIDTH), 0.02),
        "gn_b": nrm(ks[19], (DEPTH, R_WIDTH), 0.02),
        "rel_bias": nrm(ks[20], (DEPTH, B_HEADS, 2 * REL_CLIP + 1), 0.1),
        "w_mem_kv": nrm(ks[21], (DEPTH, D_MODEL, 2 * M_WIDTH), D_MODEL ** -0.5),
        "w_out": nrm(ks[22], (DEPTH, MIX_WIDTH, D_MODEL), MIX_WIDTH ** -0.5 * DEEPNORM_BETA),
        "ln_g": 1.0 + nrm(ks[23], (DEPTH, D_MODEL), 0.02),
        "ln_b": nrm(ks[24], (DEPTH, D_MODEL), 0.02),
    }


def reference(x_prompt, x_sample, mem_prompt, state_shift, state_wkv, cache_band_k, cache_band_v,
              cache_mem_k, cache_mem_v, w_in, mu_shift, w0, w2, a0, a2, k_k, k_a, r_k, gn_g, gn_b,
              rel_bias, w_mem_kv, w_out, ln_g, ln_b):
    hp, hs = x_prompt, x_sample
    Bp, T = hp.shape[:2]
    Bs, S = hs.shape[:2]
    keep = min(BAND_WINDOW, T)
    p_shift, p_wkv, p_bk, p_bv, p_mk, p_mv = [], [], [], [], [], []
    s_shift, s_wkv, s_bk, s_bv = [], [], [], []
    for l in range(DEPTH):
        rw = (mu_shift[l], w0[l], w2[l], a0[l], a2[l], k_k[l], k_a[l], r_k[l], gn_g[l], gn_b[l])
        zs, g_r, bq, bk, bv, g_b, mq, g_m = _split_in(hp @ w_in[l])
        o_r, last_p, S_p = _rwkv7(zs, jnp.zeros((Bp, SHIFT_WIDTH), hp.dtype),
                                  jnp.zeros((Bp, R_HEADS, HEAD_DIM, HEAD_DIM), jnp.float32), *rw)
        bq4, bk4, bv4 = (t.reshape(Bp, T, B_HEADS, HEAD_DIM) for t in (bq, bk, bv))
        o_b = _band_prompt(bq4, bk4, bv4, rel_bias[l])
        mk, mv = jnp.split(mem_prompt @ w_mem_kv[l], 2, axis=-1)
        mk = mk.reshape(Bp, N_MEM, M_HEADS, M_HEAD_DIM)
        mv = mv.reshape(Bp, N_MEM, M_HEADS, M_HEAD_DIM)
        o_m = _mem_attn(mq, mk, mv)
        hp = _finish(hp, o_r, o_b, o_m, g_r, g_b, g_m, w_out[l], ln_g[l], ln_b[l])
        p_shift.append(last_p); p_wkv.append(S_p)
        p_bk.append(bk4[:, T - keep:]); p_bv.append(bv4[:, T - keep:])
        p_mk.append(mk); p_mv.append(mv)
        zs, g_r, bq, bk, bv, g_b, mq, g_m = _split_in(hs @ w_in[l])
        o_r, last_s, S_s = _rwkv7(zs, state_shift[l], state_wkv[l], *rw)
        bq4, bk4, bv4 = (t.reshape(Bs, S, B_HEADS, HEAD_DIM) for t in (bq, bk, bv))
        o_b = _band_sample(bq4, bk4, bv4, cache_band_k[l], cache_band_v[l], rel_bias[l])
        o_m = _mem_attn(mq, cache_mem_k[l], cache_mem_v[l])
        hs = _finish(hs, o_r, o_b, o_m, g_r, g_b, g_m, w_out[l], ln_g[l], ln_b[l])
        s_shift.append(last_s); s_wkv.append(S_s); s_bk.append(bk4); s_bv.append(bv4)
    st = lambda xs: jnp.stack(xs, axis=0)
    return (hp, hs, st(p_shift), st(p_wkv), st(p_bk), st(p_bv), st(p_mk), st(p_mv),
            st(s_shift), st(s_wkv), st(s_bk), st(s_bv))
```

```python
import functools

import numpy as np
import jax
import jax.numpy as jnp
from jax import lax
from jax.experimental import pallas as pl
from jax.experimental.pallas import tpu as pltpu

F32 = jnp.float32
BF16 = jnp.bfloat16

D_MODEL = 1024
HEAD_DIM = 64
R_WIDTH = 512
R_HEADS = 8
LOW_RANK = 64
SHIFT_WIDTH = 3 * R_WIDTH + 2 * LOW_RANK
B_WIDTH = 256
B_HEADS = 4
M_HEADS = 4
N_MEM = 256
CHUNK = 64
BAND_CHUNKS = 8
BAND_WINDOW = BAND_CHUNKS * CHUNK
BAND_LEN = BAND_WINDOW + CHUNK
REL_CLIP = 128
ATT_WIDTH = 6 * B_WIDTH
IN_WIDTH = SHIFT_WIDTH + R_WIDTH + ATT_WIDTH
LN_EPS = 1e-5
GN_EPS = 64e-5
ALPHA = 2.0 ** 0.25
ATT_SCALE = HEAD_DIM ** -0.5

VMEM_LIMIT = 48 * 1024 * 1024

NN = ((1,), (0,))
NT = ((1,), (1,))
TN = ((0,), (0,))


def _dot(a, b, dims=NN):
    return lax.dot_general(a, b, (dims, ((), ())), preferred_element_type=F32)


def _split2(x):
    hi = x.astype(BF16)
    lo = (x - hi.astype(F32)).astype(BF16)
    return hi, lo


def _mm3(a, b, dims=NN):
    ah, al = _split2(a)
    bh, bl = _split2(b)
    return _dot(ah, bh, dims) + _dot(ah, bl, dims) + _dot(al, bh, dims)


def _mm_exact_lhs(lhs_bf16, x):
    x1 = x.astype(BF16)
    r1 = x - x1.astype(F32)
    x2 = r1.astype(BF16)
    x3 = (r1 - x2.astype(F32)).astype(BF16)
    return _dot(lhs_bf16, x1) + _dot(lhs_bf16, x2) + _dot(lhs_bf16, x3)


def _sigmoid(x):
    return 1.0 / (1.0 + jnp.exp(-x))


def _silu(x):
    return x * _sigmoid(x)


def _proj_kernel(x_ref, w_ref, *out_refs, splits):
    x = x_ref[...].astype(BF16)
    for o_ref, (lo, hi) in zip(out_refs, splits):
        o_ref[...] = _dot(x, w_ref[:, lo:hi])


def _proj(x, w_bf16, splits, tm):
    m, k = x.shape
    n = w_bf16.shape[1]
    return pl.pallas_call(
        functools.partial(_proj_kernel, splits=splits),
        name=f"proj_{m}x{n}",
        grid=(m // tm,),
        in_specs=[pl.BlockSpec((tm, k), lambda i: (i, 0)),
                  pl.BlockSpec((k, n), lambda i: (0, 0))],
        out_specs=[pl.BlockSpec((tm, hi - lo), lambda i: (i, 0)) for lo, hi in splits],
        out_shape=[jax.ShapeDtypeStruct((m, hi - lo), F32) for lo, hi in splits],
        compiler_params=pltpu.CompilerParams(
            dimension_semantics=("arbitrary",), vmem_limit_bytes=VMEM_LIMIT),
    )(x, w_bf16)


def _wkv_kernel(zs_ref, gr_ref, prev_ref, s0_ref, mu_ref, w0_ref, w2_ref, a0_ref, a2_ref,
                kk_ref, ka_ref, rk_ref, gng_ref, gnb_ref, out_ref, state_ref, carry_ref, *, bb_n, c):
    t = pl.program_id(1)

    @pl.when(t == 0)
    def _():
        carry_ref[...] = prev_ref[...]
        state_ref[...] = s0_ref[...]

    row = lax.broadcasted_iota(jnp.int32, (c, c), 0)
    col = lax.broadcasted_iota(jnp.int32, (c, c), 1)
    eye = (row == col).astype(F32)
    ltri = (col <= row).astype(BF16)
    level_masks = []
    shift = 0
    while (1 << shift) < c:
        rb = row >> shift
        level_masks.append(((rb & 1) == 1) & ((col >> shift) == rb - 1))
        shift += 1
    row2 = lax.broadcasted_iota(jnp.int32, (c, 2 * c), 0)
    col2 = lax.broadcasted_iota(jnp.int32, (c, 2 * c), 1)
    colm = col2 & (c - 1)
    mask_ak = (col2 >= c) & (colm < row2)
    mask_y = colm <= row2
    first_row = lax.broadcasted_iota(jnp.int32, (c, 1), 0) == 0

    mu = mu_ref[...]
    for bb in range(bb_n):
        zs = zs_ref[bb]
        zprev = jnp.where(first_row, carry_ref[bb], pltpu.roll(zs, 1, 0))
        carry_ref[bb] = zs[c - 1:c, :]
        xs = zs + (zprev - zs) * mu
        r = xs[:, 0:R_WIDTH]
        k = xs[:, R_WIDTH:2 * R_WIDTH]
        v = xs[:, 2 * R_WIDTH:3 * R_WIDTH]
        wd = xs[:, 3 * R_WIDTH:3 * R_WIDTH + LOW_RANK]
        ad = xs[:, 3 * R_WIDTH + LOW_RANK:]

        u = -(w0_ref[...] + _mm3(jnp.tanh(wd), w2_ref[...]))
        softplus = jnp.maximum(u, 0.0) + jnp.log(1.0 + jnp.exp(-jnp.abs(u)))
        ld = -jnp.exp(-softplus - 0.5)
        cum = _mm_exact_lhs(ltri, ld)
        a = _sigmoid(a0_ref[...] + _mm3(ad, a2_ref[...]))

        ecum = jnp.exp(cum)
        einv = jnp.exp(-cum)
        eprev = jnp.exp(cum - ld)
        pc = ecum[c - 1:c, :]
        kkr = k * kk_ref[...]
        k2 = k * (1.0 + (a - 1.0) * ka_ref[...])
        rt = r * ecum
        kh = k2 * einv
        kp = kh * pc
        rk2 = r * k2 * rk_ref[...]
        gate = _silu(gr_ref[bb])

        for h in range(R_HEADS):
            sl = slice(h * HEAD_DIM, (h + 1) * HEAD_DIM)
            kkr_h = kkr[:, sl]
            nrm = jnp.sqrt(jnp.sum(kkr_h * kkr_h, axis=-1, keepdims=True))
            kkn = kkr_h / jnp.maximum(nrm, 1e-12)
            at = -kkn * eprev[:, sl]
            bh = kkn * a[:, sl] * einv[:, sl]
            bp = bh * pc[:, sl]
            v_h = v[:, sl]
            s0 = state_ref[bb, h]

            ar = jnp.concatenate([at, rt[:, sl]], axis=0)
            bk = jnp.concatenate([bh, kh[:, sl]], axis=0)
            a4 = _mm3(ar, bk, NT)
            ars = _mm3(ar, s0, NT)
            a_top = a4[:c]
            a_ab = a_top[:, :c]

            tinv = eye + jnp.where(level_masks[0], a_ab, 0.0)
            for mask in level_masks[1:]:
                tinv = tinv + _mm3(tinv, _mm3(jnp.where(mask, a_ab, 0.0), tinv))

            vv = jnp.concatenate([v_h, v_h], axis=0)
            rhs = ars[:c] + _mm3(jnp.where(mask_ak, a_top, 0.0), vv)
            pm = _mm3(tinv, rhs)
            pv = jnp.concatenate([pm, v_h], axis=0)
            y = ars[c:] + _mm3(jnp.where(mask_y, a4[c:], 0.0), pv)
            bkp = jnp.concatenate([bp, kp[:, sl]], axis=0)
            state_ref[bb, h] = s0 * pc[:, sl] + _mm3(pv, bkp, TN)

            mean = jnp.mean(y, axis=-1, keepdims=True)
            yc = y - mean
            var = jnp.mean(yc * yc, axis=-1, keepdims=True)
            yn = yc * lax.rsqrt(var + GN_EPS) * gng_ref[:, sl] + gnb_ref[:, sl]
            bonus = jnp.sum(rk2[:, sl], axis=-1, keepdims=True) * v_h
            out_ref[bb, :, sl] = (yn + bonus) * gate[:, sl]


def _wkv(zs, gr, prev, s0, params, *, bb_n, c):
    b, t, _ = zs.shape
    full = lambda arr: pl.BlockSpec(arr.shape, lambda i, j: (0,) * arr.ndim)
    return pl.pallas_call(
        functools.partial(_wkv_kernel, bb_n=bb_n, c=c),
        name=f"wkv_c{c}",
        grid=(b // bb_n, t // c),
        in_specs=[pl.BlockSpec((bb_n, c, SHIFT_WIDTH), lambda i, j: (i, j, 0)),
                  pl.BlockSpec((bb_n, c, R_WIDTH), lambda i, j: (i, j, 0)),
                  pl.BlockSpec((bb_n, 1, SHIFT_WIDTH), lambda i, j: (i, 0, 0)),
                  pl.BlockSpec((bb_n, R_HEADS, HEAD_DIM, HEAD_DIM), lambda i, j: (i, 0, 0, 0))]
                 + [full(p) for p in params],
        out_specs=[pl.BlockSpec((bb_n, c, R_WIDTH), lambda i, j: (i, j, 0)),
                   pl.BlockSpec((bb_n, R_HEADS, HEAD_DIM, HEAD_DIM), lambda i, j: (i, 0, 0, 0))],
        out_shape=[jax.ShapeDtypeStruct((b, t, R_WIDTH), F32),
                   jax.ShapeDtypeStruct((b, R_HEADS, HEAD_DIM, HEAD_DIM), F32)],
        scratch_shapes=[pltpu.VMEM((bb_n, 1, SHIFT_WIDTH), F32)],
        compiler_params=pltpu.CompilerParams(
            dimension_semantics=("arbitrary", "arbitrary"), vmem_limit_bytes=VMEM_LIMIT),
    )(zs, gr, prev, s0, *params)


def _softmax_pv(s, v_bf16):
    m = jnp.max(s, axis=-1, keepdims=True)
    p = jnp.exp(s - m)
    l = jnp.sum(p, axis=-1, keepdims=True)
    return _dot(p.astype(BF16), v_bf16) / l


def _mem_heads(q_bf16, mk_bf16, mv_bf16):
    outs = []
    for h in range(M_HEADS):
        sl = slice(h * HEAD_DIM, (h + 1) * HEAD_DIM)
        s = _dot(q_bf16[:, sl], mk_bf16[:, sl], NT)
        outs.append(_softmax_pv(s, mv_bf16[:, sl]))
    return outs


def _attn_prompt_kernel(q_ref, kp_ref, kc_ref, vp_ref, vc_ref, gb_ref, mq_ref, gm_ref,
                        mk_ref, mv_ref, bias_ref, out_ref, kcat_ref, vcat_ref, *, tq):
    j = pl.program_id(1)
    kcat_ref[0:BAND_WINDOW] = kp_ref[0].astype(BF16)
    kcat_ref[BAND_WINDOW:] = kc_ref[0].astype(BF16)
    vcat_ref[0:BAND_WINDOW] = vp_ref[0].astype(BF16)
    vcat_ref[BAND_WINDOW:] = vc_ref[0].astype(BF16)
    q = (q_ref[0] * ATT_SCALE).astype(BF16)
    gate_b = _silu(gb_ref[0])
    kcol = lax.broadcasted_iota(jnp.int32, (CHUNK, BAND_LEN), 1)
    for i in range(tq // CHUNK):
        rows = slice(i * CHUNK, (i + 1) * CHUNK)
        first_valid = jnp.where(j == 0, BAND_WINDOW - i * CHUNK, 0)
        visible = kcol >= first_valid
        for h in range(B_HEADS):
            sl = slice(h * HEAD_DIM, (h + 1) * HEAD_DIM)
            kh = kcat_ref[i * CHUNK:i * CHUNK + BAND_LEN, sl]
            vh = vcat_ref[i * CHUNK:i * CHUNK + BAND_LEN, sl]
            s = _dot(q[rows, sl], kh, NT) + bias_ref[h]
            s = jnp.where(visible, s, -jnp.inf)
            out_ref[0, rows, sl] = _softmax_pv(s, vh) * gate_b[rows, sl]

    mq = (mq_ref[0] * ATT_SCALE).astype(BF16)
    gate_m = _silu(gm_ref[0])
    mem = _mem_heads(mq, mk_ref[0].astype(BF16), mv_ref[0].astype(BF16))
    for h in range(M_HEADS):
        sl = slice(h * HEAD_DIM, (h + 1) * HEAD_DIM)
        out_ref[0, :, B_WIDTH + h * HEAD_DIM:B_WIDTH + (h + 1) * HEAD_DIM] = mem[h] * gate_m[:, sl]


def _attn_prompt(att, memkv, bias, *, tq):
    b, t, _ = att.shape
    col = lambda cidx: pl.BlockSpec((1, tq, B_WIDTH), lambda i, j: (i, j, cidx))
    prev = lambda cidx: pl.BlockSpec((1, tq, B_WIDTH), lambda i, j: (i, jnp.maximum(j - 1, 0), cidx))
    return pl.pallas_call(
        functools.partial(_attn_prompt_kernel, tq=tq),
        name="attn_prompt",
        grid=(b, t // tq),
        in_specs=[col(0), prev(1), col(1), prev(2), col(2), col(3), col(4), col(5),
                  pl.BlockSpec((1, N_MEM, B_WIDTH), lambda i, j: (i, 0, 0)),
                  pl.BlockSpec((1, N_MEM, B_WIDTH), lambda i, j: (i, 0, 1)),
                  pl.BlockSpec(bias.shape, lambda i, j: (0, 0, 0))],
        out_specs=pl.BlockSpec((1, tq, 2 * B_WIDTH), lambda i, j: (i, j, 0)),
        out_shape=jax.ShapeDtypeStruct((b, t, 2 * B_WIDTH), F32),
        scratch_shapes=[pltpu.VMEM((2 * BAND_WINDOW, B_WIDTH), BF16),
                        pltpu.VMEM((2 * BAND_WINDOW, B_WIDTH), BF16)],
        compiler_params=pltpu.CompilerParams(
            dimension_semantics=("arbitrary", "arbitrary"), vmem_limit_bytes=VMEM_LIMIT),
    )(att, att, att, att, att, att, att, att, memkv, memkv, bias)


def _attn_sample_kernel(att_ref, ck_ref, cv_ref, mk_ref, mv_ref, bias_c_ref, bias_n_ref, out_ref):
    att = att_ref[0]
    q = (att[:, 0:B_WIDTH] * ATT_SCALE).astype(BF16)
    kn = att[:, B_WIDTH:2 * B_WIDTH].astype(BF16)
    vn = att[:, 2 * B_WIDTH:3 * B_WIDTH].astype(BF16)
    gate_b = _silu(att[:, 3 * B_WIDTH:4 * B_WIDTH])
    mq = (att[:, 4 * B_WIDTH:5 * B_WIDTH] * ATT_SCALE).astype(BF16)
    gate_m = _silu(att[:, 5 * B_WIDTH:6 * B_WIDTH])
    kc = ck_ref[0].astype(BF16)
    vc = cv_ref[0].astype(BF16)
    for h in range(B_HEADS):
        sl = slice(h * HEAD_DIM, (h + 1) * HEAD_DIM)
        s_c = _dot(q[:, sl], kc[:, sl], NT) + bias_c_ref[h]
        s_n = _dot(q[:, sl], kn[:, sl], NT) + bias_n_ref[h]
        m = jnp.maximum(jnp.max(s_c, axis=-1, keepdims=True), jnp.max(s_n, axis=-1, keepdims=True))
        p_c = jnp.exp(s_c - m)
        p_n = jnp.exp(s_n - m)
        l = jnp.sum(p_c, axis=-1, keepdims=True) + jnp.sum(p_n, axis=-1, keepdims=True)
        o = (_dot(p_c.astype(BF16), vc[:, sl]) + _dot(p_n.astype(BF16), vn[:, sl])) / l
        out_ref[0, :, sl] = o * gate_b[:, sl]
    mem = _mem_heads(mq, mk_ref[0].astype(BF16), mv_ref[0].astype(BF16))
    for h in range(M_HEADS):
        sl = slice(h * HEAD_DIM, (h + 1) * HEAD_DIM)
        out_ref[0, :, B_WIDTH + h * HEAD_DIM:B_WIDTH + (h + 1) * HEAD_DIM] = mem[h] * gate_m[:, sl]


def _attn_sample(att, cache_k, cache_v, mem_k, mem_v, bias_c, bias_n):
    b, s, _ = att.shape
    per_b = lambda arr: pl.BlockSpec((1,) + arr.shape[1:], lambda i: (i, 0, 0))
    full = lambda arr: pl.BlockSpec(arr.shape, lambda i: (0, 0, 0))
    return pl.pallas_call(
        _attn_sample_kernel,
        name="attn_sample",
        grid=(b,),
        in_specs=[per_b(att), per_b(cache_k), per_b(cache_v), per_b(mem_k), per_b(mem_v),
                  full(bias_c), full(bias_n)],
        out_specs=pl.BlockSpec((1, s, 2 * B_WIDTH), lambda i: (i, 0, 0)),
        out_shape=jax.ShapeDtypeStruct((b, s, 2 * B_WIDTH), F32),
        compiler_params=pltpu.CompilerParams(dimension_semantics=("arbitrary",)),
    )(att, cache_k, cache_v, mem_k, mem_v, bias_c, bias_n)


def _finish_kernel(x_ref, mr_ref, mbm_ref, w_ref, g_ref, b_ref, y_ref):
    o = _dot(mr_ref[...].astype(BF16), w_ref[0:R_WIDTH, :])
    o = o + _dot(mbm_ref[...].astype(BF16), w_ref[R_WIDTH:, :])
    hres = ALPHA * x_ref[...] + o
    mean = jnp.mean(hres, axis=-1, keepdims=True)
    hc = hres - mean
    var = jnp.mean(hc * hc, axis=-1, keepdims=True)
    y_ref[...] = hc * lax.rsqrt(var + LN_EPS) * g_ref[...] + b_ref[...]


def _finish(x, mix_r, mix_bm, w_out_bf16, ln_g, ln_b, tm):
    m = x.shape[0]
    rows = lambda width: pl.BlockSpec((tm, width), lambda i: (i, 0))
    full = lambda arr: pl.BlockSpec(arr.shape, lambda i: (0, 0))
    return pl.pallas_call(
        _finish_kernel,
        name=f"finish_{m}",
        grid=(m // tm,),
        in_specs=[rows(D_MODEL), rows(R_WIDTH), rows(2 * B_WIDTH), full(w_out_bf16), full(ln_g), full(ln_b)],
        out_specs=rows(D_MODEL),
        out_shape=jax.ShapeDtypeStruct((m, D_MODEL), F32),
        compiler_params=pltpu.CompilerParams(
            dimension_semantics=("arbitrary",), vmem_limit_bytes=VMEM_LIMIT),
    )(x, mix_r, mix_bm, w_out_bf16, ln_g, ln_b)


IN_SPLITS = ((0, SHIFT_WIDTH), (SHIFT_WIDTH, SHIFT_WIDTH + R_WIDTH), (SHIFT_WIDTH + R_WIDTH, IN_WIDTH))


def _rel_bias_matrix(table, q_pos, k_pos):
    rel = q_pos[:, None] - k_pos[None, :]
    return table[:, np.clip(rel, -REL_CLIP, REL_CLIP) + REL_CLIP]


def kernel(x_prompt, x_sample, mem_prompt, state_shift, state_wkv, cache_band_k, cache_band_v,
           cache_mem_k, cache_mem_v, w_in, mu_shift, w0, w2, a0, a2, k_k, k_a, r_k, gn_g, gn_b,
           rel_bias, w_mem_kv, w_out, ln_g, ln_b):
    bp, t, _ = x_prompt.shape
    bs, s, _ = x_sample.shape
    depth = w_in.shape[0]
    assert depth == 1 and t % BAND_WINDOW == 0 and s <= CHUNK
    keep = min(BAND_WINDOW, t)
    l = 0

    w_in_b = w_in[l].astype(BF16)
    w_out_b = w_out[l].astype(BF16)
    w_mem_b = w_mem_kv[l].astype(BF16)
    row = lambda p: p.reshape(1, -1)
    wkv_params = (row(mu_shift[l]), row(w0[l]), w2[l], row(a0[l]), a2[l], row(k_k[l]), row(k_a[l]),
                  row(r_k[l]), row(gn_g[l]), row(gn_b[l]))
    table = rel_bias[l]
    bias_p = _rel_bias_matrix(table, np.arange(CHUNK) + BAND_WINDOW, np.arange(BAND_LEN))
    r_rows = cache_band_k.shape[2]
    bias_s = _rel_bias_matrix(table, r_rows + np.arange(s), np.arange(r_rows + s))
    bias_sc, bias_sn = bias_s[:, :, :r_rows], bias_s[:, :, r_rows:]

    xp = x_prompt.reshape(bp * t, D_MODEL)
    zs, gr, att = _proj(xp, w_in_b, IN_SPLITS, 256)
    zs = zs.reshape(bp, t, SHIFT_WIDTH)
    att = att.reshape(bp, t, ATT_WIDTH)
    memkv, = _proj(mem_prompt.reshape(bp * N_MEM, D_MODEL), w_mem_b, ((0, 2 * B_WIDTH),), 256)
    memkv = memkv.reshape(bp, N_MEM, 2 * B_WIDTH)
    mix_r, p_wkv = _wkv(zs, gr.reshape(bp, t, R_WIDTH), jnp.zeros((bp, 1, SHIFT_WIDTH), F32),
                        jnp.zeros((bp, R_HEADS, HEAD_DIM, HEAD_DIM), F32), wkv_params, bb_n=2, c=CHUNK)
    mix_bm = _attn_prompt(att, memkv, bias_p, tq=BAND_WINDOW)
    y_prompt = _finish(xp, mix_r.reshape(bp * t, R_WIDTH), mix_bm.reshape(bp * t, 2 * B_WIDTH),
                       w_out_b, row(ln_g[l]), row(ln_b[l]), 512).reshape(bp, t, D_MODEL)
    p_shift = zs[:, -1]
    p_bk = att[:, t - keep:, B_WIDTH:2 * B_WIDTH].reshape(bp, keep, B_HEADS, HEAD_DIM)
    p_bv = att[:, t - keep:, 2 * B_WIDTH:3 * B_WIDTH].reshape(bp, keep, B_HEADS, HEAD_DIM)
    p_mk = memkv[:, :, :B_WIDTH].reshape(bp, N_MEM, M_HEADS, HEAD_DIM)
    p_mv = memkv[:, :, B_WIDTH:].reshape(bp, N_MEM, M_HEADS, HEAD_DIM)

    xs = x_sample.reshape(bs * s, D_MODEL)
    zs_s, gr_s, att_s = _proj(xs, w_in_b, IN_SPLITS, bs * s)
    zs_s = zs_s.reshape(bs, s, SHIFT_WIDTH)
    att_s = att_s.reshape(bs, s, ATT_WIDTH)
    mix_r_s, s_wkv = _wkv(zs_s, gr_s.reshape(bs, s, R_WIDTH), state_shift[l][:, None, :], state_wkv[l],
                          wkv_params, bb_n=2, c=s)
    mix_bm_s = _attn_sample(att_s,
                            cache_band_k[l].reshape(bs, r_rows, B_WIDTH),
                            cache_band_v[l].reshape(bs, r_rows, B_WIDTH),
                            cache_mem_k[l].reshape(bs, N_MEM, B_WIDTH),
                            cache_mem_v[l].reshape(bs, N_MEM, B_WIDTH), bias_sc, bias_sn)
    y_sample = _finish(xs, mix_r_s.reshape(bs * s, R_WIDTH), mix_bm_s.reshape(bs * s, 2 * B_WIDTH),
                       w_out_b, row(ln_g[l]), row(ln_b[l]), bs * s).reshape(bs, s, D_MODEL)
    s_shift = zs_s[:, -1]
    s_bk = att_s[:, :, B_WIDTH:2 * B_WIDTH].reshape(bs, s, B_HEADS, HEAD_DIM)
    s_bv = att_s[:, :, 2 * B_WIDTH:3 * B_WIDTH].reshape(bs, s, B_HEADS, HEAD_DIM)

    st = lambda a: a[None]
    return (y_prompt, y_sample, st(p_shift), st(p_wkv), st(p_bk), st(p_bv), st(p_mk), st(p_mv),
            st(s_shift), st(s_wkv), st(s_bk), st(s_bv))
```

```python
import functools

import numpy as np
import jax
import jax.numpy as jnp
from jax import lax
from jax.experimental import pallas as pl
from jax.experimental.pallas import tpu as pltpu

F32 = jnp.float32
BF16 = jnp.bfloat16

D_MODEL = 1024
HEAD_DIM = 64
R_WIDTH = 512
R_HEADS = 8
LOW_RANK = 64
SHIFT_WIDTH = 3 * R_WIDTH + 2 * LOW_RANK
B_WIDTH = 256
B_HEADS = 4
M_HEADS = 4
N_MEM = 256
CHUNK = 64
BAND_CHUNKS = 8
BAND_WINDOW = BAND_CHUNKS * CHUNK
BAND_LEN = BAND_WINDOW + CHUNK
REL_CLIP = 128
ATT_WIDTH = 6 * B_WIDTH
IN_WIDTH = SHIFT_WIDTH + R_WIDTH + ATT_WIDTH
LN_EPS = 1e-5
GN_EPS = 64e-5
ALPHA = 2.0 ** 0.25
ATT_SCALE = HEAD_DIM ** -0.5
BIAS_L = 1024

VMEM_LIMIT = 48 * 1024 * 1024

NN = ((1,), (0,))
NT = ((1,), (1,))
TN = ((0,), (0,))


def _dot(a, b, dims=NN):
    return lax.dot_general(a, b, (dims, ((), ())), preferred_element_type=F32)


def _split2(x):
    hi = x.astype(BF16)
    lo = (x - hi.astype(F32)).astype(BF16)
    return hi, lo


def _mm3(a, b, dims=NN):
    ah, al = _split2(a)
    bh, bl = _split2(b)
    return _dot(ah, bh, dims) + _dot(ah, bl, dims) + _dot(al, bh, dims)


def _mm_exact_lhs(lhs_bf16, x):
    x1 = x.astype(BF16)
    r1 = x - x1.astype(F32)
    x2 = r1.astype(BF16)
    x3 = (r1 - x2.astype(F32)).astype(BF16)
    return _dot(lhs_bf16, x1) + _dot(lhs_bf16, x2) + _dot(lhs_bf16, x3)


def _sigmoid(x):
    return 1.0 / (1.0 + jnp.exp(-x))


def _silu(x):
    return x * _sigmoid(x)


def _proj_kernel(x_ref, w_ref, *out_refs, splits):
    x = x_ref[...].astype(BF16)
    for o_ref, (lo, hi) in zip(out_refs, splits):
        o_ref[...] = _dot(x, w_ref[:, lo:hi])


def _proj(x, w_bf16, splits, tm):
    m, k = x.shape
    n = w_bf16.shape[1]
    return pl.pallas_call(
        functools.partial(_proj_kernel, splits=splits),
        name=f"proj_{m}x{n}",
        grid=(m // tm,),
        in_specs=[pl.BlockSpec((tm, k), lambda i: (i, 0)),
                  pl.BlockSpec((k, n), lambda i: (0, 0))],
        out_specs=[pl.BlockSpec((tm, hi - lo), lambda i: (i, 0)) for lo, hi in splits],
        out_shape=[jax.ShapeDtypeStruct((m, hi - lo), F32) for lo, hi in splits],
        compiler_params=pltpu.CompilerParams(
            dimension_semantics=("arbitrary",), vmem_limit_bytes=VMEM_LIMIT),
    )(x, w_bf16)


def _wkv_kernel(zs_ref, gr_ref, prev_ref, s0_ref, mu_ref, w0_ref, w2_ref, a0_ref, a2_ref,
                kk_ref, ka_ref, rk_ref, gng_ref, gnb_ref, out_ref, state_ref, carry_ref, *, bb_n, c):
    t = pl.program_id(1)

    @pl.when(t == 0)
    def _():
        carry_ref[...] = prev_ref[...]
        state_ref[...] = s0_ref[...]

    row = lax.broadcasted_iota(jnp.int32, (c, c), 0)
    col = lax.broadcasted_iota(jnp.int32, (c, c), 1)
    eye = (row == col).astype(F32)
    ltri = (col <= row).astype(BF16)
    level_masks = []
    shift = 0
    while (1 << shift) < c:
        rb = row >> shift
        level_masks.append(((rb & 1) == 1) & ((col >> shift) == rb - 1))
        shift += 1
    row2 = lax.broadcasted_iota(jnp.int32, (c, 2 * c), 0)
    col2 = lax.broadcasted_iota(jnp.int32, (c, 2 * c), 1)
    colm = col2 & (c - 1)
    mask_ak = (col2 >= c) & (colm < row2)
    mask_y = colm <= row2
    first_row = lax.broadcasted_iota(jnp.int32, (c, 1), 0) == 0

    mu = mu_ref[...]
    units = []
    for bb in range(bb_n):
        zs = zs_ref[bb]
        zprev = jnp.where(first_row, carry_ref[bb], pltpu.roll(zs, 1, 0))
        carry_ref[bb] = zs[c - 1:c, :]
        xs = zs + (zprev - zs) * mu
        r = xs[:, 0:R_WIDTH]
        k = xs[:, R_WIDTH:2 * R_WIDTH]
        v = xs[:, 2 * R_WIDTH:3 * R_WIDTH]
        wd = xs[:, 3 * R_WIDTH:3 * R_WIDTH + LOW_RANK]
        ad = xs[:, 3 * R_WIDTH + LOW_RANK:]

        u = -(w0_ref[...] + _mm3(jnp.tanh(wd), w2_ref[...]))
        softplus = jnp.maximum(u, 0.0) + jnp.log(1.0 + jnp.exp(-jnp.abs(u)))
        ld = -jnp.exp(-softplus - 0.5)
        cum = _mm_exact_lhs(ltri, ld)
        a = _sigmoid(a0_ref[...] + _mm3(ad, a2_ref[...]))

        ecum = jnp.exp(cum)
        einv = jnp.exp(-cum)
        eprev = jnp.exp(cum - ld)
        pc = ecum[c - 1:c, :]
        kkr = k * kk_ref[...]
        k2 = k * (1.0 + (a - 1.0) * ka_ref[...])
        rt = r * ecum
        kh = k2 * einv
        kp = kh * pc
        rk2 = r * k2 * rk_ref[...]
        gate = _silu(gr_ref[bb])

        for h in range(R_HEADS):
            sl = slice(h * HEAD_DIM, (h + 1) * HEAD_DIM)
            kkr_h = kkr[:, sl]
            nrm = jnp.sqrt(jnp.sum(kkr_h * kkr_h, axis=-1, keepdims=True))
            kkn = kkr_h / jnp.maximum(nrm, 1e-12)
            at = -kkn * eprev[:, sl]
            bh = kkn * a[:, sl] * einv[:, sl]
            v_h = v[:, sl]
            units.append(dict(
                bb=bb, h=h, sl=sl, v=v_h, pc=pc[:, sl], gate=gate[:, sl],
                ar=jnp.concatenate([at, rt[:, sl]], axis=0),
                bk=jnp.concatenate([bh, kh[:, sl]], axis=0),
                bkp=jnp.concatenate([bh * pc[:, sl], kp[:, sl]], axis=0),
                bonus=jnp.sum(rk2[:, sl], axis=-1, keepdims=True) * v_h))

    for un in units:
        un["a4"] = _mm3(un["ar"], un["bk"], NT)
        un["a_ab"] = un["a4"][:c, :c]
        un["tinv"] = eye + jnp.where(level_masks[0], un["a_ab"], 0.0)
    for mask in level_masks[1:]:
        for un in units:
            un["lt"] = _mm3(jnp.where(mask, un["a_ab"], 0.0), un["tinv"])
        for un in units:
            un["tinv"] = un["tinv"] + _mm3(un["tinv"], un["lt"])
    for un in units:
        vv = jnp.concatenate([un["v"], un["v"]], axis=0)
        un["akv"] = _mm3(jnp.where(mask_ak, un["a4"][:c], 0.0), vv)
    for un in units:
        un["s0"] = state_ref[un["bb"], un["h"]]
        un["ars"] = _mm3(un["ar"], un["s0"], NT)
    for un in units:
        pm = _mm3(un["tinv"], un["ars"][:c] + un["akv"])
        un["pv"] = jnp.concatenate([pm, un["v"]], axis=0)
    for un in units:
        state_ref[un["bb"], un["h"]] = un["s0"] * un["pc"] + _mm3(un["pv"], un["bkp"], TN)
    for un in units:
        y = un["ars"][c:] + _mm3(jnp.where(mask_y, un["a4"][c:], 0.0), un["pv"])
        mean = jnp.mean(y, axis=-1, keepdims=True)
        yc = y - mean
        var = jnp.mean(yc * yc, axis=-1, keepdims=True)
        yn = yc * lax.rsqrt(var + GN_EPS) * gng_ref[:, un["sl"]] + gnb_ref[:, un["sl"]]
        out_ref[un["bb"], :, un["sl"]] = (yn + un["bonus"]) * un["gate"]


def _wkv(zs, gr, prev, s0, params, *, bb_n, c):
    b, t, _ = zs.shape
    full = lambda arr: pl.BlockSpec(arr.shape, lambda i, j: (0,) * arr.ndim)
    return pl.pallas_call(
        functools.partial(_wkv_kernel, bb_n=bb_n, c=c),
        name=f"wkv_c{c}",
        grid=(b // bb_n, t // c),
        in_specs=[pl.BlockSpec((bb_n, c, SHIFT_WIDTH), lambda i, j: (i, j, 0)),
                  pl.BlockSpec((bb_n, c, R_WIDTH), lambda i, j: (i, j, 0)),
                  pl.BlockSpec((bb_n, 1, SHIFT_WIDTH), lambda i, j: (i, 0, 0)),
                  pl.BlockSpec((bb_n, R_HEADS, HEAD_DIM, HEAD_DIM), lambda i, j: (i, 0, 0, 0))]
                 + [full(p) for p in params],
        out_specs=[pl.BlockSpec((bb_n, c, R_WIDTH), lambda i, j: (i, j, 0)),
                   pl.BlockSpec((bb_n, R_HEADS, HEAD_DIM, HEAD_DIM), lambda i, j: (i, 0, 0, 0))],
        out_shape=[jax.ShapeDtypeStruct((b, t, R_WIDTH), F32),
                   jax.ShapeDtypeStruct((b, R_HEADS, HEAD_DIM, HEAD_DIM), F32)],
        scratch_shapes=[pltpu.VMEM((bb_n, 1, SHIFT_WIDTH), F32)],
        compiler_params=pltpu.CompilerParams(
            dimension_semantics=("arbitrary", "arbitrary"), vmem_limit_bytes=VMEM_LIMIT),
    )(zs, gr, prev, s0, *params)


def _fill_rel_bias(bias_ref, gtab_ref, offset):
    heads, nq, nk = bias_ref.shape
    for h in range(heads):
        g = jnp.broadcast_to(gtab_ref[h:h + 1, :], (nq, BIAS_L))
        bias_ref[h] = pltpu.roll(g, BIAS_L - offset, 1, stride=1, stride_axis=0)[:, :nk]


def _softmax_pv(s, v_bf16):
    m = jnp.max(s, axis=-1, keepdims=True)
    p = jnp.exp(s - m)
    l = jnp.sum(p, axis=-1, keepdims=True)
    return _dot(p.astype(BF16), v_bf16) / l


def _mem_heads(q_bf16, mk_bf16, mv_bf16):
    outs = []
    for h in range(M_HEADS):
        sl = slice(h * HEAD_DIM, (h + 1) * HEAD_DIM)
        s = _dot(q_bf16[:, sl], mk_bf16[:, sl], NT)
        outs.append(_softmax_pv(s, mv_bf16[:, sl]))
    return outs


def _attn_prompt_kernel(q_ref, kp_ref, kc_ref, vp_ref, vc_ref, gb_ref, mq_ref, gm_ref,
                        mk_ref, mv_ref, gtab_ref, out_ref, kcat_ref, vcat_ref, bias_ref, *, tq):
    j = pl.program_id(1)

    @pl.when((pl.program_id(0) == 0) & (j == 0))
    def _():
        _fill_rel_bias(bias_ref, gtab_ref, CHUNK - 1)

    kcat_ref[0:BAND_WINDOW] = kp_ref[0].astype(BF16)
    kcat_ref[BAND_WINDOW:] = kc_ref[0].astype(BF16)
    vcat_ref[0:BAND_WINDOW] = vp_ref[0].astype(BF16)
    vcat_ref[BAND_WINDOW:] = vc_ref[0].astype(BF16)
    q = (q_ref[0] * ATT_SCALE).astype(BF16)
    gate_b = _silu(gb_ref[0])
    kcol = lax.broadcasted_iota(jnp.int32, (CHUNK, BAND_LEN), 1)
    for i in range(tq // CHUNK):
        rows = slice(i * CHUNK, (i + 1) * CHUNK)
        first_valid = jnp.where(j == 0, BAND_WINDOW - i * CHUNK, 0)
        visible = kcol >= first_valid
        for h in range(B_HEADS):
            sl = slice(h * HEAD_DIM, (h + 1) * HEAD_DIM)
            kh = kcat_ref[i * CHUNK:i * CHUNK + BAND_LEN, sl]
            vh = vcat_ref[i * CHUNK:i * CHUNK + BAND_LEN, sl]
            s = _dot(q[rows, sl], kh, NT) + bias_ref[h]
            s = jnp.where(visible, s, -jnp.inf)
            out_ref[0, rows, sl] = _softmax_pv(s, vh) * gate_b[rows, sl]

    mq = (mq_ref[0] * ATT_SCALE).astype(BF16)
    gate_m = _silu(gm_ref[0])
    mem = _mem_heads(mq, mk_ref[0].astype(BF16), mv_ref[0].astype(BF16))
    for h in range(M_HEADS):
        sl = slice(h * HEAD_DIM, (h + 1) * HEAD_DIM)
        out_ref[0, :, B_WIDTH + h * HEAD_DIM:B_WIDTH + (h + 1) * HEAD_DIM] = mem[h] * gate_m[:, sl]


def _attn_prompt(att, memkv, gtab, *, tq):
    b, t, _ = att.shape
    col = lambda cidx: pl.BlockSpec((1, tq, B_WIDTH), lambda i, j: (i, j, cidx))
    prev = lambda cidx: pl.BlockSpec((1, tq, B_WIDTH), lambda i, j: (i, jnp.maximum(j - 1, 0), cidx))
    return pl.pallas_call(
        functools.partial(_attn_prompt_kernel, tq=tq),
        name="attn_prompt",
        grid=(b, t // tq),
        in_specs=[col(0), prev(1), col(1), prev(2), col(2), col(3), col(4), col(5),
                  pl.BlockSpec((1, N_MEM, B_WIDTH), lambda i, j: (i, 0, 0)),
                  pl.BlockSpec((1, N_MEM, B_WIDTH), lambda i, j: (i, 0, 1)),
                  pl.BlockSpec(gtab.shape, lambda i, j: (0, 0))],
        out_specs=pl.BlockSpec((1, tq, 2 * B_WIDTH), lambda i, j: (i, j, 0)),
        out_shape=jax.ShapeDtypeStruct((b, t, 2 * B_WIDTH), F32),
        scratch_shapes=[pltpu.VMEM((2 * BAND_WINDOW, B_WIDTH), BF16),
                        pltpu.VMEM((2 * BAND_WINDOW, B_WIDTH), BF16),
                        pltpu.VMEM((B_HEADS, CHUNK, BAND_LEN), F32)],
        compiler_params=pltpu.CompilerParams(
            dimension_semantics=("arbitrary", "arbitrary"), vmem_limit_bytes=VMEM_LIMIT),
    )(att, att, att, att, att, att, att, att, memkv, memkv, gtab)


def _attn_sample_kernel(att_ref, ck_ref, cv_ref, mk_ref, mv_ref, gtab_ref, out_ref, bias_ref):
    n_new = att_ref.shape[1]
    n_cached = ck_ref.shape[1]

    @pl.when(pl.program_id(0) == 0)
    def _():
        _fill_rel_bias(bias_ref, gtab_ref, n_new - 1)

    att = att_ref[0]
    q = (att[:, 0:B_WIDTH] * ATT_SCALE).astype(BF16)
    kn = att[:, B_WIDTH:2 * B_WIDTH].astype(BF16)
    vn = att[:, 2 * B_WIDTH:3 * B_WIDTH].astype(BF16)
    gate_b = _silu(att[:, 3 * B_WIDTH:4 * B_WIDTH])
    mq = (att[:, 4 * B_WIDTH:5 * B_WIDTH] * ATT_SCALE).astype(BF16)
    gate_m = _silu(att[:, 5 * B_WIDTH:6 * B_WIDTH])
    kc = ck_ref[0].astype(BF16)
    vc = cv_ref[0].astype(BF16)
    for h in range(B_HEADS):
        sl = slice(h * HEAD_DIM, (h + 1) * HEAD_DIM)
        s_c = _dot(q[:, sl], kc[:, sl], NT) + bias_ref[h, :, 0:n_cached]
        s_n = _dot(q[:, sl], kn[:, sl], NT) + bias_ref[h, :, n_cached:]
        m = jnp.maximum(jnp.max(s_c, axis=-1, keepdims=True), jnp.max(s_n, axis=-1, keepdims=True))
        p_c = jnp.exp(s_c - m)
        p_n = jnp.exp(s_n - m)
        l = jnp.sum(p_c, axis=-1, keepdims=True) + jnp.sum(p_n, axis=-1, keepdims=True)
        o = (_dot(p_c.astype(BF16), vc[:, sl]) + _dot(p_n.astype(BF16), vn[:, sl])) / l
        out_ref[0, :, sl] = o * gate_b[:, sl]
    mem = _mem_heads(mq, mk_ref[0].astype(BF16), mv_ref[0].astype(BF16))
    for h in range(M_HEADS):
        sl = slice(h * HEAD_DIM, (h + 1) * HEAD_DIM)
        out_ref[0, :, B_WIDTH + h * HEAD_DIM:B_WIDTH + (h + 1) * HEAD_DIM] = mem[h] * gate_m[:, sl]


def _attn_sample(att, cache_k, cache_v, mem_k, mem_v, gtab):
    b, s, _ = att.shape
    per_b = lambda arr: pl.BlockSpec((1,) + arr.shape[1:], lambda i: (i, 0, 0))
    return pl.pallas_call(
        _attn_sample_kernel,
        name="attn_sample",
        grid=(b,),
        in_specs=[per_b(att), per_b(cache_k), per_b(cache_v), per_b(mem_k), per_b(mem_v),
                  pl.BlockSpec(gtab.shape, lambda i: (0, 0))],
        out_specs=pl.BlockSpec((1, s, 2 * B_WIDTH), lambda i: (i, 0, 0)),
        out_shape=jax.ShapeDtypeStruct((b, s, 2 * B_WIDTH), F32),
        scratch_shapes=[pltpu.VMEM((B_HEADS, s, cache_k.shape[1] + s), F32)],
        compiler_params=pltpu.CompilerParams(dimension_semantics=("arbitrary",)),
    )(att, cache_k, cache_v, mem_k, mem_v, gtab)


def _finish_kernel(x_ref, mr_ref, mbm_ref, w_ref, g_ref, b_ref, y_ref):
    o = _dot(mr_ref[...].astype(BF16), w_ref[0:R_WIDTH, :])
    o = o + _dot(mbm_ref[...].astype(BF16), w_ref[R_WIDTH:, :])
    hres = ALPHA * x_ref[...] + o
    mean = jnp.mean(hres, axis=-1, keepdims=True)
    hc = hres - mean
    var = jnp.mean(hc * hc, axis=-1, keepdims=True)
    y_ref[...] = hc * lax.rsqrt(var + LN_EPS) * g_ref[...] + b_ref[...]


def _finish(x, mix_r, mix_bm, w_out_bf16, ln_g, ln_b, tm):
    m = x.shape[0]
    rows = lambda width: pl.BlockSpec((tm, width), lambda i: (i, 0))
    full = lambda arr: pl.BlockSpec(arr.shape, lambda i: (0, 0))
    return pl.pallas_call(
        _finish_kernel,
        name=f"finish_{m}",
        grid=(m // tm,),
        in_specs=[rows(D_MODEL), rows(R_WIDTH), rows(2 * B_WIDTH), full(w_out_bf16), full(ln_g), full(ln_b)],
        out_specs=rows(D_MODEL),
        out_shape=jax.ShapeDtypeStruct((m, D_MODEL), F32),
        compiler_params=pltpu.CompilerParams(
            dimension_semantics=("arbitrary",), vmem_limit_bytes=VMEM_LIMIT),
    )(x, mix_r, mix_bm, w_out_bf16, ln_g, ln_b)


IN_SPLITS = ((0, SHIFT_WIDTH), (SHIFT_WIDTH, SHIFT_WIDTH + R_WIDTH), (SHIFT_WIDTH + R_WIDTH, IN_WIDTH))


def _rel_bias_row(table, rel0):
    n_hi = rel0 - REL_CLIP
    n_lo = BIAS_L - n_hi - (2 * REL_CLIP + 1)
    heads = table.shape[0]
    return jnp.concatenate([jnp.broadcast_to(table[:, 2 * REL_CLIP:], (heads, n_hi)), table[:, ::-1],
                            jnp.broadcast_to(table[:, 0:1], (heads, n_lo))], axis=1)


def kernel(x_prompt, x_sample, mem_prompt, state_shift, state_wkv, cache_band_k, cache_band_v,
           cache_mem_k, cache_mem_v, w_in, mu_shift, w0, w2, a0, a2, k_k, k_a, r_k, gn_g, gn_b,
           rel_bias, w_mem_kv, w_out, ln_g, ln_b):
    bp, t, _ = x_prompt.shape
    bs, s, _ = x_sample.shape
    depth = w_in.shape[0]
    assert depth == 1 and t % BAND_WINDOW == 0 and s <= CHUNK
    keep = min(BAND_WINDOW, t)
    l = 0

    w_in_b = w_in[l].astype(BF16)
    w_out_b = w_out[l].astype(BF16)
    w_mem_b = w_mem_kv[l].astype(BF16)
    row = lambda p: p.reshape(1, -1)
    wkv_params = (row(mu_shift[l]), row(w0[l]), w2[l], row(a0[l]), a2[l], row(k_k[l]), row(k_a[l]),
                  row(r_k[l]), row(gn_g[l]), row(gn_b[l]))
    table = rel_bias[l]
    r_rows = cache_band_k.shape[2]
    gtab_p = _rel_bias_row(table, BAND_WINDOW + CHUNK - 1)
    gtab_s = _rel_bias_row(table, r_rows + s - 1)

    xp = x_prompt.reshape(bp * t, D_MODEL)
    zs, gr, att = _proj(xp, w_in_b, IN_SPLITS, 256)
    zs = zs.reshape(bp, t, SHIFT_WIDTH)
    att = att.reshape(bp, t, ATT_WIDTH)
    memkv, = _proj(mem_prompt.reshape(bp * N_MEM, D_MODEL), w_mem_b, ((0, 2 * B_WIDTH),), 256)
    memkv = memkv.reshape(bp, N_MEM, 2 * B_WIDTH)
    mix_r, p_wkv = _wkv(zs, gr.reshape(bp, t, R_WIDTH), jnp.zeros((bp, 1, SHIFT_WIDTH), F32),
                        jnp.zeros((bp, R_HEADS, HEAD_DIM, HEAD_DIM), F32), wkv_params, bb_n=2, c=CHUNK)
    mix_bm = _attn_prompt(att, memkv, gtab_p, tq=BAND_WINDOW)
    y_prompt = _finish(xp, mix_r.reshape(bp * t, R_WIDTH), mix_bm.reshape(bp * t, 2 * B_WIDTH),
                       w_out_b, row(ln_g[l]), row(ln_b[l]), 512).reshape(bp, t, D_MODEL)
    p_shift = zs[:, -1]
    p_bk = att[:, t - keep:, B_WIDTH:2 * B_WIDTH].reshape(bp, keep, B_HEADS, HEAD_DIM)
    p_bv = att[:, t - keep:, 2 * B_WIDTH:3 * B_WIDTH].reshape(bp, keep, B_HEADS, HEAD_DIM)
    p_mk = memkv[:, :, :B_WIDTH].reshape(bp, N_MEM, M_HEADS, HEAD_DIM)
    p_mv = memkv[:, :, B_WIDTH:].reshape(bp, N_MEM, M_HEADS, HEAD_DIM)

    xs = x_sample.reshape(bs * s, D_MODEL)
    zs_s, gr_s, att_s = _proj(xs, w_in_b, IN_SPLITS, bs * s)
    zs_s = zs_s.reshape(bs, s, SHIFT_WIDTH)
    att_s = att_s.reshape(bs, s, ATT_WIDTH)
    mix_r_s, s_wkv = _wkv(zs_s, gr_s.reshape(bs, s, R_WIDTH), state_shift[l][:, None, :], state_wkv[l],
                          wkv_params, bb_n=2, c=s)
    mix_bm_s = _attn_sample(att_s,
                            cache_band_k[l].reshape(bs, r_rows, B_WIDTH),
                            cache_band_v[l].reshape(bs, r_rows, B_WIDTH),
                            cache_mem_k[l].reshape(bs, N_MEM, B_WIDTH),
                            cache_mem_v[l].reshape(bs, N_MEM, B_WIDTH), gtab_s)
    y_sample = _finish(xs, mix_r_s.reshape(bs * s, R_WIDTH), mix_bm_s.reshape(bs * s, 2 * B_WIDTH),
                       w_out_b, row(ln_g[l]), row(ln_b[l]), bs * s).reshape(bs, s, D_MODEL)
    s_shift = zs_s[:, -1]
    s_bk = att_s[:, :, B_WIDTH:2 * B_WIDTH].reshape(bs, s, B_HEADS, HEAD_DIM)
    s_bv = att_s[:, :, 2 * B_WIDTH:3 * B_WIDTH].reshape(bs, s, B_HEADS, HEAD_DIM)

    st = lambda a: a[None]
    return (y_prompt, y_sample, st(p_shift), st(p_wkv), st(p_bk), st(p_bv), st(p_mk), st(p_mv),
            st(s_shift), st(s_wkv), st(s_bk), st(s_bv))
```

```python
import functools

import numpy as np
import jax
import jax.numpy as jnp
from jax import lax
from jax.experimental import pallas as pl
from jax.experimental.pallas import tpu as pltpu

F32 = jnp.float32
BF16 = jnp.bfloat16

D_MODEL = 1024
HEAD_DIM = 64
R_WIDTH = 512
R_HEADS = 8
LOW_RANK = 64
SHIFT_WIDTH = 3 * R_WIDTH + 2 * LOW_RANK
B_WIDTH = 256
B_HEADS = 4
M_HEADS = 4
N_MEM = 256
CHUNK = 64
BAND_CHUNKS = 8
BAND_WINDOW = BAND_CHUNKS * CHUNK
BAND_LEN = BAND_WINDOW + CHUNK
REL_CLIP = 128
ATT_WIDTH = 6 * B_WIDTH
IN_WIDTH = SHIFT_WIDTH + R_WIDTH + ATT_WIDTH
LN_EPS = 1e-5
GN_EPS = 64e-5
ALPHA = 2.0 ** 0.25
ATT_SCALE = HEAD_DIM ** -0.5
BIAS_L = 1024

VMEM_LIMIT = 48 * 1024 * 1024

NN = ((1,), (0,))
NT = ((1,), (1,))
TN = ((0,), (0,))


def _dot(a, b, dims=NN):
    return lax.dot_general(a, b, (dims, ((), ())), preferred_element_type=F32)


def _split2(x):
    hi = x.astype(BF16)
    lo = (x - hi.astype(F32)).astype(BF16)
    return hi, lo


def _mm3(a, b, dims=NN):
    ah, al = _split2(a)
    bh, bl = _split2(b)
    return _dot(ah, bh, dims) + _dot(ah, bl, dims) + _dot(al, bh, dims)


def _mm_exact_lhs(lhs_bf16, x):
    x1 = x.astype(BF16)
    r1 = x - x1.astype(F32)
    x2 = r1.astype(BF16)
    x3 = (r1 - x2.astype(F32)).astype(BF16)
    return _dot(lhs_bf16, x1) + _dot(lhs_bf16, x2) + _dot(lhs_bf16, x3)


def _sigmoid(x):
    return 1.0 / (1.0 + jnp.exp(-x))


def _silu(x):
    return x * _sigmoid(x)


def _proj_kernel(x_ref, w_ref, *out_refs, splits):
    x = x_ref[...].astype(BF16)
    for o_ref, (lo, hi) in zip(out_refs, splits):
        o_ref[...] = _dot(x, w_ref[:, lo:hi])


def _proj(x, w_bf16, splits, tm):
    m, k = x.shape
    n = w_bf16.shape[1]
    return pl.pallas_call(
        functools.partial(_proj_kernel, splits=splits),
        name=f"proj_{m}x{n}",
        grid=(m // tm,),
        in_specs=[pl.BlockSpec((tm, k), lambda i: (i, 0)),
                  pl.BlockSpec((k, n), lambda i: (0, 0))],
        out_specs=[pl.BlockSpec((tm, hi - lo), lambda i: (i, 0)) for lo, hi in splits],
        out_shape=[jax.ShapeDtypeStruct((m, hi - lo), F32) for lo, hi in splits],
        compiler_params=pltpu.CompilerParams(
            dimension_semantics=("arbitrary",), vmem_limit_bytes=VMEM_LIMIT),
    )(x, w_bf16)


def _wkv_kernel(zs_ref, gr_ref, prev_ref, s0_ref, mu_ref, w0_ref, w2_ref, a0_ref, a2_ref,
                kk_ref, ka_ref, rk_ref, gng_ref, gnb_ref, out_ref, state_ref, carry_ref, *, bb_n, c):
    t = pl.program_id(1)

    @pl.when(t == 0)
    def _():
        carry_ref[...] = prev_ref[...]
        state_ref[...] = s0_ref[...]

    row = lax.broadcasted_iota(jnp.int32, (c, c), 0)
    col = lax.broadcasted_iota(jnp.int32, (c, c), 1)
    eye = (row == col).astype(F32)
    ltri = (col <= row).astype(BF16)
    level_masks = []
    shift = 0
    while (1 << shift) < c:
        rb = row >> shift
        level_masks.append(((rb & 1) == 1) & ((col >> shift) == rb - 1))
        shift += 1
    row2 = lax.broadcasted_iota(jnp.int32, (c, 2 * c), 0)
    col2 = lax.broadcasted_iota(jnp.int32, (c, 2 * c), 1)
    colm = col2 & (c - 1)
    mask_ak = (col2 >= c) & (colm < row2)
    mask_y = colm <= row2
    first_row = lax.broadcasted_iota(jnp.int32, (c, 1), 0) == 0

    mu = mu_ref[...]
    units = []
    for bb in range(bb_n):
        zs = zs_ref[bb]
        zprev = jnp.where(first_row, carry_ref[bb], pltpu.roll(zs, 1, 0))
        carry_ref[bb] = zs[c - 1:c, :]
        xs = zs + (zprev - zs) * mu
        r = xs[:, 0:R_WIDTH]
        k = xs[:, R_WIDTH:2 * R_WIDTH]
        v = xs[:, 2 * R_WIDTH:3 * R_WIDTH]
        wd = xs[:, 3 * R_WIDTH:3 * R_WIDTH + LOW_RANK]
        ad = xs[:, 3 * R_WIDTH + LOW_RANK:]

        u = -(w0_ref[...] + _mm3(jnp.tanh(wd), w2_ref[...]))
        softplus = jnp.maximum(u, 0.0) + jnp.log(1.0 + jnp.exp(-jnp.abs(u)))
        ld = -jnp.exp(-softplus - 0.5)
        cum = _mm_exact_lhs(ltri, ld)
        a = _sigmoid(a0_ref[...] + _mm3(ad, a2_ref[...]))

        ecum = jnp.exp(cum)
        einv = jnp.exp(-cum)
        eprev = jnp.exp(cum - ld)
        pc = ecum[c - 1:c, :]
        kkr = k * kk_ref[...]
        k2 = k * (1.0 + (a - 1.0) * ka_ref[...])
        rt = r * ecum
        kh = k2 * einv
        kp = kh * pc
        rk2 = r * k2 * rk_ref[...]
        gate = _silu(gr_ref[bb])

        for h in range(R_HEADS):
            sl = slice(h * HEAD_DIM, (h + 1) * HEAD_DIM)
            kkr_h = kkr[:, sl]
            nrm = jnp.sqrt(jnp.sum(kkr_h * kkr_h, axis=-1, keepdims=True))
            kkn = kkr_h / jnp.maximum(nrm, 1e-12)
            at = -kkn * eprev[:, sl]
            bh = kkn * a[:, sl] * einv[:, sl]
            v_h = v[:, sl]
            units.append(dict(
                bb=bb, h=h, sl=sl, v=v_h, pc=pc[:, sl], gate=gate[:, sl],
                ar=jnp.concatenate([at, rt[:, sl]], axis=0),
                bk=jnp.concatenate([bh, kh[:, sl]], axis=0),
                bkp=jnp.concatenate([bh * pc[:, sl], kp[:, sl]], axis=0),
                bonus=jnp.sum(rk2[:, sl], axis=-1, keepdims=True) * v_h))

    zero = jnp.zeros((), BF16)
    for un in units:
        un["ar"] = un["ar"].astype(BF16)
        un["a4"] = _dot(un["ar"], un["bk"].astype(BF16), NT)
        un["a_ab"] = un["a4"][:c, :c]
        un["a_ab16"] = un["a_ab"].astype(BF16)
        un["tinv"] = eye + jnp.where(level_masks[0], un["a_ab"], 0.0)
    for mask in level_masks[1:]:
        for un in units:
            un["t16"] = un["tinv"].astype(BF16)
            un["lt"] = _dot(jnp.where(mask, un["a_ab16"], zero), un["t16"])
        for un in units:
            un["tinv"] = un["tinv"] + _dot(un["t16"], un["lt"].astype(BF16))
    for un in units:
        un["a416"] = un["a4"].astype(BF16)
        v16 = un["v"].astype(BF16)
        un["vv"] = jnp.concatenate([v16, v16], axis=0)
        un["akv"] = _dot(jnp.where(mask_ak, un["a416"][:c], zero), un["vv"])
    for un in units:
        un["s0"] = state_ref[un["bb"], un["h"]]
        un["ars"] = _dot(un["ar"], un["s0"].astype(BF16), NT)
    for un in units:
        pm = _dot(un["tinv"].astype(BF16), (un["ars"][:c] + un["akv"]).astype(BF16))
        un["pv"] = jnp.concatenate([pm.astype(BF16), un["vv"][c:]], axis=0)
    for un in units:
        state_ref[un["bb"], un["h"]] = (un["s0"] * un["pc"]
                                        + _dot(un["pv"], un["bkp"].astype(BF16), TN))
    for un in units:
        y = un["ars"][c:] + _dot(jnp.where(mask_y, un["a416"][c:], zero), un["pv"])
        mean = jnp.mean(y, axis=-1, keepdims=True)
        yc = y - mean
        var = jnp.mean(yc * yc, axis=-1, keepdims=True)
        yn = yc * lax.rsqrt(var + GN_EPS) * gng_ref[:, un["sl"]] + gnb_ref[:, un["sl"]]
        out_ref[un["bb"], :, un["sl"]] = (yn + un["bonus"]) * un["gate"]


def _wkv(zs, gr, prev, s0, params, *, bb_n, c):
    b, t, _ = zs.shape
    full = lambda arr: pl.BlockSpec(arr.shape, lambda i, j: (0,) * arr.ndim)
    return pl.pallas_call(
        functools.partial(_wkv_kernel, bb_n=bb_n, c=c),
        name=f"wkv_c{c}",
        grid=(b // bb_n, t // c),
        in_specs=[pl.BlockSpec((bb_n, c, SHIFT_WIDTH), lambda i, j: (i, j, 0)),
                  pl.BlockSpec((bb_n, c, R_WIDTH), lambda i, j: (i, j, 0)),
                  pl.BlockSpec((bb_n, 1, SHIFT_WIDTH), lambda i, j: (i, 0, 0)),
                  pl.BlockSpec((bb_n, R_HEADS, HEAD_DIM, HEAD_DIM), lambda i, j: (i, 0, 0, 0))]
                 + [full(p) for p in params],
        out_specs=[pl.BlockSpec((bb_n, c, R_WIDTH), lambda i, j: (i, j, 0)),
                   pl.BlockSpec((bb_n, R_HEADS, HEAD_DIM, HEAD_DIM), lambda i, j: (i, 0, 0, 0))],
        out_shape=[jax.ShapeDtypeStruct((b, t, R_WIDTH), F32),
                   jax.ShapeDtypeStruct((b, R_HEADS, HEAD_DIM, HEAD_DIM), F32)],
        scratch_shapes=[pltpu.VMEM((bb_n, 1, SHIFT_WIDTH), F32)],
        compiler_params=pltpu.CompilerParams(
            dimension_semantics=("arbitrary", "arbitrary"), vmem_limit_bytes=VMEM_LIMIT),
    )(zs, gr, prev, s0, *params)


def _fill_rel_bias(bias_ref, gtab_ref, offset):
    heads, nq, nk = bias_ref.shape
    for h in range(heads):
        g = jnp.broadcast_to(gtab_ref[h:h + 1, :], (nq, BIAS_L))
        bias_ref[h] = pltpu.roll(g, BIAS_L - offset, 1, stride=1, stride_axis=0)[:, :nk]


def _softmax_pv(s, v_bf16):
    m = jnp.max(s, axis=-1, keepdims=True)
    p = jnp.exp(s - m)
    l = jnp.sum(p, axis=-1, keepdims=True)
    return _dot(p.astype(BF16), v_bf16) / l


def _mem_heads(q_bf16, mk_bf16, mv_bf16):
    outs = []
    for h in range(M_HEADS):
        sl = slice(h * HEAD_DIM, (h + 1) * HEAD_DIM)
        s = _dot(q_bf16[:, sl], mk_bf16[:, sl], NT)
        outs.append(_softmax_pv(s, mv_bf16[:, sl]))
    return outs


def _attn_prompt_kernel(q_ref, kp_ref, kc_ref, vp_ref, vc_ref, gb_ref, mq_ref, gm_ref,
                        mk_ref, mv_ref, gtab_ref, out_ref, kcat_ref, vcat_ref, bias_ref, *, tq):
    j = pl.program_id(1)

    @pl.when((pl.program_id(0) == 0) & (j == 0))
    def _():
        _fill_rel_bias(bias_ref, gtab_ref, CHUNK - 1)

    kcat_ref[0:BAND_WINDOW] = kp_ref[0].astype(BF16)
    kcat_ref[BAND_WINDOW:] = kc_ref[0].astype(BF16)
    vcat_ref[0:BAND_WINDOW] = vp_ref[0].astype(BF16)
    vcat_ref[BAND_WINDOW:] = vc_ref[0].astype(BF16)
    q = (q_ref[0] * ATT_SCALE).astype(BF16)
    gate_b = _silu(gb_ref[0])
    kcol = lax.broadcasted_iota(jnp.int32, (CHUNK, BAND_LEN), 1)
    for i in range(tq // CHUNK):
        rows = slice(i * CHUNK, (i + 1) * CHUNK)
        first_valid = jnp.where(j == 0, BAND_WINDOW - i * CHUNK, 0)
        visible = kcol >= first_valid
        for h in range(B_HEADS):
            sl = slice(h * HEAD_DIM, (h + 1) * HEAD_DIM)
            kh = kcat_ref[i * CHUNK:i * CHUNK + BAND_LEN, sl]
            vh = vcat_ref[i * CHUNK:i * CHUNK + BAND_LEN, sl]
            s = _dot(q[rows, sl], kh, NT) + bias_ref[h]
            s = jnp.where(visible, s, -jnp.inf)
            out_ref[0, rows, sl] = _softmax_pv(s, vh) * gate_b[rows, sl]

    mq = (mq_ref[0] * ATT_SCALE).astype(BF16)
    gate_m = _silu(gm_ref[0])
    mem = _mem_heads(mq, mk_ref[0].astype(BF16), mv_ref[0].astype(BF16))
    for h in range(M_HEADS):
        sl = slice(h * HEAD_DIM, (h + 1) * HEAD_DIM)
        out_ref[0, :, B_WIDTH + h * HEAD_DIM:B_WIDTH + (h + 1) * HEAD_DIM] = mem[h] * gate_m[:, sl]


def _attn_prompt(att, memkv, gtab, *, tq):
    b, t, _ = att.shape
    col = lambda cidx: pl.BlockSpec((1, tq, B_WIDTH), lambda i, j: (i, j, cidx))
    prev = lambda cidx: pl.BlockSpec((1, tq, B_WIDTH), lambda i, j: (i, jnp.maximum(j - 1, 0), cidx))
    return pl.pallas_call(
        functools.partial(_attn_prompt_kernel, tq=tq),
        name="attn_prompt",
        grid=(b, t // tq),
        in_specs=[col(0), prev(1), col(1), prev(2), col(2), col(3), col(4), col(5),
                  pl.BlockSpec((1, N_MEM, B_WIDTH), lambda i, j: (i, 0, 0)),
                  pl.BlockSpec((1, N_MEM, B_WIDTH), lambda i, j: (i, 0, 1)),
                  pl.BlockSpec(gtab.shape, lambda i, j: (0, 0))],
        out_specs=pl.BlockSpec((1, tq, 2 * B_WIDTH), lambda i, j: (i, j, 0)),
        out_shape=jax.ShapeDtypeStruct((b, t, 2 * B_WIDTH), F32),
        scratch_shapes=[pltpu.VMEM((2 * BAND_WINDOW, B_WIDTH), BF16),
                        pltpu.VMEM((2 * BAND_WINDOW, B_WIDTH), BF16),
                        pltpu.VMEM((B_HEADS, CHUNK, BAND_LEN), F32)],
        compiler_params=pltpu.CompilerParams(
            dimension_semantics=("arbitrary", "arbitrary"), vmem_limit_bytes=VMEM_LIMIT),
    )(att, att, att, att, att, att, att, att, memkv, memkv, gtab)


def _attn_sample_kernel(att_ref, ck_ref, cv_ref, mk_ref, mv_ref, gtab_ref, out_ref, bias_ref):
    n_new = att_ref.shape[1]
    n_cached = ck_ref.shape[1]

    @pl.when(pl.program_id(0) == 0)
    def _():
        _fill_rel_bias(bias_ref, gtab_ref, n_new - 1)

    att = att_ref[0]
    q = (att[:, 0:B_WIDTH] * ATT_SCALE).astype(BF16)
    kn = att[:, B_WIDTH:2 * B_WIDTH].astype(BF16)
    vn = att[:, 2 * B_WIDTH:3 * B_WIDTH].astype(BF16)
    gate_b = _silu(att[:, 3 * B_WIDTH:4 * B_WIDTH])
    mq = (att[:, 4 * B_WIDTH:5 * B_WIDTH] * ATT_SCALE).astype(BF16)
    gate_m = _silu(att[:, 5 * B_WIDTH:6 * B_WIDTH])
    kc = ck_ref[0].astype(BF16)
    vc = cv_ref[0].astype(BF16)
    for h in range(B_HEADS):
        sl = slice(h * HEAD_DIM, (h + 1) * HEAD_DIM)
        s_c = _dot(q[:, sl], kc[:, sl], NT) + bias_ref[h, :, 0:n_cached]
        s_n = _dot(q[:, sl], kn[:, sl], NT) + bias_ref[h, :, n_cached:]
        m = jnp.maximum(jnp.max(s_c, axis=-1, keepdims=True), jnp.max(s_n, axis=-1, keepdims=True))
        p_c = jnp.exp(s_c - m)
        p_n = jnp.exp(s_n - m)
        l = jnp.sum(p_c, axis=-1, keepdims=True) + jnp.sum(p_n, axis=-1, keepdims=True)
        o = (_dot(p_c.astype(BF16), vc[:, sl]) + _dot(p_n.astype(BF16), vn[:, sl])) / l
        out_ref[0, :, sl] = o * gate_b[:, sl]
    mem = _mem_heads(mq, mk_ref[0].astype(BF16), mv_ref[0].astype(BF16))
    for h in range(M_HEADS):
        sl = slice(h * HEAD_DIM, (h + 1) * HEAD_DIM)
        out_ref[0, :, B_WIDTH + h * HEAD_DIM:B_WIDTH + (h + 1) * HEAD_DIM] = mem[h] * gate_m[:, sl]


def _attn_sample(att, cache_k, cache_v, mem_k, mem_v, gtab):
    b, s, _ = att.shape
    per_b = lambda arr: pl.BlockSpec((1,) + arr.shape[1:], lambda i: (i, 0, 0))
    return pl.pallas_call(
        _attn_sample_kernel,
        name="attn_sample",
        grid=(b,),
        in_specs=[per_b(att), per_b(cache_k), per_b(cache_v), per_b(mem_k), per_b(mem_v),
                  pl.BlockSpec(gtab.shape, lambda i: (0, 0))],
        out_specs=pl.BlockSpec((1, s, 2 * B_WIDTH), lambda i: (i, 0, 0)),
        out_shape=jax.ShapeDtypeStruct((b, s, 2 * B_WIDTH), F32),
        scratch_shapes=[pltpu.VMEM((B_HEADS, s, cache_k.shape[1] + s), F32)],
        compiler_params=pltpu.CompilerParams(dimension_semantics=("arbitrary",)),
    )(att, cache_k, cache_v, mem_k, mem_v, gtab)


def _finish_kernel(x_ref, mr_ref, mbm_ref, w_ref, g_ref, b_ref, y_ref):
    o = _dot(mr_ref[...].astype(BF16), w_ref[0:R_WIDTH, :])
    o = o + _dot(mbm_ref[...].astype(BF16), w_ref[R_WIDTH:, :])
    hres = ALPHA * x_ref[...] + o
    mean = jnp.mean(hres, axis=-1, keepdims=True)
    hc = hres - mean
    var = jnp.mean(hc * hc, axis=-1, keepdims=True)
    y_ref[...] = hc * lax.rsqrt(var + LN_EPS) * g_ref[...] + b_ref[...]


def _finish(x, mix_r, mix_bm, w_out_bf16, ln_g, ln_b, tm):
    m = x.shape[0]
    rows = lambda width: pl.BlockSpec((tm, width), lambda i: (i, 0))
    full = lambda arr: pl.BlockSpec(arr.shape, lambda i: (0, 0))
    return pl.pallas_call(
        _finish_kernel,
        name=f"finish_{m}",
        grid=(m // tm,),
        in_specs=[rows(D_MODEL), rows(R_WIDTH), rows(2 * B_WIDTH), full(w_out_bf16), full(ln_g), full(ln_b)],
        out_specs=rows(D_MODEL),
        out_shape=jax.ShapeDtypeStruct((m, D_MODEL), F32),
        compiler_params=pltpu.CompilerParams(
            dimension_semantics=("arbitrary",), vmem_limit_bytes=VMEM_LIMIT),
    )(x, mix_r, mix_bm, w_out_bf16, ln_g, ln_b)


IN_SPLITS = ((0, SHIFT_WIDTH), (SHIFT_WIDTH, SHIFT_WIDTH + R_WIDTH), (SHIFT_WIDTH + R_WIDTH, IN_WIDTH))


def _rel_bias_row(table, rel0):
    n_hi = rel0 - REL_CLIP
    n_lo = BIAS_L - n_hi - (2 * REL_CLIP + 1)
    heads = table.shape[0]
    return jnp.concatenate([jnp.broadcast_to(table[:, 2 * REL_CLIP:], (heads, n_hi)), table[:, ::-1],
                            jnp.broadcast_to(table[:, 0:1], (heads, n_lo))], axis=1)


def kernel(x_prompt, x_sample, mem_prompt, state_shift, state_wkv, cache_band_k, cache_band_v,
           cache_mem_k, cache_mem_v, w_in, mu_shift, w0, w2, a0, a2, k_k, k_a, r_k, gn_g, gn_b,
           rel_bias, w_mem_kv, w_out, ln_g, ln_b):
    bp, t, _ = x_prompt.shape
    bs, s, _ = x_sample.shape
    depth = w_in.shape[0]
    assert depth == 1 and t % BAND_WINDOW == 0 and s <= CHUNK
    keep = min(BAND_WINDOW, t)
    l = 0

    w_in_b = w_in[l].astype(BF16)
    w_out_b = w_out[l].astype(BF16)
    w_mem_b = w_mem_kv[l].astype(BF16)
    row = lambda p: p.reshape(1, -1)
    wkv_params = (row(mu_shift[l]), row(w0[l]), w2[l], row(a0[l]), a2[l], row(k_k[l]), row(k_a[l]),
                  row(r_k[l]), row(gn_g[l]), row(gn_b[l]))
    table = rel_bias[l]
    r_rows = cache_band_k.shape[2]
    gtab_p = _rel_bias_row(table, BAND_WINDOW + CHUNK - 1)
    gtab_s = _rel_bias_row(table, r_rows + s - 1)

    xp = x_prompt.reshape(bp * t, D_MODEL)
    zs, gr, att = _proj(xp, w_in_b, IN_SPLITS, 256)
    zs = zs.reshape(bp, t, SHIFT_WIDTH)
    att = att.reshape(bp, t, ATT_WIDTH)
    memkv, = _proj(mem_prompt.reshape(bp * N_MEM, D_MODEL), w_mem_b, ((0, 2 * B_WIDTH),), 256)
    memkv = memkv.reshape(bp, N_MEM, 2 * B_WIDTH)
    mix_r, p_wkv = _wkv(zs, gr.reshape(bp, t, R_WIDTH), jnp.zeros((bp, 1, SHIFT_WIDTH), F32),
                        jnp.zeros((bp, R_HEADS, HEAD_DIM, HEAD_DIM), F32), wkv_params, bb_n=2, c=CHUNK)
    mix_bm = _attn_prompt(att, memkv, gtab_p, tq=BAND_WINDOW)
    y_prompt = _finish(xp, mix_r.reshape(bp * t, R_WIDTH), mix_bm.reshape(bp * t, 2 * B_WIDTH),
                       w_out_b, row(ln_g[l]), row(ln_b[l]), 512).reshape(bp, t, D_MODEL)
    p_shift = zs[:, -1]
    p_bk = att[:, t - keep:, B_WIDTH:2 * B_WIDTH].reshape(bp, keep, B_HEADS, HEAD_DIM)
    p_bv = att[:, t - keep:, 2 * B_WIDTH:3 * B_WIDTH].reshape(bp, keep, B_HEADS, HEAD_DIM)
    p_mk = memkv[:, :, :B_WIDTH].reshape(bp, N_MEM, M_HEADS, HEAD_DIM)
    p_mv = memkv[:, :, B_WIDTH:].reshape(bp, N_MEM, M_HEADS, HEAD_DIM)

    xs = x_sample.reshape(bs * s, D_MODEL)
    zs_s, gr_s, att_s = _proj(xs, w_in_b, IN_SPLITS, bs * s)
    zs_s = zs_s.reshape(bs, s, SHIFT_WIDTH)
    att_s = att_s.reshape(bs, s, ATT_WIDTH)
    mix_r_s, s_wkv = _wkv(zs_s, gr_s.reshape(bs, s, R_WIDTH), state_shift[l][:, None, :], state_wkv[l],
                          wkv_params, bb_n=2, c=s)
    mix_bm_s = _attn_sample(att_s,
                            cache_band_k[l].reshape(bs, r_rows, B_WIDTH),
                            cache_band_v[l].reshape(bs, r_rows, B_WIDTH),
                            cache_mem_k[l].reshape(bs, N_MEM, B_WIDTH),
                            cache_mem_v[l].reshape(bs, N_MEM, B_WIDTH), gtab_s)
    y_sample = _finish(xs, mix_r_s.reshape(bs * s, R_WIDTH), mix_bm_s.reshape(bs * s, 2 * B_WIDTH),
                       w_out_b, row(ln_g[l]), row(ln_b[l]), bs * s).reshape(bs, s, D_MODEL)
    s_shift = zs_s[:, -1]
    s_bk = att_s[:, :, B_WIDTH:2 * B_WIDTH].reshape(bs, s, B_HEADS, HEAD_DIM)
    s_bv = att_s[:, :, 2 * B_WIDTH:3 * B_WIDTH].reshape(bs, s, B_HEADS, HEAD_DIM)

    st = lambda a: a[None]
    return (y_prompt, y_sample, st(p_shift), st(p_wkv), st(p_bk), st(p_bv), st(p_mk), st(p_mv),
            st(s_shift), st(s_wkv), st(s_bk), st(s_bv))
```

```python
import functools

import numpy as np
import jax
import jax.numpy as jnp
from jax import lax
from jax.experimental import pallas as pl
from jax.experimental.pallas import tpu as pltpu

F32 = jnp.float32
BF16 = jnp.bfloat16

D_MODEL = 1024
HEAD_DIM = 64
R_WIDTH = 512
R_HEADS = 8
LOW_RANK = 64
SHIFT_WIDTH = 3 * R_WIDTH + 2 * LOW_RANK
B_WIDTH = 256
B_HEADS = 4
M_HEADS = 4
N_MEM = 256
CHUNK = 64
BAND_CHUNKS = 8
BAND_WINDOW = BAND_CHUNKS * CHUNK
BAND_LEN = BAND_WINDOW + CHUNK
REL_CLIP = 128
ATT_WIDTH = 6 * B_WIDTH
IN_WIDTH = SHIFT_WIDTH + R_WIDTH + ATT_WIDTH
LN_EPS = 1e-5
GN_EPS = 64e-5
ALPHA = 2.0 ** 0.25
ATT_SCALE = HEAD_DIM ** -0.5
BIAS_L = 1024
PAIR = 2 * HEAD_DIM
ATTN_GROUP_CHUNKS = 2
MEM_ROWS = 128
MEM_GROUP_BLOCKS = 2

VMEM_LIMIT = 48 * 1024 * 1024

NN = ((1,), (0,))
NT = ((1,), (1,))
TN = ((0,), (0,))


def _dot(a, b, dims=NN):
    return lax.dot_general(a, b, (dims, ((), ())), preferred_element_type=F32)


def _split2(x):
    hi = x.astype(BF16)
    lo = (x - hi.astype(F32)).astype(BF16)
    return hi, lo


def _mm3(a, b, dims=NN):
    ah, al = _split2(a)
    bh, bl = _split2(b)
    return _dot(ah, bh, dims) + _dot(ah, bl, dims) + _dot(al, bh, dims)


def _mm_exact_lhs(lhs_bf16, x):
    x1 = x.astype(BF16)
    r1 = x - x1.astype(F32)
    x2 = r1.astype(BF16)
    x3 = (r1 - x2.astype(F32)).astype(BF16)
    return _dot(lhs_bf16, x1) + _dot(lhs_bf16, x2) + _dot(lhs_bf16, x3)


def _sigmoid(x):
    return 1.0 / (1.0 + jnp.exp(-x))


def _silu(x):
    return x * _sigmoid(x)


def _proj_kernel(x_ref, w_ref, *out_refs, splits):
    x = x_ref[...].astype(BF16)
    for o_ref, (lo, hi) in zip(out_refs, splits):
        o_ref[...] = _dot(x, w_ref[:, lo:hi])


def _proj(x, w_bf16, splits, tm):
    m, k = x.shape
    n = w_bf16.shape[1]
    return pl.pallas_call(
        functools.partial(_proj_kernel, splits=splits),
        name=f"proj_{m}x{n}",
        grid=(m // tm,),
        in_specs=[pl.BlockSpec((tm, k), lambda i: (i, 0)),
                  pl.BlockSpec((k, n), lambda i: (0, 0))],
        out_specs=[pl.BlockSpec((tm, hi - lo), lambda i: (i, 0)) for lo, hi in splits],
        out_shape=[jax.ShapeDtypeStruct((m, hi - lo), F32) for lo, hi in splits],
        compiler_params=pltpu.CompilerParams(
            dimension_semantics=("arbitrary",), vmem_limit_bytes=VMEM_LIMIT),
    )(x, w_bf16)


def _wkv_kernel(zs_ref, gr_ref, prev_ref, s0_ref, mu_ref, w0_ref, w2_ref, a0_ref, a2_ref,
                kk_ref, ka_ref, rk_ref, gng_ref, gnb_ref, out_ref, state_ref, carry_ref, *, bb_n, c):
    t = pl.program_id(1)

    @pl.when(t == 0)
    def _():
        carry_ref[...] = prev_ref[...]
        state_ref[...] = s0_ref[...]

    row = lax.broadcasted_iota(jnp.int32, (c, c), 0)
    col = lax.broadcasted_iota(jnp.int32, (c, c), 1)
    eye = (row == col).astype(F32)
    ltri = (col <= row).astype(BF16)
    level_masks = []
    shift = 0
    while (1 << shift) < c:
        rb = row >> shift
        level_masks.append(((rb & 1) == 1) & ((col >> shift) == rb - 1))
        shift += 1
    row2 = lax.broadcasted_iota(jnp.int32, (c, 2 * c), 0)
    col2 = lax.broadcasted_iota(jnp.int32, (c, 2 * c), 1)
    colm = col2 & (c - 1)
    mask_ak = (col2 >= c) & (colm < row2)
    mask_y = colm <= row2
    first_row = lax.broadcasted_iota(jnp.int32, (c, 1), 0) == 0

    mu = mu_ref[...]
    units = []
    for bb in range(bb_n):
        zs = zs_ref[bb]
        zprev = jnp.where(first_row, carry_ref[bb], pltpu.roll(zs, 1, 0))
        carry_ref[bb] = zs[c - 1:c, :]
        xs = zs + (zprev - zs) * mu
        r = xs[:, 0:R_WIDTH]
        k = xs[:, R_WIDTH:2 * R_WIDTH]
        v = xs[:, 2 * R_WIDTH:3 * R_WIDTH]
        wd = xs[:, 3 * R_WIDTH:3 * R_WIDTH + LOW_RANK]
        ad = xs[:, 3 * R_WIDTH + LOW_RANK:]

        u = -(w0_ref[...] + _mm3(jnp.tanh(wd), w2_ref[...]))
        softplus = jnp.maximum(u, 0.0) + jnp.log(1.0 + jnp.exp(-jnp.abs(u)))
        ld = -jnp.exp(-softplus - 0.5)
        cum = _mm_exact_lhs(ltri, ld)
        a = _sigmoid(a0_ref[...] + _mm3(ad, a2_ref[...]))

        ecum = jnp.exp(cum)
        einv = jnp.exp(-cum)
        eprev = jnp.exp(cum - ld)
        pc = ecum[c - 1:c, :]
        kkr = k * kk_ref[...]
        k2 = k * (1.0 + (a - 1.0) * ka_ref[...])
        rt = r * ecum
        kh = k2 * einv
        kp = kh * pc
        rk2 = r * k2 * rk_ref[...]
        gate = _silu(gr_ref[bb])

        for h in range(R_HEADS):
            sl = slice(h * HEAD_DIM, (h + 1) * HEAD_DIM)
            kkr_h = kkr[:, sl]
            nrm = jnp.sqrt(jnp.sum(kkr_h * kkr_h, axis=-1, keepdims=True))
            kkn = kkr_h / jnp.maximum(nrm, 1e-12)
            at = -kkn * eprev[:, sl]
            bh = kkn * a[:, sl] * einv[:, sl]
            v_h = v[:, sl]
            units.append(dict(
                bb=bb, h=h, sl=sl, v=v_h, pc=pc[:, sl], gate=gate[:, sl],
                ar=jnp.concatenate([at, rt[:, sl]], axis=0),
                bk=jnp.concatenate([bh, kh[:, sl]], axis=0),
                bkp=jnp.concatenate([bh * pc[:, sl], kp[:, sl]], axis=0),
                bonus=jnp.sum(rk2[:, sl], axis=-1, keepdims=True) * v_h))

    zero = jnp.zeros((), BF16)
    for un in units:
        un["ar"] = un["ar"].astype(BF16)
        un["a4"] = _dot(un["ar"], un["bk"].astype(BF16), NT)
        un["a_ab"] = un["a4"][:c, :c]
        un["a_ab16"] = un["a_ab"].astype(BF16)
        un["tinv"] = eye + jnp.where(level_masks[0], un["a_ab"], 0.0)
    for mask in level_masks[1:]:
        for un in units:
            un["t16"] = un["tinv"].astype(BF16)
            un["lt"] = _dot(jnp.where(mask, un["a_ab16"], zero), un["t16"])
        for un in units:
            un["tinv"] = un["tinv"] + _dot(un["t16"], un["lt"].astype(BF16))
    for un in units:
        un["a416"] = un["a4"].astype(BF16)
        v16 = un["v"].astype(BF16)
        un["vv"] = jnp.concatenate([v16, v16], axis=0)
        un["akv"] = _dot(jnp.where(mask_ak, un["a416"][:c], zero), un["vv"])
    for un in units:
        un["s0"] = state_ref[un["bb"], un["h"]]
        un["ars"] = _dot(un["ar"], un["s0"].astype(BF16), NT)
    for un in units:
        pm = _dot(un["tinv"].astype(BF16), (un["ars"][:c] + un["akv"]).astype(BF16))
        un["pv"] = jnp.concatenate([pm.astype(BF16), un["vv"][c:]], axis=0)
    for un in units:
        state_ref[un["bb"], un["h"]] = (un["s0"] * un["pc"]
                                        + _dot(un["pv"], un["bkp"].astype(BF16), TN))
    for un in units:
        y = un["ars"][c:] + _dot(jnp.where(mask_y, un["a416"][c:], zero), un["pv"])
        mean = jnp.mean(y, axis=-1, keepdims=True)
        yc = y - mean
        var = jnp.mean(yc * yc, axis=-1, keepdims=True)
        yn = yc * lax.rsqrt(var + GN_EPS) * gng_ref[:, un["sl"]] + gnb_ref[:, un["sl"]]
        out_ref[un["bb"], :, un["sl"]] = (yn + un["bonus"]) * un["gate"]


def _wkv(zs, gr, prev, s0, params, *, bb_n, c):
    b, t, _ = zs.shape
    full = lambda arr: pl.BlockSpec(arr.shape, lambda i, j: (0,) * arr.ndim)
    return pl.pallas_call(
        functools.partial(_wkv_kernel, bb_n=bb_n, c=c),
        name=f"wkv_c{c}",
        grid=(b // bb_n, t // c),
        in_specs=[pl.BlockSpec((bb_n, c, SHIFT_WIDTH), lambda i, j: (i, j, 0)),
                  pl.BlockSpec((bb_n, c, R_WIDTH), lambda i, j: (i, j, 0)),
                  pl.BlockSpec((bb_n, 1, SHIFT_WIDTH), lambda i, j: (i, 0, 0)),
                  pl.BlockSpec((bb_n, R_HEADS, HEAD_DIM, HEAD_DIM), lambda i, j: (i, 0, 0, 0))]
                 + [full(p) for p in params],
        out_specs=[pl.BlockSpec((bb_n, c, R_WIDTH), lambda i, j: (i, j, 0)),
                   pl.BlockSpec((bb_n, R_HEADS, HEAD_DIM, HEAD_DIM), lambda i, j: (i, 0, 0, 0))],
        out_shape=[jax.ShapeDtypeStruct((b, t, R_WIDTH), F32),
                   jax.ShapeDtypeStruct((b, R_HEADS, HEAD_DIM, HEAD_DIM), F32)],
        scratch_shapes=[pltpu.VMEM((bb_n, 1, SHIFT_WIDTH), F32)],
        compiler_params=pltpu.CompilerParams(
            dimension_semantics=("arbitrary", "arbitrary"), vmem_limit_bytes=VMEM_LIMIT),
    )(zs, gr, prev, s0, *params)


def _fill_rel_bias(bias_ref, gtab_ref, offset):
    heads, nq, nk = bias_ref.shape
    for h in range(heads):
        g = jnp.broadcast_to(gtab_ref[h:h + 1, :], (nq, BIAS_L))
        bias_ref[h] = pltpu.roll(g, BIAS_L - offset, 1, stride=1, stride_axis=0)[:, :nk]


def _softmax_pv(s, v_bf16):
    m = jnp.max(s, axis=-1, keepdims=True)
    p = jnp.exp(s - m)
    l = jnp.sum(p, axis=-1, keepdims=True)
    return _dot(p.astype(BF16), v_bf16) / l


def _mem_heads(q_bf16, mk_bf16, mv_bf16):
    outs = []
    for h in range(M_HEADS):
        sl = slice(h * HEAD_DIM, (h + 1) * HEAD_DIM)
        s = _dot(q_bf16[:, sl], mk_bf16[:, sl], NT)
        outs.append(_softmax_pv(s, mv_bf16[:, sl]))
    return outs


def _stack_heads(q2, left):
    zero = jnp.zeros((), q2.dtype)
    return jnp.concatenate([jnp.where(left, q2, zero), jnp.where(left, zero, q2)], axis=0)


def _attend_pairs(jobs, left):
    for jb in jobs:
        s = _dot(jb["lhs"], jb["k"], NT)
        if jb.get("bias") is not None:
            s = s + jb["bias"]
        if jb.get("visible") is not None:
            s = jnp.where(jb["visible"], s, -jnp.inf)
        jb["s"] = s
    for jb in jobs:
        m = jnp.max(jb["s"], axis=-1, keepdims=True)
        p = jnp.exp(jb["s"] - m)
        jb["l"] = jnp.sum(p, axis=-1, keepdims=True)
        jb["p"] = p.astype(BF16)
    outs = []
    for jb in jobs:
        o2 = _dot(jb["p"], jb["v"]) / jb["l"]
        n = o2.shape[0] // 2
        outs.append(jnp.where(left, o2[:n], o2[n:]))
    return outs


def _attn_prompt_kernel(q_ref, kp_ref, kc_ref, vp_ref, vc_ref, gb_ref, mq_ref, gm_ref,
                        mk_ref, mv_ref, gtab_ref, out_ref, kcat_ref, vcat_ref, bias_ref, bias2_ref, *, tq):
    j = pl.program_id(1)

    @pl.when((pl.program_id(0) == 0) & (j == 0))
    def _():
        _fill_rel_bias(bias_ref, gtab_ref, CHUNK - 1)
        for h in range(B_HEADS):
            bias2_ref[h // 2, (h % 2) * CHUNK:(h % 2 + 1) * CHUNK, :] = bias_ref[h]

    kcat_ref[0:BAND_WINDOW] = kp_ref[0].astype(BF16)
    kcat_ref[BAND_WINDOW:] = kc_ref[0].astype(BF16)
    vcat_ref[0:BAND_WINDOW] = vp_ref[0].astype(BF16)
    vcat_ref[BAND_WINDOW:] = vc_ref[0].astype(BF16)
    q = (q_ref[0] * ATT_SCALE).astype(BF16)
    gate_b = _silu(gb_ref[0])
    left = lax.broadcasted_iota(jnp.int32, (1, PAIR), 1) < HEAD_DIM
    kcol = lax.broadcasted_iota(jnp.int32, (1, BAND_LEN), 1)
    n_chunks = tq // CHUNK
    for i0 in range(0, n_chunks, ATTN_GROUP_CHUNKS):
        jobs = []
        for i in range(i0, i0 + ATTN_GROUP_CHUNKS):
            rows = slice(i * CHUNK, (i + 1) * CHUNK)
            keys = slice(i * CHUNK, i * CHUNK + BAND_LEN)
            visible = kcol >= jnp.where(j == 0, BAND_WINDOW - i * CHUNK, 0)
            for pr in range(B_HEADS // 2):
                lanes = slice(pr * PAIR, (pr + 1) * PAIR)
                jobs.append(dict(rows=rows, lanes=lanes, lhs=_stack_heads(q[rows, lanes], left),
                                 k=kcat_ref[keys, lanes], v=vcat_ref[keys, lanes],
                                 bias=bias2_ref[pr], visible=visible))
        for jb, o in zip(jobs, _attend_pairs(jobs, left)):
            out_ref[0, jb["rows"], jb["lanes"]] = o * gate_b[jb["rows"], jb["lanes"]]

    mq = (mq_ref[0] * ATT_SCALE).astype(BF16)
    gate_m = _silu(gm_ref[0])
    mk = mk_ref[0].astype(BF16)
    mv = mv_ref[0].astype(BF16)
    for r0 in range(0, tq, MEM_ROWS * MEM_GROUP_BLOCKS):
        jobs = []
        for r in range(r0, r0 + MEM_ROWS * MEM_GROUP_BLOCKS, MEM_ROWS):
            rows = slice(r, r + MEM_ROWS)
            for pr in range(M_HEADS // 2):
                lanes = slice(pr * PAIR, (pr + 1) * PAIR)
                jobs.append(dict(rows=rows, lanes=lanes, lhs=_stack_heads(mq[rows, lanes], left),
                                 k=mk[:, lanes], v=mv[:, lanes]))
        for jb, o in zip(jobs, _attend_pairs(jobs, left)):
            out_ref[0, jb["rows"], B_WIDTH + jb["lanes"].start:B_WIDTH + jb["lanes"].stop] = (
                o * gate_m[jb["rows"], jb["lanes"]])


def _attn_prompt(att, memkv, gtab, *, tq):
    b, t, _ = att.shape
    col = lambda cidx: pl.BlockSpec((1, tq, B_WIDTH), lambda i, j: (i, j, cidx))
    prev = lambda cidx: pl.BlockSpec((1, tq, B_WIDTH), lambda i, j: (i, jnp.maximum(j - 1, 0), cidx))
    return pl.pallas_call(
        functools.partial(_attn_prompt_kernel, tq=tq),
        name="attn_prompt",
        grid=(b, t // tq),
        in_specs=[col(0), prev(1), col(1), prev(2), col(2), col(3), col(4), col(5),
                  pl.BlockSpec((1, N_MEM, B_WIDTH), lambda i, j: (i, 0, 0)),
                  pl.BlockSpec((1, N_MEM, B_WIDTH), lambda i, j: (i, 0, 1)),
                  pl.BlockSpec(gtab.shape, lambda i, j: (0, 0))],
        out_specs=pl.BlockSpec((1, tq, 2 * B_WIDTH), lambda i, j: (i, j, 0)),
        out_shape=jax.ShapeDtypeStruct((b, t, 2 * B_WIDTH), F32),
        scratch_shapes=[pltpu.VMEM((2 * BAND_WINDOW, B_WIDTH), BF16),
                        pltpu.VMEM((2 * BAND_WINDOW, B_WIDTH), BF16),
                        pltpu.VMEM((B_HEADS, CHUNK, BAND_LEN), F32),
                        pltpu.VMEM((B_HEADS // 2, 2 * CHUNK, BAND_LEN), F32)],
        compiler_params=pltpu.CompilerParams(
            dimension_semantics=("arbitrary", "arbitrary"), vmem_limit_bytes=VMEM_LIMIT),
    )(att, att, att, att, att, att, att, att, memkv, memkv, gtab)


def _attn_sample_kernel(att_ref, ck_ref, cv_ref, mk_ref, mv_ref, gtab_ref, out_ref, bias_ref):
    n_new = att_ref.shape[1]
    n_cached = ck_ref.shape[1]

    @pl.when(pl.program_id(0) == 0)
    def _():
        _fill_rel_bias(bias_ref, gtab_ref, n_new - 1)

    att = att_ref[0]
    q = (att[:, 0:B_WIDTH] * ATT_SCALE).astype(BF16)
    kn = att[:, B_WIDTH:2 * B_WIDTH].astype(BF16)
    vn = att[:, 2 * B_WIDTH:3 * B_WIDTH].astype(BF16)
    gate_b = _silu(att[:, 3 * B_WIDTH:4 * B_WIDTH])
    mq = (att[:, 4 * B_WIDTH:5 * B_WIDTH] * ATT_SCALE).astype(BF16)
    gate_m = _silu(att[:, 5 * B_WIDTH:6 * B_WIDTH])
    kc = ck_ref[0].astype(BF16)
    vc = cv_ref[0].astype(BF16)
    for h in range(B_HEADS):
        sl = slice(h * HEAD_DIM, (h + 1) * HEAD_DIM)
        s_c = _dot(q[:, sl], kc[:, sl], NT) + bias_ref[h, :, 0:n_cached]
        s_n = _dot(q[:, sl], kn[:, sl], NT) + bias_ref[h, :, n_cached:]
        m = jnp.maximum(jnp.max(s_c, axis=-1, keepdims=True), jnp.max(s_n, axis=-1, keepdims=True))
        p_c = jnp.exp(s_c - m)
        p_n = jnp.exp(s_n - m)
        l = jnp.sum(p_c, axis=-1, keepdims=True) + jnp.sum(p_n, axis=-1, keepdims=True)
        o = (_dot(p_c.astype(BF16), vc[:, sl]) + _dot(p_n.astype(BF16), vn[:, sl])) / l
        out_ref[0, :, sl] = o * gate_b[:, sl]
    mem = _mem_heads(mq, mk_ref[0].astype(BF16), mv_ref[0].astype(BF16))
    for h in range(M_HEADS):
        sl = slice(h * HEAD_DIM, (h + 1) * HEAD_DIM)
        out_ref[0, :, B_WIDTH + h * HEAD_DIM:B_WIDTH + (h + 1) * HEAD_DIM] = mem[h] * gate_m[:, sl]


def _attn_sample(att, cache_k, cache_v, mem_k, mem_v, gtab):
    b, s, _ = att.shape
    per_b = lambda arr: pl.BlockSpec((1,) + arr.shape[1:], lambda i: (i, 0, 0))
    return pl.pallas_call(
        _attn_sample_kernel,
        name="attn_sample",
        grid=(b,),
        in_specs=[per_b(att), per_b(cache_k), per_b(cache_v), per_b(mem_k), per_b(mem_v),
                  pl.BlockSpec(gtab.shape, lambda i: (0, 0))],
        out_specs=pl.BlockSpec((1, s, 2 * B_WIDTH), lambda i: (i, 0, 0)),
        out_shape=jax.ShapeDtypeStruct((b, s, 2 * B_WIDTH), F32),
        scratch_shapes=[pltpu.VMEM((B_HEADS, s, cache_k.shape[1] + s), F32)],
        compiler_params=pltpu.CompilerParams(dimension_semantics=("arbitrary",)),
    )(att, cache_k, cache_v, mem_k, mem_v, gtab)


def _finish_kernel(x_ref, mr_ref, mbm_ref, w_ref, g_ref, b_ref, y_ref):
    o = _dot(mr_ref[...].astype(BF16), w_ref[0:R_WIDTH, :])
    o = o + _dot(mbm_ref[...].astype(BF16), w_ref[R_WIDTH:, :])
    hres = ALPHA * x_ref[...] + o
    mean = jnp.mean(hres, axis=-1, keepdims=True)
    hc = hres - mean
    var = jnp.mean(hc * hc, axis=-1, keepdims=True)
    y_ref[...] = hc * lax.rsqrt(var + LN_EPS) * g_ref[...] + b_ref[...]


def _finish(x, mix_r, mix_bm, w_out_bf16, ln_g, ln_b, tm):
    m = x.shape[0]
    rows = lambda width: pl.BlockSpec((tm, width), lambda i: (i, 0))
    full = lambda arr: pl.BlockSpec(arr.shape, lambda i: (0, 0))
    return pl.pallas_call(
        _finish_kernel,
        name=f"finish_{m}",
        grid=(m // tm,),
        in_specs=[rows(D_MODEL), rows(R_WIDTH), rows(2 * B_WIDTH), full(w_out_bf16), full(ln_g), full(ln_b)],
        out_specs=rows(D_MODEL),
        out_shape=jax.ShapeDtypeStruct((m, D_MODEL), F32),
        compiler_params=pltpu.CompilerParams(
            dimension_semantics=("arbitrary",), vmem_limit_bytes=VMEM_LIMIT),
    )(x, mix_r, mix_bm, w_out_bf16, ln_g, ln_b)


IN_SPLITS = ((0, SHIFT_WIDTH), (SHIFT_WIDTH, SHIFT_WIDTH + R_WIDTH), (SHIFT_WIDTH + R_WIDTH, IN_WIDTH))


def _rel_bias_row(table, rel0):
    n_hi = rel0 - REL_CLIP
    n_lo = BIAS_L - n_hi - (2 * REL_CLIP + 1)
    heads = table.shape[0]
    return jnp.concatenate([jnp.broadcast_to(table[:, 2 * REL_CLIP:], (heads, n_hi)), table[:, ::-1],
                            jnp.broadcast_to(table[:, 0:1], (heads, n_lo))], axis=1)


def kernel(x_prompt, x_sample, mem_prompt, state_shift, state_wkv, cache_band_k, cache_band_v,
           cache_mem_k, cache_mem_v, w_in, mu_shift, w0, w2, a0, a2, k_k, k_a, r_k, gn_g, gn_b,
           rel_bias, w_mem_kv, w_out, ln_g, ln_b):
    bp, t, _ = x_prompt.shape
    bs, s, _ = x_sample.shape
    depth = w_in.shape[0]
    assert depth == 1 and t % BAND_WINDOW == 0 and s <= CHUNK
    keep = min(BAND_WINDOW, t)
    l = 0

    w_in_b = w_in[l].astype(BF16)
    w_out_b = w_out[l].astype(BF16)
    w_mem_b = w_mem_kv[l].astype(BF16)
    row = lambda p: p.reshape(1, -1)
    wkv_params = (row(mu_shift[l]), row(w0[l]), w2[l], row(a0[l]), a2[l], row(k_k[l]), row(k_a[l]),
                  row(r_k[l]), row(gn_g[l]), row(gn_b[l]))
    table = rel_bias[l]
    r_rows = cache_band_k.shape[2]
    gtab_p = _rel_bias_row(table, BAND_WINDOW + CHUNK - 1)
    gtab_s = _rel_bias_row(table, r_rows + s - 1)

    xp = x_prompt.reshape(bp * t, D_MODEL)
    zs, gr, att = _proj(xp, w_in_b, IN_SPLITS, 256)
    zs = zs.reshape(bp, t, SHIFT_WIDTH)
    att = att.reshape(bp, t, ATT_WIDTH)
    memkv, = _proj(mem_prompt.reshape(bp * N_MEM, D_MODEL), w_mem_b, ((0, 2 * B_WIDTH),), 256)
    memkv = memkv.reshape(bp, N_MEM, 2 * B_WIDTH)
    mix_r, p_wkv = _wkv(zs, gr.reshape(bp, t, R_WIDTH), jnp.zeros((bp, 1, SHIFT_WIDTH), F32),
                        jnp.zeros((bp, R_HEADS, HEAD_DIM, HEAD_DIM), F32), wkv_params, bb_n=2, c=CHUNK)
    mix_bm = _attn_prompt(att, memkv, gtab_p, tq=BAND_WINDOW)
    y_prompt = _finish(xp, mix_r.reshape(bp * t, R_WIDTH), mix_bm.reshape(bp * t, 2 * B_WIDTH),
                       w_out_b, row(ln_g[l]), row(ln_b[l]), 512).reshape(bp, t, D_MODEL)
    p_shift = zs[:, -1]
    p_bk = att[:, t - keep:, B_WIDTH:2 * B_WIDTH].reshape(bp, keep, B_HEADS, HEAD_DIM)
    p_bv = att[:, t - keep:, 2 * B_WIDTH:3 * B_WIDTH].reshape(bp, keep, B_HEADS, HEAD_DIM)
    p_mk = memkv[:, :, :B_WIDTH].reshape(bp, N_MEM, M_HEADS, HEAD_DIM)
    p_mv = memkv[:, :, B_WIDTH:].reshape(bp, N_MEM, M_HEADS, HEAD_DIM)

    xs = x_sample.reshape(bs * s, D_MODEL)
    zs_s, gr_s, att_s = _proj(xs, w_in_b, IN_SPLITS, bs * s)
    zs_s = zs_s.reshape(bs, s, SHIFT_WIDTH)
    att_s = att_s.reshape(bs, s, ATT_WIDTH)
    mix_r_s, s_wkv = _wkv(zs_s, gr_s.reshape(bs, s, R_WIDTH), state_shift[l][:, None, :], state_wkv[l],
                          wkv_params, bb_n=2, c=s)
    mix_bm_s = _attn_sample(att_s,
                            cache_band_k[l].reshape(bs, r_rows, B_WIDTH),
                            cache_band_v[l].reshape(bs, r_rows, B_WIDTH),
                            cache_mem_k[l].reshape(bs, N_MEM, B_WIDTH),
                            cache_mem_v[l].reshape(bs, N_MEM, B_WIDTH), gtab_s)
    y_sample = _finish(xs, mix_r_s.reshape(bs * s, R_WIDTH), mix_bm_s.reshape(bs * s, 2 * B_WIDTH),
                       w_out_b, row(ln_g[l]), row(ln_b[l]), bs * s).reshape(bs, s, D_MODEL)
    s_shift = zs_s[:, -1]
    s_bk = att_s[:, :, B_WIDTH:2 * B_WIDTH].reshape(bs, s, B_HEADS, HEAD_DIM)
    s_bv = att_s[:, :, 2 * B_WIDTH:3 * B_WIDTH].reshape(bs, s, B_HEADS, HEAD_DIM)

    st = lambda a: a[None]
    return (y_prompt, y_sample, st(p_shift), st(p_wkv), st(p_bk), st(p_bv), st(p_mk), st(p_mv),
            st(s_shift), st(s_wkv), st(s_bk), st(s_bv))
```

```python
import functools

import numpy as np
import jax
import jax.numpy as jnp
from jax import lax
from jax.experimental import pallas as pl
from jax.experimental.pallas import tpu as pltpu

F32 = jnp.float32
BF16 = jnp.bfloat16

D_MODEL = 1024
HEAD_DIM = 64
R_WIDTH = 512
R_HEADS = 8
LOW_RANK = 64
SHIFT_WIDTH = 3 * R_WIDTH + 2 * LOW_RANK
B_WIDTH = 256
B_HEADS = 4
M_HEADS = 4
N_MEM = 256
CHUNK = 64
BAND_CHUNKS = 8
BAND_WINDOW = BAND_CHUNKS * CHUNK
BAND_LEN = BAND_WINDOW + CHUNK
REL_CLIP = 128
ATT_WIDTH = 6 * B_WIDTH
IN_WIDTH = SHIFT_WIDTH + R_WIDTH + ATT_WIDTH
LN_EPS = 1e-5
GN_EPS = 64e-5
ALPHA = 2.0 ** 0.25
ATT_SCALE = HEAD_DIM ** -0.5
BIAS_L = 1024
PAIR = 2 * HEAD_DIM
ATTN_GROUP_CHUNKS = 2
MEM_ROWS = 128
MEM_GROUP_BLOCKS = 2

VMEM_LIMIT = 48 * 1024 * 1024

NN = ((1,), (0,))
NT = ((1,), (1,))
TN = ((0,), (0,))


def _dot(a, b, dims=NN):
    return lax.dot_general(a, b, (dims, ((), ())), preferred_element_type=F32)


def _split2(x):
    hi = x.astype(BF16)
    lo = (x - hi.astype(F32)).astype(BF16)
    return hi, lo


def _mm3(a, b, dims=NN):
    ah, al = _split2(a)
    bh, bl = _split2(b)
    return _dot(ah, bh, dims) + _dot(ah, bl, dims) + _dot(al, bh, dims)


def _mm_exact_lhs(lhs_bf16, x):
    x1 = x.astype(BF16)
    r1 = x - x1.astype(F32)
    x2 = r1.astype(BF16)
    x3 = (r1 - x2.astype(F32)).astype(BF16)
    return _dot(lhs_bf16, x1) + _dot(lhs_bf16, x2) + _dot(lhs_bf16, x3)


def _sigmoid(x):
    return 1.0 / (1.0 + jnp.exp(-x))


def _silu(x):
    return x * _sigmoid(x)


def _proj_kernel(x_ref, w_ref, *out_refs, splits):
    x = x_ref[...].astype(BF16)
    for o_ref, (lo, hi) in zip(out_refs, splits):
        o_ref[...] = _dot(x, w_ref[:, lo:hi])


def _proj(x, w_bf16, splits, tm):
    m, k = x.shape
    n = w_bf16.shape[1]
    return pl.pallas_call(
        functools.partial(_proj_kernel, splits=splits),
        name=f"proj_{m}x{n}",
        grid=(m // tm,),
        in_specs=[pl.BlockSpec((tm, k), lambda i: (i, 0)),
                  pl.BlockSpec((k, n), lambda i: (0, 0))],
        out_specs=[pl.BlockSpec((tm, hi - lo), lambda i: (i, 0)) for lo, hi in splits],
        out_shape=[jax.ShapeDtypeStruct((m, hi - lo), F32) for lo, hi in splits],
        compiler_params=pltpu.CompilerParams(
            dimension_semantics=("arbitrary",), vmem_limit_bytes=VMEM_LIMIT),
    )(x, w_bf16)


def _pair_blocks(x2, left):
    zero = jnp.zeros((), x2.dtype)
    return jnp.concatenate([jnp.where(left, x2, zero), jnp.where(left, zero, x2)], axis=0)


def _pair_sum(x2, left):
    s0 = jnp.sum(jnp.where(left, x2, 0.0), axis=-1, keepdims=True)
    s1 = jnp.sum(jnp.where(left, 0.0, x2), axis=-1, keepdims=True)
    return jnp.where(left, s0, s1)


def _wkv_kernel(zs_ref, gr_ref, prev_ref, s0_ref, mu_ref, w0_ref, w2_ref, a0_ref, a2_ref,
                kk_ref, ka_ref, rk_ref, gng_ref, gnb_ref, out_ref, state_ref, carry_ref, *, bb_n, c):
    t = pl.program_id(1)

    @pl.when(t == 0)
    def _():
        carry_ref[...] = prev_ref[...]
        state_ref[...] = s0_ref[...]

    row = lax.broadcasted_iota(jnp.int32, (c, c), 0)
    col = lax.broadcasted_iota(jnp.int32, (c, c), 1)
    ltri = (col <= row).astype(BF16)
    row2 = lax.broadcasted_iota(jnp.int32, (c, 2 * c), 0)
    col2 = lax.broadcasted_iota(jnp.int32, (c, 2 * c), 1) & (c - 1)
    eye2 = (row2 == col2).astype(BF16)
    strict2 = col2 < row2
    incl4 = ((lax.broadcasted_iota(jnp.int32, (c, 4 * c), 1) & (c - 1))
             <= lax.broadcasted_iota(jnp.int32, (c, 4 * c), 0))
    level_masks = []
    shift = 0
    while (1 << shift) < c:
        rb = row2 >> shift
        level_masks.append(((rb & 1) == 1) & ((col2 >> shift) == rb - 1))
        shift += 1
    left_s = lax.broadcasted_iota(jnp.int32, (1, 2 * c), 1) < c
    left = lax.broadcasted_iota(jnp.int32, (1, PAIR), 1) < HEAD_DIM
    first_row = lax.broadcasted_iota(jnp.int32, (c, 1), 0) == 0
    zero = jnp.zeros((), BF16)

    mu = mu_ref[...]
    units = []
    for bb in range(bb_n):
        zs = zs_ref[bb]
        zprev = jnp.where(first_row, carry_ref[bb], pltpu.roll(zs, 1, 0))
        carry_ref[bb] = zs[c - 1:c, :]
        xs = zs + (zprev - zs) * mu
        r = xs[:, 0:R_WIDTH]
        k = xs[:, R_WIDTH:2 * R_WIDTH]
        v = xs[:, 2 * R_WIDTH:3 * R_WIDTH]
        wd = xs[:, 3 * R_WIDTH:3 * R_WIDTH + LOW_RANK]
        ad = xs[:, 3 * R_WIDTH + LOW_RANK:]

        u = -(w0_ref[...] + _mm3(jnp.tanh(wd), w2_ref[...]))
        softplus = jnp.maximum(u, 0.0) + jnp.log(1.0 + jnp.exp(-jnp.abs(u)))
        ld = -jnp.exp(-softplus - 0.5)
        cum = _mm_exact_lhs(ltri, ld)
        a = _sigmoid(a0_ref[...] + _mm3(ad, a2_ref[...]))

        ecum = jnp.exp(cum)
        einv = jnp.exp(-cum)
        eprev = jnp.exp(cum - ld)
        pc = ecum[c - 1:c, :]
        kkr = k * kk_ref[...]
        k2 = k * (1.0 + (a - 1.0) * ka_ref[...])
        rt = r * ecum
        kh = k2 * einv
        kp = kh * pc
        rk2 = r * k2 * rk_ref[...]
        gate = _silu(gr_ref[bb])

        for pr in range(R_HEADS // 2):
            sl = slice(pr * PAIR, (pr + 1) * PAIR)
            kkr_p = kkr[:, sl]
            nrm = jnp.sqrt(_pair_sum(kkr_p * kkr_p, left))
            kkn = kkr_p / jnp.maximum(nrm, 1e-12)
            at = -kkn * eprev[:, sl]
            bh = kkn * a[:, sl] * einv[:, sl]
            v_p = v[:, sl]
            v16 = v_p.astype(BF16)
            units.append(dict(
                bb=bb, pr=pr, sl=sl, pc=pc[:, sl], gate=gate[:, sl], v_bd=_pair_blocks(v16, left), v16=v16,
                ar=jnp.concatenate([at, rt[:, sl]], axis=0).astype(BF16),
                bk_bd=jnp.concatenate([_pair_blocks(bh.astype(BF16), left),
                                       _pair_blocks(kh[:, sl].astype(BF16), left)], axis=0),
                bkp=jnp.concatenate([bh * pc[:, sl], kp[:, sl]], axis=0).astype(BF16),
                bonus=_pair_sum(rk2[:, sl], left) * v_p))

    for un in units:
        un["a4"] = _dot(un["ar"], un["bk_bd"], NT).astype(BF16)
        un["a_ab"] = un["a4"][:c, :2 * c]
        un["tinv"] = jnp.where(level_masks[0], un["a_ab"], eye2)
    for mask in level_masks[1:]:
        for un in units:
            un["lt"] = _dot(jnp.where(mask, un["a_ab"], zero), _pair_blocks(un["tinv"], left_s)).astype(BF16)
        for un in units:
            new = _dot(un["tinv"], _pair_blocks(un["lt"], left_s)).astype(BF16)
            un["tinv"] = jnp.where(mask, new, un["tinv"])
    for un in units:
        un["akv"] = _dot(jnp.where(strict2, un["a4"][:c, 2 * c:], zero), un["v_bd"])
    for un in units:
        un["s0"] = state_ref[un["bb"], un["pr"]]
        un["ars"] = _dot(un["ar"], _pair_blocks(un["s0"].astype(BF16), left), NT)
    for un in units:
        rhs = (un["ars"][:c] + un["akv"]).astype(BF16)
        un["pm"] = _dot(un["tinv"], _pair_blocks(rhs, left)).astype(BF16)
    for un in units:
        pv = jnp.concatenate([un["pm"], un["v16"]], axis=0)
        cross = _dot(pv, un["bkp"], TN)
        state_ref[un["bb"], un["pr"]] = un["s0"] * un["pc"] + jnp.where(left, cross[:HEAD_DIM], cross[HEAD_DIM:])
    for un in units:
        pv_bd = jnp.concatenate([_pair_blocks(un["pm"], left), un["v_bd"]], axis=0)
        y = un["ars"][c:] + _dot(jnp.where(incl4, un["a4"][c:], zero), pv_bd)
        mean = _pair_sum(y, left) * (1.0 / HEAD_DIM)
        yc = y - mean
        var = _pair_sum(yc * yc, left) * (1.0 / HEAD_DIM)
        yn = yc * lax.rsqrt(var + GN_EPS) * gng_ref[:, un["sl"]] + gnb_ref[:, un["sl"]]
        out_ref[un["bb"], :, un["sl"]] = (yn + un["bonus"]) * un["gate"]


def _pair_state(s):
    b = s.shape[0]
    s = s.reshape(b, R_HEADS // 2, 2, HEAD_DIM, HEAD_DIM)
    return jnp.swapaxes(s, 2, 3).reshape(b, R_HEADS // 2, HEAD_DIM, PAIR)


def _unpair_state(s):
    b = s.shape[0]
    s = s.reshape(b, R_HEADS // 2, HEAD_DIM, 2, HEAD_DIM)
    return jnp.swapaxes(s, 2, 3).reshape(b, R_HEADS, HEAD_DIM, HEAD_DIM)


def _wkv(zs, gr, prev, s0, params, *, bb_n, c):
    b, t, _ = zs.shape
    full = lambda arr: pl.BlockSpec(arr.shape, lambda i, j: (0,) * arr.ndim)
    return pl.pallas_call(
        functools.partial(_wkv_kernel, bb_n=bb_n, c=c),
        name=f"wkv_c{c}",
        grid=(b // bb_n, t // c),
        in_specs=[pl.BlockSpec((bb_n, c, SHIFT_WIDTH), lambda i, j: (i, j, 0)),
                  pl.BlockSpec((bb_n, c, R_WIDTH), lambda i, j: (i, j, 0)),
                  pl.BlockSpec((bb_n, 1, SHIFT_WIDTH), lambda i, j: (i, 0, 0)),
                  pl.BlockSpec((bb_n, R_HEADS // 2, HEAD_DIM, PAIR), lambda i, j: (i, 0, 0, 0))]
                 + [full(p) for p in params],
        out_specs=[pl.BlockSpec((bb_n, c, R_WIDTH), lambda i, j: (i, j, 0)),
                   pl.BlockSpec((bb_n, R_HEADS // 2, HEAD_DIM, PAIR), lambda i, j: (i, 0, 0, 0))],
        out_shape=[jax.ShapeDtypeStruct((b, t, R_WIDTH), F32),
                   jax.ShapeDtypeStruct((b, R_HEADS // 2, HEAD_DIM, PAIR), F32)],
        scratch_shapes=[pltpu.VMEM((bb_n, 1, SHIFT_WIDTH), F32)],
        compiler_params=pltpu.CompilerParams(
            dimension_semantics=("arbitrary", "arbitrary"), vmem_limit_bytes=VMEM_LIMIT),
    )(zs, gr, prev, _pair_state(s0), *params)


def _fill_rel_bias(bias_ref, gtab_ref, offset):
    heads, nq, nk = bias_ref.shape
    for h in range(heads):
        g = jnp.broadcast_to(gtab_ref[h:h + 1, :], (nq, BIAS_L))
        bias_ref[h] = pltpu.roll(g, BIAS_L - offset, 1, stride=1, stride_axis=0)[:, :nk]


def _softmax_pv(s, v_bf16):
    m = jnp.max(s, axis=-1, keepdims=True)
    p = jnp.exp(s - m)
    l = jnp.sum(p, axis=-1, keepdims=True)
    return _dot(p.astype(BF16), v_bf16) / l


def _mem_heads(q_bf16, mk_bf16, mv_bf16):
    outs = []
    for h in range(M_HEADS):
        sl = slice(h * HEAD_DIM, (h + 1) * HEAD_DIM)
        s = _dot(q_bf16[:, sl], mk_bf16[:, sl], NT)
        outs.append(_softmax_pv(s, mv_bf16[:, sl]))
    return outs


def _stack_heads(q2, left):
    zero = jnp.zeros((), q2.dtype)
    return jnp.concatenate([jnp.where(left, q2, zero), jnp.where(left, zero, q2)], axis=0)


def _attend_pairs(jobs, left):
    for jb in jobs:
        s = _dot(jb["lhs"], jb["k"], NT)
        if jb.get("bias") is not None:
            s = s + jb["bias"]
        if jb.get("visible") is not None:
            s = jnp.where(jb["visible"], s, -jnp.inf)
        jb["s"] = s
    for jb in jobs:
        m = jnp.max(jb["s"], axis=-1, keepdims=True)
        p = jnp.exp(jb["s"] - m)
        jb["l"] = jnp.sum(p, axis=-1, keepdims=True)
        jb["p"] = p.astype(BF16)
    outs = []
    for jb in jobs:
        o2 = _dot(jb["p"], jb["v"]) / jb["l"]
        n = o2.shape[0] // 2
        outs.append(jnp.where(left, o2[:n], o2[n:]))
    return outs


def _attn_prompt_kernel(q_ref, kp_ref, kc_ref, vp_ref, vc_ref, gb_ref, mq_ref, gm_ref,
                        mk_ref, mv_ref, gtab_ref, out_ref, kcat_ref, vcat_ref, bias_ref, bias2_ref, *, tq):
    j = pl.program_id(1)

    @pl.when((pl.program_id(0) == 0) & (j == 0))
    def _():
        _fill_rel_bias(bias_ref, gtab_ref, CHUNK - 1)
        for h in range(B_HEADS):
            bias2_ref[h // 2, (h % 2) * CHUNK:(h % 2 + 1) * CHUNK, :] = bias_ref[h]

    kcat_ref[0:BAND_WINDOW] = kp_ref[0].astype(BF16)
    kcat_ref[BAND_WINDOW:] = kc_ref[0].astype(BF16)
    vcat_ref[0:BAND_WINDOW] = vp_ref[0].astype(BF16)
    vcat_ref[BAND_WINDOW:] = vc_ref[0].astype(BF16)
    q = (q_ref[0] * ATT_SCALE).astype(BF16)
    gate_b = _silu(gb_ref[0])
    left = lax.broadcasted_iota(jnp.int32, (1, PAIR), 1) < HEAD_DIM
    kcol = lax.broadcasted_iota(jnp.int32, (1, BAND_LEN), 1)
    n_chunks = tq // CHUNK
    for i0 in range(0, n_chunks, ATTN_GROUP_CHUNKS):
        jobs = []
        for i in range(i0, i0 + ATTN_GROUP_CHUNKS):
            rows = slice(i * CHUNK, (i + 1) * CHUNK)
            keys = slice(i * CHUNK, i * CHUNK + BAND_LEN)
            visible = kcol >= jnp.where(j == 0, BAND_WINDOW - i * CHUNK, 0)
            for pr in range(B_HEADS // 2):
                lanes = slice(pr * PAIR, (pr + 1) * PAIR)
                jobs.append(dict(rows=rows, lanes=lanes, lhs=_stack_heads(q[rows, lanes], left),
                                 k=kcat_ref[keys, lanes], v=vcat_ref[keys, lanes],
                                 bias=bias2_ref[pr], visible=visible))
        for jb, o in zip(jobs, _attend_pairs(jobs, left)):
            out_ref[0, jb["rows"], jb["lanes"]] = o * gate_b[jb["rows"], jb["lanes"]]

    mq = (mq_ref[0] * ATT_SCALE).astype(BF16)
    gate_m = _silu(gm_ref[0])
    mk = mk_ref[0].astype(BF16)
    mv = mv_ref[0].astype(BF16)
    for r0 in range(0, tq, MEM_ROWS * MEM_GROUP_BLOCKS):
        jobs = []
        for r in range(r0, r0 + MEM_ROWS * MEM_GROUP_BLOCKS, MEM_ROWS):
            rows = slice(r, r + MEM_ROWS)
            for pr in range(M_HEADS // 2):
                lanes = slice(pr * PAIR, (pr + 1) * PAIR)
                jobs.append(dict(rows=rows, lanes=lanes, lhs=_stack_heads(mq[rows, lanes], left),
                                 k=mk[:, lanes], v=mv[:, lanes]))
        for jb, o in zip(jobs, _attend_pairs(jobs, left)):
            out_ref[0, jb["rows"], B_WIDTH + jb["lanes"].start:B_WIDTH + jb["lanes"].stop] = (
                o * gate_m[jb["rows"], jb["lanes"]])


def _attn_prompt(att, memkv, gtab, *, tq):
    b, t, _ = att.shape
    col = lambda cidx: pl.BlockSpec((1, tq, B_WIDTH), lambda i, j: (i, j, cidx))
    prev = lambda cidx: pl.BlockSpec((1, tq, B_WIDTH), lambda i, j: (i, jnp.maximum(j - 1, 0), cidx))
    return pl.pallas_call(
        functools.partial(_attn_prompt_kernel, tq=tq),
        name="attn_prompt",
        grid=(b, t // tq),
        in_specs=[col(0), prev(1), col(1), prev(2), col(2), col(3), col(4), col(5),
                  pl.BlockSpec((1, N_MEM, B_WIDTH), lambda i, j: (i, 0, 0)),
                  pl.BlockSpec((1, N_MEM, B_WIDTH), lambda i, j: (i, 0, 1)),
                  pl.BlockSpec(gtab.shape, lambda i, j: (0, 0))],
        out_specs=pl.BlockSpec((1, tq, 2 * B_WIDTH), lambda i, j: (i, j, 0)),
        out_shape=jax.ShapeDtypeStruct((b, t, 2 * B_WIDTH), F32),
        scratch_shapes=[pltpu.VMEM((2 * BAND_WINDOW, B_WIDTH), BF16),
                        pltpu.VMEM((2 * BAND_WINDOW, B_WIDTH), BF16),
                        pltpu.VMEM((B_HEADS, CHUNK, BAND_LEN), F32),
                        pltpu.VMEM((B_HEADS // 2, 2 * CHUNK, BAND_LEN), F32)],
        compiler_params=pltpu.CompilerParams(
            dimension_semantics=("arbitrary", "arbitrary"), vmem_limit_bytes=VMEM_LIMIT),
    )(att, att, att, att, att, att, att, att, memkv, memkv, gtab)


def _attn_sample_kernel(att_ref, ck_ref, cv_ref, mk_ref, mv_ref, gtab_ref, out_ref, bias_ref):
    n_new = att_ref.shape[1]
    n_cached = ck_ref.shape[1]

    @pl.when(pl.program_id(0) == 0)
    def _():
        _fill_rel_bias(bias_ref, gtab_ref, n_new - 1)

    att = att_ref[0]
    q = (att[:, 0:B_WIDTH] * ATT_SCALE).astype(BF16)
    kn = att[:, B_WIDTH:2 * B_WIDTH].astype(BF16)
    vn = att[:, 2 * B_WIDTH:3 * B_WIDTH].astype(BF16)
    gate_b = _silu(att[:, 3 * B_WIDTH:4 * B_WIDTH])
    mq = (att[:, 4 * B_WIDTH:5 * B_WIDTH] * ATT_SCALE).astype(BF16)
    gate_m = _silu(att[:, 5 * B_WIDTH:6 * B_WIDTH])
    kc = ck_ref[0].astype(BF16)
    vc = cv_ref[0].astype(BF16)
    for h in range(B_HEADS):
        sl = slice(h * HEAD_DIM, (h + 1) * HEAD_DIM)
        s_c = _dot(q[:, sl], kc[:, sl], NT) + bias_ref[h, :, 0:n_cached]
        s_n = _dot(q[:, sl], kn[:, sl], NT) + bias_ref[h, :, n_cached:]
        m = jnp.maximum(jnp.max(s_c, axis=-1, keepdims=True), jnp.max(s_n, axis=-1, keepdims=True))
        p_c = jnp.exp(s_c - m)
        p_n = jnp.exp(s_n - m)
        l = jnp.sum(p_c, axis=-1, keepdims=True) + jnp.sum(p_n, axis=-1, keepdims=True)
        o = (_dot(p_c.astype(BF16), vc[:, sl]) + _dot(p_n.astype(BF16), vn[:, sl])) / l
        out_ref[0, :, sl] = o * gate_b[:, sl]
    mem = _mem_heads(mq, mk_ref[0].astype(BF16), mv_ref[0].astype(BF16))
    for h in range(M_HEADS):
        sl = slice(h * HEAD_DIM, (h + 1) * HEAD_DIM)
        out_ref[0, :, B_WIDTH + h * HEAD_DIM:B_WIDTH + (h + 1) * HEAD_DIM] = mem[h] * gate_m[:, sl]


def _attn_sample(att, cache_k, cache_v, mem_k, mem_v, gtab):
    b, s, _ = att.shape
    per_b = lambda arr: pl.BlockSpec((1,) + arr.shape[1:], lambda i: (i, 0, 0))
    return pl.pallas_call(
        _attn_sample_kernel,
        name="attn_sample",
        grid=(b,),
        in_specs=[per_b(att), per_b(cache_k), per_b(cache_v), per_b(mem_k), per_b(mem_v),
                  pl.BlockSpec(gtab.shape, lambda i: (0, 0))],
        out_specs=pl.BlockSpec((1, s, 2 * B_WIDTH), lambda i: (i, 0, 0)),
        out_shape=jax.ShapeDtypeStruct((b, s, 2 * B_WIDTH), F32),
        scratch_shapes=[pltpu.VMEM((B_HEADS, s, cache_k.shape[1] + s), F32)],
        compiler_params=pltpu.CompilerParams(dimension_semantics=("arbitrary",)),
    )(att, cache_k, cache_v, mem_k, mem_v, gtab)


def _finish_kernel(x_ref, mr_ref, mbm_ref, w_ref, g_ref, b_ref, y_ref):
    o = _dot(mr_ref[...].astype(BF16), w_ref[0:R_WIDTH, :])
    o = o + _dot(mbm_ref[...].astype(BF16), w_ref[R_WIDTH:, :])
    hres = ALPHA * x_ref[...] + o
    mean = jnp.mean(hres, axis=-1, keepdims=True)
    hc = hres - mean
    var = jnp.mean(hc * hc, axis=-1, keepdims=True)
    y_ref[...] = hc * lax.rsqrt(var + LN_EPS) * g_ref[...] + b_ref[...]


def _finish(x, mix_r, mix_bm, w_out_bf16, ln_g, ln_b, tm):
    m = x.shape[0]
    rows = lambda width: pl.BlockSpec((tm, width), lambda i: (i, 0))
    full = lambda arr: pl.BlockSpec(arr.shape, lambda i: (0, 0))
    return pl.pallas_call(
        _finish_kernel,
        name=f"finish_{m}",
        grid=(m // tm,),
        in_specs=[rows(D_MODEL), rows(R_WIDTH), rows(2 * B_WIDTH), full(w_out_bf16), full(ln_g), full(ln_b)],
        out_specs=rows(D_MODEL),
        out_shape=jax.ShapeDtypeStruct((m, D_MODEL), F32),
        compiler_params=pltpu.CompilerParams(
            dimension_semantics=("arbitrary",), vmem_limit_bytes=VMEM_LIMIT),
    )(x, mix_r, mix_bm, w_out_bf16, ln_g, ln_b)


IN_SPLITS = ((0, SHIFT_WIDTH), (SHIFT_WIDTH, SHIFT_WIDTH + R_WIDTH), (SHIFT_WIDTH + R_WIDTH, IN_WIDTH))


def _rel_bias_row(table, rel0):
    n_hi = rel0 - REL_CLIP
    n_lo = BIAS_L - n_hi - (2 * REL_CLIP + 1)
    heads = table.shape[0]
    return jnp.concatenate([jnp.broadcast_to(table[:, 2 * REL_CLIP:], (heads, n_hi)), table[:, ::-1],
                            jnp.broadcast_to(table[:, 0:1], (heads, n_lo))], axis=1)


def kernel(x_prompt, x_sample, mem_prompt, state_shift, state_wkv, cache_band_k, cache_band_v,
           cache_mem_k, cache_mem_v, w_in, mu_shift, w0, w2, a0, a2, k_k, k_a, r_k, gn_g, gn_b,
           rel_bias, w_mem_kv, w_out, ln_g, ln_b):
    bp, t, _ = x_prompt.shape
    bs, s, _ = x_sample.shape
    depth = w_in.shape[0]
    assert depth == 1 and t % BAND_WINDOW == 0 and s <= CHUNK
    keep = min(BAND_WINDOW, t)
    l = 0

    w_in_b = w_in[l].astype(BF16)
    w_out_b = w_out[l].astype(BF16)
    w_mem_b = w_mem_kv[l].astype(BF16)
    row = lambda p: p.reshape(1, -1)
    wkv_params = (row(mu_shift[l]), row(w0[l]), w2[l], row(a0[l]), a2[l], row(k_k[l]), row(k_a[l]),
                  row(r_k[l]), row(gn_g[l]), row(gn_b[l]))
    table = rel_bias[l]
    r_rows = cache_band_k.shape[2]
    gtab_p = _rel_bias_row(table, BAND_WINDOW + CHUNK - 1)
    gtab_s = _rel_bias_row(table, r_rows + s - 1)

    xp = x_prompt.reshape(bp * t, D_MODEL)
    zs, gr, att = _proj(xp, w_in_b, IN_SPLITS, 256)
    zs = zs.reshape(bp, t, SHIFT_WIDTH)
    att = att.reshape(bp, t, ATT_WIDTH)
    memkv, = _proj(mem_prompt.reshape(bp * N_MEM, D_MODEL), w_mem_b, ((0, 2 * B_WIDTH),), 256)
    memkv = memkv.reshape(bp, N_MEM, 2 * B_WIDTH)
    mix_r, p_wkv = _wkv(zs, gr.reshape(bp, t, R_WIDTH), jnp.zeros((bp, 1, SHIFT_WIDTH), F32),
                        jnp.zeros((bp, R_HEADS, HEAD_DIM, HEAD_DIM), F32), wkv_params, bb_n=2, c=CHUNK)
    mix_bm = _attn_prompt(att, memkv, gtab_p, tq=BAND_WINDOW)
    y_prompt = _finish(xp, mix_r.reshape(bp * t, R_WIDTH), mix_bm.reshape(bp * t, 2 * B_WIDTH),
                       w_out_b, row(ln_g[l]), row(ln_b[l]), 512).reshape(bp, t, D_MODEL)
    p_shift = zs[:, -1]
    p_bk = att[:, t - keep:, B_WIDTH:2 * B_WIDTH].reshape(bp, keep, B_HEADS, HEAD_DIM)
    p_bv = att[:, t - keep:, 2 * B_WIDTH:3 * B_WIDTH].reshape(bp, keep, B_HEADS, HEAD_DIM)
    p_mk = memkv[:, :, :B_WIDTH].reshape(bp, N_MEM, M_HEADS, HEAD_DIM)
    p_mv = memkv[:, :, B_WIDTH:].reshape(bp, N_MEM, M_HEADS, HEAD_DIM)

    xs = x_sample.reshape(bs * s, D_MODEL)
    zs_s, gr_s, att_s = _proj(xs, w_in_b, IN_SPLITS, bs * s)
    zs_s = zs_s.reshape(bs, s, SHIFT_WIDTH)
    att_s = att_s.reshape(bs, s, ATT_WIDTH)
    mix_r_s, s_wkv = _wkv(zs_s, gr_s.reshape(bs, s, R_WIDTH), state_shift[l][:, None, :], state_wkv[l],
                          wkv_params, bb_n=2, c=s)
    mix_bm_s = _attn_sample(att_s,
                            cache_band_k[l].reshape(bs, r_rows, B_WIDTH),
                            cache_band_v[l].reshape(bs, r_rows, B_WIDTH),
                            cache_mem_k[l].reshape(bs, N_MEM, B_WIDTH),
                            cache_mem_v[l].reshape(bs, N_MEM, B_WIDTH), gtab_s)
    y_sample = _finish(xs, mix_r_s.reshape(bs * s, R_WIDTH), mix_bm_s.reshape(bs * s, 2 * B_WIDTH),
                       w_out_b, row(ln_g[l]), row(ln_b[l]), bs * s).reshape(bs, s, D_MODEL)
    s_shift = zs_s[:, -1]
    s_bk = att_s[:, :, B_WIDTH:2 * B_WIDTH].reshape(bs, s, B_HEADS, HEAD_DIM)
    s_bv = att_s[:, :, 2 * B_WIDTH:3 * B_WIDTH].reshape(bs, s, B_HEADS, HEAD_DIM)

    st = lambda a: a[None]
    return (y_prompt, y_sample, st(p_shift), st(_unpair_state(p_wkv)), st(p_bk), st(p_bv), st(p_mk), st(p_mv),
            st(s_shift), st(_unpair_state(s_wkv)), st(s_bk), st(s_bv))
```

```python
import functools

import numpy as np
import jax
import jax.numpy as jnp
from jax import lax
from jax.experimental import pallas as pl
from jax.experimental.pallas import tpu as pltpu

F32 = jnp.float32
BF16 = jnp.bfloat16

D_MODEL = 1024
HEAD_DIM = 64
R_WIDTH = 512
R_HEADS = 8
LOW_RANK = 64
SHIFT_WIDTH = 3 * R_WIDTH + 2 * LOW_RANK
B_WIDTH = 256
B_HEADS = 4
M_HEADS = 4
N_MEM = 256
CHUNK = 64
BAND_CHUNKS = 8
BAND_WINDOW = BAND_CHUNKS * CHUNK
BAND_LEN = BAND_WINDOW + CHUNK
REL_CLIP = 128
ATT_WIDTH = 6 * B_WIDTH
IN_WIDTH = SHIFT_WIDTH + R_WIDTH + ATT_WIDTH
LN_EPS = 1e-5
GN_EPS = 64e-5
ALPHA = 2.0 ** 0.25
ATT_SCALE = HEAD_DIM ** -0.5
BIAS_L = 1024
PAIR = 2 * HEAD_DIM
ATTN_GROUP_CHUNKS = 2
MEM_ROWS = 128
MEM_GROUP_BLOCKS = 2

VMEM_LIMIT = 48 * 1024 * 1024

NN = ((1,), (0,))
NT = ((1,), (1,))
TN = ((0,), (0,))


def _dot(a, b, dims=NN):
    return lax.dot_general(a, b, (dims, ((), ())), preferred_element_type=F32)


def _split2(x):
    hi = x.astype(BF16)
    lo = (x - hi.astype(F32)).astype(BF16)
    return hi, lo


def _mm3(a, b, dims=NN):
    ah, al = _split2(a)
    bh, bl = _split2(b)
    return _dot(ah, bh, dims) + _dot(ah, bl, dims) + _dot(al, bh, dims)


def _mm_exact_lhs(lhs_bf16, x):
    x1 = x.astype(BF16)
    r1 = x - x1.astype(F32)
    x2 = r1.astype(BF16)
    x3 = (r1 - x2.astype(F32)).astype(BF16)
    return _dot(lhs_bf16, x1) + _dot(lhs_bf16, x2) + _dot(lhs_bf16, x3)


def _sigmoid(x):
    return 1.0 / (1.0 + jnp.exp(-x))


def _silu(x):
    return x * _sigmoid(x)


def _proj_kernel(x_ref, w_ref, *out_refs, splits):
    x = x_ref[...].astype(BF16)
    for o_ref, (lo, hi) in zip(out_refs, splits):
        o_ref[...] = _dot(x, w_ref[:, lo:hi])


def _proj(x, w_bf16, splits, tm):
    m, k = x.shape
    n = w_bf16.shape[1]
    return pl.pallas_call(
        functools.partial(_proj_kernel, splits=splits),
        name=f"proj_{m}x{n}",
        grid=(m // tm,),
        in_specs=[pl.BlockSpec((tm, k), lambda i: (i, 0)),
                  pl.BlockSpec((k, n), lambda i: (0, 0))],
        out_specs=[pl.BlockSpec((tm, hi - lo), lambda i: (i, 0)) for lo, hi in splits],
        out_shape=[jax.ShapeDtypeStruct((m, hi - lo), F32) for lo, hi in splits],
        compiler_params=pltpu.CompilerParams(
            dimension_semantics=("arbitrary",), vmem_limit_bytes=VMEM_LIMIT),
    )(x, w_bf16)


def _pair_blocks(x2, left):
    zero = jnp.zeros((), x2.dtype)
    return jnp.concatenate([jnp.where(left, x2, zero), jnp.where(left, zero, x2)], axis=0)


def _pair_sum(x2, left):
    s0 = jnp.sum(jnp.where(left, x2, 0.0), axis=-1, keepdims=True)
    s1 = jnp.sum(jnp.where(left, 0.0, x2), axis=-1, keepdims=True)
    return jnp.where(left, s0, s1)


def _wkv_kernel(zs_ref, gr_ref, prev_ref, s0_ref, mu_ref, w0_ref, w2_ref, a0_ref, a2_ref,
                kk_ref, ka_ref, rk_ref, gng_ref, gnb_ref, out_ref, sfin_ref, carry_ref, state_ref, *, bb_n, c):
    t = pl.program_id(1)

    @pl.when(t == 0)
    def _():
        carry_ref[...] = prev_ref[...]
        for bb in range(bb_n):
            for h in range(R_HEADS):
                state_ref[bb, h // 2, :, (h % 2) * HEAD_DIM:(h % 2 + 1) * HEAD_DIM] = s0_ref[bb, h]

    row = lax.broadcasted_iota(jnp.int32, (c, c), 0)
    col = lax.broadcasted_iota(jnp.int32, (c, c), 1)
    ltri = (col <= row).astype(BF16)
    row2 = lax.broadcasted_iota(jnp.int32, (c, 2 * c), 0)
    col2 = lax.broadcasted_iota(jnp.int32, (c, 2 * c), 1) & (c - 1)
    eye2 = (row2 == col2).astype(BF16)
    strict2 = col2 < row2
    incl4 = ((lax.broadcasted_iota(jnp.int32, (c, 4 * c), 1) & (c - 1))
             <= lax.broadcasted_iota(jnp.int32, (c, 4 * c), 0))
    level_masks = []
    shift = 0
    while (1 << shift) < c:
        rb = row2 >> shift
        level_masks.append(((rb & 1) == 1) & ((col2 >> shift) == rb - 1))
        shift += 1
    left_s = lax.broadcasted_iota(jnp.int32, (1, 2 * c), 1) < c
    left = lax.broadcasted_iota(jnp.int32, (1, PAIR), 1) < HEAD_DIM
    first_row = lax.broadcasted_iota(jnp.int32, (c, 1), 0) == 0
    zero = jnp.zeros((), BF16)

    mu = mu_ref[...]
    units = []
    for bb in range(bb_n):
        zs = zs_ref[bb]
        zprev = jnp.where(first_row, carry_ref[bb], pltpu.roll(zs, 1, 0))
        carry_ref[bb] = zs[c - 1:c, :]
        xs = zs + (zprev - zs) * mu
        r = xs[:, 0:R_WIDTH]
        k = xs[:, R_WIDTH:2 * R_WIDTH]
        v = xs[:, 2 * R_WIDTH:3 * R_WIDTH]
        wd = xs[:, 3 * R_WIDTH:3 * R_WIDTH + LOW_RANK]
        ad = xs[:, 3 * R_WIDTH + LOW_RANK:]

        u = -(w0_ref[...] + _mm3(jnp.tanh(wd), w2_ref[...]))
        softplus = jnp.maximum(u, 0.0) + jnp.log(1.0 + jnp.exp(-jnp.abs(u)))
        ld = -jnp.exp(-softplus - 0.5)
        cum = _mm_exact_lhs(ltri, ld)
        a = _sigmoid(a0_ref[...] + _mm3(ad, a2_ref[...]))

        ecum = jnp.exp(cum)
        einv = jnp.exp(-cum)
        eprev = jnp.exp(cum - ld)
        pc = ecum[c - 1:c, :]
        kkr = k * kk_ref[...]
        k2 = k * (1.0 + (a - 1.0) * ka_ref[...])
        rt = r * ecum
        kh = k2 * einv
        kp = kh * pc
        rk2 = r * k2 * rk_ref[...]
        gate = _silu(gr_ref[bb])

        for pr in range(R_HEADS // 2):
            sl = slice(pr * PAIR, (pr + 1) * PAIR)
            kkr_p = kkr[:, sl]
            nrm = jnp.sqrt(_pair_sum(kkr_p * kkr_p, left))
            kkn = kkr_p / jnp.maximum(nrm, 1e-12)
            at = -kkn * eprev[:, sl]
            bh = kkn * a[:, sl] * einv[:, sl]
            v_p = v[:, sl]
            v16 = v_p.astype(BF16)
            units.append(dict(
                bb=bb, pr=pr, sl=sl, pc=pc[:, sl], gate=gate[:, sl], v_bd=_pair_blocks(v16, left), v16=v16,
                ar=jnp.concatenate([at, rt[:, sl]], axis=0).astype(BF16),
                bk_bd=jnp.concatenate([_pair_blocks(bh.astype(BF16), left),
                                       _pair_blocks(kh[:, sl].astype(BF16), left)], axis=0),
                bkp=jnp.concatenate([bh * pc[:, sl], kp[:, sl]], axis=0).astype(BF16),
                bonus=_pair_sum(rk2[:, sl], left) * v_p))

    for un in units:
        un["a4"] = _dot(un["ar"], un["bk_bd"], NT).astype(BF16)
        un["a_ab"] = un["a4"][:c, :2 * c]
        un["tinv"] = jnp.where(level_masks[0], un["a_ab"], eye2)
    for mask in level_masks[1:]:
        for un in units:
            un["lt"] = _dot(jnp.where(mask, un["a_ab"], zero), _pair_blocks(un["tinv"], left_s)).astype(BF16)
        for un in units:
            new = _dot(un["tinv"], _pair_blocks(un["lt"], left_s)).astype(BF16)
            un["tinv"] = jnp.where(mask, new, un["tinv"])
    for un in units:
        un["akv"] = _dot(jnp.where(strict2, un["a4"][:c, 2 * c:], zero), un["v_bd"])
    for un in units:
        un["s0"] = state_ref[un["bb"], un["pr"]]
        un["ars"] = _dot(un["ar"], _pair_blocks(un["s0"].astype(BF16), left), NT)
    for un in units:
        rhs = (un["ars"][:c] + un["akv"]).astype(BF16)
        un["pm"] = _dot(un["tinv"], _pair_blocks(rhs, left)).astype(BF16)
    for un in units:
        pv = jnp.concatenate([un["pm"], un["v16"]], axis=0)
        cross = _dot(pv, un["bkp"], TN)
        state_ref[un["bb"], un["pr"]] = un["s0"] * un["pc"] + jnp.where(left, cross[:HEAD_DIM], cross[HEAD_DIM:])
    for un in units:
        pv_bd = jnp.concatenate([_pair_blocks(un["pm"], left), un["v_bd"]], axis=0)
        y = un["ars"][c:] + _dot(jnp.where(incl4, un["a4"][c:], zero), pv_bd)
        mean = _pair_sum(y, left) * (1.0 / HEAD_DIM)
        yc = y - mean
        var = _pair_sum(yc * yc, left) * (1.0 / HEAD_DIM)
        yn = yc * lax.rsqrt(var + GN_EPS) * gng_ref[:, un["sl"]] + gnb_ref[:, un["sl"]]
        out_ref[un["bb"], :, un["sl"]] = (yn + un["bonus"]) * un["gate"]

    @pl.when(t == pl.num_programs(1) - 1)
    def _():
        for bb in range(bb_n):
            for h in range(R_HEADS):
                sfin_ref[bb, h] = state_ref[bb, h // 2, :, (h % 2) * HEAD_DIM:(h % 2 + 1) * HEAD_DIM]


def _wkv(zs, gr, prev, s0, params, *, bb_n, c):
    b, t, _ = zs.shape
    full = lambda arr: pl.BlockSpec(arr.shape, lambda i, j: (0,) * arr.ndim)
    return pl.pallas_call(
        functools.partial(_wkv_kernel, bb_n=bb_n, c=c),
        name=f"wkv_c{c}",
        grid=(b // bb_n, t // c),
        in_specs=[pl.BlockSpec((bb_n, c, SHIFT_WIDTH), lambda i, j: (i, j, 0)),
                  pl.BlockSpec((bb_n, c, R_WIDTH), lambda i, j: (i, j, 0)),
                  pl.BlockSpec((bb_n, 1, SHIFT_WIDTH), lambda i, j: (i, 0, 0)),
                  pl.BlockSpec((bb_n, R_HEADS, HEAD_DIM, HEAD_DIM), lambda i, j: (i, 0, 0, 0))]
                 + [full(p) for p in params],
        out_specs=[pl.BlockSpec((bb_n, c, R_WIDTH), lambda i, j: (i, j, 0)),
                   pl.BlockSpec((bb_n, R_HEADS, HEAD_DIM, HEAD_DIM), lambda i, j: (i, 0, 0, 0))],
        out_shape=[jax.ShapeDtypeStruct((b, t, R_WIDTH), F32),
                   jax.ShapeDtypeStruct((b, R_HEADS, HEAD_DIM, HEAD_DIM), F32)],
        scratch_shapes=[pltpu.VMEM((bb_n, 1, SHIFT_WIDTH), F32),
                        pltpu.VMEM((bb_n, R_HEADS // 2, HEAD_DIM, PAIR), F32)],
        compiler_params=pltpu.CompilerParams(
            dimension_semantics=("arbitrary", "arbitrary"), vmem_limit_bytes=VMEM_LIMIT),
    )(zs, gr, prev, s0, *params)


def _fill_rel_bias(bias_ref, gtab_ref, offset):
    heads, nq, nk = bias_ref.shape
    for h in range(heads):
        g = jnp.broadcast_to(gtab_ref[h:h + 1, :], (nq, BIAS_L))
        bias_ref[h] = pltpu.roll(g, BIAS_L - offset, 1, stride=1, stride_axis=0)[:, :nk]


def _stack_heads(q2, left):
    zero = jnp.zeros((), q2.dtype)
    return jnp.concatenate([jnp.where(left, q2, zero), jnp.where(left, zero, q2)], axis=0)


def _attend_pairs(jobs, left):
    for jb in jobs:
        s = _dot(jb["lhs"], jb["k"], NT)
        if jb.get("bias") is not None:
            s = s + jb["bias"]
        if jb.get("visible") is not None:
            s = jnp.where(jb["visible"], s, -jnp.inf)
        jb["s"] = s
    for jb in jobs:
        m = jnp.max(jb["s"], axis=-1, keepdims=True)
        p = jnp.exp(jb["s"] - m)
        jb["l"] = jnp.sum(p, axis=-1, keepdims=True)
        jb["p"] = p.astype(BF16)
    outs = []
    for jb in jobs:
        o2 = _dot(jb["p"], jb["v"]) / jb["l"]
        n = o2.shape[0] // 2
        outs.append(jnp.where(left, o2[:n], o2[n:]))
    return outs


def _attn_prompt_kernel(q_ref, kp_ref, kc_ref, vp_ref, vc_ref, gb_ref, mq_ref, gm_ref,
                        mk_ref, mv_ref, gtab_ref, out_ref, kcat_ref, vcat_ref, bias_ref, bias2_ref, *, tq):
    j = pl.program_id(1)

    @pl.when((pl.program_id(0) == 0) & (j == 0))
    def _():
        _fill_rel_bias(bias_ref, gtab_ref, CHUNK - 1)
        for h in range(B_HEADS):
            bias2_ref[h // 2, (h % 2) * CHUNK:(h % 2 + 1) * CHUNK, :] = bias_ref[h]

    kcat_ref[0:BAND_WINDOW] = kp_ref[0].astype(BF16)
    kcat_ref[BAND_WINDOW:] = kc_ref[0].astype(BF16)
    vcat_ref[0:BAND_WINDOW] = vp_ref[0].astype(BF16)
    vcat_ref[BAND_WINDOW:] = vc_ref[0].astype(BF16)
    q = (q_ref[0] * ATT_SCALE).astype(BF16)
    gate_b = _silu(gb_ref[0])
    left = lax.broadcasted_iota(jnp.int32, (1, PAIR), 1) < HEAD_DIM
    kcol = lax.broadcasted_iota(jnp.int32, (1, BAND_LEN), 1)
    n_chunks = tq // CHUNK
    for i0 in range(0, n_chunks, ATTN_GROUP_CHUNKS):
        jobs = []
        for i in range(i0, i0 + ATTN_GROUP_CHUNKS):
            rows = slice(i * CHUNK, (i + 1) * CHUNK)
            keys = slice(i * CHUNK, i * CHUNK + BAND_LEN)
            visible = kcol >= jnp.where(j == 0, BAND_WINDOW - i * CHUNK, 0)
            for pr in range(B_HEADS // 2):
                lanes = slice(pr * PAIR, (pr + 1) * PAIR)
                jobs.append(dict(rows=rows, lanes=lanes, lhs=_stack_heads(q[rows, lanes], left),
                                 k=kcat_ref[keys, lanes], v=vcat_ref[keys, lanes],
                                 bias=bias2_ref[pr], visible=visible))
        for jb, o in zip(jobs, _attend_pairs(jobs, left)):
            out_ref[0, jb["rows"], jb["lanes"]] = o * gate_b[jb["rows"], jb["lanes"]]

    mq = (mq_ref[0] * ATT_SCALE).astype(BF16)
    gate_m = _silu(gm_ref[0])
    mk = mk_ref[0].astype(BF16)
    mv = mv_ref[0].astype(BF16)
    for r0 in range(0, tq, MEM_ROWS * MEM_GROUP_BLOCKS):
        jobs = []
        for r in range(r0, r0 + MEM_ROWS * MEM_GROUP_BLOCKS, MEM_ROWS):
            rows = slice(r, r + MEM_ROWS)
            for pr in range(M_HEADS // 2):
                lanes = slice(pr * PAIR, (pr + 1) * PAIR)
                jobs.append(dict(rows=rows, lanes=lanes, lhs=_stack_heads(mq[rows, lanes], left),
                                 k=mk[:, lanes], v=mv[:, lanes]))
        for jb, o in zip(jobs, _attend_pairs(jobs, left)):
            out_ref[0, jb["rows"], B_WIDTH + jb["lanes"].start:B_WIDTH + jb["lanes"].stop] = (
                o * gate_m[jb["rows"], jb["lanes"]])


def _attn_prompt(att, memkv, gtab, *, tq):
    b, t, _ = att.shape
    col = lambda cidx: pl.BlockSpec((1, tq, B_WIDTH), lambda i, j: (i, j, cidx))
    prev = lambda cidx: pl.BlockSpec((1, tq, B_WIDTH), lambda i, j: (i, jnp.maximum(j - 1, 0), cidx))
    return pl.pallas_call(
        functools.partial(_attn_prompt_kernel, tq=tq),
        name="attn_prompt",
        grid=(b, t // tq),
        in_specs=[col(0), prev(1), col(1), prev(2), col(2), col(3), col(4), col(5),
                  pl.BlockSpec((1, N_MEM, B_WIDTH), lambda i, j: (i, 0, 0)),
                  pl.BlockSpec((1, N_MEM, B_WIDTH), lambda i, j: (i, 0, 1)),
                  pl.BlockSpec(gtab.shape, lambda i, j: (0, 0))],
        out_specs=pl.BlockSpec((1, tq, 2 * B_WIDTH), lambda i, j: (i, j, 0)),
        out_shape=jax.ShapeDtypeStruct((b, t, 2 * B_WIDTH), F32),
        scratch_shapes=[pltpu.VMEM((2 * BAND_WINDOW, B_WIDTH), BF16),
                        pltpu.VMEM((2 * BAND_WINDOW, B_WIDTH), BF16),
                        pltpu.VMEM((B_HEADS, CHUNK, BAND_LEN), F32),
                        pltpu.VMEM((B_HEADS // 2, 2 * CHUNK, BAND_LEN), F32)],
        compiler_params=pltpu.CompilerParams(
            dimension_semantics=("arbitrary", "arbitrary"), vmem_limit_bytes=VMEM_LIMIT),
    )(att, att, att, att, att, att, att, att, memkv, memkv, gtab)


def _attn_sample_kernel(att_ref, ck_ref, cv_ref, mk_ref, mv_ref, gtab_ref, out_ref, bias_ref, bias2_ref):
    n_seq, n_new, _ = att_ref.shape

    @pl.when(pl.program_id(0) == 0)
    def _():
        _fill_rel_bias(bias_ref, gtab_ref, n_new - 1)
        for h in range(B_HEADS):
            bias2_ref[h // 2, (h % 2) * n_new:(h % 2 + 1) * n_new, :] = bias_ref[h]

    left = lax.broadcasted_iota(jnp.int32, (1, PAIR), 1) < HEAD_DIM
    jobs = []
    for b in range(n_seq):
        att = att_ref[b]
        q = (att[:, 0:B_WIDTH] * ATT_SCALE).astype(BF16)
        k_all = jnp.concatenate([ck_ref[b].astype(BF16), att[:, B_WIDTH:2 * B_WIDTH].astype(BF16)], axis=0)
        v_all = jnp.concatenate([cv_ref[b].astype(BF16), att[:, 2 * B_WIDTH:3 * B_WIDTH].astype(BF16)], axis=0)
        gate_b = _silu(att[:, 3 * B_WIDTH:4 * B_WIDTH])
        mq = (att[:, 4 * B_WIDTH:5 * B_WIDTH] * ATT_SCALE).astype(BF16)
        gate_m = _silu(att[:, 5 * B_WIDTH:6 * B_WIDTH])
        mk = mk_ref[b].astype(BF16)
        mv = mv_ref[b].astype(BF16)
        for pr in range(B_HEADS // 2):
            lanes = slice(pr * PAIR, (pr + 1) * PAIR)
            jobs.append(dict(b=b, out=lanes, gate=gate_b[:, lanes], lhs=_stack_heads(q[:, lanes], left),
                             k=k_all[:, lanes], v=v_all[:, lanes], bias=bias2_ref[pr]))
        for pr in range(M_HEADS // 2):
            lanes = slice(pr * PAIR, (pr + 1) * PAIR)
            jobs.append(dict(b=b, out=slice(B_WIDTH + lanes.start, B_WIDTH + lanes.stop), gate=gate_m[:, lanes],
                             lhs=_stack_heads(mq[:, lanes], left), k=mk[:, lanes], v=mv[:, lanes]))
    for jb, o in zip(jobs, _attend_pairs(jobs, left)):
        out_ref[jb["b"], :, jb["out"]] = o * jb["gate"]


def _attn_sample(att, cache_k, cache_v, mem_k, mem_v, gtab, *, n_seq):
    b, s, _ = att.shape
    per_b = lambda arr: pl.BlockSpec((n_seq,) + arr.shape[1:], lambda i: (i, 0, 0))
    n_keys = cache_k.shape[1] + s
    return pl.pallas_call(
        _attn_sample_kernel,
        name="attn_sample",
        grid=(b // n_seq,),
        in_specs=[per_b(att), per_b(cache_k), per_b(cache_v), per_b(mem_k), per_b(mem_v),
                  pl.BlockSpec(gtab.shape, lambda i: (0, 0))],
        out_specs=pl.BlockSpec((n_seq, s, 2 * B_WIDTH), lambda i: (i, 0, 0)),
        out_shape=jax.ShapeDtypeStruct((b, s, 2 * B_WIDTH), F32),
        scratch_shapes=[pltpu.VMEM((B_HEADS, s, n_keys), F32),
                        pltpu.VMEM((B_HEADS // 2, 2 * s, n_keys), F32)],
        compiler_params=pltpu.CompilerParams(dimension_semantics=("arbitrary",)),
    )(att, cache_k, cache_v, mem_k, mem_v, gtab)


def _finish_kernel(x_ref, mr_ref, mbm_ref, w_ref, g_ref, b_ref, y_ref):
    o = _dot(mr_ref[...].astype(BF16), w_ref[0:R_WIDTH, :])
    o = o + _dot(mbm_ref[...].astype(BF16), w_ref[R_WIDTH:, :])
    hres = ALPHA * x_ref[...] + o
    mean = jnp.mean(hres, axis=-1, keepdims=True)
    hc = hres - mean
    var = jnp.mean(hc * hc, axis=-1, keepdims=True)
    y_ref[...] = hc * lax.rsqrt(var + LN_EPS) * g_ref[...] + b_ref[...]


def _finish(x, mix_r, mix_bm, w_out_bf16, ln_g, ln_b, tm):
    m = x.shape[0]
    rows = lambda width: pl.BlockSpec((tm, width), lambda i: (i, 0))
    full = lambda arr: pl.BlockSpec(arr.shape, lambda i: (0, 0))
    return pl.pallas_call(
        _finish_kernel,
        name=f"finish_{m}",
        grid=(m // tm,),
        in_specs=[rows(D_MODEL), rows(R_WIDTH), rows(2 * B_WIDTH), full(w_out_bf16), full(ln_g), full(ln_b)],
        out_specs=rows(D_MODEL),
        out_shape=jax.ShapeDtypeStruct((m, D_MODEL), F32),
        compiler_params=pltpu.CompilerParams(
            dimension_semantics=("arbitrary",), vmem_limit_bytes=VMEM_LIMIT),
    )(x, mix_r, mix_bm, w_out_bf16, ln_g, ln_b)


IN_SPLITS = ((0, SHIFT_WIDTH), (SHIFT_WIDTH, SHIFT_WIDTH + R_WIDTH), (SHIFT_WIDTH + R_WIDTH, IN_WIDTH))


def _rel_bias_row(table, rel0):
    n_hi = rel0 - REL_CLIP
    n_lo = BIAS_L - n_hi - (2 * REL_CLIP + 1)
    heads = table.shape[0]
    return jnp.concatenate([jnp.broadcast_to(table[:, 2 * REL_CLIP:], (heads, n_hi)), table[:, ::-1],
                            jnp.broadcast_to(table[:, 0:1], (heads, n_lo))], axis=1)


def kernel(x_prompt, x_sample, mem_prompt, state_shift, state_wkv, cache_band_k, cache_band_v,
           cache_mem_k, cache_mem_v, w_in, mu_shift, w0, w2, a0, a2, k_k, k_a, r_k, gn_g, gn_b,
           rel_bias, w_mem_kv, w_out, ln_g, ln_b):
    bp, t, _ = x_prompt.shape
    bs, s, _ = x_sample.shape
    depth = w_in.shape[0]
    assert depth == 1 and t % BAND_WINDOW == 0 and s <= CHUNK
    keep = min(BAND_WINDOW, t)
    l = 0

    w_in_b = w_in[l].astype(BF16)
    w_out_b = w_out[l].astype(BF16)
    w_mem_b = w_mem_kv[l].astype(BF16)
    row = lambda p: p.reshape(1, -1)
    wkv_params = (row(mu_shift[l]), row(w0[l]), w2[l], row(a0[l]), a2[l], row(k_k[l]), row(k_a[l]),
                  row(r_k[l]), row(gn_g[l]), row(gn_b[l]))
    table = rel_bias[l]
    r_rows = cache_band_k.shape[2]
    gtab_p = _rel_bias_row(table, BAND_WINDOW + CHUNK - 1)
    gtab_s = _rel_bias_row(table, r_rows + s - 1)

    xp = x_prompt.reshape(bp * t, D_MODEL)
    zs, gr, att = _proj(xp, w_in_b, IN_SPLITS, 256)
    zs = zs.reshape(bp, t, SHIFT_WIDTH)
    att = att.reshape(bp, t, ATT_WIDTH)
    memkv, = _proj(mem_prompt.reshape(bp * N_MEM, D_MODEL), w_mem_b, ((0, 2 * B_WIDTH),), 256)
    memkv = memkv.reshape(bp, N_MEM, 2 * B_WIDTH)
    mix_r, p_wkv = _wkv(zs, gr.reshape(bp, t, R_WIDTH), jnp.zeros((bp, 1, SHIFT_WIDTH), F32),
                        jnp.zeros((bp, R_HEADS, HEAD_DIM, HEAD_DIM), F32), wkv_params, bb_n=2, c=CHUNK)
    mix_bm = _attn_prompt(att, memkv, gtab_p, tq=BAND_WINDOW)
    y_prompt = _finish(xp, mix_r.reshape(bp * t, R_WIDTH), mix_bm.reshape(bp * t, 2 * B_WIDTH),
                       w_out_b, row(ln_g[l]), row(ln_b[l]), 512).reshape(bp, t, D_MODEL)
    p_shift = zs[:, -1]
    p_bk = att[:, t - keep:, B_WIDTH:2 * B_WIDTH].reshape(bp, keep, B_HEADS, HEAD_DIM)
    p_bv = att[:, t - keep:, 2 * B_WIDTH:3 * B_WIDTH].reshape(bp, keep, B_HEADS, HEAD_DIM)
    p_mk = memkv[:, :, :B_WIDTH].reshape(bp, N_MEM, M_HEADS, HEAD_DIM)
    p_mv = memkv[:, :, B_WIDTH:].reshape(bp, N_MEM, M_HEADS, HEAD_DIM)

    xs = x_sample.reshape(bs * s, D_MODEL)
    zs_s, gr_s, att_s = _proj(xs, w_in_b, IN_SPLITS, bs * s)
    zs_s = zs_s.reshape(bs, s, SHIFT_WIDTH)
    att_s = att_s.reshape(bs, s, ATT_WIDTH)
    mix_r_s, s_wkv = _wkv(zs_s, gr_s.reshape(bs, s, R_WIDTH), state_shift[l][:, None, :], state_wkv[l],
                          wkv_params, bb_n=2, c=s)
    mix_bm_s = _attn_sample(att_s,
                            cache_band_k[l].reshape(bs, r_rows, B_WIDTH),
                            cache_band_v[l].reshape(bs, r_rows, B_WIDTH),
                            cache_mem_k[l].reshape(bs, N_MEM, B_WIDTH),
                            cache_mem_v[l].reshape(bs, N_MEM, B_WIDTH), gtab_s, n_seq=2)
    y_sample = _finish(xs, mix_r_s.reshape(bs * s, R_WIDTH), mix_bm_s.reshape(bs * s, 2 * B_WIDTH),
                       w_out_b, row(ln_g[l]), row(ln_b[l]), bs * s).reshape(bs, s, D_MODEL)
    s_shift = zs_s[:, -1]
    s_bk = att_s[:, :, B_WIDTH:2 * B_WIDTH].reshape(bs, s, B_HEADS, HEAD_DIM)
    s_bv = att_s[:, :, 2 * B_WIDTH:3 * B_WIDTH].reshape(bs, s, B_HEADS, HEAD_DIM)

    st = lambda a: a[None]
    return (y_prompt, y_sample, st(p_shift), st(p_wkv), st(p_bk), st(p_bv), st(p_mk), st(p_mv),
            st(s_shift), st(s_wkv), st(s_bk), st(s_bv))
```

```python
import functools

import numpy as np
import jax
import jax.numpy as jnp
from jax import lax
from jax.experimental import pallas as pl
from jax.experimental.pallas import tpu as pltpu

F32 = jnp.float32
BF16 = jnp.bfloat16

D_MODEL = 1024
HEAD_DIM = 64
R_WIDTH = 512
R_HEADS = 8
LOW_RANK = 64
SHIFT_WIDTH = 3 * R_WIDTH + 2 * LOW_RANK
B_WIDTH = 256
B_HEADS = 4
M_HEADS = 4
N_MEM = 256
CHUNK = 64
BAND_CHUNKS = 8
BAND_WINDOW = BAND_CHUNKS * CHUNK
BAND_LEN = BAND_WINDOW + CHUNK
REL_CLIP = 128
ATT_WIDTH = 6 * B_WIDTH
IN_WIDTH = SHIFT_WIDTH + R_WIDTH + ATT_WIDTH
LN_EPS = 1e-5
GN_EPS = 64e-5
ALPHA = 2.0 ** 0.25
ATT_SCALE = HEAD_DIM ** -0.5
BIAS_L = 1024
PAIR = 2 * HEAD_DIM
ATTN_GROUP_CHUNKS = 2
MEM_ROWS = 128
MEM_GROUP_BLOCKS = 2
WKV_CHUNKS_PER_STEP = 4

VMEM_LIMIT = 48 * 1024 * 1024

NN = ((1,), (0,))
NT = ((1,), (1,))
TN = ((0,), (0,))


def _dot(a, b, dims=NN):
    return lax.dot_general(a, b, (dims, ((), ())), preferred_element_type=F32)


def _split2(x):
    hi = x.astype(BF16)
    lo = (x - hi.astype(F32)).astype(BF16)
    return hi, lo


def _mm3(a, b, dims=NN):
    ah, al = _split2(a)
    bh, bl = _split2(b)
    return _dot(ah, bh, dims) + _dot(ah, bl, dims) + _dot(al, bh, dims)


def _mm_exact_lhs(lhs_bf16, x):
    x1 = x.astype(BF16)
    r1 = x - x1.astype(F32)
    x2 = r1.astype(BF16)
    x3 = (r1 - x2.astype(F32)).astype(BF16)
    return _dot(lhs_bf16, x1) + _dot(lhs_bf16, x2) + _dot(lhs_bf16, x3)


def _sigmoid(x):
    return 1.0 / (1.0 + jnp.exp(-x))


def _silu(x):
    return x * _sigmoid(x)


def _proj_kernel(x_ref, w_ref, *out_refs, splits):
    x = x_ref[...].astype(BF16)
    for o_ref, (lo, hi) in zip(out_refs, splits):
        o_ref[...] = _dot(x, w_ref[:, lo:hi])


def _proj(x, w_bf16, splits, tm):
    m, k = x.shape
    n = w_bf16.shape[1]
    return pl.pallas_call(
        functools.partial(_proj_kernel, splits=splits),
        name=f"proj_{m}x{n}",
        grid=(m // tm,),
        in_specs=[pl.BlockSpec((tm, k), lambda i: (i, 0)),
                  pl.BlockSpec((k, n), lambda i: (0, 0))],
        out_specs=[pl.BlockSpec((tm, hi - lo), lambda i: (i, 0)) for lo, hi in splits],
        out_shape=[jax.ShapeDtypeStruct((m, hi - lo), F32) for lo, hi in splits],
        compiler_params=pltpu.CompilerParams(
            dimension_semantics=("arbitrary",), vmem_limit_bytes=VMEM_LIMIT),
    )(x, w_bf16)


def _pair_blocks(x2, left):
    zero = jnp.zeros((), x2.dtype)
    return jnp.concatenate([jnp.where(left, x2, zero), jnp.where(left, zero, x2)], axis=0)


def _pair_sum(x2, left):
    s0 = jnp.sum(jnp.where(left, x2, 0.0), axis=-1, keepdims=True)
    s1 = jnp.sum(jnp.where(left, 0.0, x2), axis=-1, keepdims=True)
    return jnp.where(left, s0, s1)


def _wkv_kernel(zs_ref, gr_ref, prev_ref, s0_ref, mu_ref, w0_ref, w2_ref, a0_ref, a2_ref,
                kk_ref, ka_ref, rk_ref, gng_ref, gnb_ref, out_ref, sfin_ref, carry_ref, state_ref,
                *, bb_n, c, n_ch):
    t = pl.program_id(1)
    tt = n_ch * c

    @pl.when(t == 0)
    def _():
        carry_ref[...] = prev_ref[...]
        for bb in range(bb_n):
            for h in range(R_HEADS):
                state_ref[bb, h // 2, :, (h % 2) * HEAD_DIM:(h % 2 + 1) * HEAD_DIM] = s0_ref[bb, h]

    row = lax.broadcasted_iota(jnp.int32, (tt, tt), 0)
    col = lax.broadcasted_iota(jnp.int32, (tt, tt), 1)
    shift_c = c.bit_length() - 1
    ltri = ((col <= row) & ((col >> shift_c) == (row >> shift_c))).astype(BF16)
    row2 = lax.broadcasted_iota(jnp.int32, (c, 2 * c), 0)
    col2 = lax.broadcasted_iota(jnp.int32, (c, 2 * c), 1) & (c - 1)
    eye2 = (row2 == col2).astype(BF16)
    strict2 = col2 < row2
    incl4 = ((lax.broadcasted_iota(jnp.int32, (c, 4 * c), 1) & (c - 1))
             <= lax.broadcasted_iota(jnp.int32, (c, 4 * c), 0))
    level_masks = []
    shift = 0
    while (1 << shift) < c:
        rb = row2 >> shift
        level_masks.append(((rb & 1) == 1) & ((col2 >> shift) == rb - 1))
        shift += 1
    left_s = lax.broadcasted_iota(jnp.int32, (1, 2 * c), 1) < c
    left = lax.broadcasted_iota(jnp.int32, (1, PAIR), 1) < HEAD_DIM
    first_row = lax.broadcasted_iota(jnp.int32, (tt, 1), 0) == 0
    zero = jnp.zeros((), BF16)

    mu = mu_ref[...]
    chunks = [[] for _ in range(n_ch)]
    for bb in range(bb_n):
        zs = zs_ref[bb]
        zprev = jnp.where(first_row, carry_ref[bb], pltpu.roll(zs, 1, 0))
        carry_ref[bb] = zs[tt - 1:tt, :]
        xs = zs + (zprev - zs) * mu
        r = xs[:, 0:R_WIDTH]
        k = xs[:, R_WIDTH:2 * R_WIDTH]
        v = xs[:, 2 * R_WIDTH:3 * R_WIDTH]
        wd = xs[:, 3 * R_WIDTH:3 * R_WIDTH + LOW_RANK]
        ad = xs[:, 3 * R_WIDTH + LOW_RANK:]

        u = -(w0_ref[...] + _mm3(jnp.tanh(wd), w2_ref[...]))
        softplus = jnp.maximum(u, 0.0) + jnp.log(1.0 + jnp.exp(-jnp.abs(u)))
        ld = -jnp.exp(-softplus - 0.5)
        cum = _mm_exact_lhs(ltri, ld)
        a = _sigmoid(a0_ref[...] + _mm3(ad, a2_ref[...]))

        ecum = jnp.exp(cum)
        einv = jnp.exp(-cum)
        eprev = jnp.exp(cum - ld)
        kkr = k * kk_ref[...]
        k2 = k * (1.0 + (a - 1.0) * ka_ref[...])
        rt = r * ecum
        kh = k2 * einv
        rk2 = r * k2 * rk_ref[...]
        gate = _silu(gr_ref[bb])

        for pr in range(R_HEADS // 2):
            sl = slice(pr * PAIR, (pr + 1) * PAIR)
            kkr_p = kkr[:, sl]
            nrm = jnp.sqrt(_pair_sum(kkr_p * kkr_p, left))
            kkn = kkr_p / jnp.maximum(nrm, 1e-12)
            at = -kkn * eprev[:, sl]
            bh = kkn * a[:, sl] * einv[:, sl]
            bonus = _pair_sum(rk2[:, sl], left) * v[:, sl]
            for ch in range(n_ch):
                rows = slice(ch * c, (ch + 1) * c)
                pc = ecum[(ch + 1) * c - 1:(ch + 1) * c, sl]
                v16 = v[rows, sl].astype(BF16)
                bh_c = bh[rows]
                kh_c = kh[rows, sl]
                chunks[ch].append(dict(
                    bb=bb, pr=pr, sl=sl, rows=rows, pc=pc, gate=gate[rows, sl], bonus=bonus[rows],
                    v_bd=_pair_blocks(v16, left), v16=v16,
                    ar=jnp.concatenate([at[rows], rt[rows, sl]], axis=0).astype(BF16),
                    bk_bd=jnp.concatenate([_pair_blocks(bh_c.astype(BF16), left),
                                           _pair_blocks(kh_c.astype(BF16), left)], axis=0),
                    bkp=jnp.concatenate([bh_c * pc, kh_c * pc], axis=0).astype(BF16)))

    def st_a4(units):
        for un in units:
            un["a4"] = _dot(un["ar"], un["bk_bd"], NT).astype(BF16)
            un["a_ab"] = un["a4"][:c, :2 * c]
            un["tinv"] = jnp.where(level_masks[0], un["a_ab"], eye2)

    def st_lt(mask):
        def run(units):
            for un in units:
                un["lt"] = _dot(jnp.where(mask, un["a_ab"], zero), _pair_blocks(un["tinv"], left_s)).astype(BF16)
        return run

    def st_tinv(mask):
        def run(units):
            for un in units:
                new = _dot(un["tinv"], _pair_blocks(un["lt"], left_s)).astype(BF16)
                un["tinv"] = jnp.where(mask, new, un["tinv"])
        return run

    def st_akv(units):
        for un in units:
            un["akv"] = _dot(jnp.where(strict2, un["a4"][:c, 2 * c:], zero), un["v_bd"])

    def st_ars(units):
        for un in units:
            un["s0"] = state[un["bb"], un["pr"]]
            un["ars"] = _dot(un["ar"], _pair_blocks(un["s0"].astype(BF16), left), NT)

    def st_pm(units):
        for un in units:
            rhs = (un["ars"][:c] + un["akv"]).astype(BF16)
            un["pm"] = _dot(un["tinv"], _pair_blocks(rhs, left)).astype(BF16)

    def st_state(units):
        for un in units:
            pv = jnp.concatenate([un["pm"], un["v16"]], axis=0)
            cross = _dot(pv, un["bkp"], TN)
            state[un["bb"], un["pr"]] = (un["s0"] * un["pc"]
                                         + jnp.where(left, cross[:HEAD_DIM], cross[HEAD_DIM:]))

    def st_y(units):
        for un in units:
            pv_bd = jnp.concatenate([_pair_blocks(un["pm"], left), un["v_bd"]], axis=0)
            y = un["ars"][c:] + _dot(jnp.where(incl4, un["a4"][c:], zero), pv_bd)
            mean = _pair_sum(y, left) * (1.0 / HEAD_DIM)
            yc = y - mean
            var = _pair_sum(yc * yc, left) * (1.0 / HEAD_DIM)
            yn = yc * lax.rsqrt(var + GN_EPS) * gng_ref[:, un["sl"]] + gnb_ref[:, un["sl"]]
            out_ref[un["bb"], un["rows"], un["sl"]] = (yn + un["bonus"]) * un["gate"]

    free_stages = [st_a4]
    for mask in level_masks[1:]:
        free_stages += [st_lt(mask), st_tinv(mask)]
    free_stages.append(st_akv)
    state_stages = [st_ars, st_pm, st_state, st_y]

    state = {(bb, pr): state_ref[bb, pr] for bb in range(bb_n) for pr in range(R_HEADS // 2)}
    for stage in free_stages:
        stage(chunks[0])
    for ch in range(n_ch):
        ahead = free_stages if ch + 1 < n_ch else []
        per_slot = -(-len(ahead) // len(state_stages))
        for i, stage in enumerate(state_stages):
            stage(chunks[ch])
            for nxt in ahead[i * per_slot:(i + 1) * per_slot]:
                nxt(chunks[ch + 1])
    for (bb, pr), val in state.items():
        state_ref[bb, pr] = val

    @pl.when(t == pl.num_programs(1) - 1)
    def _():
        for bb in range(bb_n):
            for h in range(R_HEADS):
                sfin_ref[bb, h] = state_ref[bb, h // 2, :, (h % 2) * HEAD_DIM:(h % 2 + 1) * HEAD_DIM]


def _wkv(zs, gr, prev, s0, params, *, bb_n, c, n_ch):
    b, t, _ = zs.shape
    tt = n_ch * c
    full = lambda arr: pl.BlockSpec(arr.shape, lambda i, j: (0,) * arr.ndim)
    return pl.pallas_call(
        functools.partial(_wkv_kernel, bb_n=bb_n, c=c, n_ch=n_ch),
        name=f"wkv_c{c}",
        grid=(b // bb_n, t // tt),
        in_specs=[pl.BlockSpec((bb_n, tt, SHIFT_WIDTH), lambda i, j: (i, j, 0)),
                  pl.BlockSpec((bb_n, tt, R_WIDTH), lambda i, j: (i, j, 0)),
                  pl.BlockSpec((bb_n, 1, SHIFT_WIDTH), lambda i, j: (i, 0, 0)),
                  pl.BlockSpec((bb_n, R_HEADS, HEAD_DIM, HEAD_DIM), lambda i, j: (i, 0, 0, 0))]
                 + [full(p) for p in params],
        out_specs=[pl.BlockSpec((bb_n, tt, R_WIDTH), lambda i, j: (i, j, 0)),
                   pl.BlockSpec((bb_n, R_HEADS, HEAD_DIM, HEAD_DIM), lambda i, j: (i, 0, 0, 0))],
        out_shape=[jax.ShapeDtypeStruct((b, t, R_WIDTH), F32),
                   jax.ShapeDtypeStruct((b, R_HEADS, HEAD_DIM, HEAD_DIM), F32)],
        scratch_shapes=[pltpu.VMEM((bb_n, 1, SHIFT_WIDTH), F32),
                        pltpu.VMEM((bb_n, R_HEADS // 2, HEAD_DIM, PAIR), F32)],
        compiler_params=pltpu.CompilerParams(
            dimension_semantics=("arbitrary", "arbitrary"), vmem_limit_bytes=VMEM_LIMIT),
    )(zs, gr, prev, s0, *params)


def _fill_rel_bias(bias_ref, gtab_ref, offset):
    heads, nq, nk = bias_ref.shape
    for h in range(heads):
        g = jnp.broadcast_to(gtab_ref[h:h + 1, :], (nq, BIAS_L))
        bias_ref[h] = pltpu.roll(g, BIAS_L - offset, 1, stride=1, stride_axis=0)[:, :nk]


def _stack_heads(q2, left):
    zero = jnp.zeros((), q2.dtype)
    return jnp.concatenate([jnp.where(left, q2, zero), jnp.where(left, zero, q2)], axis=0)


def _attend_pairs(jobs, left):
    for jb in jobs:
        s = _dot(jb["lhs"], jb["k"], NT)
        if jb.get("bias") is not None:
            s = s + jb["bias"]
        if jb.get("visible") is not None:
            s = jnp.where(jb["visible"], s, -jnp.inf)
        jb["s"] = s
    for jb in jobs:
        m = jnp.max(jb["s"], axis=-1, keepdims=True)
        p = jnp.exp(jb["s"] - m)
        jb["l"] = jnp.sum(p, axis=-1, keepdims=True)
        jb["p"] = p.astype(BF16)
    outs = []
    for jb in jobs:
        o2 = _dot(jb["p"], jb["v"]) / jb["l"]
        n = o2.shape[0] // 2
        outs.append(jnp.where(left, o2[:n], o2[n:]))
    return outs


def _attn_prompt_kernel(q_ref, kp_ref, kc_ref, vp_ref, vc_ref, gb_ref, mq_ref, gm_ref,
                        mk_ref, mv_ref, gtab_ref, out_ref, kcat_ref, vcat_ref, bias_ref, bias2_ref, *, tq):
    j = pl.program_id(1)

    @pl.when((pl.program_id(0) == 0) & (j == 0))
    def _():
        _fill_rel_bias(bias_ref, gtab_ref, CHUNK - 1)
        for h in range(B_HEADS):
            bias2_ref[h // 2, (h % 2) * CHUNK:(h % 2 + 1) * CHUNK, :] = bias_ref[h]

    kcat_ref[0:BAND_WINDOW] = kp_ref[0].astype(BF16)
    kcat_ref[BAND_WINDOW:] = kc_ref[0].astype(BF16)
    vcat_ref[0:BAND_WINDOW] = vp_ref[0].astype(BF16)
    vcat_ref[BAND_WINDOW:] = vc_ref[0].astype(BF16)
    q = (q_ref[0] * ATT_SCALE).astype(BF16)
    gate_b = _silu(gb_ref[0])
    left = lax.broadcasted_iota(jnp.int32, (1, PAIR), 1) < HEAD_DIM
    kcol = lax.broadcasted_iota(jnp.int32, (1, BAND_LEN), 1)
    n_chunks = tq // CHUNK
    for i0 in range(0, n_chunks, ATTN_GROUP_CHUNKS):
        jobs = []
        for i in range(i0, i0 + ATTN_GROUP_CHUNKS):
            rows = slice(i * CHUNK, (i + 1) * CHUNK)
            keys = slice(i * CHUNK, i * CHUNK + BAND_LEN)
            visible = kcol >= jnp.where(j == 0, BAND_WINDOW - i * CHUNK, 0)
            for pr in range(B_HEADS // 2):
                lanes = slice(pr * PAIR, (pr + 1) * PAIR)
                jobs.append(dict(rows=rows, lanes=lanes, lhs=_stack_heads(q[rows, lanes], left),
                                 k=kcat_ref[keys, lanes], v=vcat_ref[keys, lanes],
                                 bias=bias2_ref[pr], visible=visible))
        for jb, o in zip(jobs, _attend_pairs(jobs, left)):
            out_ref[0, jb["rows"], jb["lanes"]] = o * gate_b[jb["rows"], jb["lanes"]]

    mq = (mq_ref[0] * ATT_SCALE).astype(BF16)
    gate_m = _silu(gm_ref[0])
    mk = mk_ref[0].astype(BF16)
    mv = mv_ref[0].astype(BF16)
    for r0 in range(0, tq, MEM_ROWS * MEM_GROUP_BLOCKS):
        jobs = []
        for r in range(r0, r0 + MEM_ROWS * MEM_GROUP_BLOCKS, MEM_ROWS):
            rows = slice(r, r + MEM_ROWS)
            for pr in range(M_HEADS // 2):
                lanes = slice(pr * PAIR, (pr + 1) * PAIR)
                jobs.append(dict(rows=rows, lanes=lanes, lhs=_stack_heads(mq[rows, lanes], left),
                                 k=mk[:, lanes], v=mv[:, lanes]))
        for jb, o in zip(jobs, _attend_pairs(jobs, left)):
            out_ref[0, jb["rows"], B_WIDTH + jb["lanes"].start:B_WIDTH + jb["lanes"].stop] = (
                o * gate_m[jb["rows"], jb["lanes"]])


def _attn_prompt(att, memkv, gtab, *, tq):
    b, t, _ = att.shape
    col = lambda cidx: pl.BlockSpec((1, tq, B_WIDTH), lambda i, j: (i, j, cidx))
    prev = lambda cidx: pl.BlockSpec((1, tq, B_WIDTH), lambda i, j: (i, jnp.maximum(j - 1, 0), cidx))
    return pl.pallas_call(
        functools.partial(_attn_prompt_kernel, tq=tq),
        name="attn_prompt",
        grid=(b, t // tq),
        in_specs=[col(0), prev(1), col(1), prev(2), col(2), col(3), col(4), col(5),
                  pl.BlockSpec((1, N_MEM, B_WIDTH), lambda i, j: (i, 0, 0)),
                  pl.BlockSpec((1, N_MEM, B_WIDTH), lambda i, j: (i, 0, 1)),
                  pl.BlockSpec(gtab.shape, lambda i, j: (0, 0))],
        out_specs=pl.BlockSpec((1, tq, 2 * B_WIDTH), lambda i, j: (i, j, 0)),
        out_shape=jax.ShapeDtypeStruct((b, t, 2 * B_WIDTH), F32),
        scratch_shapes=[pltpu.VMEM((2 * BAND_WINDOW, B_WIDTH), BF16),
                        pltpu.VMEM((2 * BAND_WINDOW, B_WIDTH), BF16),
                        pltpu.VMEM((B_HEADS, CHUNK, BAND_LEN), F32),
                        pltpu.VMEM((B_HEADS // 2, 2 * CHUNK, BAND_LEN), F32)],
        compiler_params=pltpu.CompilerParams(
            dimension_semantics=("arbitrary", "arbitrary"), vmem_limit_bytes=VMEM_LIMIT),
    )(att, att, att, att, att, att, att, att, memkv, memkv, gtab)


def _attn_sample_kernel(att_ref, ck_ref, cv_ref, mk_ref, mv_ref, gtab_ref, out_ref, bias_ref, bias2_ref):
    n_seq, n_new, _ = att_ref.shape

    @pl.when(pl.program_id(0) == 0)
    def _():
        _fill_rel_bias(bias_ref, gtab_ref, n_new - 1)
        for h in range(B_HEADS):
            bias2_ref[h // 2, (h % 2) * n_new:(h % 2 + 1) * n_new, :] = bias_ref[h]

    left = lax.broadcasted_iota(jnp.int32, (1, PAIR), 1) < HEAD_DIM
    jobs = []
    for b in range(n_seq):
        att = att_ref[b]
        q = (att[:, 0:B_WIDTH] * ATT_SCALE).astype(BF16)
        k_all = jnp.concatenate([ck_ref[b].astype(BF16), att[:, B_WIDTH:2 * B_WIDTH].astype(BF16)], axis=0)
        v_all = jnp.concatenate([cv_ref[b].astype(BF16), att[:, 2 * B_WIDTH:3 * B_WIDTH].astype(BF16)], axis=0)
        gate_b = _silu(att[:, 3 * B_WIDTH:4 * B_WIDTH])
        mq = (att[:, 4 * B_WIDTH:5 * B_WIDTH] * ATT_SCALE).astype(BF16)
        gate_m = _silu(att[:, 5 * B_WIDTH:6 * B_WIDTH])
        mk = mk_ref[b].astype(BF16)
        mv = mv_ref[b].astype(BF16)
        for pr in range(B_HEADS // 2):
            lanes = slice(pr * PAIR, (pr + 1) * PAIR)
            jobs.append(dict(b=b, out=lanes, gate=gate_b[:, lanes], lhs=_stack_heads(q[:, lanes], left),
                             k=k_all[:, lanes], v=v_all[:, lanes], bias=bias2_ref[pr]))
        for pr in range(M_HEADS // 2):
            lanes = slice(pr * PAIR, (pr + 1) * PAIR)
            jobs.append(dict(b=b, out=slice(B_WIDTH + lanes.start, B_WIDTH + lanes.stop), gate=gate_m[:, lanes],
                             lhs=_stack_heads(mq[:, lanes], left), k=mk[:, lanes], v=mv[:, lanes]))
    for jb, o in zip(jobs, _attend_pairs(jobs, left)):
        out_ref[jb["b"], :, jb["out"]] = o * jb["gate"]


def _attn_sample(att, cache_k, cache_v, mem_k, mem_v, gtab, *, n_seq):
    b, s, _ = att.shape
    per_b = lambda arr: pl.BlockSpec((n_seq,) + arr.shape[1:], lambda i: (i, 0, 0))
    n_keys = cache_k.shape[1] + s
    return pl.pallas_call(
        _attn_sample_kernel,
        name="attn_sample",
        grid=(b // n_seq,),
        in_specs=[per_b(att), per_b(cache_k), per_b(cache_v), per_b(mem_k), per_b(mem_v),
                  pl.BlockSpec(gtab.shape, lambda i: (0, 0))],
        out_specs=pl.BlockSpec((n_seq, s, 2 * B_WIDTH), lambda i: (i, 0, 0)),
        out_shape=jax.ShapeDtypeStruct((b, s, 2 * B_WIDTH), F32),
        scratch_shapes=[pltpu.VMEM((B_HEADS, s, n_keys), F32),
                        pltpu.VMEM((B_HEADS // 2, 2 * s, n_keys), F32)],
        compiler_params=pltpu.CompilerParams(dimension_semantics=("arbitrary",)),
    )(att, cache_k, cache_v, mem_k, mem_v, gtab)


def _finish_kernel(x_ref, mr_ref, mbm_ref, w_ref, g_ref, b_ref, y_ref):
    o = _dot(mr_ref[...].astype(BF16), w_ref[0:R_WIDTH, :])
    o = o + _dot(mbm_ref[...].astype(BF16), w_ref[R_WIDTH:, :])
    hres = ALPHA * x_ref[...] + o
    mean = jnp.mean(hres, axis=-1, keepdims=True)
    hc = hres - mean
    var = jnp.mean(hc * hc, axis=-1, keepdims=True)
    y_ref[...] = hc * lax.rsqrt(var + LN_EPS) * g_ref[...] + b_ref[...]


def _finish(x, mix_r, mix_bm, w_out_bf16, ln_g, ln_b, tm):
    m = x.shape[0]
    rows = lambda width: pl.BlockSpec((tm, width), lambda i: (i, 0))
    full = lambda arr: pl.BlockSpec(arr.shape, lambda i: (0, 0))
    return pl.pallas_call(
        _finish_kernel,
        name=f"finish_{m}",
        grid=(m // tm,),
        in_specs=[rows(D_MODEL), rows(R_WIDTH), rows(2 * B_WIDTH), full(w_out_bf16), full(ln_g), full(ln_b)],
        out_specs=rows(D_MODEL),
        out_shape=jax.ShapeDtypeStruct((m, D_MODEL), F32),
        compiler_params=pltpu.CompilerParams(
            dimension_semantics=("arbitrary",), vmem_limit_bytes=VMEM_LIMIT),
    )(x, mix_r, mix_bm, w_out_bf16, ln_g, ln_b)


IN_SPLITS = ((0, SHIFT_WIDTH), (SHIFT_WIDTH, SHIFT_WIDTH + R_WIDTH), (SHIFT_WIDTH + R_WIDTH, IN_WIDTH))


def _rel_bias_row(table, rel0):
    n_hi = rel0 - REL_CLIP
    n_lo = BIAS_L - n_hi - (2 * REL_CLIP + 1)
    heads = table.shape[0]
    return jnp.concatenate([jnp.broadcast_to(table[:, 2 * REL_CLIP:], (heads, n_hi)), table[:, ::-1],
                            jnp.broadcast_to(table[:, 0:1], (heads, n_lo))], axis=1)


def kernel(x_prompt, x_sample, mem_prompt, state_shift, state_wkv, cache_band_k, cache_band_v,
           cache_mem_k, cache_mem_v, w_in, mu_shift, w0, w2, a0, a2, k_k, k_a, r_k, gn_g, gn_b,
           rel_bias, w_mem_kv, w_out, ln_g, ln_b):
    bp, t, _ = x_prompt.shape
    bs, s, _ = x_sample.shape
    depth = w_in.shape[0]
    assert depth == 1 and t % BAND_WINDOW == 0 and s <= CHUNK
    keep = min(BAND_WINDOW, t)
    l = 0

    w_in_b = w_in[l].astype(BF16)
    w_out_b = w_out[l].astype(BF16)
    w_mem_b = w_mem_kv[l].astype(BF16)
    row = lambda p: p.reshape(1, -1)
    wkv_params = (row(mu_shift[l]), row(w0[l]), w2[l], row(a0[l]), a2[l], row(k_k[l]), row(k_a[l]),
                  row(r_k[l]), row(gn_g[l]), row(gn_b[l]))
    table = rel_bias[l]
    r_rows = cache_band_k.shape[2]
    gtab_p = _rel_bias_row(table, BAND_WINDOW + CHUNK - 1)
    gtab_s = _rel_bias_row(table, r_rows + s - 1)

    xp = x_prompt.reshape(bp * t, D_MODEL)
    zs, gr, att = _proj(xp, w_in_b, IN_SPLITS, 256)
    zs = zs.reshape(bp, t, SHIFT_WIDTH)
    att = att.reshape(bp, t, ATT_WIDTH)
    memkv, = _proj(mem_prompt.reshape(bp * N_MEM, D_MODEL), w_mem_b, ((0, 2 * B_WIDTH),), 256)
    memkv = memkv.reshape(bp, N_MEM, 2 * B_WIDTH)
    mix_r, p_wkv = _wkv(zs, gr.reshape(bp, t, R_WIDTH), jnp.zeros((bp, 1, SHIFT_WIDTH), F32),
                        jnp.zeros((bp, R_HEADS, HEAD_DIM, HEAD_DIM), F32), wkv_params, bb_n=2, c=CHUNK, n_ch=WKV_CHUNKS_PER_STEP)
    mix_bm = _attn_prompt(att, memkv, gtab_p, tq=BAND_WINDOW)
    y_prompt = _finish(xp, mix_r.reshape(bp * t, R_WIDTH), mix_bm.reshape(bp * t, 2 * B_WIDTH),
                       w_out_b, row(ln_g[l]), row(ln_b[l]), 512).reshape(bp, t, D_MODEL)
    p_shift = zs[:, -1]
    p_bk = att[:, t - keep:, B_WIDTH:2 * B_WIDTH].reshape(bp, keep, B_HEADS, HEAD_DIM)
    p_bv = att[:, t - keep:, 2 * B_WIDTH:3 * B_WIDTH].reshape(bp, keep, B_HEADS, HEAD_DIM)
    p_mk = memkv[:, :, :B_WIDTH].reshape(bp, N_MEM, M_HEADS, HEAD_DIM)
    p_mv = memkv[:, :, B_WIDTH:].reshape(bp, N_MEM, M_HEADS, HEAD_DIM)

    xs = x_sample.reshape(bs * s, D_MODEL)
    zs_s, gr_s, att_s = _proj(xs, w_in_b, IN_SPLITS, bs * s)
    zs_s = zs_s.reshape(bs, s, SHIFT_WIDTH)
    att_s = att_s.reshape(bs, s, ATT_WIDTH)
    mix_r_s, s_wkv = _wkv(zs_s, gr_s.reshape(bs, s, R_WIDTH), state_shift[l][:, None, :], state_wkv[l],
                          wkv_params, bb_n=4, c=s, n_ch=1)
    mix_bm_s = _attn_sample(att_s,
                            cache_band_k[l].reshape(bs, r_rows, B_WIDTH),
                            cache_band_v[l].reshape(bs, r_rows, B_WIDTH),
                            cache_mem_k[l].reshape(bs, N_MEM, B_WIDTH),
                            cache_mem_v[l].reshape(bs, N_MEM, B_WIDTH), gtab_s, n_seq=2)
    y_sample = _finish(xs, mix_r_s.reshape(bs * s, R_WIDTH), mix_bm_s.reshape(bs * s, 2 * B_WIDTH),
                       w_out_b, row(ln_g[l]), row(ln_b[l]), bs * s).reshape(bs, s, D_MODEL)
    s_shift = zs_s[:, -1]
    s_bk = att_s[:, :, B_WIDTH:2 * B_WIDTH].reshape(bs, s, B_HEADS, HEAD_DIM)
    s_bv = att_s[:, :, 2 * B_WIDTH:3 * B_WIDTH].reshape(bs, s, B_HEADS, HEAD_DIM)

    st = lambda a: a[None]
    return (y_prompt, y_sample, st(p_shift), st(p_wkv), st(p_bk), st(p_bv), st(p_mk), st(p_mv),
            st(s_shift), st(s_wkv), st(s_bk), st(s_bv))
```

```python
import functools

import numpy as np
import jax
import jax.numpy as jnp
from jax import lax
from jax.experimental import pallas as pl
from jax.experimental.pallas import tpu as pltpu

F32 = jnp.float32
BF16 = jnp.bfloat16

D_MODEL = 1024
HEAD_DIM = 64
R_WIDTH = 512
R_HEADS = 8
LOW_RANK = 64
SHIFT_WIDTH = 3 * R_WIDTH + 2 * LOW_RANK
B_WIDTH = 256
B_HEADS = 4
M_HEADS = 4
N_MEM = 256
CHUNK = 64
BAND_CHUNKS = 8
BAND_WINDOW = BAND_CHUNKS * CHUNK
BAND_LEN = BAND_WINDOW + CHUNK
REL_CLIP = 128
ATT_WIDTH = 6 * B_WIDTH
IN_WIDTH = SHIFT_WIDTH + R_WIDTH + ATT_WIDTH
LN_EPS = 1e-5
GN_EPS = 64e-5
ALPHA = 2.0 ** 0.25
ATT_SCALE = HEAD_DIM ** -0.5
LOG2E = float(np.log2(np.e))
BIAS_L = 1024
PAIR = 2 * HEAD_DIM
ATTN_GROUP_CHUNKS = 2
MEM_ROWS = 128
MEM_GROUP_BLOCKS = 2
WKV_CHUNKS_PER_STEP = 4

VMEM_LIMIT = 48 * 1024 * 1024

NN = ((1,), (0,))
NT = ((1,), (1,))
TN = ((0,), (0,))


def _dot(a, b, dims=NN):
    return lax.dot_general(a, b, (dims, ((), ())), preferred_element_type=F32)


def _split2(x):
    hi = x.astype(BF16)
    lo = (x - hi.astype(F32)).astype(BF16)
    return hi, lo


def _mm3(a, b, dims=NN):
    ah, al = _split2(a)
    bh, bl = _split2(b)
    return _dot(ah, bh, dims) + _dot(ah, bl, dims) + _dot(al, bh, dims)


def _mm_exact_lhs(lhs_bf16, x):
    x1 = x.astype(BF16)
    r1 = x - x1.astype(F32)
    x2 = r1.astype(BF16)
    x3 = (r1 - x2.astype(F32)).astype(BF16)
    return _dot(lhs_bf16, x1) + _dot(lhs_bf16, x2) + _dot(lhs_bf16, x3)


def _sigmoid(x):
    return 1.0 / (1.0 + jnp.exp(-x))


def _silu(x):
    return x * _sigmoid(x)


def _proj_kernel(x_ref, w_ref, *out_refs, splits):
    x = x_ref[...].astype(BF16)
    for o_ref, (lo, hi) in zip(out_refs, splits):
        o_ref[...] = _dot(x, w_ref[:, lo:hi])


def _proj(x, w_bf16, splits, tm):
    m, k = x.shape
    n = w_bf16.shape[1]
    return pl.pallas_call(
        functools.partial(_proj_kernel, splits=splits),
        name=f"proj_{m}x{n}",
        grid=(m // tm,),
        in_specs=[pl.BlockSpec((tm, k), lambda i: (i, 0)),
                  pl.BlockSpec((k, n), lambda i: (0, 0))],
        out_specs=[pl.BlockSpec((tm, hi - lo), lambda i: (i, 0)) for lo, hi in splits],
        out_shape=[jax.ShapeDtypeStruct((m, hi - lo), F32) for lo, hi in splits],
        compiler_params=pltpu.CompilerParams(
            dimension_semantics=("arbitrary",), vmem_limit_bytes=VMEM_LIMIT),
    )(x, w_bf16)


def _pair_blocks(x2, left):
    zero = jnp.zeros((), x2.dtype)
    return jnp.concatenate([jnp.where(left, x2, zero), jnp.where(left, zero, x2)], axis=0)


def _pair_sum(x2, left):
    s0 = jnp.sum(jnp.where(left, x2, 0.0), axis=-1, keepdims=True)
    s1 = jnp.sum(jnp.where(left, 0.0, x2), axis=-1, keepdims=True)
    return jnp.where(left, s0, s1)


def _wkv_kernel(zs_ref, gr_ref, prev_ref, s0_ref, mu_ref, w0_ref, w2_ref, a0_ref, a2_ref,
                kk_ref, ka_ref, rk_ref, gng_ref, gnb_ref, out_ref, sfin_ref, carry_ref, state_ref,
                *, bb_n, c, n_ch):
    t = pl.program_id(1)
    tt = n_ch * c

    @pl.when(t == 0)
    def _():
        carry_ref[...] = prev_ref[...]
        for bb in range(bb_n):
            for h in range(R_HEADS):
                state_ref[bb, h // 2, :, (h % 2) * HEAD_DIM:(h % 2 + 1) * HEAD_DIM] = s0_ref[bb, h]

    row = lax.broadcasted_iota(jnp.int32, (tt, tt), 0)
    col = lax.broadcasted_iota(jnp.int32, (tt, tt), 1)
    shift_c = c.bit_length() - 1
    ltri = ((col <= row) & ((col >> shift_c) == (row >> shift_c))).astype(BF16)
    row2 = lax.broadcasted_iota(jnp.int32, (c, 2 * c), 0)
    col2 = lax.broadcasted_iota(jnp.int32, (c, 2 * c), 1) & (c - 1)
    eye2 = (row2 == col2).astype(BF16)
    strict2 = col2 < row2
    incl4 = ((lax.broadcasted_iota(jnp.int32, (c, 4 * c), 1) & (c - 1))
             <= lax.broadcasted_iota(jnp.int32, (c, 4 * c), 0))
    level_masks = []
    shift = 0
    while (1 << shift) < c:
        rb = row2 >> shift
        level_masks.append(((rb & 1) == 1) & ((col2 >> shift) == rb - 1))
        shift += 1
    left_s = lax.broadcasted_iota(jnp.int32, (1, 2 * c), 1) < c
    left = lax.broadcasted_iota(jnp.int32, (1, PAIR), 1) < HEAD_DIM
    first_row = lax.broadcasted_iota(jnp.int32, (tt, 1), 0) == 0
    zero = jnp.zeros((), BF16)

    mu = mu_ref[...]
    chunks = [[] for _ in range(n_ch)]
    for bb in range(bb_n):
        zs = zs_ref[bb]
        zprev = jnp.where(first_row, carry_ref[bb], pltpu.roll(zs, 1, 0))
        carry_ref[bb] = zs[tt - 1:tt, :]
        xs = zs + (zprev - zs) * mu
        r = xs[:, 0:R_WIDTH]
        k = xs[:, R_WIDTH:2 * R_WIDTH]
        v = xs[:, 2 * R_WIDTH:3 * R_WIDTH]
        wd = xs[:, 3 * R_WIDTH:3 * R_WIDTH + LOW_RANK]
        ad = xs[:, 3 * R_WIDTH + LOW_RANK:]

        u = -(w0_ref[...] + _mm3(jnp.tanh(wd), w2_ref[...]))
        softplus = jnp.maximum(u, 0.0) + jnp.log(1.0 + jnp.exp(-jnp.abs(u)))
        ld = -jnp.exp(-softplus - 0.5)
        cum = _mm_exact_lhs(ltri, ld)
        a = _sigmoid(a0_ref[...] + _mm3(ad, a2_ref[...]))

        ecum = jnp.exp(cum)
        einv = jnp.exp(-cum)
        eprev = jnp.exp(cum - ld)
        kkr = k * kk_ref[...]
        k2 = k * (1.0 + (a - 1.0) * ka_ref[...])
        rt = r * ecum
        kh = k2 * einv
        rk2 = r * k2 * rk_ref[...]
        gate = _silu(gr_ref[bb])

        for pr in range(R_HEADS // 2):
            sl = slice(pr * PAIR, (pr + 1) * PAIR)
            kkr_p = kkr[:, sl]
            nrm = jnp.sqrt(_pair_sum(kkr_p * kkr_p, left))
            kkn = kkr_p / jnp.maximum(nrm, 1e-12)
            at = -kkn * eprev[:, sl]
            bh = kkn * a[:, sl] * einv[:, sl]
            bonus = _pair_sum(rk2[:, sl], left) * v[:, sl]
            for ch in range(n_ch):
                rows = slice(ch * c, (ch + 1) * c)
                pc = ecum[(ch + 1) * c - 1:(ch + 1) * c, sl]
                v16 = v[rows, sl].astype(BF16)
                bh_c = bh[rows]
                kh_c = kh[rows, sl]
                chunks[ch].append(dict(
                    bb=bb, pr=pr, sl=sl, rows=rows, pc=pc, gate=gate[rows, sl], bonus=bonus[rows],
                    v_bd=_pair_blocks(v16, left), v16=v16,
                    ar=jnp.concatenate([at[rows], rt[rows, sl]], axis=0).astype(BF16),
                    bk_bd=jnp.concatenate([_pair_blocks(bh_c.astype(BF16), left),
                                           _pair_blocks(kh_c.astype(BF16), left)], axis=0),
                    bkp=jnp.concatenate([bh_c * pc, kh_c * pc], axis=0).astype(BF16)))

    def st_a4(units):
        for un in units:
            un["a4"] = _dot(un["ar"], un["bk_bd"], NT).astype(BF16)
            un["a_ab"] = un["a4"][:c, :2 * c]
            un["tinv"] = jnp.where(level_masks[0], un["a_ab"], eye2)

    def st_lt(mask):
        def run(units):
            for un in units:
                un["lt"] = _dot(jnp.where(mask, un["a_ab"], zero), _pair_blocks(un["tinv"], left_s)).astype(BF16)
        return run

    def st_tinv(mask):
        def run(units):
            for un in units:
                new = _dot(un["tinv"], _pair_blocks(un["lt"], left_s)).astype(BF16)
                un["tinv"] = jnp.where(mask, new, un["tinv"])
        return run

    def st_akv(units):
        for un in units:
            un["akv"] = _dot(jnp.where(strict2, un["a4"][:c, 2 * c:], zero), un["v_bd"])

    def st_ars(units):
        for un in units:
            un["s0"] = state[un["bb"], un["pr"]]
            un["ars"] = _dot(un["ar"], _pair_blocks(un["s0"].astype(BF16), left), NT)

    def st_pm(units):
        for un in units:
            rhs = (un["ars"][:c] + un["akv"]).astype(BF16)
            un["pm"] = _dot(un["tinv"], _pair_blocks(rhs, left)).astype(BF16)

    def st_state(units):
        for un in units:
            pv = jnp.concatenate([un["pm"], un["v16"]], axis=0)
            cross = _dot(pv, un["bkp"], TN)
            state[un["bb"], un["pr"]] = (un["s0"] * un["pc"]
                                         + jnp.where(left, cross[:HEAD_DIM], cross[HEAD_DIM:]))

    def st_y(units):
        for un in units:
            pv_bd = jnp.concatenate([_pair_blocks(un["pm"], left), un["v_bd"]], axis=0)
            y = un["ars"][c:] + _dot(jnp.where(incl4, un["a4"][c:], zero), pv_bd)
            mean = _pair_sum(y, left) * (1.0 / HEAD_DIM)
            yc = y - mean
            var = _pair_sum(yc * yc, left) * (1.0 / HEAD_DIM)
            yn = yc * lax.rsqrt(var + GN_EPS) * gng_ref[:, un["sl"]] + gnb_ref[:, un["sl"]]
            out_ref[un["bb"], un["rows"], un["sl"]] = (yn + un["bonus"]) * un["gate"]

    free_stages = [st_a4]
    for mask in level_masks[1:]:
        free_stages += [st_lt(mask), st_tinv(mask)]
    free_stages.append(st_akv)
    state_stages = [st_ars, st_pm, st_state, st_y]

    state = {(bb, pr): state_ref[bb, pr] for bb in range(bb_n) for pr in range(R_HEADS // 2)}
    for stage in free_stages:
        stage(chunks[0])
    for ch in range(n_ch):
        ahead = free_stages if ch + 1 < n_ch else []
        per_slot = -(-len(ahead) // len(state_stages))
        for i, stage in enumerate(state_stages):
            stage(chunks[ch])
            for nxt in ahead[i * per_slot:(i + 1) * per_slot]:
                nxt(chunks[ch + 1])
    for (bb, pr), val in state.items():
        state_ref[bb, pr] = val

    @pl.when(t == pl.num_programs(1) - 1)
    def _():
        for bb in range(bb_n):
            for h in range(R_HEADS):
                sfin_ref[bb, h] = state_ref[bb, h // 2, :, (h % 2) * HEAD_DIM:(h % 2 + 1) * HEAD_DIM]


def _wkv(zs, gr, prev, s0, params, *, bb_n, c, n_ch):
    b, t, _ = zs.shape
    tt = n_ch * c
    full = lambda arr: pl.BlockSpec(arr.shape, lambda i, j: (0,) * arr.ndim)
    return pl.pallas_call(
        functools.partial(_wkv_kernel, bb_n=bb_n, c=c, n_ch=n_ch),
        name=f"wkv_c{c}",
        grid=(b // bb_n, t // tt),
        in_specs=[pl.BlockSpec((bb_n, tt, SHIFT_WIDTH), lambda i, j: (i, j, 0)),
                  pl.BlockSpec((bb_n, tt, R_WIDTH), lambda i, j: (i, j, 0)),
                  pl.BlockSpec((bb_n, 1, SHIFT_WIDTH), lambda i, j: (i, 0, 0)),
                  pl.BlockSpec((bb_n, R_HEADS, HEAD_DIM, HEAD_DIM), lambda i, j: (i, 0, 0, 0))]
                 + [full(p) for p in params],
        out_specs=[pl.BlockSpec((bb_n, tt, R_WIDTH), lambda i, j: (i, j, 0)),
                   pl.BlockSpec((bb_n, R_HEADS, HEAD_DIM, HEAD_DIM), lambda i, j: (i, 0, 0, 0))],
        out_shape=[jax.ShapeDtypeStruct((b, t, R_WIDTH), F32),
                   jax.ShapeDtypeStruct((b, R_HEADS, HEAD_DIM, HEAD_DIM), F32)],
        scratch_shapes=[pltpu.VMEM((bb_n, 1, SHIFT_WIDTH), F32),
                        pltpu.VMEM((bb_n, R_HEADS // 2, HEAD_DIM, PAIR), F32)],
        compiler_params=pltpu.CompilerParams(
            dimension_semantics=("arbitrary", "arbitrary"), vmem_limit_bytes=VMEM_LIMIT),
    )(zs, gr, prev, s0, *params)


def _fill_rel_bias(bias_ref, gtab_ref, offset):
    heads, nq, nk = bias_ref.shape
    for h in range(heads):
        g = jnp.broadcast_to(gtab_ref[h:h + 1, :], (nq, BIAS_L))
        bias_ref[h] = pltpu.roll(g, BIAS_L - offset, 1, stride=1, stride_axis=0)[:, :nk]


def _stack_heads(q2, left):
    zero = jnp.zeros((), q2.dtype)
    return jnp.concatenate([jnp.where(left, q2, zero), jnp.where(left, zero, q2)], axis=0)


def _attend_pairs(jobs, left):
    for jb in jobs:
        s = _dot(jb["lhs"], jb["k"], NT)
        if jb.get("bias") is not None:
            s = s + jb["bias"]
        jb["s"] = s
    for jb in jobs:
        m = jnp.max(jb["s"], axis=-1, keepdims=True)
        p = jnp.exp2(jb["s"] - m)
        jb["l"] = jnp.sum(p, axis=-1, keepdims=True)
        jb["p"] = p.astype(BF16)
    outs = []
    for jb in jobs:
        o2 = _dot(jb["p"], jb["v"]) / jb["l"]
        n = o2.shape[0] // 2
        outs.append(jnp.where(left, o2[:n], o2[n:]))
    return outs


def _layer_norm(h, g, b):
    mean = jnp.mean(h, axis=-1, keepdims=True)
    hc = h - mean
    var = jnp.mean(hc * hc, axis=-1, keepdims=True)
    return hc * lax.rsqrt(var + LN_EPS) * g + b


def _attn_prompt_kernel(q_ref, kp_ref, kc_ref, vp_ref, vc_ref, gb_ref, mq_ref, gm_ref, mk_ref, mv_ref, gtab_ref,
                        x_ref, mr_ref, w_ref, lng_ref, lnb_ref, y_ref,
                        kcat_ref, vcat_ref, bias_ref, biasv_ref, mix_ref, *, tq):
    j = pl.program_id(1)
    n_chunks = tq // CHUNK

    @pl.when((pl.program_id(0) == 0) & (j == 0))
    def _():
        _fill_rel_bias(bias_ref, gtab_ref, CHUNK - 1)
        kcol = lax.broadcasted_iota(jnp.int32, (1, BAND_LEN), 1)
        for h in range(B_HEADS):
            rows = slice((h % 2) * CHUNK, (h % 2 + 1) * CHUNK)
            scaled = bias_ref[h] * LOG2E
            biasv_ref[0, h // 2, rows, :] = scaled
            for i in range(n_chunks):
                biasv_ref[1 + i, h // 2, rows, :] = jnp.where(kcol >= BAND_WINDOW - i * CHUNK, scaled, -jnp.inf)

    kcat_ref[0:BAND_WINDOW] = kp_ref[0].astype(BF16)
    kcat_ref[BAND_WINDOW:] = kc_ref[0].astype(BF16)
    vcat_ref[0:BAND_WINDOW] = vp_ref[0].astype(BF16)
    vcat_ref[BAND_WINDOW:] = vc_ref[0].astype(BF16)
    q = (q_ref[0] * (ATT_SCALE * LOG2E)).astype(BF16)
    gate_b = _silu(gb_ref[0])
    left = lax.broadcasted_iota(jnp.int32, (1, PAIR), 1) < HEAD_DIM
    for i0 in range(0, n_chunks, ATTN_GROUP_CHUNKS):
        jobs = []
        for i in range(i0, i0 + ATTN_GROUP_CHUNKS):
            rows = slice(i * CHUNK, (i + 1) * CHUNK)
            keys = slice(i * CHUNK, i * CHUNK + BAND_LEN)
            variant = jnp.where(j == 0, 1 + i, 0)
            for pr in range(B_HEADS // 2):
                lanes = slice(pr * PAIR, (pr + 1) * PAIR)
                jobs.append(dict(rows=rows, lanes=lanes, lhs=_stack_heads(q[rows, lanes], left),
                                 k=kcat_ref[keys, lanes], v=vcat_ref[keys, lanes],
                                 bias=biasv_ref[variant, pr]))
        for jb, o in zip(jobs, _attend_pairs(jobs, left)):
            mix_ref[jb["rows"], jb["lanes"]] = (o * gate_b[jb["rows"], jb["lanes"]]).astype(BF16)

    mq = (mq_ref[0] * (ATT_SCALE * LOG2E)).astype(BF16)
    gate_m = _silu(gm_ref[0])
    mk = mk_ref[0].astype(BF16)
    mv = mv_ref[0].astype(BF16)
    for r0 in range(0, tq, MEM_ROWS * MEM_GROUP_BLOCKS):
        jobs = []
        for r in range(r0, r0 + MEM_ROWS * MEM_GROUP_BLOCKS, MEM_ROWS):
            rows = slice(r, r + MEM_ROWS)
            for pr in range(M_HEADS // 2):
                lanes = slice(pr * PAIR, (pr + 1) * PAIR)
                jobs.append(dict(rows=rows, lanes=lanes, lhs=_stack_heads(mq[rows, lanes], left),
                                 k=mk[:, lanes], v=mv[:, lanes]))
        for jb, o in zip(jobs, _attend_pairs(jobs, left)):
            mix_ref[jb["rows"], B_WIDTH + jb["lanes"].start:B_WIDTH + jb["lanes"].stop] = (
                o * gate_m[jb["rows"], jb["lanes"]]).astype(BF16)

    o = _dot(mr_ref[0].astype(BF16), w_ref[0:R_WIDTH, :]) + _dot(mix_ref[...], w_ref[R_WIDTH:, :])
    y_ref[0] = _layer_norm(ALPHA * x_ref[0] + o, lng_ref[...], lnb_ref[...])


def _attn_prompt(att, memkv, gtab, x, mix_r, w_out_bf16, ln_g, ln_b, *, tq):
    b, t, _ = att.shape
    col = lambda cidx: pl.BlockSpec((1, tq, B_WIDTH), lambda i, j: (i, j, cidx))
    prev = lambda cidx: pl.BlockSpec((1, tq, B_WIDTH), lambda i, j: (i, jnp.maximum(j - 1, 0), cidx))
    tile = lambda width: pl.BlockSpec((1, tq, width), lambda i, j: (i, j, 0))
    full = lambda arr: pl.BlockSpec(arr.shape, lambda i, j: (0, 0))
    return pl.pallas_call(
        functools.partial(_attn_prompt_kernel, tq=tq),
        name="attn_prompt",
        grid=(b, t // tq),
        in_specs=[col(0), prev(1), col(1), prev(2), col(2), col(3), col(4), col(5),
                  pl.BlockSpec((1, N_MEM, B_WIDTH), lambda i, j: (i, 0, 0)),
                  pl.BlockSpec((1, N_MEM, B_WIDTH), lambda i, j: (i, 0, 1)),
                  full(gtab), tile(D_MODEL), tile(R_WIDTH), full(w_out_bf16), full(ln_g), full(ln_b)],
        out_specs=tile(D_MODEL),
        out_shape=jax.ShapeDtypeStruct((b, t, D_MODEL), F32),
        scratch_shapes=[pltpu.VMEM((2 * BAND_WINDOW, B_WIDTH), BF16),
                        pltpu.VMEM((2 * BAND_WINDOW, B_WIDTH), BF16),
                        pltpu.VMEM((B_HEADS, CHUNK, BAND_LEN), F32),
                        pltpu.VMEM((1 + tq // CHUNK, B_HEADS // 2, 2 * CHUNK, BAND_LEN), F32),
                        pltpu.VMEM((tq, 2 * B_WIDTH), BF16)],
        compiler_params=pltpu.CompilerParams(
            dimension_semantics=("arbitrary", "arbitrary"), vmem_limit_bytes=VMEM_LIMIT),
    )(att, att, att, att, att, att, att, att, memkv, memkv, gtab, x, mix_r, w_out_bf16, ln_g, ln_b)


def _attn_sample_kernel(att_ref, ck_ref, cv_ref, mk_ref, mv_ref, gtab_ref, out_ref, bias_ref, bias2_ref):
    n_seq, n_new, _ = att_ref.shape

    @pl.when(pl.program_id(0) == 0)
    def _():
        _fill_rel_bias(bias_ref, gtab_ref, n_new - 1)
        for h in range(B_HEADS):
            bias2_ref[h // 2, (h % 2) * n_new:(h % 2 + 1) * n_new, :] = bias_ref[h] * LOG2E

    left = lax.broadcasted_iota(jnp.int32, (1, PAIR), 1) < HEAD_DIM
    jobs = []
    for b in range(n_seq):
        att = att_ref[b]
        q = (att[:, 0:B_WIDTH] * (ATT_SCALE * LOG2E)).astype(BF16)
        k_all = jnp.concatenate([ck_ref[b].astype(BF16), att[:, B_WIDTH:2 * B_WIDTH].astype(BF16)], axis=0)
        v_all = jnp.concatenate([cv_ref[b].astype(BF16), att[:, 2 * B_WIDTH:3 * B_WIDTH].astype(BF16)], axis=0)
        gate_b = _silu(att[:, 3 * B_WIDTH:4 * B_WIDTH])
        mq = (att[:, 4 * B_WIDTH:5 * B_WIDTH] * (ATT_SCALE * LOG2E)).astype(BF16)
        gate_m = _silu(att[:, 5 * B_WIDTH:6 * B_WIDTH])
        mk = mk_ref[b].astype(BF16)
        mv = mv_ref[b].astype(BF16)
        for pr in range(B_HEADS // 2):
            lanes = slice(pr * PAIR, (pr + 1) * PAIR)
            jobs.append(dict(b=b, out=lanes, gate=gate_b[:, lanes], lhs=_stack_heads(q[:, lanes], left),
                             k=k_all[:, lanes], v=v_all[:, lanes], bias=bias2_ref[pr]))
        for pr in range(M_HEADS // 2):
            lanes = slice(pr * PAIR, (pr + 1) * PAIR)
            jobs.append(dict(b=b, out=slice(B_WIDTH + lanes.start, B_WIDTH + lanes.stop), gate=gate_m[:, lanes],
                             lhs=_stack_heads(mq[:, lanes], left), k=mk[:, lanes], v=mv[:, lanes]))
    for jb, o in zip(jobs, _attend_pairs(jobs, left)):
        out_ref[jb["b"], :, jb["out"]] = o * jb["gate"]


def _attn_sample(att, cache_k, cache_v, mem_k, mem_v, gtab, *, n_seq):
    b, s, _ = att.shape
    per_b = lambda arr: pl.BlockSpec((n_seq,) + arr.shape[1:], lambda i: (i, 0, 0))
    n_keys = cache_k.shape[1] + s
    return pl.pallas_call(
        _attn_sample_kernel,
        name="attn_sample",
        grid=(b // n_seq,),
        in_specs=[per_b(att), per_b(cache_k), per_b(cache_v), per_b(mem_k), per_b(mem_v),
                  pl.BlockSpec(gtab.shape, lambda i: (0, 0))],
        out_specs=pl.BlockSpec((n_seq, s, 2 * B_WIDTH), lambda i: (i, 0, 0)),
        out_shape=jax.ShapeDtypeStruct((b, s, 2 * B_WIDTH), F32),
        scratch_shapes=[pltpu.VMEM((B_HEADS, s, n_keys), F32),
                        pltpu.VMEM((B_HEADS // 2, 2 * s, n_keys), F32)],
        compiler_params=pltpu.CompilerParams(dimension_semantics=("arbitrary",)),
    )(att, cache_k, cache_v, mem_k, mem_v, gtab)


def _finish_kernel(x_ref, mr_ref, mbm_ref, w_ref, g_ref, b_ref, y_ref):
    o = _dot(mr_ref[...].astype(BF16), w_ref[0:R_WIDTH, :])
    o = o + _dot(mbm_ref[...].astype(BF16), w_ref[R_WIDTH:, :])
    y_ref[...] = _layer_norm(ALPHA * x_ref[...] + o, g_ref[...], b_ref[...])


def _finish(x, mix_r, mix_bm, w_out_bf16, ln_g, ln_b, tm):
    m = x.shape[0]
    rows = lambda width: pl.BlockSpec((tm, width), lambda i: (i, 0))
    full = lambda arr: pl.BlockSpec(arr.shape, lambda i: (0, 0))
    return pl.pallas_call(
        _finish_kernel,
        name=f"finish_{m}",
        grid=(m // tm,),
        in_specs=[rows(D_MODEL), rows(R_WIDTH), rows(2 * B_WIDTH), full(w_out_bf16), full(ln_g), full(ln_b)],
        out_specs=rows(D_MODEL),
        out_shape=jax.ShapeDtypeStruct((m, D_MODEL), F32),
        compiler_params=pltpu.CompilerParams(
            dimension_semantics=("arbitrary",), vmem_limit_bytes=VMEM_LIMIT),
    )(x, mix_r, mix_bm, w_out_bf16, ln_g, ln_b)


IN_SPLITS = ((0, SHIFT_WIDTH), (SHIFT_WIDTH, SHIFT_WIDTH + R_WIDTH), (SHIFT_WIDTH + R_WIDTH, IN_WIDTH))


def _rel_bias_row(table, rel0):
    n_hi = rel0 - REL_CLIP
    n_lo = BIAS_L - n_hi - (2 * REL_CLIP + 1)
    heads = table.shape[0]
    return jnp.concatenate([jnp.broadcast_to(table[:, 2 * REL_CLIP:], (heads, n_hi)), table[:, ::-1],
                            jnp.broadcast_to(table[:, 0:1], (heads, n_lo))], axis=1)


def kernel(x_prompt, x_sample, mem_prompt, state_shift, state_wkv, cache_band_k, cache_band_v,
           cache_mem_k, cache_mem_v, w_in, mu_shift, w0, w2, a0, a2, k_k, k_a, r_k, gn_g, gn_b,
           rel_bias, w_mem_kv, w_out, ln_g, ln_b):
    bp, t, _ = x_prompt.shape
    bs, s, _ = x_sample.shape
    depth = w_in.shape[0]
    assert depth == 1 and t % BAND_WINDOW == 0 and s <= CHUNK
    keep = min(BAND_WINDOW, t)
    l = 0

    w_in_b = w_in[l].astype(BF16)
    w_out_b = w_out[l].astype(BF16)
    w_mem_b = w_mem_kv[l].astype(BF16)
    row = lambda p: p.reshape(1, -1)
    wkv_params = (row(mu_shift[l]), row(w0[l]), w2[l], row(a0[l]), a2[l], row(k_k[l]), row(k_a[l]),
                  row(r_k[l]), row(gn_g[l]), row(gn_b[l]))
    table = rel_bias[l]
    r_rows = cache_band_k.shape[2]
    gtab_p = _rel_bias_row(table, BAND_WINDOW + CHUNK - 1)
    gtab_s = _rel_bias_row(table, r_rows + s - 1)

    xp = x_prompt.reshape(bp * t, D_MODEL)
    zs, gr, att = _proj(xp, w_in_b, IN_SPLITS, 256)
    zs = zs.reshape(bp, t, SHIFT_WIDTH)
    att = att.reshape(bp, t, ATT_WIDTH)
    memkv, = _proj(mem_prompt.reshape(bp * N_MEM, D_MODEL), w_mem_b, ((0, 2 * B_WIDTH),), 256)
    memkv = memkv.reshape(bp, N_MEM, 2 * B_WIDTH)
    mix_r, p_wkv = _wkv(zs, gr.reshape(bp, t, R_WIDTH), jnp.zeros((bp, 1, SHIFT_WIDTH), F32),
                        jnp.zeros((bp, R_HEADS, HEAD_DIM, HEAD_DIM), F32), wkv_params, bb_n=2, c=CHUNK, n_ch=WKV_CHUNKS_PER_STEP)
    y_prompt = _attn_prompt(att, memkv, gtab_p, x_prompt, mix_r, w_out_b, row(ln_g[l]), row(ln_b[l]),
                            tq=BAND_WINDOW)
    p_shift = zs[:, -1]
    p_bk = att[:, t - keep:, B_WIDTH:2 * B_WIDTH].reshape(bp, keep, B_HEADS, HEAD_DIM)
    p_bv = att[:, t - keep:, 2 * B_WIDTH:3 * B_WIDTH].reshape(bp, keep, B_HEADS, HEAD_DIM)
    p_mk = memkv[:, :, :B_WIDTH].reshape(bp, N_MEM, M_HEADS, HEAD_DIM)
    p_mv = memkv[:, :, B_WIDTH:].reshape(bp, N_MEM, M_HEADS, HEAD_DIM)

    xs = x_sample.reshape(bs * s, D_MODEL)
    zs_s, gr_s, att_s = _proj(xs, w_in_b, IN_SPLITS, bs * s)
    zs_s = zs_s.reshape(bs, s, SHIFT_WIDTH)
    att_s = att_s.reshape(bs, s, ATT_WIDTH)
    mix_r_s, s_wkv = _wkv(zs_s, gr_s.reshape(bs, s, R_WIDTH), state_shift[l][:, None, :], state_wkv[l],
                          wkv_params, bb_n=4, c=s, n_ch=1)
    mix_bm_s = _attn_sample(att_s,
                            cache_band_k[l].reshape(bs, r_rows, B_WIDTH),
                            cache_band_v[l].reshape(bs, r_rows, B_WIDTH),
                            cache_mem_k[l].reshape(bs, N_MEM, B_WIDTH),
                            cache_mem_v[l].reshape(bs, N_MEM, B_WIDTH), gtab_s, n_seq=2)
    y_sample = _finish(xs, mix_r_s.reshape(bs * s, R_WIDTH), mix_bm_s.reshape(bs * s, 2 * B_WIDTH),
                       w_out_b, row(ln_g[l]), row(ln_b[l]), bs * s).reshape(bs, s, D_MODEL)
    s_shift = zs_s[:, -1]
    s_bk = att_s[:, :, B_WIDTH:2 * B_WIDTH].reshape(bs, s, B_HEADS, HEAD_DIM)
    s_bv = att_s[:, :, 2 * B_WIDTH:3 * B_WIDTH].reshape(bs, s, B_HEADS, HEAD_DIM)

    st = lambda a: a[None]
    return (y_prompt, y_sample, st(p_shift), st(p_wkv), st(p_bk), st(p_bv), st(p_mk), st(p_mv),
            st(s_shift), st(s_wkv), st(s_bk), st(s_bv))
```

```python
import functools

import numpy as np
import jax
import jax.numpy as jnp
from jax import lax
from jax.experimental import pallas as pl
from jax.experimental.pallas import tpu as pltpu

F32 = jnp.float32
BF16 = jnp.bfloat16

D_MODEL = 1024
HEAD_DIM = 64
R_WIDTH = 512
R_HEADS = 8
LOW_RANK = 64
SHIFT_WIDTH = 3 * R_WIDTH + 2 * LOW_RANK
B_WIDTH = 256
B_HEADS = 4
M_HEADS = 4
N_MEM = 256
CHUNK = 64
BAND_CHUNKS = 8
BAND_WINDOW = BAND_CHUNKS * CHUNK
BAND_LEN = BAND_WINDOW + CHUNK
REL_CLIP = 128
ATT_WIDTH = 6 * B_WIDTH
IN_WIDTH = SHIFT_WIDTH + R_WIDTH + ATT_WIDTH
LN_EPS = 1e-5
GN_EPS = 64e-5
ALPHA = 2.0 ** 0.25
ATT_SCALE = HEAD_DIM ** -0.5
LOG2E = float(np.log2(np.e))
BIAS_L = 1024
PAIR = 2 * HEAD_DIM
ATTN_GROUP_CHUNKS = 2
MEM_ROWS = 128
MEM_GROUP_BLOCKS = 2
WKV_CHUNKS_PER_STEP = 4
WKV_GROUP_CHUNKS = 2

VMEM_LIMIT = 48 * 1024 * 1024

NN = ((1,), (0,))
NT = ((1,), (1,))
TN = ((0,), (0,))


def _dot(a, b, dims=NN):
    return lax.dot_general(a, b, (dims, ((), ())), preferred_element_type=F32)


def _split2(x):
    hi = x.astype(BF16)
    lo = (x - hi.astype(F32)).astype(BF16)
    return hi, lo


def _mm3(a, b, dims=NN):
    ah, al = _split2(a)
    bh, bl = _split2(b)
    return _dot(ah, bh, dims) + _dot(ah, bl, dims) + _dot(al, bh, dims)


def _mm_exact_lhs(lhs_bf16, x):
    x1 = x.astype(BF16)
    r1 = x - x1.astype(F32)
    x2 = r1.astype(BF16)
    x3 = (r1 - x2.astype(F32)).astype(BF16)
    return _dot(lhs_bf16, x1) + _dot(lhs_bf16, x2) + _dot(lhs_bf16, x3)


def _sigmoid(x):
    return 1.0 / (1.0 + jnp.exp(-x))


def _silu(x):
    return x * _sigmoid(x)


def _proj_kernel(x_ref, w_ref, *out_refs, splits):
    x = x_ref[...].astype(BF16)
    for o_ref, (lo, hi) in zip(out_refs, splits):
        o_ref[...] = _dot(x, w_ref[:, lo:hi])


def _proj(x, w_bf16, splits, tm):
    m, k = x.shape
    n = w_bf16.shape[1]
    return pl.pallas_call(
        functools.partial(_proj_kernel, splits=splits),
        name=f"proj_{m}x{n}",
        grid=(m // tm,),
        in_specs=[pl.BlockSpec((tm, k), lambda i: (i, 0)),
                  pl.BlockSpec((k, n), lambda i: (0, 0))],
        out_specs=[pl.BlockSpec((tm, hi - lo), lambda i: (i, 0)) for lo, hi in splits],
        out_shape=[jax.ShapeDtypeStruct((m, hi - lo), F32) for lo, hi in splits],
        compiler_params=pltpu.CompilerParams(
            dimension_semantics=("arbitrary",), vmem_limit_bytes=VMEM_LIMIT),
    )(x, w_bf16)


def _pair_blocks(x2, left):
    zero = jnp.zeros((), x2.dtype)
    return jnp.concatenate([jnp.where(left, x2, zero), jnp.where(left, zero, x2)], axis=0)


def _pair_sum(x2, left):
    s0 = jnp.sum(jnp.where(left, x2, 0.0), axis=-1, keepdims=True)
    s1 = jnp.sum(jnp.where(left, 0.0, x2), axis=-1, keepdims=True)
    return jnp.where(left, s0, s1)


def _wkv_kernel(zs_ref, gr_ref, prev_ref, s0_ref, mu_ref, w0_ref, w2_ref, a0_ref, a2_ref,
                kk_ref, ka_ref, rk_ref, gng_ref, gnb_ref, out_ref, sfin_ref, carry_ref, state_ref,
                *, bb_n, c, n_ch, n_grp):
    t = pl.program_id(1)
    tt = n_ch * c

    @pl.when(t == 0)
    def _():
        carry_ref[...] = prev_ref[...]
        for bb in range(bb_n):
            for h in range(R_HEADS):
                state_ref[bb, h // 2, :, (h % 2) * HEAD_DIM:(h % 2 + 1) * HEAD_DIM] = s0_ref[bb, h]

    row = lax.broadcasted_iota(jnp.int32, (c, c), 0)
    col = lax.broadcasted_iota(jnp.int32, (c, c), 1)
    ltri = (col <= row).astype(BF16)
    row2 = lax.broadcasted_iota(jnp.int32, (c, 2 * c), 0)
    col2 = lax.broadcasted_iota(jnp.int32, (c, 2 * c), 1) & (c - 1)
    eye2 = (row2 == col2).astype(BF16)
    strict2 = col2 < row2
    incl4 = ((lax.broadcasted_iota(jnp.int32, (c, 4 * c), 1) & (c - 1))
             <= lax.broadcasted_iota(jnp.int32, (c, 4 * c), 0))
    level_masks = []
    shift = 0
    while (1 << shift) < c:
        rb = row2 >> shift
        level_masks.append(((rb & 1) == 1) & ((col2 >> shift) == rb - 1))
        shift += 1
    left_s = lax.broadcasted_iota(jnp.int32, (1, 2 * c), 1) < c
    left = lax.broadcasted_iota(jnp.int32, (1, PAIR), 1) < HEAD_DIM
    first_row = lax.broadcasted_iota(jnp.int32, (c, 1), 0) == 0
    zero = jnp.zeros((), BF16)

    mu = mu_ref[...]
    chunks = [[] for _ in range(n_ch)]
    prev_rows = [carry_ref[bb] for bb in range(bb_n)]
    for bb in range(bb_n):
        carry_ref[bb] = zs_ref[bb, tt - 1:tt, :]

    def prep_tasks(ch):
        rows = slice(ch * c, (ch + 1) * c)
        tasks = []
        for bb in range(bb_n):
            wide = {}

            def full_width(bb=bb, wide=wide):
                zs = zs_ref[bb, rows, :]
                before = prev_rows[bb] if ch == 0 else zs_ref[bb, ch * c - 1:ch * c, :]
                zprev = jnp.where(first_row, before, pltpu.roll(zs, 1, 0))
                xs = zs + (zprev - zs) * mu
                r = xs[:, 0:R_WIDTH]
                k = xs[:, R_WIDTH:2 * R_WIDTH]
                wd = xs[:, 3 * R_WIDTH:3 * R_WIDTH + LOW_RANK]
                ad = xs[:, 3 * R_WIDTH + LOW_RANK:]
                u = -(w0_ref[...] + _mm3(jnp.tanh(wd), w2_ref[...]))
                softplus = jnp.maximum(u, 0.0) + jnp.log(1.0 + jnp.exp(-jnp.abs(u)))
                ld = -jnp.exp(-softplus - 0.5)
                cum = _mm_exact_lhs(ltri, ld)
                a = _sigmoid(a0_ref[...] + _mm3(ad, a2_ref[...]))
                ecum = jnp.exp(cum)
                k2 = k * (1.0 + (a - 1.0) * ka_ref[...])
                wide.update(v=xs[:, 2 * R_WIDTH:3 * R_WIDTH], a=a, ecum=ecum, einv=jnp.exp(-cum),
                            eprev=jnp.exp(cum - ld), kkr=k * kk_ref[...], rt=r * ecum, k2=k2,
                            rk2=r * k2 * rk_ref[...], gate=_silu(gr_ref[bb, rows, :]))

            def pair(pr, bb=bb, wide=wide):
                sl = slice(pr * PAIR, (pr + 1) * PAIR)
                kkr_p = wide["kkr"][:, sl]
                nrm = jnp.sqrt(_pair_sum(kkr_p * kkr_p, left))
                kkn = kkr_p / jnp.maximum(nrm, 1e-12)
                at = -kkn * wide["eprev"][:, sl]
                bh = kkn * wide["a"][:, sl] * wide["einv"][:, sl]
                kh = wide["k2"][:, sl] * wide["einv"][:, sl]
                pc = wide["ecum"][c - 1:c, sl]
                v_p = wide["v"][:, sl]
                v16 = v_p.astype(BF16)
                chunks[ch].append(dict(
                    bb=bb, pr=pr, sl=sl, rows=rows, pc=pc, gate=wide["gate"][:, sl],
                    bonus=_pair_sum(wide["rk2"][:, sl], left) * v_p, v_bd=_pair_blocks(v16, left), v16=v16,
                    ar=jnp.concatenate([at, wide["rt"][:, sl]], axis=0).astype(BF16),
                    bk_bd=jnp.concatenate([_pair_blocks(bh.astype(BF16), left),
                                           _pair_blocks(kh.astype(BF16), left)], axis=0),
                    bkp=jnp.concatenate([bh * pc, kh * pc], axis=0).astype(BF16)))

            tasks.append(full_width)
            tasks += [functools.partial(pair, pr) for pr in range(R_HEADS // 2)]
        return tasks

    def st_a4(units):
        for un in units:
            un["a4"] = _dot(un["ar"], un["bk_bd"], NT).astype(BF16)
            un["a_ab"] = un["a4"][:c, :2 * c]
            un["tinv"] = jnp.where(level_masks[0], un["a_ab"], eye2)

    def st_lt(mask):
        def run(units):
            for un in units:
                un["lt"] = _dot(jnp.where(mask, un["a_ab"], zero), _pair_blocks(un["tinv"], left_s)).astype(BF16)
        return run

    def st_tinv(mask):
        def run(units):
            for un in units:
                new = _dot(un["tinv"], _pair_blocks(un["lt"], left_s)).astype(BF16)
                un["tinv"] = jnp.where(mask, new, un["tinv"])
        return run

    def st_akv(units):
        for un in units:
            un["akv"] = _dot(jnp.where(strict2, un["a4"][:c, 2 * c:], zero), un["v_bd"])

    def st_ars(units):
        for un in units:
            un["s0"] = state[un["bb"], un["pr"]]
            un["ars"] = _dot(un["ar"], _pair_blocks(un["s0"].astype(BF16), left), NT)

    def st_pm(units):
        for un in units:
            rhs = (un["ars"][:c] + un["akv"]).astype(BF16)
            un["pm"] = _dot(un["tinv"], _pair_blocks(rhs, left)).astype(BF16)

    def st_state(units):
        for un in units:
            pv = jnp.concatenate([un["pm"], un["v16"]], axis=0)
            cross = _dot(pv, un["bkp"], TN)
            state[un["bb"], un["pr"]] = (un["s0"] * un["pc"]
                                         + jnp.where(left, cross[:HEAD_DIM], cross[HEAD_DIM:]))

    def st_y(units):
        for un in units:
            pv_bd = jnp.concatenate([_pair_blocks(un["pm"], left), un["v_bd"]], axis=0)
            y = un["ars"][c:] + _dot(jnp.where(incl4, un["a4"][c:], zero), pv_bd)
            mean = _pair_sum(y, left) * (1.0 / HEAD_DIM)
            yc = y - mean
            var = _pair_sum(yc * yc, left) * (1.0 / HEAD_DIM)
            yn = yc * lax.rsqrt(var + GN_EPS) * gng_ref[:, un["sl"]] + gnb_ref[:, un["sl"]]
            out_ref[un["bb"], un["rows"], un["sl"]] = (yn + un["bonus"]) * un["gate"]

    free_stages = [st_a4]
    for mask in level_masks[1:]:
        free_stages += [st_lt(mask), st_tinv(mask)]
    free_stages.append(st_akv)
    state_stages = [st_ars, st_pm, st_state, st_y]

    def stage_tasks(stages, chs):
        return [functools.partial(lambda stage: stage([un for ch in chs for un in chunks[ch]]), stage)
                for stage in stages]

    def run_interleaved(*task_lists):
        keyed = [((i + 0.5) / len(tasks), n, i, task)
                 for n, tasks in enumerate(task_lists) for i, task in enumerate(tasks)]
        for _, _, _, task in sorted(keyed, key=lambda e: e[:3]):
            task()

    groups = [list(range(g0, min(g0 + n_grp, n_ch))) for g0 in range(0, n_ch, n_grp)]
    prep_group = lambda g: [task for ch in groups[g] for task in prep_tasks(ch)] if g < len(groups) else []
    free_group = lambda g: stage_tasks(free_stages, groups[g]) if g < len(groups) else []
    state = {(bb, pr): state_ref[bb, pr] for bb in range(bb_n) for pr in range(R_HEADS // 2)}
    for g in range(len(groups)):
        run_interleaved(prep_group(g))
    run_interleaved(free_group(0))
    for g, chs in enumerate(groups):
        run_interleaved([task for ch in chs for task in stage_tasks(state_stages, [ch])],
                        free_group(g + 1))
    for (bb, pr), val in state.items():
        state_ref[bb, pr] = val

    @pl.when(t == pl.num_programs(1) - 1)
    def _():
        for bb in range(bb_n):
            for h in range(R_HEADS):
                sfin_ref[bb, h] = state_ref[bb, h // 2, :, (h % 2) * HEAD_DIM:(h % 2 + 1) * HEAD_DIM]


def _wkv(zs, gr, prev, s0, params, *, bb_n, c, n_ch, n_grp=1):
    b, t, _ = zs.shape
    tt = n_ch * c
    full = lambda arr: pl.BlockSpec(arr.shape, lambda i, j: (0,) * arr.ndim)
    return pl.pallas_call(
        functools.partial(_wkv_kernel, bb_n=bb_n, c=c, n_ch=n_ch, n_grp=n_grp),
        name=f"wkv_c{c}",
        grid=(b // bb_n, t // tt),
        in_specs=[pl.BlockSpec((bb_n, tt, SHIFT_WIDTH), lambda i, j: (i, j, 0)),
                  pl.BlockSpec((bb_n, tt, R_WIDTH), lambda i, j: (i, j, 0)),
                  pl.BlockSpec((bb_n, 1, SHIFT_WIDTH), lambda i, j: (i, 0, 0)),
                  pl.BlockSpec((bb_n, R_HEADS, HEAD_DIM, HEAD_DIM), lambda i, j: (i, 0, 0, 0))]
                 + [full(p) for p in params],
        out_specs=[pl.BlockSpec((bb_n, tt, R_WIDTH), lambda i, j: (i, j, 0)),
                   pl.BlockSpec((bb_n, R_HEADS, HEAD_DIM, HEAD_DIM), lambda i, j: (i, 0, 0, 0))],
        out_shape=[jax.ShapeDtypeStruct((b, t, R_WIDTH), F32),
                   jax.ShapeDtypeStruct((b, R_HEADS, HEAD_DIM, HEAD_DIM), F32)],
        scratch_shapes=[pltpu.VMEM((bb_n, 1, SHIFT_WIDTH), F32),
                        pltpu.VMEM((bb_n, R_HEADS // 2, HEAD_DIM, PAIR), F32)],
        compiler_params=pltpu.CompilerParams(
            dimension_semantics=("arbitrary", "arbitrary"), vmem_limit_bytes=VMEM_LIMIT),
    )(zs, gr, prev, s0, *params)


def _fill_rel_bias(bias_ref, gtab_ref, offset):
    heads, nq, nk = bias_ref.shape
    for h in range(heads):
        g = jnp.broadcast_to(gtab_ref[h:h + 1, :], (nq, BIAS_L))
        bias_ref[h] = pltpu.roll(g, BIAS_L - offset, 1, stride=1, stride_axis=0)[:, :nk]


def _stack_heads(q2, left):
    zero = jnp.zeros((), q2.dtype)
    return jnp.concatenate([jnp.where(left, q2, zero), jnp.where(left, zero, q2)], axis=0)


def _attend_pairs(jobs, left):
    for jb in jobs:
        s = _dot(jb["lhs"], jb["k"], NT)
        if jb.get("bias") is not None:
            s = s + jb["bias"]
        jb["s"] = s
    for jb in jobs:
        m = jnp.max(jb["s"], axis=-1, keepdims=True)
        p = jnp.exp2(jb["s"] - m)
        jb["l"] = jnp.sum(p, axis=-1, keepdims=True)
        jb["p"] = p.astype(BF16)
    outs = []
    for jb in jobs:
        o2 = _dot(jb["p"], jb["v"]) / jb["l"]
        n = o2.shape[0] // 2
        outs.append(jnp.where(left, o2[:n], o2[n:]))
    return outs


def _layer_norm(h, g, b):
    mean = jnp.mean(h, axis=-1, keepdims=True)
    hc = h - mean
    var = jnp.mean(hc * hc, axis=-1, keepdims=True)
    return hc * lax.rsqrt(var + LN_EPS) * g + b


def _attn_prompt_kernel(q_ref, kp_ref, kc_ref, vp_ref, vc_ref, gb_ref, mq_ref, gm_ref, mk_ref, mv_ref, gtab_ref,
                        x_ref, mr_ref, w_ref, lng_ref, lnb_ref, y_ref,
                        kcat_ref, vcat_ref, bias_ref, biasv_ref, mix_ref, *, tq):
    j = pl.program_id(1)
    n_chunks = tq // CHUNK

    @pl.when((pl.program_id(0) == 0) & (j == 0))
    def _():
        _fill_rel_bias(bias_ref, gtab_ref, CHUNK - 1)
        kcol = lax.broadcasted_iota(jnp.int32, (1, BAND_LEN), 1)
        for h in range(B_HEADS):
            rows = slice((h % 2) * CHUNK, (h % 2 + 1) * CHUNK)
            scaled = bias_ref[h] * LOG2E
            biasv_ref[0, h // 2, rows, :] = scaled
            for i in range(n_chunks):
                biasv_ref[1 + i, h // 2, rows, :] = jnp.where(kcol >= BAND_WINDOW - i * CHUNK, scaled, -jnp.inf)

    kcat_ref[0:BAND_WINDOW] = kp_ref[0].astype(BF16)
    kcat_ref[BAND_WINDOW:] = kc_ref[0].astype(BF16)
    vcat_ref[0:BAND_WINDOW] = vp_ref[0].astype(BF16)
    vcat_ref[BAND_WINDOW:] = vc_ref[0].astype(BF16)
    q = (q_ref[0] * (ATT_SCALE * LOG2E)).astype(BF16)
    gate_b = _silu(gb_ref[0])
    left = lax.broadcasted_iota(jnp.int32, (1, PAIR), 1) < HEAD_DIM
    for i0 in range(0, n_chunks, ATTN_GROUP_CHUNKS):
        jobs = []
        for i in range(i0, i0 + ATTN_GROUP_CHUNKS):
            rows = slice(i * CHUNK, (i + 1) * CHUNK)
            keys = slice(i * CHUNK, i * CHUNK + BAND_LEN)
            variant = jnp.where(j == 0, 1 + i, 0)
            for pr in range(B_HEADS // 2):
                lanes = slice(pr * PAIR, (pr + 1) * PAIR)
                jobs.append(dict(rows=rows, lanes=lanes, lhs=_stack_heads(q[rows, lanes], left),
                                 k=kcat_ref[keys, lanes], v=vcat_ref[keys, lanes],
                                 bias=biasv_ref[variant, pr]))
        for jb, o in zip(jobs, _attend_pairs(jobs, left)):
            mix_ref[jb["rows"], jb["lanes"]] = (o * gate_b[jb["rows"], jb["lanes"]]).astype(BF16)

    mq = (mq_ref[0] * (ATT_SCALE * LOG2E)).astype(BF16)
    gate_m = _silu(gm_ref[0])
    mk = mk_ref[0].astype(BF16)
    mv = mv_ref[0].astype(BF16)
    for r0 in range(0, tq, MEM_ROWS * MEM_GROUP_BLOCKS):
        jobs = []
        for r in range(r0, r0 + MEM_ROWS * MEM_GROUP_BLOCKS, MEM_ROWS):
            rows = slice(r, r + MEM_ROWS)
            for pr in range(M_HEADS // 2):
                lanes = slice(pr * PAIR, (pr + 1) * PAIR)
                jobs.append(dict(rows=rows, lanes=lanes, lhs=_stack_heads(mq[rows, lanes], left),
                                 k=mk[:, lanes], v=mv[:, lanes]))
        for jb, o in zip(jobs, _attend_pairs(jobs, left)):
            mix_ref[jb["rows"], B_WIDTH + jb["lanes"].start:B_WIDTH + jb["lanes"].stop] = (
                o * gate_m[jb["rows"], jb["lanes"]]).astype(BF16)

    o = _dot(mr_ref[0].astype(BF16), w_ref[0:R_WIDTH, :]) + _dot(mix_ref[...], w_ref[R_WIDTH:, :])
    y_ref[0] = _layer_norm(ALPHA * x_ref[0] + o, lng_ref[...], lnb_ref[...])


def _attn_prompt(att, memkv, gtab, x, mix_r, w_out_bf16, ln_g, ln_b, *, tq):
    b, t, _ = att.shape
    col = lambda cidx: pl.BlockSpec((1, tq, B_WIDTH), lambda i, j: (i, j, cidx))
    prev = lambda cidx: pl.BlockSpec((1, tq, B_WIDTH), lambda i, j: (i, jnp.maximum(j - 1, 0), cidx))
    tile = lambda width: pl.BlockSpec((1, tq, width), lambda i, j: (i, j, 0))
    full = lambda arr: pl.BlockSpec(arr.shape, lambda i, j: (0, 0))
    return pl.pallas_call(
        functools.partial(_attn_prompt_kernel, tq=tq),
        name="attn_prompt",
        grid=(b, t // tq),
        in_specs=[col(0), prev(1), col(1), prev(2), col(2), col(3), col(4), col(5),
                  pl.BlockSpec((1, N_MEM, B_WIDTH), lambda i, j: (i, 0, 0)),
                  pl.BlockSpec((1, N_MEM, B_WIDTH), lambda i, j: (i, 0, 1)),
                  full(gtab), tile(D_MODEL), tile(R_WIDTH), full(w_out_bf16), full(ln_g), full(ln_b)],
        out_specs=tile(D_MODEL),
        out_shape=jax.ShapeDtypeStruct((b, t, D_MODEL), F32),
        scratch_shapes=[pltpu.VMEM((2 * BAND_WINDOW, B_WIDTH), BF16),
                        pltpu.VMEM((2 * BAND_WINDOW, B_WIDTH), BF16),
                        pltpu.VMEM((B_HEADS, CHUNK, BAND_LEN), F32),
                        pltpu.VMEM((1 + tq // CHUNK, B_HEADS // 2, 2 * CHUNK, BAND_LEN), F32),
                        pltpu.VMEM((tq, 2 * B_WIDTH), BF16)],
        compiler_params=pltpu.CompilerParams(
            dimension_semantics=("arbitrary", "arbitrary"), vmem_limit_bytes=VMEM_LIMIT),
    )(att, att, att, att, att, att, att, att, memkv, memkv, gtab, x, mix_r, w_out_bf16, ln_g, ln_b)


def _attn_sample_kernel(att_ref, ck_ref, cv_ref, mk_ref, mv_ref, gtab_ref, out_ref, bias_ref, bias2_ref):
    n_seq, n_new, _ = att_ref.shape

    @pl.when(pl.program_id(0) == 0)
    def _():
        _fill_rel_bias(bias_ref, gtab_ref, n_new - 1)
        for h in range(B_HEADS):
            bias2_ref[h // 2, (h % 2) * n_new:(h % 2 + 1) * n_new, :] = bias_ref[h] * LOG2E

    left = lax.broadcasted_iota(jnp.int32, (1, PAIR), 1) < HEAD_DIM
    jobs = []
    for b in range(n_seq):
        att = att_ref[b]
        q = (att[:, 0:B_WIDTH] * (ATT_SCALE * LOG2E)).astype(BF16)
        k_all = jnp.concatenate([ck_ref[b].astype(BF16), att[:, B_WIDTH:2 * B_WIDTH].astype(BF16)], axis=0)
        v_all = jnp.concatenate([cv_ref[b].astype(BF16), att[:, 2 * B_WIDTH:3 * B_WIDTH].astype(BF16)], axis=0)
        gate_b = _silu(att[:, 3 * B_WIDTH:4 * B_WIDTH])
        mq = (att[:, 4 * B_WIDTH:5 * B_WIDTH] * (ATT_SCALE * LOG2E)).astype(BF16)
        gate_m = _silu(att[:, 5 * B_WIDTH:6 * B_WIDTH])
        mk = mk_ref[b].astype(BF16)
        mv = mv_ref[b].astype(BF16)
        for pr in range(B_HEADS // 2):
            lanes = slice(pr * PAIR, (pr + 1) * PAIR)
            jobs.append(dict(b=b, out=lanes, gate=gate_b[:, lanes], lhs=_stack_heads(q[:, lanes], left),
                             k=k_all[:, lanes], v=v_all[:, lanes], bias=bias2_ref[pr]))
        for pr in range(M_HEADS // 2):
            lanes = slice(pr * PAIR, (pr + 1) * PAIR)
            jobs.append(dict(b=b, out=slice(B_WIDTH + lanes.start, B_WIDTH + lanes.stop), gate=gate_m[:, lanes],
                             lhs=_stack_heads(mq[:, lanes], left), k=mk[:, lanes], v=mv[:, lanes]))
    for jb, o in zip(jobs, _attend_pairs(jobs, left)):
        out_ref[jb["b"], :, jb["out"]] = o * jb["gate"]


def _attn_sample(att, cache_k, cache_v, mem_k, mem_v, gtab, *, n_seq):
    b, s, _ = att.shape
    per_b = lambda arr: pl.BlockSpec((n_seq,) + arr.shape[1:], lambda i: (i, 0, 0))
    n_keys = cache_k.shape[1] + s
    return pl.pallas_call(
        _attn_sample_kernel,
        name="attn_sample",
        grid=(b // n_seq,),
        in_specs=[per_b(att), per_b(cache_k), per_b(cache_v), per_b(mem_k), per_b(mem_v),
                  pl.BlockSpec(gtab.shape, lambda i: (0, 0))],
        out_specs=pl.BlockSpec((n_seq, s, 2 * B_WIDTH), lambda i: (i, 0, 0)),
        out_shape=jax.ShapeDtypeStruct((b, s, 2 * B_WIDTH), F32),
        scratch_shapes=[pltpu.VMEM((B_HEADS, s, n_keys), F32),
                        pltpu.VMEM((B_HEADS // 2, 2 * s, n_keys), F32)],
        compiler_params=pltpu.CompilerParams(dimension_semantics=("arbitrary",)),
    )(att, cache_k, cache_v, mem_k, mem_v, gtab)


def _finish_kernel(x_ref, mr_ref, mbm_ref, w_ref, g_ref, b_ref, y_ref):
    o = _dot(mr_ref[...].astype(BF16), w_ref[0:R_WIDTH, :])
    o = o + _dot(mbm_ref[...].astype(BF16), w_ref[R_WIDTH:, :])
    y_ref[...] = _layer_norm(ALPHA * x_ref[...] + o, g_ref[...], b_ref[...])


def _finish(x, mix_r, mix_bm, w_out_bf16, ln_g, ln_b, tm):
    m = x.shape[0]
    rows = lambda width: pl.BlockSpec((tm, width), lambda i: (i, 0))
    full = lambda arr: pl.BlockSpec(arr.shape, lambda i: (0, 0))
    return pl.pallas_call(
        _finish_kernel,
        name=f"finish_{m}",
        grid=(m // tm,),
        in_specs=[rows(D_MODEL), rows(R_WIDTH), rows(2 * B_WIDTH), full(w_out_bf16), full(ln_g), full(ln_b)],
        out_specs=rows(D_MODEL),
        out_shape=jax.ShapeDtypeStruct((m, D_MODEL), F32),
        compiler_params=pltpu.CompilerParams(
            dimension_semantics=("arbitrary",), vmem_limit_bytes=VMEM_LIMIT),
    )(x, mix_r, mix_bm, w_out_bf16, ln_g, ln_b)


IN_SPLITS = ((0, SHIFT_WIDTH), (SHIFT_WIDTH, SHIFT_WIDTH + R_WIDTH), (SHIFT_WIDTH + R_WIDTH, IN_WIDTH))


def _rel_bias_row(table, rel0):
    n_hi = rel0 - REL_CLIP
    n_lo = BIAS_L - n_hi - (2 * REL_CLIP + 1)
    heads = table.shape[0]
    return jnp.concatenate([jnp.broadcast_to(table[:, 2 * REL_CLIP:], (heads, n_hi)), table[:, ::-1],
                            jnp.broadcast_to(table[:, 0:1], (heads, n_lo))], axis=1)


def kernel(x_prompt, x_sample, mem_prompt, state_shift, state_wkv, cache_band_k, cache_band_v,
           cache_mem_k, cache_mem_v, w_in, mu_shift, w0, w2, a0, a2, k_k, k_a, r_k, gn_g, gn_b,
           rel_bias, w_mem_kv, w_out, ln_g, ln_b):
    bp, t, _ = x_prompt.shape
    bs, s, _ = x_sample.shape
    depth = w_in.shape[0]
    assert depth == 1 and t % BAND_WINDOW == 0 and s <= CHUNK
    keep = min(BAND_WINDOW, t)
    l = 0

    w_in_b = w_in[l].astype(BF16)
    w_out_b = w_out[l].astype(BF16)
    w_mem_b = w_mem_kv[l].astype(BF16)
    row = lambda p: p.reshape(1, -1)
    wkv_params = (row(mu_shift[l]), row(w0[l]), w2[l], row(a0[l]), a2[l], row(k_k[l]), row(k_a[l]),
                  row(r_k[l]), row(gn_g[l]), row(gn_b[l]))
    table = rel_bias[l]
    r_rows = cache_band_k.shape[2]
    gtab_p = _rel_bias_row(table, BAND_WINDOW + CHUNK - 1)
    gtab_s = _rel_bias_row(table, r_rows + s - 1)

    xp = x_prompt.reshape(bp * t, D_MODEL)
    zs, gr, att = _proj(xp, w_in_b, IN_SPLITS, 256)
    zs = zs.reshape(bp, t, SHIFT_WIDTH)
    att = att.reshape(bp, t, ATT_WIDTH)
    memkv, = _proj(mem_prompt.reshape(bp * N_MEM, D_MODEL), w_mem_b, ((0, 2 * B_WIDTH),), 256)
    memkv = memkv.reshape(bp, N_MEM, 2 * B_WIDTH)
    mix_r, p_wkv = _wkv(zs, gr.reshape(bp, t, R_WIDTH), jnp.zeros((bp, 1, SHIFT_WIDTH), F32),
                        jnp.zeros((bp, R_HEADS, HEAD_DIM, HEAD_DIM), F32), wkv_params, bb_n=2, c=CHUNK, n_ch=WKV_CHUNKS_PER_STEP,
                        n_grp=WKV_GROUP_CHUNKS)
    y_prompt = _attn_prompt(att, memkv, gtab_p, x_prompt, mix_r, w_out_b, row(ln_g[l]), row(ln_b[l]),
                            tq=BAND_WINDOW)
    p_shift = zs[:, -1]
    p_bk = att[:, t - keep:, B_WIDTH:2 * B_WIDTH].reshape(bp, keep, B_HEADS, HEAD_DIM)
    p_bv = att[:, t - keep:, 2 * B_WIDTH:3 * B_WIDTH].reshape(bp, keep, B_HEADS, HEAD_DIM)
    p_mk = memkv[:, :, :B_WIDTH].reshape(bp, N_MEM, M_HEADS, HEAD_DIM)
    p_mv = memkv[:, :, B_WIDTH:].reshape(bp, N_MEM, M_HEADS, HEAD_DIM)

    xs = x_sample.reshape(bs * s, D_MODEL)
    zs_s, gr_s, att_s = _proj(xs, w_in_b, IN_SPLITS, bs * s)
    zs_s = zs_s.reshape(bs, s, SHIFT_WIDTH)
    att_s = att_s.reshape(bs, s, ATT_WIDTH)
    mix_r_s, s_wkv = _wkv(zs_s, gr_s.reshape(bs, s, R_WIDTH), state_shift[l][:, None, :], state_wkv[l],
                          wkv_params, bb_n=4, c=s, n_ch=1)
    mix_bm_s = _attn_sample(att_s,
                            cache_band_k[l].reshape(bs, r_rows, B_WIDTH),
                            cache_band_v[l].reshape(bs, r_rows, B_WIDTH),
                            cache_mem_k[l].reshape(bs, N_MEM, B_WIDTH),
                            cache_mem_v[l].reshape(bs, N_MEM, B_WIDTH), gtab_s, n_seq=2)
    y_sample = _finish(xs, mix_r_s.reshape(bs * s, R_WIDTH), mix_bm_s.reshape(bs * s, 2 * B_WIDTH),
                       w_out_b, row(ln_g[l]), row(ln_b[l]), bs * s).reshape(bs, s, D_MODEL)
    s_shift = zs_s[:, -1]
    s_bk = att_s[:, :, B_WIDTH:2 * B_WIDTH].reshape(bs, s, B_HEADS, HEAD_DIM)
    s_bv = att_s[:, :, 2 * B_WIDTH:3 * B_WIDTH].reshape(bs, s, B_HEADS, HEAD_DIM)

    st = lambda a: a[None]
    return (y_prompt, y_sample, st(p_shift), st(p_wkv), st(p_bk), st(p_bv), st(p_mk), st(p_mv),
            st(s_shift), st(s_wkv), st(s_bk), st(s_bv))
```

```python
import functools

import numpy as np
import jax
import jax.numpy as jnp
from jax import lax
from jax.experimental import pallas as pl
from jax.experimental.pallas import tpu as pltpu

F32 = jnp.float32
BF16 = jnp.bfloat16

D_MODEL = 1024
HEAD_DIM = 64
R_WIDTH = 512
R_HEADS = 8
LOW_RANK = 64
SHIFT_WIDTH = 3 * R_WIDTH + 2 * LOW_RANK
B_WIDTH = 256
B_HEADS = 4
M_HEADS = 4
N_MEM = 256
CHUNK = 64
BAND_CHUNKS = 8
BAND_WINDOW = BAND_CHUNKS * CHUNK
BAND_LEN = BAND_WINDOW + CHUNK
REL_CLIP = 128
ATT_WIDTH = 6 * B_WIDTH
IN_WIDTH = SHIFT_WIDTH + R_WIDTH + ATT_WIDTH
LN_EPS = 1e-5
GN_EPS = 64e-5
ALPHA = 2.0 ** 0.25
ATT_SCALE = HEAD_DIM ** -0.5
LOG2E = float(np.log2(np.e))
BIAS_L = 1024
PAIR = 2 * HEAD_DIM
ATTN_GROUP_CHUNKS = 2
MEM_ROWS = 128
MEM_GROUP_BLOCKS = 2
WKV_CHUNKS_PER_STEP = 4
WKV_GROUP_CHUNKS = 2
STAGE_SPLIT = 16

VMEM_LIMIT = 48 * 1024 * 1024

NN = ((1,), (0,))
NT = ((1,), (1,))
TN = ((0,), (0,))


def _dot(a, b, dims=NN):
    return lax.dot_general(a, b, (dims, ((), ())), preferred_element_type=F32)


def _split2(x):
    hi = x.astype(BF16)
    lo = (x - hi.astype(F32)).astype(BF16)
    return hi, lo


def _mm3(a, b, dims=NN):
    ah, al = _split2(a)
    bh, bl = _split2(b)
    return _dot(ah, bh, dims) + _dot(ah, bl, dims) + _dot(al, bh, dims)


def _mm_exact_lhs(lhs_bf16, x):
    x1 = x.astype(BF16)
    r1 = x - x1.astype(F32)
    x2 = r1.astype(BF16)
    x3 = (r1 - x2.astype(F32)).astype(BF16)
    return _dot(lhs_bf16, x1) + _dot(lhs_bf16, x2) + _dot(lhs_bf16, x3)


def _sigmoid(x):
    return 1.0 / (1.0 + jnp.exp(-x))


def _silu(x):
    return x * _sigmoid(x)


def _proj_kernel(x_ref, w_ref, *out_refs, splits):
    x = x_ref[...].astype(BF16)
    for o_ref, (lo, hi) in zip(out_refs, splits):
        o_ref[...] = _dot(x, w_ref[:, lo:hi])


def _proj(x, w_bf16, splits, tm):
    m, k = x.shape
    n = w_bf16.shape[1]
    return pl.pallas_call(
        functools.partial(_proj_kernel, splits=splits),
        name=f"proj_{m}x{n}",
        grid=(m // tm,),
        in_specs=[pl.BlockSpec((tm, k), lambda i: (i, 0)),
                  pl.BlockSpec((k, n), lambda i: (0, 0))],
        out_specs=[pl.BlockSpec((tm, hi - lo), lambda i: (i, 0)) for lo, hi in splits],
        out_shape=[jax.ShapeDtypeStruct((m, hi - lo), F32) for lo, hi in splits],
        compiler_params=pltpu.CompilerParams(
            dimension_semantics=("arbitrary",), vmem_limit_bytes=VMEM_LIMIT),
    )(x, w_bf16)


def _pair_blocks(x2, left):
    zero = jnp.zeros((), x2.dtype)
    return jnp.concatenate([jnp.where(left, x2, zero), jnp.where(left, zero, x2)], axis=0)


def _pair_sum(x2, left):
    s0 = jnp.sum(jnp.where(left, x2, 0.0), axis=-1, keepdims=True)
    s1 = jnp.sum(jnp.where(left, 0.0, x2), axis=-1, keepdims=True)
    return jnp.where(left, s0, s1)


UNIT_COMMON = ("ar", "bkp", "v16", "pc", "gate", "bonus")
UNIT_FREE_OUT = ("a4", "tinv", "akv")
UNIT_FREE_IN = ("bk_bd",)


def _wkv_kernel(z0_ref, g0_ref, zn_ref, gn_ref, prev_ref, s0_ref, mu_ref, w0_ref, w2_ref, a0_ref, a2_ref,
                kk_ref, ka_ref, rk_ref, gng_ref, gnb_ref, out_ref, sfin_ref, carry_ref, state_ref, *unit_refs,
                bb_n, c, n_ch, n_grp, pipelined):
    t = pl.program_id(1)
    tt = n_ch * c
    n_pairs = R_HEADS // 2
    per_chunk = bb_n * n_pairs

    row2 = lax.broadcasted_iota(jnp.int32, (c, 2 * c), 0)
    col2 = lax.broadcasted_iota(jnp.int32, (c, 2 * c), 1) & (c - 1)
    eye2 = (row2 == col2).astype(BF16)
    strict2 = col2 < row2
    incl4 = ((lax.broadcasted_iota(jnp.int32, (c, 4 * c), 1) & (c - 1))
             <= lax.broadcasted_iota(jnp.int32, (c, 4 * c), 0))
    level_masks = []
    shift = 0
    while (1 << shift) < c:
        rb = row2 >> shift
        level_masks.append(((rb & 1) == 1) & ((col2 >> shift) == rb - 1))
        shift += 1
    left_s = lax.broadcasted_iota(jnp.int32, (1, 2 * c), 1) < c
    left = lax.broadcasted_iota(jnp.int32, (1, PAIR), 1) < HEAD_DIM
    zero = jnp.zeros((), BF16)
    row_t = lax.broadcasted_iota(jnp.int32, (tt, tt), 0)
    col_t = lax.broadcasted_iota(jnp.int32, (tt, tt), 1)
    shift_c = c.bit_length() - 1
    ltri = ((col_t <= row_t) & ((col_t >> shift_c) == (row_t >> shift_c))).astype(BF16)
    first_row = lax.broadcasted_iota(jnp.int32, (tt, 1), 0) == 0
    mu = mu_ref[...]

    def make_tile(zs_ref, gr_ref, prev_rows):
        tile = [dict() for _ in range(bb_n)]
        chunks = [[] for _ in range(n_ch)]

        def whole_tile(bb):
            zs = zs_ref[bb]
            zprev = jnp.where(first_row, prev_rows[bb], pltpu.roll(zs, 1, 0))
            xs = zs + (zprev - zs) * mu
            wd = xs[:, 3 * R_WIDTH:3 * R_WIDTH + LOW_RANK]
            ad = xs[:, 3 * R_WIDTH + LOW_RANK:]
            u = -(w0_ref[...] + _mm3(jnp.tanh(wd), w2_ref[...]))
            softplus = jnp.maximum(u, 0.0) + jnp.log(1.0 + jnp.exp(-jnp.abs(u)))
            ld = -jnp.exp(-softplus - 0.5)
            tile[bb].update(xs=xs, ld=ld, cum=_mm_exact_lhs(ltri, ld),
                            a_pre=a0_ref[...] + _mm3(ad, a2_ref[...]))

        def prep_tasks(ch):
            rows = slice(ch * c, (ch + 1) * c)
            tasks = []
            for bb in range(bb_n):
                wide = {}

                def full_width(bb=bb, wide=wide):
                    xs = tile[bb]["xs"][rows]
                    r = xs[:, 0:R_WIDTH]
                    k = xs[:, R_WIDTH:2 * R_WIDTH]
                    ld = tile[bb]["ld"][rows]
                    cum = tile[bb]["cum"][rows]
                    a = _sigmoid(tile[bb]["a_pre"][rows])
                    ecum = jnp.exp(cum)
                    k2 = k * (1.0 + (a - 1.0) * ka_ref[...])
                    wide.update(v=xs[:, 2 * R_WIDTH:3 * R_WIDTH], a=a, ecum=ecum, einv=jnp.exp(-cum),
                                eprev=jnp.exp(cum - ld), kkr=k * kk_ref[...], rt=r * ecum, k2=k2,
                                rk2=r * k2 * rk_ref[...], gate=_silu(gr_ref[bb, rows, :]))

                def pair(pr, bb=bb, wide=wide):
                    sl = slice(pr * PAIR, (pr + 1) * PAIR)
                    kkr_p = wide["kkr"][:, sl]
                    nrm = jnp.sqrt(_pair_sum(kkr_p * kkr_p, left))
                    kkn = kkr_p / jnp.maximum(nrm, 1e-12)
                    at = -kkn * wide["eprev"][:, sl]
                    bh = kkn * wide["a"][:, sl] * wide["einv"][:, sl]
                    kh = wide["k2"][:, sl] * wide["einv"][:, sl]
                    pc = wide["ecum"][c - 1:c, sl]
                    v_p = wide["v"][:, sl]
                    v16 = v_p.astype(BF16)
                    chunks[ch].append(dict(
                        bb=bb, pr=pr, sl=sl, rows=rows, pc=pc, gate=wide["gate"][:, sl],
                        bonus=_pair_sum(wide["rk2"][:, sl], left) * v_p, v_bd=_pair_blocks(v16, left), v16=v16,
                        ar=jnp.concatenate([at, wide["rt"][:, sl]], axis=0).astype(BF16),
                        bk_bd=jnp.concatenate([_pair_blocks(bh.astype(BF16), left),
                                               _pair_blocks(kh.astype(BF16), left)], axis=0),
                        bkp=jnp.concatenate([bh * pc, kh * pc], axis=0).astype(BF16)))

                tasks.append(full_width)
                tasks += [functools.partial(pair, pr) for pr in range(n_pairs)]
            return tasks

        return chunks, [functools.partial(whole_tile, bb) for bb in range(bb_n)], prep_tasks

    def st_a4(units):
        for un in units:
            un["a4"] = _dot(un["ar"], un["bk_bd"], NT).astype(BF16)
            un["tinv"] = jnp.where(level_masks[0], un["a4"][:c, :2 * c], eye2)

    def st_lt(mask):
        def run(units):
            for un in units:
                a_ab = un["a4"][:c, :2 * c]
                un["lt"] = _dot(jnp.where(mask, a_ab, zero), _pair_blocks(un["tinv"], left_s)).astype(BF16)
        return run

    def st_tinv(mask):
        def run(units):
            for un in units:
                new = _dot(un["tinv"], _pair_blocks(un["lt"], left_s)).astype(BF16)
                un["tinv"] = jnp.where(mask, new, un["tinv"])
        return run

    def st_akv(units):
        for un in units:
            un["akv"] = _dot(jnp.where(strict2, un["a4"][:c, 2 * c:], zero), un["v_bd"])

    def st_ars(units):
        for un in units:
            un["s0"] = state[un["bb"], un["pr"]]
            un["ars"] = _dot(un["ar"], _pair_blocks(un["s0"].astype(BF16), left), NT)

    def st_pm(units):
        for un in units:
            rhs = (un["ars"][:c] + un["akv"]).astype(BF16)
            un["pm"] = _dot(un["tinv"], _pair_blocks(rhs, left)).astype(BF16)

    def st_state(units):
        for un in units:
            pv = jnp.concatenate([un["pm"], un["v16"]], axis=0)
            cross = _dot(pv, un["bkp"], TN)
            state[un["bb"], un["pr"]] = (un["s0"] * un["pc"]
                                         + jnp.where(left, cross[:HEAD_DIM], cross[HEAD_DIM:]))

    def st_y(units):
        for un in units:
            pv_bd = jnp.concatenate([_pair_blocks(un["pm"], left), un["v_bd"]], axis=0)
            y = un["ars"][c:] + _dot(jnp.where(incl4, un["a4"][c:], zero), pv_bd)
            mean = _pair_sum(y, left) * (1.0 / HEAD_DIM)
            yc = y - mean
            var = _pair_sum(yc * yc, left) * (1.0 / HEAD_DIM)
            yn = yc * lax.rsqrt(var + GN_EPS) * gng_ref[:, un["sl"]] + gnb_ref[:, un["sl"]]
            out_ref[un["bb"], un["rows"], un["sl"]] = (yn + un["bonus"]) * un["gate"]

    free_stages = [st_a4]
    for mask in level_masks[1:]:
        free_stages += [st_lt(mask), st_tinv(mask)]
    free_stages.append(st_akv)
    state_stages = [st_ars, st_pm, st_state, st_y]

    def stage_tasks(stages, chunks, chs):
        def run(stage, lo):
            units = [un for ch in chs for un in chunks[ch]]
            stage(units[lo:lo + STAGE_SPLIT])

        return [functools.partial(run, stage, lo) for stage in stages
                for lo in range(0, len(chs) * per_chunk, STAGE_SPLIT)]

    def run_interleaved(*task_lists):
        keyed = [((i + 0.5) / len(tasks), n, i, task)
                 for n, tasks in enumerate(task_lists) for i, task in enumerate(tasks)]
        for _, _, _, task in sorted(keyed, key=lambda e: e[:3]):
            task()

    groups = [list(range(g0, min(g0 + n_grp, n_ch))) for g0 in range(0, n_ch, n_grp)]

    if pipelined:
        refs = dict(zip(UNIT_COMMON + UNIT_FREE_OUT + UNIT_FREE_IN, unit_refs))
        n_g = n_grp * per_chunk

        def store_units(chunks, chs, fields, base):
            units = [un for ch in chs for un in chunks[ch]]
            for idx, un in enumerate(units):
                for f in fields:
                    refs[f][(base if f in UNIT_COMMON else 0) + idx] = un[f]

        def load_units(chs, fields, base):
            chunks = {}
            idx = 0
            for ch in chs:
                chunks[ch] = []
                for bb in range(bb_n):
                    for pr in range(n_pairs):
                        un = dict(bb=bb, pr=pr, sl=slice(pr * PAIR, (pr + 1) * PAIR),
                                  rows=slice(ch * c, (ch + 1) * c))
                        for f in fields:
                            un[f] = refs[f][(base if f in UNIT_COMMON else 0) + idx]
                        un["v_bd"] = _pair_blocks(un["v16"], left)
                        chunks[ch].append(un)
                        idx += 1
            return chunks

        def hand_over(chunks):
            store_units(chunks, groups[0], UNIT_COMMON + UNIT_FREE_OUT, 0)
            store_units(chunks, groups[1], UNIT_COMMON + UNIT_FREE_IN, n_g)

        @pl.when(t == 0)
        def _():
            for bb in range(bb_n):
                for h in range(R_HEADS):
                    state_ref[bb, h // 2, :, (h % 2) * HEAD_DIM:(h % 2 + 1) * HEAD_DIM] = s0_ref[bb, h]
            first, whole, prep = make_tile(z0_ref, g0_ref, [prev_ref[bb] for bb in range(bb_n)])
            run_interleaved(whole)
            run_interleaved([task for ch in range(n_ch) for task in prep(ch)])
            run_interleaved(stage_tasks(free_stages, first, groups[0]))
            hand_over(first)
            carry_ref[...] = z0_ref[:, tt - 1:tt, :]

        cur = load_units(groups[0], UNIT_COMMON + UNIT_FREE_OUT, 0)
        cur.update(load_units(groups[1], UNIT_COMMON + UNIT_FREE_IN, n_g))
        nxt, whole, prep = make_tile(zn_ref, gn_ref, [carry_ref[bb] for bb in range(bb_n)])
        state = {(bb, pr): state_ref[bb, pr] for bb in range(bb_n) for pr in range(n_pairs)}
        run_interleaved([task for ch in groups[0] for task in stage_tasks(state_stages, cur, [ch])],
                        stage_tasks(free_stages, cur, groups[1]),
                        whole + [task for ch in groups[0] for task in prep(ch)])
        run_interleaved([task for ch in groups[1] for task in stage_tasks(state_stages, cur, [ch])],
                        stage_tasks(free_stages, nxt, groups[0]),
                        [task for ch in groups[1] for task in prep(ch)])
        hand_over(nxt)
        carry_ref[...] = zn_ref[:, tt - 1:tt, :]
    else:
        @pl.when(t == 0)
        def _():
            carry_ref[...] = prev_ref[...]
            for bb in range(bb_n):
                for h in range(R_HEADS):
                    state_ref[bb, h // 2, :, (h % 2) * HEAD_DIM:(h % 2 + 1) * HEAD_DIM] = s0_ref[bb, h]

        cur, whole, prep = make_tile(z0_ref, g0_ref, [carry_ref[bb] for bb in range(bb_n)])
        state = {(bb, pr): state_ref[bb, pr] for bb in range(bb_n) for pr in range(n_pairs)}
        run_interleaved(whole)
        carry_ref[...] = z0_ref[:, tt - 1:tt, :]
        run_interleaved([task for ch in range(n_ch) for task in prep(ch)])
        run_interleaved(stage_tasks(free_stages, cur, groups[0]))
        for g, chs in enumerate(groups):
            run_interleaved([task for ch in chs for task in stage_tasks(state_stages, cur, [ch])],
                            stage_tasks(free_stages, cur, groups[g + 1]) if g + 1 < len(groups) else [])
    for (bb, pr), val in state.items():
        state_ref[bb, pr] = val

    @pl.when(t == pl.num_programs(1) - 1)
    def _():
        for bb in range(bb_n):
            for h in range(R_HEADS):
                sfin_ref[bb, h] = state_ref[bb, h // 2, :, (h % 2) * HEAD_DIM:(h % 2 + 1) * HEAD_DIM]


def _wkv(zs, gr, prev, s0, params, *, bb_n, c, n_ch, n_grp=1):
    b, t, _ = zs.shape
    tt = n_ch * c
    n_t = t // tt
    pipelined = n_ch == 2 * n_grp and n_t > 1
    full = lambda arr: pl.BlockSpec(arr.shape, lambda i, j: (0,) * arr.ndim)
    first = lambda width: pl.BlockSpec((bb_n, tt, width), lambda i, j: (i, 0 if pipelined else j, 0))
    ahead = lambda width: pl.BlockSpec((bb_n, tt, width), lambda i, j: (i, jnp.minimum(j + 1, n_t - 1), 0))
    unit_scratch = []
    if pipelined:
        n_g = n_grp * bb_n * (R_HEADS // 2)
        shapes = dict(ar=((2 * c, PAIR), BF16), bkp=((2 * c, PAIR), BF16), v16=((c, PAIR), BF16),
                      pc=((1, PAIR), F32), gate=((c, PAIR), F32), bonus=((c, PAIR), F32),
                      a4=((2 * c, 4 * c), BF16), tinv=((c, 2 * c), BF16), akv=((c, PAIR), F32),
                      bk_bd=((4 * c, PAIR), BF16))
        for f in UNIT_COMMON + UNIT_FREE_OUT + UNIT_FREE_IN:
            shape, dtype = shapes[f]
            unit_scratch.append(pltpu.VMEM(((2 * n_g if f in UNIT_COMMON else n_g),) + shape, dtype))
    return pl.pallas_call(
        functools.partial(_wkv_kernel, bb_n=bb_n, c=c, n_ch=n_ch, n_grp=n_grp, pipelined=pipelined),
        name=f"wkv_c{c}",
        grid=(b // bb_n, n_t),
        in_specs=[first(SHIFT_WIDTH), first(R_WIDTH), ahead(SHIFT_WIDTH), ahead(R_WIDTH),
                  pl.BlockSpec((bb_n, 1, SHIFT_WIDTH), lambda i, j: (i, 0, 0)),
                  pl.BlockSpec((bb_n, R_HEADS, HEAD_DIM, HEAD_DIM), lambda i, j: (i, 0, 0, 0))]
                 + [full(p) for p in params],
        out_specs=[pl.BlockSpec((bb_n, tt, R_WIDTH), lambda i, j: (i, j, 0)),
                   pl.BlockSpec((bb_n, R_HEADS, HEAD_DIM, HEAD_DIM), lambda i, j: (i, 0, 0, 0))],
        out_shape=[jax.ShapeDtypeStruct((b, t, R_WIDTH), F32),
                   jax.ShapeDtypeStruct((b, R_HEADS, HEAD_DIM, HEAD_DIM), F32)],
        scratch_shapes=[pltpu.VMEM((bb_n, 1, SHIFT_WIDTH), F32),
                        pltpu.VMEM((bb_n, R_HEADS // 2, HEAD_DIM, PAIR), F32)] + unit_scratch,
        compiler_params=pltpu.CompilerParams(
            dimension_semantics=("arbitrary", "arbitrary"), vmem_limit_bytes=VMEM_LIMIT),
    )(zs, gr, zs, gr, prev, s0, *params)


def _fill_rel_bias(bias_ref, gtab_ref, offset):
    heads, nq, nk = bias_ref.shape
    for h in range(heads):
        g = jnp.broadcast_to(gtab_ref[h:h + 1, :], (nq, BIAS_L))
        bias_ref[h] = pltpu.roll(g, BIAS_L - offset, 1, stride=1, stride_axis=0)[:, :nk]


def _stack_heads(q2, left):
    zero = jnp.zeros((), q2.dtype)
    return jnp.concatenate([jnp.where(left, q2, zero), jnp.where(left, zero, q2)], axis=0)


def _attend_pairs(jobs, left):
    for jb in jobs:
        s = _dot(jb["lhs"], jb["k"], NT)
        if jb.get("bias") is not None:
            s = s + jb["bias"]
        jb["s"] = s
    for jb in jobs:
        m = jnp.max(jb["s"], axis=-1, keepdims=True)
        p = jnp.exp2(jb["s"] - m)
        jb["l"] = jnp.sum(p, axis=-1, keepdims=True)
        jb["p"] = p.astype(BF16)
    outs = []
    for jb in jobs:
        o2 = _dot(jb["p"], jb["v"]) / jb["l"]
        n = o2.shape[0] // 2
        outs.append(jnp.where(left, o2[:n], o2[n:]))
    return outs


def _layer_norm(h, g, b):
    mean = jnp.mean(h, axis=-1, keepdims=True)
    hc = h - mean
    var = jnp.mean(hc * hc, axis=-1, keepdims=True)
    return hc * lax.rsqrt(var + LN_EPS) * g + b


def _attn_prompt_kernel(q_ref, kp_ref, kc_ref, vp_ref, vc_ref, gb_ref, mq_ref, gm_ref, mk_ref, mv_ref, gtab_ref,
                        x_ref, mr_ref, w_ref, lng_ref, lnb_ref, y_ref,
                        kcat_ref, vcat_ref, bias_ref, biasv_ref, mix_ref, *, tq):
    j = pl.program_id(1)
    n_chunks = tq // CHUNK

    @pl.when((pl.program_id(0) == 0) & (j == 0))
    def _():
        _fill_rel_bias(bias_ref, gtab_ref, CHUNK - 1)
        kcol = lax.broadcasted_iota(jnp.int32, (1, BAND_LEN), 1)
        for h in range(B_HEADS):
            rows = slice((h % 2) * CHUNK, (h % 2 + 1) * CHUNK)
            scaled = bias_ref[h] * LOG2E
            biasv_ref[0, h // 2, rows, :] = scaled
            for i in range(n_chunks):
                biasv_ref[1 + i, h // 2, rows, :] = jnp.where(kcol >= BAND_WINDOW - i * CHUNK, scaled, -jnp.inf)

    kcat_ref[0:BAND_WINDOW] = kp_ref[0].astype(BF16)
    kcat_ref[BAND_WINDOW:] = kc_ref[0].astype(BF16)
    vcat_ref[0:BAND_WINDOW] = vp_ref[0].astype(BF16)
    vcat_ref[BAND_WINDOW:] = vc_ref[0].astype(BF16)
    q = (q_ref[0] * (ATT_SCALE * LOG2E)).astype(BF16)
    gate_b = _silu(gb_ref[0])
    left = lax.broadcasted_iota(jnp.int32, (1, PAIR), 1) < HEAD_DIM
    for i0 in range(0, n_chunks, ATTN_GROUP_CHUNKS):
        jobs = []
        for i in range(i0, i0 + ATTN_GROUP_CHUNKS):
            rows = slice(i * CHUNK, (i + 1) * CHUNK)
            keys = slice(i * CHUNK, i * CHUNK + BAND_LEN)
            variant = jnp.where(j == 0, 1 + i, 0)
            for pr in range(B_HEADS // 2):
                lanes = slice(pr * PAIR, (pr + 1) * PAIR)
                jobs.append(dict(rows=rows, lanes=lanes, lhs=_stack_heads(q[rows, lanes], left),
                                 k=kcat_ref[keys, lanes], v=vcat_ref[keys, lanes],
                                 bias=biasv_ref[variant, pr]))
        for jb, o in zip(jobs, _attend_pairs(jobs, left)):
            mix_ref[jb["rows"], jb["lanes"]] = (o * gate_b[jb["rows"], jb["lanes"]]).astype(BF16)

    mq = (mq_ref[0] * (ATT_SCALE * LOG2E)).astype(BF16)
    gate_m = _silu(gm_ref[0])
    mk = mk_ref[0].astype(BF16)
    mv = mv_ref[0].astype(BF16)
    for r0 in range(0, tq, MEM_ROWS * MEM_GROUP_BLOCKS):
        jobs = []
        for r in range(r0, r0 + MEM_ROWS * MEM_GROUP_BLOCKS, MEM_ROWS):
            rows = slice(r, r + MEM_ROWS)
            for pr in range(M_HEADS // 2):
                lanes = slice(pr * PAIR, (pr + 1) * PAIR)
                jobs.append(dict(rows=rows, lanes=lanes, lhs=_stack_heads(mq[rows, lanes], left),
                                 k=mk[:, lanes], v=mv[:, lanes]))
        for jb, o in zip(jobs, _attend_pairs(jobs, left)):
            mix_ref[jb["rows"], B_WIDTH + jb["lanes"].start:B_WIDTH + jb["lanes"].stop] = (
                o * gate_m[jb["rows"], jb["lanes"]]).astype(BF16)

    o = _dot(mr_ref[0].astype(BF16), w_ref[0:R_WIDTH, :]) + _dot(mix_ref[...], w_ref[R_WIDTH:, :])
    y_ref[0] = _layer_norm(ALPHA * x_ref[0] + o, lng_ref[...], lnb_ref[...])


def _attn_prompt(att, memkv, gtab, x, mix_r, w_out_bf16, ln_g, ln_b, *, tq):
    b, t, _ = att.shape
    col = lambda cidx: pl.BlockSpec((1, tq, B_WIDTH), lambda i, j: (i, j, cidx))
    prev = lambda cidx: pl.BlockSpec((1, tq, B_WIDTH), lambda i, j: (i, jnp.maximum(j - 1, 0), cidx))
    tile = lambda width: pl.BlockSpec((1, tq, width), lambda i, j: (i, j, 0))
    full = lambda arr: pl.BlockSpec(arr.shape, lambda i, j: (0, 0))
    return pl.pallas_call(
        functools.partial(_attn_prompt_kernel, tq=tq),
        name="attn_prompt",
        grid=(b, t // tq),
        in_specs=[col(0), prev(1), col(1), prev(2), col(2), col(3), col(4), col(5),
                  pl.BlockSpec((1, N_MEM, B_WIDTH), lambda i, j: (i, 0, 0)),
                  pl.BlockSpec((1, N_MEM, B_WIDTH), lambda i, j: (i, 0, 1)),
                  full(gtab), tile(D_MODEL), tile(R_WIDTH), full(w_out_bf16), full(ln_g), full(ln_b)],
        out_specs=tile(D_MODEL),
        out_shape=jax.ShapeDtypeStruct((b, t, D_MODEL), F32),
        scratch_shapes=[pltpu.VMEM((2 * BAND_WINDOW, B_WIDTH), BF16),
                        pltpu.VMEM((2 * BAND_WINDOW, B_WIDTH), BF16),
                        pltpu.VMEM((B_HEADS, CHUNK, BAND_LEN), F32),
                        pltpu.VMEM((1 + tq // CHUNK, B_HEADS // 2, 2 * CHUNK, BAND_LEN), F32),
                        pltpu.VMEM((tq, 2 * B_WIDTH), BF16)],
        compiler_params=pltpu.CompilerParams(
            dimension_semantics=("arbitrary", "arbitrary"), vmem_limit_bytes=VMEM_LIMIT),
    )(att, att, att, att, att, att, att, att, memkv, memkv, gtab, x, mix_r, w_out_bf16, ln_g, ln_b)


def _attn_sample_kernel(att_ref, ck_ref, cv_ref, mk_ref, mv_ref, gtab_ref, out_ref, bias_ref, bias2_ref):
    n_seq, n_new, _ = att_ref.shape

    @pl.when(pl.program_id(0) == 0)
    def _():
        _fill_rel_bias(bias_ref, gtab_ref, n_new - 1)
        for h in range(B_HEADS):
            bias2_ref[h // 2, (h % 2) * n_new:(h % 2 + 1) * n_new, :] = bias_ref[h] * LOG2E

    left = lax.broadcasted_iota(jnp.int32, (1, PAIR), 1) < HEAD_DIM
    jobs = []
    for b in range(n_seq):
        att = att_ref[b]
        q = (att[:, 0:B_WIDTH] * (ATT_SCALE * LOG2E)).astype(BF16)
        k_all = jnp.concatenate([ck_ref[b].astype(BF16), att[:, B_WIDTH:2 * B_WIDTH].astype(BF16)], axis=0)
        v_all = jnp.concatenate([cv_ref[b].astype(BF16), att[:, 2 * B_WIDTH:3 * B_WIDTH].astype(BF16)], axis=0)
        gate_b = _silu(att[:, 3 * B_WIDTH:4 * B_WIDTH])
        mq = (att[:, 4 * B_WIDTH:5 * B_WIDTH] * (ATT_SCALE * LOG2E)).astype(BF16)
        gate_m = _silu(att[:, 5 * B_WIDTH:6 * B_WIDTH])
        mk = mk_ref[b].astype(BF16)
        mv = mv_ref[b].astype(BF16)
        for pr in range(B_HEADS // 2):
            lanes = slice(pr * PAIR, (pr + 1) * PAIR)
            jobs.append(dict(b=b, out=lanes, gate=gate_b[:, lanes], lhs=_stack_heads(q[:, lanes], left),
                             k=k_all[:, lanes], v=v_all[:, lanes], bias=bias2_ref[pr]))
        for pr in range(M_HEADS // 2):
            lanes = slice(pr * PAIR, (pr + 1) * PAIR)
            jobs.append(dict(b=b, out=slice(B_WIDTH + lanes.start, B_WIDTH + lanes.stop), gate=gate_m[:, lanes],
                             lhs=_stack_heads(mq[:, lanes], left), k=mk[:, lanes], v=mv[:, lanes]))
    for jb, o in zip(jobs, _attend_pairs(jobs, left)):
        out_ref[jb["b"], :, jb["out"]] = o * jb["gate"]


def _attn_sample(att, cache_k, cache_v, mem_k, mem_v, gtab, *, n_seq):
    b, s, _ = att.shape
    per_b = lambda arr: pl.BlockSpec((n_seq,) + arr.shape[1:], lambda i: (i, 0, 0))
    n_keys = cache_k.shape[1] + s
    return pl.pallas_call(
        _attn_sample_kernel,
        name="attn_sample",
        grid=(b // n_seq,),
        in_specs=[per_b(att), per_b(cache_k), per_b(cache_v), per_b(mem_k), per_b(mem_v),
                  pl.BlockSpec(gtab.shape, lambda i: (0, 0))],
        out_specs=pl.BlockSpec((n_seq, s, 2 * B_WIDTH), lambda i: (i, 0, 0)),
        out_shape=jax.ShapeDtypeStruct((b, s, 2 * B_WIDTH), F32),
        scratch_shapes=[pltpu.VMEM((B_HEADS, s, n_keys), F32),
                        pltpu.VMEM((B_HEADS // 2, 2 * s, n_keys), F32)],
        compiler_params=pltpu.CompilerParams(dimension_semantics=("arbitrary",)),
    )(att, cache_k, cache_v, mem_k, mem_v, gtab)


def _finish_kernel(x_ref, mr_ref, mbm_ref, w_ref, g_ref, b_ref, y_ref):
    o = _dot(mr_ref[...].astype(BF16), w_ref[0:R_WIDTH, :])
    o = o + _dot(mbm_ref[...].astype(BF16), w_ref[R_WIDTH:, :])
    y_ref[...] = _layer_norm(ALPHA * x_ref[...] + o, g_ref[...], b_ref[...])


def _finish(x, mix_r, mix_bm, w_out_bf16, ln_g, ln_b, tm):
    m = x.shape[0]
    rows = lambda width: pl.BlockSpec((tm, width), lambda i: (i, 0))
    full = lambda arr: pl.BlockSpec(arr.shape, lambda i: (0, 0))
    return pl.pallas_call(
        _finish_kernel,
        name=f"finish_{m}",
        grid=(m // tm,),
        in_specs=[rows(D_MODEL), rows(R_WIDTH), rows(2 * B_WIDTH), full(w_out_bf16), full(ln_g), full(ln_b)],
        out_specs=rows(D_MODEL),
        out_shape=jax.ShapeDtypeStruct((m, D_MODEL), F32),
        compiler_params=pltpu.CompilerParams(
            dimension_semantics=("arbitrary",), vmem_limit_bytes=VMEM_LIMIT),
    )(x, mix_r, mix_bm, w_out_bf16, ln_g, ln_b)


IN_SPLITS = ((0, SHIFT_WIDTH), (SHIFT_WIDTH, SHIFT_WIDTH + R_WIDTH), (SHIFT_WIDTH + R_WIDTH, IN_WIDTH))


def _rel_bias_row(table, rel0):
    n_hi = rel0 - REL_CLIP
    n_lo = BIAS_L - n_hi - (2 * REL_CLIP + 1)
    heads = table.shape[0]
    return jnp.concatenate([jnp.broadcast_to(table[:, 2 * REL_CLIP:], (heads, n_hi)), table[:, ::-1],
                            jnp.broadcast_to(table[:, 0:1], (heads, n_lo))], axis=1)


def kernel(x_prompt, x_sample, mem_prompt, state_shift, state_wkv, cache_band_k, cache_band_v,
           cache_mem_k, cache_mem_v, w_in, mu_shift, w0, w2, a0, a2, k_k, k_a, r_k, gn_g, gn_b,
           rel_bias, w_mem_kv, w_out, ln_g, ln_b):
    bp, t, _ = x_prompt.shape
    bs, s, _ = x_sample.shape
    depth = w_in.shape[0]
    assert depth == 1 and t % BAND_WINDOW == 0 and s <= CHUNK
    keep = min(BAND_WINDOW, t)
    l = 0

    w_in_b = w_in[l].astype(BF16)
    w_out_b = w_out[l].astype(BF16)
    w_mem_b = w_mem_kv[l].astype(BF16)
    row = lambda p: p.reshape(1, -1)
    wkv_params = (row(mu_shift[l]), row(w0[l]), w2[l], row(a0[l]), a2[l], row(k_k[l]), row(k_a[l]),
                  row(r_k[l]), row(gn_g[l]), row(gn_b[l]))
    table = rel_bias[l]
    r_rows = cache_band_k.shape[2]
    gtab_p = _rel_bias_row(table, BAND_WINDOW + CHUNK - 1)
    gtab_s = _rel_bias_row(table, r_rows + s - 1)

    xp = x_prompt.reshape(bp * t, D_MODEL)
    zs, gr, att = _proj(xp, w_in_b, IN_SPLITS, 256)
    zs = zs.reshape(bp, t, SHIFT_WIDTH)
    att = att.reshape(bp, t, ATT_WIDTH)
    memkv, = _proj(mem_prompt.reshape(bp * N_MEM, D_MODEL), w_mem_b, ((0, 2 * B_WIDTH),), 256)
    memkv = memkv.reshape(bp, N_MEM, 2 * B_WIDTH)
    mix_r, p_wkv = _wkv(zs, gr.reshape(bp, t, R_WIDTH), jnp.zeros((bp, 1, SHIFT_WIDTH), F32),
                        jnp.zeros((bp, R_HEADS, HEAD_DIM, HEAD_DIM), F32), wkv_params,
                        bb_n=2, c=CHUNK, n_ch=WKV_CHUNKS_PER_STEP, n_grp=WKV_GROUP_CHUNKS)
    y_prompt = _attn_prompt(att, memkv, gtab_p, x_prompt, mix_r, w_out_b, row(ln_g[l]), row(ln_b[l]),
                            tq=BAND_WINDOW)
    p_shift = zs[:, -1]
    p_bk = att[:, t - keep:, B_WIDTH:2 * B_WIDTH].reshape(bp, keep, B_HEADS, HEAD_DIM)
    p_bv = att[:, t - keep:, 2 * B_WIDTH:3 * B_WIDTH].reshape(bp, keep, B_HEADS, HEAD_DIM)
    p_mk = memkv[:, :, :B_WIDTH].reshape(bp, N_MEM, M_HEADS, HEAD_DIM)
    p_mv = memkv[:, :, B_WIDTH:].reshape(bp, N_MEM, M_HEADS, HEAD_DIM)

    xs = x_sample.reshape(bs * s, D_MODEL)
    zs_s, gr_s, att_s = _proj(xs, w_in_b, IN_SPLITS, bs * s)
    zs_s = zs_s.reshape(bs, s, SHIFT_WIDTH)
    att_s = att_s.reshape(bs, s, ATT_WIDTH)
    mix_r_s, s_wkv = _wkv(zs_s, gr_s.reshape(bs, s, R_WIDTH), state_shift[l][:, None, :], state_wkv[l],
                          wkv_params, bb_n=4, c=s, n_ch=1)
    mix_bm_s = _attn_sample(att_s,
                            cache_band_k[l].reshape(bs, r_rows, B_WIDTH),
                            cache_band_v[l].reshape(bs, r_rows, B_WIDTH),
                            cache_mem_k[l].reshape(bs, N_MEM, B_WIDTH),
                            cache_mem_v[l].reshape(bs, N_MEM, B_WIDTH), gtab_s, n_seq=2)
    y_sample = _finish(xs, mix_r_s.reshape(bs * s, R_WIDTH), mix_bm_s.reshape(bs * s, 2 * B_WIDTH),
                       w_out_b, row(ln_g[l]), row(ln_b[l]), bs * s).reshape(bs, s, D_MODEL)
    s_shift = zs_s[:, -1]
    s_bk = att_s[:, :, B_WIDTH:2 * B_WIDTH].reshape(bs, s, B_HEADS, HEAD_DIM)
    s_bv = att_s[:, :, 2 * B_WIDTH:3 * B_WIDTH].reshape(bs, s, B_HEADS, HEAD_DIM)

    st = lambda a: a[None]
    return (y_prompt, y_sample, st(p_shift), st(p_wkv), st(p_bk), st(p_bv), st(p_mk), st(p_mv),
            st(s_shift), st(s_wkv), st(s_bk), st(s_bv))
```

```python
import functools

import numpy as np
import jax
import jax.numpy as jnp
from jax import lax
from jax.experimental import pallas as pl
from jax.experimental.pallas import tpu as pltpu

F32 = jnp.float32
BF16 = jnp.bfloat16

D_MODEL = 1024
HEAD_DIM = 64
R_WIDTH = 512
R_HEADS = 8
LOW_RANK = 64
SHIFT_WIDTH = 3 * R_WIDTH + 2 * LOW_RANK
B_WIDTH = 256
B_HEADS = 4
M_HEADS = 4
N_MEM = 256
CHUNK = 64
BAND_CHUNKS = 8
BAND_WINDOW = BAND_CHUNKS * CHUNK
BAND_LEN = BAND_WINDOW + CHUNK
REL_CLIP = 128
ATT_WIDTH = 6 * B_WIDTH
IN_WIDTH = SHIFT_WIDTH + R_WIDTH + ATT_WIDTH
LN_EPS = 1e-5
GN_EPS = 64e-5
ALPHA = 2.0 ** 0.25
ATT_SCALE = HEAD_DIM ** -0.5
LOG2E = float(np.log2(np.e))
BIAS_L = 1024
PAIR = 2 * HEAD_DIM
ATTN_GROUP_CHUNKS = 2
MEM_ROWS = 128
MEM_GROUP_BLOCKS = 2
WKV_CHUNKS_PER_STEP = 4
WKV_GROUP_CHUNKS = 2
STAGE_SPLIT = 16
PROJ_ROWS = 256
WKV_PROMPT_SEQS = 2
WKV_SAMPLE_SEQS = 16
ATTN_SAMPLE_SEQS = 4

VMEM_LIMIT = 48 * 1024 * 1024

NN = ((1,), (0,))
NT = ((1,), (1,))
TN = ((0,), (0,))


def _dot(a, b, dims=NN):
    return lax.dot_general(a, b, (dims, ((), ())), preferred_element_type=F32)


def _split2(x):
    hi = x.astype(BF16)
    lo = (x - hi.astype(F32)).astype(BF16)
    return hi, lo


def _mm3(a, b, dims=NN):
    ah, al = _split2(a)
    bh, bl = _split2(b)
    return _dot(ah, bh, dims) + _dot(ah, bl, dims) + _dot(al, bh, dims)


def _mm_exact_lhs(lhs_bf16, x):
    x1 = x.astype(BF16)
    r1 = x - x1.astype(F32)
    x2 = r1.astype(BF16)
    x3 = (r1 - x2.astype(F32)).astype(BF16)
    return _dot(lhs_bf16, x1) + _dot(lhs_bf16, x2) + _dot(lhs_bf16, x3)


def _sigmoid(x):
    return 1.0 / (1.0 + jnp.exp(-x))


def _silu(x):
    return x * _sigmoid(x)


def _proj_kernel(x_ref, w_ref, *out_refs, splits):
    x = x_ref[...].astype(BF16)
    for o_ref, (lo, hi) in zip(out_refs, splits):
        o_ref[...] = _dot(x, w_ref[:, lo:hi])


def _proj(x, w_bf16, splits, tm):
    m, k = x.shape
    n = w_bf16.shape[1]
    return pl.pallas_call(
        functools.partial(_proj_kernel, splits=splits),
        name=f"proj_{m}x{n}",
        grid=(m // tm,),
        in_specs=[pl.BlockSpec((tm, k), lambda i: (i, 0)),
                  pl.BlockSpec((k, n), lambda i: (0, 0))],
        out_specs=[pl.BlockSpec((tm, hi - lo), lambda i: (i, 0)) for lo, hi in splits],
        out_shape=[jax.ShapeDtypeStruct((m, hi - lo), F32) for lo, hi in splits],
        compiler_params=pltpu.CompilerParams(
            dimension_semantics=("arbitrary",), vmem_limit_bytes=VMEM_LIMIT),
    )(x, w_bf16)


def _pair_blocks(x2, left):
    zero = jnp.zeros((), x2.dtype)
    return jnp.concatenate([jnp.where(left, x2, zero), jnp.where(left, zero, x2)], axis=0)


def _pair_sum(x2, left):
    s0 = jnp.sum(jnp.where(left, x2, 0.0), axis=-1, keepdims=True)
    s1 = jnp.sum(jnp.where(left, 0.0, x2), axis=-1, keepdims=True)
    return jnp.where(left, s0, s1)


UNIT_COMMON = ("ar", "bkp", "v16", "pc", "gate", "bonus")
UNIT_FREE_OUT = ("a4", "tinv", "akv")
UNIT_FREE_IN = ("bk_bd",)


def _wkv_kernel(z0_ref, g0_ref, zn_ref, gn_ref, prev_ref, s0_ref, mu_ref, w0_ref, w2_ref, a0_ref, a2_ref,
                kk_ref, ka_ref, rk_ref, gng_ref, gnb_ref, out_ref, sfin_ref, carry_ref, state_ref, *unit_refs,
                bb_n, c, n_ch, n_grp, pipelined):
    t = pl.program_id(1)
    tt = n_ch * c
    n_pairs = R_HEADS // 2
    per_chunk = bb_n * n_pairs

    row2 = lax.broadcasted_iota(jnp.int32, (c, 2 * c), 0)
    col2 = lax.broadcasted_iota(jnp.int32, (c, 2 * c), 1) & (c - 1)
    eye2 = (row2 == col2).astype(BF16)
    strict2 = col2 < row2
    incl4 = ((lax.broadcasted_iota(jnp.int32, (c, 4 * c), 1) & (c - 1))
             <= lax.broadcasted_iota(jnp.int32, (c, 4 * c), 0))
    level_masks = []
    shift = 0
    while (1 << shift) < c:
        rb = row2 >> shift
        level_masks.append(((rb & 1) == 1) & ((col2 >> shift) == rb - 1))
        shift += 1
    left_s = lax.broadcasted_iota(jnp.int32, (1, 2 * c), 1) < c
    left = lax.broadcasted_iota(jnp.int32, (1, PAIR), 1) < HEAD_DIM
    zero = jnp.zeros((), BF16)
    row_t = lax.broadcasted_iota(jnp.int32, (tt, tt), 0)
    col_t = lax.broadcasted_iota(jnp.int32, (tt, tt), 1)
    shift_c = c.bit_length() - 1
    ltri = ((col_t <= row_t) & ((col_t >> shift_c) == (row_t >> shift_c))).astype(BF16)
    first_row = lax.broadcasted_iota(jnp.int32, (tt, 1), 0) == 0
    mu = mu_ref[...]

    def make_tile(zs_ref, gr_ref, prev_rows):
        tile = [dict() for _ in range(bb_n)]
        chunks = [[] for _ in range(n_ch)]

        def whole_tile(bb):
            zs = zs_ref[bb]
            zprev = jnp.where(first_row, prev_rows[bb], pltpu.roll(zs, 1, 0))
            xs = zs + (zprev - zs) * mu
            wd = xs[:, 3 * R_WIDTH:3 * R_WIDTH + LOW_RANK]
            ad = xs[:, 3 * R_WIDTH + LOW_RANK:]
            u = -(w0_ref[...] + _mm3(jnp.tanh(wd), w2_ref[...]))
            softplus = jnp.maximum(u, 0.0) + jnp.log(1.0 + jnp.exp(-jnp.abs(u)))
            ld = -jnp.exp(-softplus - 0.5)
            tile[bb].update(xs=xs, ld=ld, cum=_mm_exact_lhs(ltri, ld),
                            a_pre=a0_ref[...] + _mm3(ad, a2_ref[...]))

        def prep_tasks(ch):
            rows = slice(ch * c, (ch + 1) * c)
            tasks = []
            for bb in range(bb_n):
                wide = {}

                def full_width(bb=bb, wide=wide):
                    xs = tile[bb]["xs"][rows]
                    r = xs[:, 0:R_WIDTH]
                    k = xs[:, R_WIDTH:2 * R_WIDTH]
                    ld = tile[bb]["ld"][rows]
                    cum = tile[bb]["cum"][rows]
                    a = _sigmoid(tile[bb]["a_pre"][rows])
                    ecum = jnp.exp(cum)
                    k2 = k * (1.0 + (a - 1.0) * ka_ref[...])
                    wide.update(v=xs[:, 2 * R_WIDTH:3 * R_WIDTH], a=a, ecum=ecum, einv=jnp.exp(-cum),
                                eprev=jnp.exp(cum - ld), kkr=k * kk_ref[...], rt=r * ecum, k2=k2,
                                rk2=r * k2 * rk_ref[...], gate=_silu(gr_ref[bb, rows, :]))

                def pair(pr, bb=bb, wide=wide):
                    sl = slice(pr * PAIR, (pr + 1) * PAIR)
                    kkr_p = wide["kkr"][:, sl]
                    nrm = jnp.sqrt(_pair_sum(kkr_p * kkr_p, left))
                    kkn = kkr_p / jnp.maximum(nrm, 1e-12)
                    at = -kkn * wide["eprev"][:, sl]
                    bh = kkn * wide["a"][:, sl] * wide["einv"][:, sl]
                    kh = wide["k2"][:, sl] * wide["einv"][:, sl]
                    pc = wide["ecum"][c - 1:c, sl]
                    v_p = wide["v"][:, sl]
                    v16 = v_p.astype(BF16)
                    chunks[ch].append(dict(
                        bb=bb, pr=pr, sl=sl, rows=rows, pc=pc, gate=wide["gate"][:, sl],
                        bonus=_pair_sum(wide["rk2"][:, sl], left) * v_p, v_bd=_pair_blocks(v16, left), v16=v16,
                        ar=jnp.concatenate([at, wide["rt"][:, sl]], axis=0).astype(BF16),
                        bk_bd=jnp.concatenate([_pair_blocks(bh.astype(BF16), left),
                                               _pair_blocks(kh.astype(BF16), left)], axis=0),
                        bkp=jnp.concatenate([bh * pc, kh * pc], axis=0).astype(BF16)))

                tasks.append(full_width)
                tasks += [functools.partial(pair, pr) for pr in range(n_pairs)]
            return tasks

        return chunks, [functools.partial(whole_tile, bb) for bb in range(bb_n)], prep_tasks

    def st_a4(units):
        for un in units:
            un["a4"] = _dot(un["ar"], un["bk_bd"], NT).astype(BF16)
            un["tinv"] = jnp.where(level_masks[0], un["a4"][:c, :2 * c], eye2)

    def st_lt(mask):
        def run(units):
            for un in units:
                a_ab = un["a4"][:c, :2 * c]
                un["lt"] = _dot(jnp.where(mask, a_ab, zero), _pair_blocks(un["tinv"], left_s)).astype(BF16)
        return run

    def st_tinv(mask):
        def run(units):
            for un in units:
                new = _dot(un["tinv"], _pair_blocks(un["lt"], left_s)).astype(BF16)
                un["tinv"] = jnp.where(mask, new, un["tinv"])
        return run

    def st_akv(units):
        for un in units:
            un["akv"] = _dot(jnp.where(strict2, un["a4"][:c, 2 * c:], zero), un["v_bd"])

    def st_ars(units):
        for un in units:
            un["s0"] = state[un["bb"], un["pr"]]
            un["ars"] = _dot(un["ar"], _pair_blocks(un["s0"].astype(BF16), left), NT)

    def st_pm(units):
        for un in units:
            rhs = (un["ars"][:c] + un["akv"]).astype(BF16)
            un["pm"] = _dot(un["tinv"], _pair_blocks(rhs, left)).astype(BF16)

    def st_state(units):
        for un in units:
            pv = jnp.concatenate([un["pm"], un["v16"]], axis=0)
            cross = _dot(pv, un["bkp"], TN)
            state[un["bb"], un["pr"]] = (un["s0"] * un["pc"]
                                         + jnp.where(left, cross[:HEAD_DIM], cross[HEAD_DIM:]))

    def st_y(units):
        for un in units:
            pv_bd = jnp.concatenate([_pair_blocks(un["pm"], left), un["v_bd"]], axis=0)
            y = un["ars"][c:] + _dot(jnp.where(incl4, un["a4"][c:], zero), pv_bd)
            mean = _pair_sum(y, left) * (1.0 / HEAD_DIM)
            yc = y - mean
            var = _pair_sum(yc * yc, left) * (1.0 / HEAD_DIM)
            yn = yc * lax.rsqrt(var + GN_EPS) * gng_ref[:, un["sl"]] + gnb_ref[:, un["sl"]]
            out_ref[un["bb"], un["rows"], un["sl"]] = (yn + un["bonus"]) * un["gate"]

    free_stages = [st_a4]
    for mask in level_masks[1:]:
        free_stages += [st_lt(mask), st_tinv(mask)]
    free_stages.append(st_akv)
    state_stages = [st_ars, st_pm, st_state, st_y]

    def stage_tasks(stages, chunks, chs):
        def run(stage, lo):
            units = [un for ch in chs for un in chunks[ch]]
            stage(units[lo:lo + STAGE_SPLIT])

        return [functools.partial(run, stage, lo) for stage in stages
                for lo in range(0, len(chs) * per_chunk, STAGE_SPLIT)]

    def run_interleaved(*task_lists):
        keyed = [((i + 0.5) / len(tasks), n, i, task)
                 for n, tasks in enumerate(task_lists) for i, task in enumerate(tasks)]
        for _, _, _, task in sorted(keyed, key=lambda e: e[:3]):
            task()

    groups = [list(range(g0, min(g0 + n_grp, n_ch))) for g0 in range(0, n_ch, n_grp)]

    if pipelined:
        refs = dict(zip(UNIT_COMMON + UNIT_FREE_OUT + UNIT_FREE_IN, unit_refs))
        n_g = n_grp * per_chunk

        def store_units(chunks, chs, fields, base):
            units = [un for ch in chs for un in chunks[ch]]
            for idx, un in enumerate(units):
                for f in fields:
                    refs[f][(base if f in UNIT_COMMON else 0) + idx] = un[f]

        def load_units(chs, fields, base):
            chunks = {}
            idx = 0
            for ch in chs:
                chunks[ch] = []
                for bb in range(bb_n):
                    for pr in range(n_pairs):
                        un = dict(bb=bb, pr=pr, sl=slice(pr * PAIR, (pr + 1) * PAIR),
                                  rows=slice(ch * c, (ch + 1) * c))
                        for f in fields:
                            un[f] = refs[f][(base if f in UNIT_COMMON else 0) + idx]
                        un["v_bd"] = _pair_blocks(un["v16"], left)
                        chunks[ch].append(un)
                        idx += 1
            return chunks

        def hand_over(chunks):
            store_units(chunks, groups[0], UNIT_COMMON + UNIT_FREE_OUT, 0)
            store_units(chunks, groups[1], UNIT_COMMON + UNIT_FREE_IN, n_g)

        @pl.when(t == 0)
        def _():
            for bb in range(bb_n):
                for h in range(R_HEADS):
                    state_ref[bb, h // 2, :, (h % 2) * HEAD_DIM:(h % 2 + 1) * HEAD_DIM] = s0_ref[bb, h]
            first, whole, prep = make_tile(z0_ref, g0_ref, [prev_ref[bb] for bb in range(bb_n)])
            run_interleaved(whole)
            run_interleaved([task for ch in range(n_ch) for task in prep(ch)])
            run_interleaved(stage_tasks(free_stages, first, groups[0]))
            hand_over(first)
            carry_ref[...] = z0_ref[:, tt - 1:tt, :]

        cur = load_units(groups[0], UNIT_COMMON + UNIT_FREE_OUT, 0)
        cur.update(load_units(groups[1], UNIT_COMMON + UNIT_FREE_IN, n_g))
        nxt, whole, prep = make_tile(zn_ref, gn_ref, [carry_ref[bb] for bb in range(bb_n)])
        state = {(bb, pr): state_ref[bb, pr] for bb in range(bb_n) for pr in range(n_pairs)}
        run_interleaved([task for ch in groups[0] for task in stage_tasks(state_stages, cur, [ch])],
                        stage_tasks(free_stages, cur, groups[1]),
                        whole + [task for ch in groups[0] for task in prep(ch)])
        run_interleaved([task for ch in groups[1] for task in stage_tasks(state_stages, cur, [ch])],
                        stage_tasks(free_stages, nxt, groups[0]),
                        [task for ch in groups[1] for task in prep(ch)])
        hand_over(nxt)
        carry_ref[...] = zn_ref[:, tt - 1:tt, :]
    else:
        @pl.when(t == 0)
        def _():
            carry_ref[...] = prev_ref[...]
            for bb in range(bb_n):
                for h in range(R_HEADS):
                    state_ref[bb, h // 2, :, (h % 2) * HEAD_DIM:(h % 2 + 1) * HEAD_DIM] = s0_ref[bb, h]

        cur, whole, prep = make_tile(z0_ref, g0_ref, [carry_ref[bb] for bb in range(bb_n)])
        state = {(bb, pr): state_ref[bb, pr] for bb in range(bb_n) for pr in range(n_pairs)}
        run_interleaved(whole)
        carry_ref[...] = z0_ref[:, tt - 1:tt, :]
        run_interleaved([task for ch in range(n_ch) for task in prep(ch)])
        run_interleaved(stage_tasks(free_stages, cur, groups[0]))
        for g, chs in enumerate(groups):
            run_interleaved([task for ch in chs for task in stage_tasks(state_stages, cur, [ch])],
                            stage_tasks(free_stages, cur, groups[g + 1]) if g + 1 < len(groups) else [])
    for (bb, pr), val in state.items():
        state_ref[bb, pr] = val

    @pl.when(t == pl.num_programs(1) - 1)
    def _():
        for bb in range(bb_n):
            for h in range(R_HEADS):
                sfin_ref[bb, h] = state_ref[bb, h // 2, :, (h % 2) * HEAD_DIM:(h % 2 + 1) * HEAD_DIM]


def _wkv(zs, gr, prev, s0, params, *, bb_n, c, n_ch, n_grp=1):
    b, t, _ = zs.shape
    tt = n_ch * c
    n_t = t // tt
    pipelined = n_ch == 2 * n_grp and n_t > 1
    full = lambda arr: pl.BlockSpec(arr.shape, lambda i, j: (0,) * arr.ndim)
    first = lambda width: pl.BlockSpec((bb_n, tt, width), lambda i, j: (i, 0 if pipelined else j, 0))
    ahead = lambda width: pl.BlockSpec((bb_n, tt, width), lambda i, j: (i, jnp.minimum(j + 1, n_t - 1), 0))
    unit_scratch = []
    if pipelined:
        n_g = n_grp * bb_n * (R_HEADS // 2)
        shapes = dict(ar=((2 * c, PAIR), BF16), bkp=((2 * c, PAIR), BF16), v16=((c, PAIR), BF16),
                      pc=((1, PAIR), F32), gate=((c, PAIR), F32), bonus=((c, PAIR), F32),
                      a4=((2 * c, 4 * c), BF16), tinv=((c, 2 * c), BF16), akv=((c, PAIR), F32),
                      bk_bd=((4 * c, PAIR), BF16))
        for f in UNIT_COMMON + UNIT_FREE_OUT + UNIT_FREE_IN:
            shape, dtype = shapes[f]
            unit_scratch.append(pltpu.VMEM(((2 * n_g if f in UNIT_COMMON else n_g),) + shape, dtype))
    return pl.pallas_call(
        functools.partial(_wkv_kernel, bb_n=bb_n, c=c, n_ch=n_ch, n_grp=n_grp, pipelined=pipelined),
        name=f"wkv_c{c}",
        grid=(b // bb_n, n_t),
        in_specs=[first(SHIFT_WIDTH), first(R_WIDTH), ahead(SHIFT_WIDTH), ahead(R_WIDTH),
                  pl.BlockSpec((bb_n, 1, SHIFT_WIDTH), lambda i, j: (i, 0, 0)),
                  pl.BlockSpec((bb_n, R_HEADS, HEAD_DIM, HEAD_DIM), lambda i, j: (i, 0, 0, 0))]
                 + [full(p) for p in params],
        out_specs=[pl.BlockSpec((bb_n, tt, R_WIDTH), lambda i, j: (i, j, 0)),
                   pl.BlockSpec((bb_n, R_HEADS, HEAD_DIM, HEAD_DIM), lambda i, j: (i, 0, 0, 0))],
        out_shape=[jax.ShapeDtypeStruct((b, t, R_WIDTH), F32),
                   jax.ShapeDtypeStruct((b, R_HEADS, HEAD_DIM, HEAD_DIM), F32)],
        scratch_shapes=[pltpu.VMEM((bb_n, 1, SHIFT_WIDTH), F32),
                        pltpu.VMEM((bb_n, R_HEADS // 2, HEAD_DIM, PAIR), F32)] + unit_scratch,
        compiler_params=pltpu.CompilerParams(
            dimension_semantics=("arbitrary", "arbitrary"), vmem_limit_bytes=VMEM_LIMIT),
    )(zs, gr, zs, gr, prev, s0, *params)


def _fill_rel_bias(bias_ref, gtab_ref, offset):
    heads, nq, nk = bias_ref.shape
    for h in range(heads):
        g = jnp.broadcast_to(gtab_ref[h:h + 1, :], (nq, BIAS_L))
        bias_ref[h] = pltpu.roll(g, BIAS_L - offset, 1, stride=1, stride_axis=0)[:, :nk]


def _stack_heads(q2, left):
    zero = jnp.zeros((), q2.dtype)
    return jnp.concatenate([jnp.where(left, q2, zero), jnp.where(left, zero, q2)], axis=0)


def _attend_stages(jobs, left, write):
    def scores():
        for jb in jobs:
            s = _dot(jb["lhs"], jb["k"], NT)
            jb["s"] = s if jb.get("bias") is None else s + jb["bias"]

    def weights():
        for jb in jobs:
            m = jnp.max(jb["s"], axis=-1, keepdims=True)
            p = jnp.exp2(jb["s"] - m)
            jb["l"] = jnp.sum(p, axis=-1, keepdims=True)
            jb["p"] = p.astype(BF16)

    def values():
        for jb in jobs:
            o2 = _dot(jb["p"], jb["v"]) / jb["l"]
            n = o2.shape[0] // 2
            write(jb, jnp.where(left, o2[:n], o2[n:]))

    return [scores, weights, values]


def _run_staggered(groups):
    depth = max(len(g) for g in groups)
    for step in range(len(groups) + depth - 1):
        for g in range(len(groups) - 1, -1, -1):
            if 0 <= step - g < len(groups[g]):
                groups[g][step - g]()


def _layer_norm(h, g, b):
    mean = jnp.mean(h, axis=-1, keepdims=True)
    hc = h - mean
    var = jnp.mean(hc * hc, axis=-1, keepdims=True)
    return hc * lax.rsqrt(var + LN_EPS) * g + b


def _attn_prompt_kernel(q_ref, kp_ref, kc_ref, vp_ref, vc_ref, gb_ref, mq_ref, gm_ref, mk_ref, mv_ref, gtab_ref,
                        x_ref, mr_ref, w_ref, lng_ref, lnb_ref, y_ref,
                        kcat_ref, vcat_ref, bias_ref, biasv_ref, mix_ref, *, tq):
    j = pl.program_id(1)
    n_chunks = tq // CHUNK

    @pl.when((pl.program_id(0) == 0) & (j == 0))
    def _():
        _fill_rel_bias(bias_ref, gtab_ref, CHUNK - 1)
        kcol = lax.broadcasted_iota(jnp.int32, (1, BAND_LEN), 1)
        for h in range(B_HEADS):
            rows = slice((h % 2) * CHUNK, (h % 2 + 1) * CHUNK)
            scaled = bias_ref[h] * LOG2E
            biasv_ref[0, h // 2, rows, :] = scaled
            for i in range(n_chunks):
                biasv_ref[1 + i, h // 2, rows, :] = jnp.where(kcol >= BAND_WINDOW - i * CHUNK, scaled, -jnp.inf)

    kcat_ref[0:BAND_WINDOW] = kp_ref[0].astype(BF16)
    kcat_ref[BAND_WINDOW:] = kc_ref[0].astype(BF16)
    vcat_ref[0:BAND_WINDOW] = vp_ref[0].astype(BF16)
    vcat_ref[BAND_WINDOW:] = vc_ref[0].astype(BF16)
    q = (q_ref[0] * (ATT_SCALE * LOG2E)).astype(BF16)
    gate_b = _silu(gb_ref[0])
    left = lax.broadcasted_iota(jnp.int32, (1, PAIR), 1) < HEAD_DIM
    mq = (mq_ref[0] * (ATT_SCALE * LOG2E)).astype(BF16)
    gate_m = _silu(gm_ref[0])
    mk = mk_ref[0].astype(BF16)
    mv = mv_ref[0].astype(BF16)

    def write_band(jb, o):
        mix_ref[jb["rows"], jb["lanes"]] = (o * gate_b[jb["rows"], jb["lanes"]]).astype(BF16)

    def write_mem(jb, o):
        mix_ref[jb["rows"], B_WIDTH + jb["lanes"].start:B_WIDTH + jb["lanes"].stop] = (
            o * gate_m[jb["rows"], jb["lanes"]]).astype(BF16)

    groups = []
    for i0 in range(0, n_chunks, ATTN_GROUP_CHUNKS):
        jobs = []
        for i in range(i0, i0 + ATTN_GROUP_CHUNKS):
            rows = slice(i * CHUNK, (i + 1) * CHUNK)
            keys = slice(i * CHUNK, i * CHUNK + BAND_LEN)
            variant = jnp.where(j == 0, 1 + i, 0)
            for pr in range(B_HEADS // 2):
                lanes = slice(pr * PAIR, (pr + 1) * PAIR)
                jobs.append(dict(rows=rows, lanes=lanes, lhs=_stack_heads(q[rows, lanes], left),
                                 k=kcat_ref[keys, lanes], v=vcat_ref[keys, lanes],
                                 bias=biasv_ref[variant, pr]))
        groups.append(_attend_stages(jobs, left, write_band))
    for r0 in range(0, tq, MEM_ROWS * MEM_GROUP_BLOCKS):
        jobs = []
        for r in range(r0, r0 + MEM_ROWS * MEM_GROUP_BLOCKS, MEM_ROWS):
            rows = slice(r, r + MEM_ROWS)
            for pr in range(M_HEADS // 2):
                lanes = slice(pr * PAIR, (pr + 1) * PAIR)
                jobs.append(dict(rows=rows, lanes=lanes, lhs=_stack_heads(mq[rows, lanes], left),
                                 k=mk[:, lanes], v=mv[:, lanes]))
        groups.append(_attend_stages(jobs, left, write_mem))
    _run_staggered(groups)

    o = _dot(mr_ref[0].astype(BF16), w_ref[0:R_WIDTH, :]) + _dot(mix_ref[...], w_ref[R_WIDTH:, :])
    y_ref[0] = _layer_norm(ALPHA * x_ref[0] + o, lng_ref[...], lnb_ref[...])


def _attn_prompt(att, memkv, gtab, x, mix_r, w_out_bf16, ln_g, ln_b, *, tq):
    b, t, _ = att.shape
    col = lambda cidx: pl.BlockSpec((1, tq, B_WIDTH), lambda i, j: (i, j, cidx))
    prev = lambda cidx: pl.BlockSpec((1, tq, B_WIDTH), lambda i, j: (i, jnp.maximum(j - 1, 0), cidx))
    tile = lambda width: pl.BlockSpec((1, tq, width), lambda i, j: (i, j, 0))
    full = lambda arr: pl.BlockSpec(arr.shape, lambda i, j: (0, 0))
    return pl.pallas_call(
        functools.partial(_attn_prompt_kernel, tq=tq),
        name="attn_prompt",
        grid=(b, t // tq),
        in_specs=[col(0), prev(1), col(1), prev(2), col(2), col(3), col(4), col(5),
                  pl.BlockSpec((1, N_MEM, B_WIDTH), lambda i, j: (i, 0, 0)),
                  pl.BlockSpec((1, N_MEM, B_WIDTH), lambda i, j: (i, 0, 1)),
                  full(gtab), tile(D_MODEL), tile(R_WIDTH), full(w_out_bf16), full(ln_g), full(ln_b)],
        out_specs=tile(D_MODEL),
        out_shape=jax.ShapeDtypeStruct((b, t, D_MODEL), F32),
        scratch_shapes=[pltpu.VMEM((2 * BAND_WINDOW, B_WIDTH), BF16),
                        pltpu.VMEM((2 * BAND_WINDOW, B_WIDTH), BF16),
                        pltpu.VMEM((B_HEADS, CHUNK, BAND_LEN), F32),
                        pltpu.VMEM((1 + tq // CHUNK, B_HEADS // 2, 2 * CHUNK, BAND_LEN), F32),
                        pltpu.VMEM((tq, 2 * B_WIDTH), BF16)],
        compiler_params=pltpu.CompilerParams(
            dimension_semantics=("arbitrary", "arbitrary"), vmem_limit_bytes=VMEM_LIMIT),
    )(att, att, att, att, att, att, att, att, memkv, memkv, gtab, x, mix_r, w_out_bf16, ln_g, ln_b)


def _attn_sample_kernel(att_ref, ck_ref, cv_ref, mk_ref, mv_ref, gtab_ref, out_ref, bias_ref, bias2_ref):
    n_seq, n_new, _ = att_ref.shape

    @pl.when(pl.program_id(0) == 0)
    def _():
        _fill_rel_bias(bias_ref, gtab_ref, n_new - 1)
        for h in range(B_HEADS):
            bias2_ref[h // 2, (h % 2) * n_new:(h % 2 + 1) * n_new, :] = bias_ref[h] * LOG2E

    left = lax.broadcasted_iota(jnp.int32, (1, PAIR), 1) < HEAD_DIM
    jobs = []
    for b in range(n_seq):
        att = att_ref[b]
        q = (att[:, 0:B_WIDTH] * (ATT_SCALE * LOG2E)).astype(BF16)
        k_all = jnp.concatenate([ck_ref[b].astype(BF16), att[:, B_WIDTH:2 * B_WIDTH].astype(BF16)], axis=0)
        v_all = jnp.concatenate([cv_ref[b].astype(BF16), att[:, 2 * B_WIDTH:3 * B_WIDTH].astype(BF16)], axis=0)
        gate_b = _silu(att[:, 3 * B_WIDTH:4 * B_WIDTH])
        mq = (att[:, 4 * B_WIDTH:5 * B_WIDTH] * (ATT_SCALE * LOG2E)).astype(BF16)
        gate_m = _silu(att[:, 5 * B_WIDTH:6 * B_WIDTH])
        mk = mk_ref[b].astype(BF16)
        mv = mv_ref[b].astype(BF16)
        for pr in range(B_HEADS // 2):
            lanes = slice(pr * PAIR, (pr + 1) * PAIR)
            jobs.append(dict(b=b, out=lanes, gate=gate_b[:, lanes], lhs=_stack_heads(q[:, lanes], left),
                             k=k_all[:, lanes], v=v_all[:, lanes], bias=bias2_ref[pr]))
        for pr in range(M_HEADS // 2):
            lanes = slice(pr * PAIR, (pr + 1) * PAIR)
            jobs.append(dict(b=b, out=slice(B_WIDTH + lanes.start, B_WIDTH + lanes.stop), gate=gate_m[:, lanes],
                             lhs=_stack_heads(mq[:, lanes], left), k=mk[:, lanes], v=mv[:, lanes]))
    def write(jb, o):
        out_ref[jb["b"], :, jb["out"]] = o * jb["gate"]

    _run_staggered([_attend_stages(jobs, left, write)])


def _attn_sample(att, cache_k, cache_v, mem_k, mem_v, gtab, *, n_seq):
    b, s, _ = att.shape
    per_b = lambda arr: pl.BlockSpec((n_seq,) + arr.shape[1:], lambda i: (i, 0, 0))
    n_keys = cache_k.shape[1] + s
    return pl.pallas_call(
        _attn_sample_kernel,
        name="attn_sample",
        grid=(b // n_seq,),
        in_specs=[per_b(att), per_b(cache_k), per_b(cache_v), per_b(mem_k), per_b(mem_v),
                  pl.BlockSpec(gtab.shape, lambda i: (0, 0))],
        out_specs=pl.BlockSpec((n_seq, s, 2 * B_WIDTH), lambda i: (i, 0, 0)),
        out_shape=jax.ShapeDtypeStruct((b, s, 2 * B_WIDTH), F32),
        scratch_shapes=[pltpu.VMEM((B_HEADS, s, n_keys), F32),
                        pltpu.VMEM((B_HEADS // 2, 2 * s, n_keys), F32)],
        compiler_params=pltpu.CompilerParams(dimension_semantics=("arbitrary",)),
    )(att, cache_k, cache_v, mem_k, mem_v, gtab)


def _finish_kernel(x_ref, mr_ref, mbm_ref, w_ref, g_ref, b_ref, y_ref):
    o = _dot(mr_ref[...].astype(BF16), w_ref[0:R_WIDTH, :])
    o = o + _dot(mbm_ref[...].astype(BF16), w_ref[R_WIDTH:, :])
    y_ref[...] = _layer_norm(ALPHA * x_ref[...] + o, g_ref[...], b_ref[...])


def _finish(x, mix_r, mix_bm, w_out_bf16, ln_g, ln_b, tm):
    m = x.shape[0]
    rows = lambda width: pl.BlockSpec((tm, width), lambda i: (i, 0))
    full = lambda arr: pl.BlockSpec(arr.shape, lambda i: (0, 0))
    return pl.pallas_call(
        _finish_kernel,
        name=f"finish_{m}",
        grid=(m // tm,),
        in_specs=[rows(D_MODEL), rows(R_WIDTH), rows(2 * B_WIDTH), full(w_out_bf16), full(ln_g), full(ln_b)],
        out_specs=rows(D_MODEL),
        out_shape=jax.ShapeDtypeStruct((m, D_MODEL), F32),
        compiler_params=pltpu.CompilerParams(
            dimension_semantics=("arbitrary",), vmem_limit_bytes=VMEM_LIMIT),
    )(x, mix_r, mix_bm, w_out_bf16, ln_g, ln_b)


IN_SPLITS = ((0, SHIFT_WIDTH), (SHIFT_WIDTH, SHIFT_WIDTH + R_WIDTH), (SHIFT_WIDTH + R_WIDTH, IN_WIDTH))


def _rel_bias_row(table, rel0):
    n_hi = rel0 - REL_CLIP
    n_lo = BIAS_L - n_hi - (2 * REL_CLIP + 1)
    heads = table.shape[0]
    return jnp.concatenate([jnp.broadcast_to(table[:, 2 * REL_CLIP:], (heads, n_hi)), table[:, ::-1],
                            jnp.broadcast_to(table[:, 0:1], (heads, n_lo))], axis=1)


def kernel(x_prompt, x_sample, mem_prompt, state_shift, state_wkv, cache_band_k, cache_band_v,
           cache_mem_k, cache_mem_v, w_in, mu_shift, w0, w2, a0, a2, k_k, k_a, r_k, gn_g, gn_b,
           rel_bias, w_mem_kv, w_out, ln_g, ln_b):
    bp, t, _ = x_prompt.shape
    bs, s, _ = x_sample.shape
    depth = w_in.shape[0]
    assert depth == 1 and t % BAND_WINDOW == 0 and t % (WKV_CHUNKS_PER_STEP * CHUNK) == 0
    assert s <= CHUNK and s & (s - 1) == 0 and bp % WKV_PROMPT_SEQS == 0
    assert bs % WKV_SAMPLE_SEQS == 0 and bs % ATTN_SAMPLE_SEQS == 0 and (bp * t) % PROJ_ROWS == 0
    keep = min(BAND_WINDOW, t)
    l = 0

    w_in_b = w_in[l].astype(BF16)
    w_out_b = w_out[l].astype(BF16)
    w_mem_b = w_mem_kv[l].astype(BF16)
    row = lambda p: p.reshape(1, -1)
    wkv_params = (row(mu_shift[l]), row(w0[l]), w2[l], row(a0[l]), a2[l], row(k_k[l]), row(k_a[l]),
                  row(r_k[l]), row(gn_g[l]), row(gn_b[l]))
    table = rel_bias[l]
    r_rows = cache_band_k.shape[2]
    gtab_p = _rel_bias_row(table, BAND_WINDOW + CHUNK - 1)
    gtab_s = _rel_bias_row(table, r_rows + s - 1)

    xp = x_prompt.reshape(bp * t, D_MODEL)
    zs, gr, att = _proj(xp, w_in_b, IN_SPLITS, PROJ_ROWS)
    zs = zs.reshape(bp, t, SHIFT_WIDTH)
    att = att.reshape(bp, t, ATT_WIDTH)
    memkv, = _proj(mem_prompt.reshape(bp * N_MEM, D_MODEL), w_mem_b, ((0, 2 * B_WIDTH),), PROJ_ROWS)
    memkv = memkv.reshape(bp, N_MEM, 2 * B_WIDTH)
    mix_r, p_wkv = _wkv(zs, gr.reshape(bp, t, R_WIDTH), jnp.zeros((bp, 1, SHIFT_WIDTH), F32),
                        jnp.zeros((bp, R_HEADS, HEAD_DIM, HEAD_DIM), F32), wkv_params,
                        bb_n=WKV_PROMPT_SEQS, c=CHUNK, n_ch=WKV_CHUNKS_PER_STEP, n_grp=WKV_GROUP_CHUNKS)
    y_prompt = _attn_prompt(att, memkv, gtab_p, x_prompt, mix_r, w_out_b, row(ln_g[l]), row(ln_b[l]),
                            tq=BAND_WINDOW)
    p_shift = zs[:, -1]
    p_bk = att[:, t - keep:, B_WIDTH:2 * B_WIDTH].reshape(bp, keep, B_HEADS, HEAD_DIM)
    p_bv = att[:, t - keep:, 2 * B_WIDTH:3 * B_WIDTH].reshape(bp, keep, B_HEADS, HEAD_DIM)
    p_mk = memkv[:, :, :B_WIDTH].reshape(bp, N_MEM, M_HEADS, HEAD_DIM)
    p_mv = memkv[:, :, B_WIDTH:].reshape(bp, N_MEM, M_HEADS, HEAD_DIM)

    xs = x_sample.reshape(bs * s, D_MODEL)
    zs_s, gr_s, att_s = _proj(xs, w_in_b, IN_SPLITS, bs * s)
    zs_s = zs_s.reshape(bs, s, SHIFT_WIDTH)
    att_s = att_s.reshape(bs, s, ATT_WIDTH)
    mix_r_s, s_wkv = _wkv(zs_s, gr_s.reshape(bs, s, R_WIDTH), state_shift[l][:, None, :], state_wkv[l],
                          wkv_params, bb_n=WKV_SAMPLE_SEQS, c=s, n_ch=1)
    mix_bm_s = _attn_sample(att_s,
                            cache_band_k[l].reshape(bs, r_rows, B_WIDTH),
                            cache_band_v[l].reshape(bs, r_rows, B_WIDTH),
                            cache_mem_k[l].reshape(bs, N_MEM, B_WIDTH),
                            cache_mem_v[l].reshape(bs, N_MEM, B_WIDTH), gtab_s, n_seq=ATTN_SAMPLE_SEQS)
    y_sample = _finish(xs, mix_r_s.reshape(bs * s, R_WIDTH), mix_bm_s.reshape(bs * s, 2 * B_WIDTH),
                       w_out_b, row(ln_g[l]), row(ln_b[l]), bs * s).reshape(bs, s, D_MODEL)
    s_shift = zs_s[:, -1]
    s_bk = att_s[:, :, B_WIDTH:2 * B_WIDTH].reshape(bs, s, B_HEADS, HEAD_DIM)
    s_bv = att_s[:, :, 2 * B_WIDTH:3 * B_WIDTH].reshape(bs, s, B_HEADS, HEAD_DIM)

    st = lambda a: a[None]
    return (y_prompt, y_sample, st(p_shift), st(p_wkv), st(p_bk), st(p_bv), st(p_mk), st(p_mv),
            st(s_shift), st(s_wkv), st(s_bk), st(s_bv))
```

```python
import functools

import numpy as np
import jax
import jax.numpy as jnp
from jax import lax
from jax.experimental import pallas as pl
from jax.experimental.pallas import tpu as pltpu

F32 = jnp.float32
BF16 = jnp.bfloat16

D_MODEL = 1024
HEAD_DIM = 64
R_WIDTH = 512
R_HEADS = 8
LOW_RANK = 64
SHIFT_WIDTH = 3 * R_WIDTH + 2 * LOW_RANK
B_WIDTH = 256
B_HEADS = 4
M_HEADS = 4
N_MEM = 256
CHUNK = 64
BAND_CHUNKS = 8
BAND_WINDOW = BAND_CHUNKS * CHUNK
BAND_LEN = BAND_WINDOW + CHUNK
REL_CLIP = 128
ATT_WIDTH = 6 * B_WIDTH
IN_WIDTH = SHIFT_WIDTH + R_WIDTH + ATT_WIDTH
LN_EPS = 1e-5
GN_EPS = 64e-5
ALPHA = 2.0 ** 0.25
ATT_SCALE = HEAD_DIM ** -0.5
LOG2E = float(np.log2(np.e))
BIAS_L = 1024
PAIR = 2 * HEAD_DIM
ATTN_GROUP_CHUNKS = 2
MEM_ROWS = 128
MEM_GROUP_BLOCKS = 2
WKV_CHUNKS_PER_STEP = 4
WKV_GROUP_CHUNKS = 2
STAGE_SPLIT = 16
PROJ_ROWS = 256
WKV_PROMPT_SEQS = 2
WKV_SAMPLE_SEQS = 16
ATTN_SAMPLE_SEQS = 4

VMEM_LIMIT = 48 * 1024 * 1024

NN = ((1,), (0,))
NT = ((1,), (1,))
TN = ((0,), (0,))


def _dot(a, b, dims=NN):
    return lax.dot_general(a, b, (dims, ((), ())), preferred_element_type=F32)


def _split2(x):
    hi = x.astype(BF16)
    lo = (x - hi.astype(F32)).astype(BF16)
    return hi, lo


def _mm3(a, b, dims=NN):
    ah, al = _split2(a)
    bh, bl = _split2(b)
    return _dot(ah, bh, dims) + _dot(ah, bl, dims) + _dot(al, bh, dims)


def _mm_exact_lhs(lhs_bf16, x):
    x1 = x.astype(BF16)
    r1 = x - x1.astype(F32)
    x2 = r1.astype(BF16)
    x3 = (r1 - x2.astype(F32)).astype(BF16)
    return _dot(lhs_bf16, x1) + _dot(lhs_bf16, x2) + _dot(lhs_bf16, x3)


def _sigmoid(x):
    return 1.0 / (1.0 + jnp.exp(-x))


def _silu(x):
    return x * _sigmoid(x)


def _proj_kernel(x_ref, w_ref, *out_refs, splits):
    x = x_ref[...].astype(BF16)
    for o_ref, (lo, hi) in zip(out_refs, splits):
        o_ref[...] = _dot(x, w_ref[:, lo:hi])


def _proj(x, w_bf16, splits, tm):
    m, k = x.shape
    n = w_bf16.shape[1]
    return pl.pallas_call(
        functools.partial(_proj_kernel, splits=splits),
        name=f"proj_{m}x{n}",
        grid=(m // tm,),
        in_specs=[pl.BlockSpec((tm, k), lambda i: (i, 0)),
                  pl.BlockSpec((k, n), lambda i: (0, 0))],
        out_specs=[pl.BlockSpec((tm, hi - lo), lambda i: (i, 0)) for lo, hi in splits],
        out_shape=[jax.ShapeDtypeStruct((m, hi - lo), F32) for lo, hi in splits],
        compiler_params=pltpu.CompilerParams(
            dimension_semantics=("arbitrary",), vmem_limit_bytes=VMEM_LIMIT),
    )(x, w_bf16)


def _pair_blocks(x2, left):
    zero = jnp.zeros((), x2.dtype)
    return jnp.concatenate([jnp.where(left, x2, zero), jnp.where(left, zero, x2)], axis=0)


def _pair_sum(x2, left):
    s0 = jnp.sum(jnp.where(left, x2, 0.0), axis=-1, keepdims=True)
    s1 = jnp.sum(jnp.where(left, 0.0, x2), axis=-1, keepdims=True)
    return jnp.where(left, s0, s1)


UNIT_COMMON = ("ar", "bkp", "v16", "pc", "gate", "bonus")
UNIT_FREE_OUT = ("a4", "tinv", "akv")
UNIT_FREE_IN = ("bk_bd",)


def _wkv_kernel(z0_ref, g0_ref, zn_ref, gn_ref, prev_ref, s0_ref, mu_ref, w0_ref, w2_ref, a0_ref, a2_ref,
                kk_ref, ka_ref, rk_ref, gng_ref, gnb_ref, out_ref, sfin_ref, carry_ref, state_ref, *unit_refs,
                bb_n, c, n_ch, n_grp, pipelined):
    t = pl.program_id(1)
    tt = n_ch * c
    n_pairs = R_HEADS // 2
    per_chunk = bb_n * n_pairs

    row2 = lax.broadcasted_iota(jnp.int32, (c, 2 * c), 0)
    col2 = lax.broadcasted_iota(jnp.int32, (c, 2 * c), 1) & (c - 1)
    eye2 = (row2 == col2).astype(BF16)
    strict2 = col2 < row2
    incl4 = ((lax.broadcasted_iota(jnp.int32, (c, 4 * c), 1) & (c - 1))
             <= lax.broadcasted_iota(jnp.int32, (c, 4 * c), 0))
    level_masks = []
    shift = 0
    while (1 << shift) < c:
        rb = row2 >> shift
        level_masks.append(((rb & 1) == 1) & ((col2 >> shift) == rb - 1))
        shift += 1
    left_s = lax.broadcasted_iota(jnp.int32, (1, 2 * c), 1) < c
    left = lax.broadcasted_iota(jnp.int32, (1, PAIR), 1) < HEAD_DIM
    zero = jnp.zeros((), BF16)
    row_t = lax.broadcasted_iota(jnp.int32, (tt, tt), 0)
    col_t = lax.broadcasted_iota(jnp.int32, (tt, tt), 1)
    shift_c = c.bit_length() - 1
    ltri = ((col_t <= row_t) & ((col_t >> shift_c) == (row_t >> shift_c))).astype(BF16)
    first_row = lax.broadcasted_iota(jnp.int32, (tt, 1), 0) == 0
    mu = mu_ref[...]

    def field(un, f):
        if f not in un:
            un[f] = _pair_blocks(field(un, "v16"), left) if f == "v_bd" else un["stored"][f]()
        return un[f]

    def make_tile(zs_ref, gr_ref, prev_rows):
        tile = [dict() for _ in range(bb_n)]
        chunks = [[] for _ in range(n_ch)]

        def whole_tile(bb):
            zs = zs_ref[bb]
            zprev = jnp.where(first_row, prev_rows[bb], pltpu.roll(zs, 1, 0))
            xs = zs + (zprev - zs) * mu
            wd = xs[:, 3 * R_WIDTH:3 * R_WIDTH + LOW_RANK]
            ad = xs[:, 3 * R_WIDTH + LOW_RANK:]
            u = -(w0_ref[...] + _mm3(jnp.tanh(wd), w2_ref[...]))
            softplus = jnp.maximum(u, 0.0) + jnp.log(1.0 + jnp.exp(-jnp.abs(u)))
            ld = -jnp.exp(-softplus - 0.5)
            tile[bb].update(xs=xs, ld=ld, cum=_mm_exact_lhs(ltri, ld),
                            a_pre=a0_ref[...] + _mm3(ad, a2_ref[...]))

        def prep_tasks(ch):
            rows = slice(ch * c, (ch + 1) * c)
            tasks = []
            for bb in range(bb_n):
                wide = {}

                def full_width(bb=bb, wide=wide):
                    xs = tile[bb]["xs"][rows]
                    r = xs[:, 0:R_WIDTH]
                    k = xs[:, R_WIDTH:2 * R_WIDTH]
                    ld = tile[bb]["ld"][rows]
                    cum = tile[bb]["cum"][rows]
                    a = _sigmoid(tile[bb]["a_pre"][rows])
                    ecum = jnp.exp(cum)
                    k2 = k * (1.0 + (a - 1.0) * ka_ref[...])
                    wide.update(v=xs[:, 2 * R_WIDTH:3 * R_WIDTH], a=a, ecum=ecum, einv=jnp.exp(-cum),
                                eprev=jnp.exp(cum - ld), kkr=k * kk_ref[...], rt=r * ecum, k2=k2,
                                rk2=r * k2 * rk_ref[...], gate=_silu(gr_ref[bb, rows, :]))

                def pair(pr, bb=bb, wide=wide):
                    sl = slice(pr * PAIR, (pr + 1) * PAIR)
                    kkr_p = wide["kkr"][:, sl]
                    nrm = jnp.sqrt(_pair_sum(kkr_p * kkr_p, left))
                    kkn = kkr_p / jnp.maximum(nrm, 1e-12)
                    at = -kkn * wide["eprev"][:, sl]
                    bh = kkn * wide["a"][:, sl] * wide["einv"][:, sl]
                    kh = wide["k2"][:, sl] * wide["einv"][:, sl]
                    pc = wide["ecum"][c - 1:c, sl]
                    v_p = wide["v"][:, sl]
                    v16 = v_p.astype(BF16)
                    chunks[ch].append(dict(
                        bb=bb, pr=pr, sl=sl, rows=rows, pc=pc, gate=wide["gate"][:, sl],
                        bonus=_pair_sum(wide["rk2"][:, sl], left) * v_p, v_bd=_pair_blocks(v16, left), v16=v16,
                        ar=jnp.concatenate([at, wide["rt"][:, sl]], axis=0).astype(BF16),
                        bk_bd=jnp.concatenate([_pair_blocks(bh.astype(BF16), left),
                                               _pair_blocks(kh.astype(BF16), left)], axis=0),
                        bkp=jnp.concatenate([bh * pc, kh * pc], axis=0).astype(BF16)))

                tasks.append(full_width)
                tasks += [functools.partial(pair, pr) for pr in range(n_pairs)]
            return tasks

        return chunks, [functools.partial(whole_tile, bb) for bb in range(bb_n)], prep_tasks

    def st_a4(units):
        for un in units:
            un["a4"] = _dot(field(un, "ar"), field(un, "bk_bd"), NT).astype(BF16)
            un["tinv"] = jnp.where(level_masks[0], un["a4"][:c, :2 * c], eye2)

    def st_lt(mask):
        def run(units):
            for un in units:
                a_ab = un["a4"][:c, :2 * c]
                un["lt"] = _dot(jnp.where(mask, a_ab, zero), _pair_blocks(un["tinv"], left_s)).astype(BF16)
        return run

    def st_tinv(mask):
        def run(units):
            for un in units:
                new = _dot(un["tinv"], _pair_blocks(un["lt"], left_s)).astype(BF16)
                un["tinv"] = jnp.where(mask, new, un["tinv"])
        return run

    def st_akv(units):
        for un in units:
            un["akv"] = _dot(jnp.where(strict2, un["a4"][:c, 2 * c:], zero), field(un, "v_bd"))

    def st_ars(units):
        for un in units:
            un["s0"] = state[un["bb"], un["pr"]]
            un["ars"] = _dot(field(un, "ar"), _pair_blocks(un["s0"].astype(BF16), left), NT)

    def st_pm(units):
        for un in units:
            rhs = (un["ars"][:c] + field(un, "akv")).astype(BF16)
            un["pm"] = _dot(field(un, "tinv"), _pair_blocks(rhs, left)).astype(BF16)

    def st_state(units):
        for un in units:
            pv = jnp.concatenate([un["pm"], field(un, "v16")], axis=0)
            cross = _dot(pv, field(un, "bkp"), TN)
            state[un["bb"], un["pr"]] = (un["s0"] * field(un, "pc")
                                         + jnp.where(left, cross[:HEAD_DIM], cross[HEAD_DIM:]))

    def st_y(units):
        for un in units:
            pv_bd = jnp.concatenate([_pair_blocks(un["pm"], left), field(un, "v_bd")], axis=0)
            y = un["ars"][c:] + _dot(jnp.where(incl4, field(un, "a4")[c:], zero), pv_bd)
            mean = _pair_sum(y, left) * (1.0 / HEAD_DIM)
            yc = y - mean
            var = _pair_sum(yc * yc, left) * (1.0 / HEAD_DIM)
            yn = yc * lax.rsqrt(var + GN_EPS) * gng_ref[:, un["sl"]] + gnb_ref[:, un["sl"]]
            out_ref[un["bb"], un["rows"], un["sl"]] = (yn + field(un, "bonus")) * field(un, "gate")

    free_stages = [st_a4]
    for mask in level_masks[1:]:
        free_stages += [st_lt(mask), st_tinv(mask)]
    free_stages.append(st_akv)
    state_stages = [st_ars, st_pm, st_state, st_y]

    def stage_tasks(stages, chunks, chs):
        def run(stage, lo):
            units = [un for ch in chs for un in chunks[ch]]
            stage(units[lo:lo + STAGE_SPLIT])

        return [functools.partial(run, stage, lo) for stage in stages
                for lo in range(0, len(chs) * per_chunk, STAGE_SPLIT)]

    def run_interleaved(*task_lists):
        keyed = [((i + 0.5) / len(tasks), n, i, task)
                 for n, tasks in enumerate(task_lists) for i, task in enumerate(tasks)]
        for _, _, _, task in sorted(keyed, key=lambda e: e[:3]):
            task()

    groups = [list(range(g0, min(g0 + n_grp, n_ch))) for g0 in range(0, n_ch, n_grp)]

    if pipelined:
        refs = dict(zip(UNIT_COMMON + UNIT_FREE_OUT + UNIT_FREE_IN, unit_refs))
        n_g = n_grp * per_chunk

        def store_units(chunks, chs, fields, base):
            units = [un for ch in chs for un in chunks[ch]]
            for idx, un in enumerate(units):
                for f in fields:
                    refs[f][(base if f in UNIT_COMMON else 0) + idx] = un[f]

        def load_units(chs, fields, base):
            chunks = {}
            idx = 0
            for ch in chs:
                chunks[ch] = []
                for bb in range(bb_n):
                    for pr in range(n_pairs):
                        un = dict(bb=bb, pr=pr, sl=slice(pr * PAIR, (pr + 1) * PAIR),
                                  rows=slice(ch * c, (ch + 1) * c))
                        un["stored"] = {f: functools.partial(lambda f, i: refs[f][i], f,
                                                             (base if f in UNIT_COMMON else 0) + idx)
                                        for f in fields}
                        chunks[ch].append(un)
                        idx += 1
            return chunks

        def hand_over(chunks):
            store_units(chunks, groups[0], UNIT_COMMON + UNIT_FREE_OUT, 0)
            store_units(chunks, groups[1], UNIT_COMMON + UNIT_FREE_IN, n_g)

        @pl.when(t == 0)
        def _():
            for bb in range(bb_n):
                for h in range(R_HEADS):
                    state_ref[bb, h // 2, :, (h % 2) * HEAD_DIM:(h % 2 + 1) * HEAD_DIM] = s0_ref[bb, h]
            first, whole, prep = make_tile(z0_ref, g0_ref, [prev_ref[bb] for bb in range(bb_n)])
            run_interleaved(whole)
            run_interleaved([task for ch in range(n_ch) for task in prep(ch)])
            run_interleaved(stage_tasks(free_stages, first, groups[0]))
            hand_over(first)
            carry_ref[...] = z0_ref[:, tt - 1:tt, :]

        cur = load_units(groups[0], UNIT_COMMON + UNIT_FREE_OUT, 0)
        cur.update(load_units(groups[1], UNIT_COMMON + UNIT_FREE_IN, n_g))
        nxt, whole, prep = make_tile(zn_ref, gn_ref, [carry_ref[bb] for bb in range(bb_n)])
        state = {(bb, pr): state_ref[bb, pr] for bb in range(bb_n) for pr in range(n_pairs)}
        run_interleaved([task for ch in groups[0] for task in stage_tasks(state_stages, cur, [ch])],
                        stage_tasks(free_stages, cur, groups[1]),
                        whole + [task for ch in groups[0] for task in prep(ch)])
        run_interleaved([task for ch in groups[1] for task in stage_tasks(state_stages, cur, [ch])],
                        stage_tasks(free_stages, nxt, groups[0]),
                        [task for ch in groups[1] for task in prep(ch)])
        hand_over(nxt)
        carry_ref[...] = zn_ref[:, tt - 1:tt, :]
    else:
        @pl.when(t == 0)
        def _():
            carry_ref[...] = prev_ref[...]
            for bb in range(bb_n):
                for h in range(R_HEADS):
                    state_ref[bb, h // 2, :, (h % 2) * HEAD_DIM:(h % 2 + 1) * HEAD_DIM] = s0_ref[bb, h]

        cur, whole, prep = make_tile(z0_ref, g0_ref, [carry_ref[bb] for bb in range(bb_n)])
        state = {(bb, pr): state_ref[bb, pr] for bb in range(bb_n) for pr in range(n_pairs)}
        run_interleaved(whole)
        carry_ref[...] = z0_ref[:, tt - 1:tt, :]
        run_interleaved([task for ch in range(n_ch) for task in prep(ch)])
        run_interleaved(stage_tasks(free_stages, cur, groups[0]))
        for g, chs in enumerate(groups):
            run_interleaved([task for ch in chs for task in stage_tasks(state_stages, cur, [ch])],
                            stage_tasks(free_stages, cur, groups[g + 1]) if g + 1 < len(groups) else [])
    for (bb, pr), val in state.items():
        state_ref[bb, pr] = val

    @pl.when(t == pl.num_programs(1) - 1)
    def _():
        for bb in range(bb_n):
            for h in range(R_HEADS):
                sfin_ref[bb, h] = state_ref[bb, h // 2, :, (h % 2) * HEAD_DIM:(h % 2 + 1) * HEAD_DIM]


def _wkv(zs, gr, prev, s0, params, *, bb_n, c, n_ch, n_grp=1):
    b, t, _ = zs.shape
    tt = n_ch * c
    n_t = t // tt
    pipelined = n_ch == 2 * n_grp and n_t > 1
    full = lambda arr: pl.BlockSpec(arr.shape, lambda i, j: (0,) * arr.ndim)
    first = lambda width: pl.BlockSpec((bb_n, tt, width), lambda i, j: (i, 0 if pipelined else j, 0))
    ahead = lambda width: pl.BlockSpec((bb_n, tt, width), lambda i, j: (i, jnp.minimum(j + 1, n_t - 1), 0))
    unit_scratch = []
    if pipelined:
        n_g = n_grp * bb_n * (R_HEADS // 2)
        shapes = dict(ar=((2 * c, PAIR), BF16), bkp=((2 * c, PAIR), BF16), v16=((c, PAIR), BF16),
                      pc=((1, PAIR), F32), gate=((c, PAIR), F32), bonus=((c, PAIR), F32),
                      a4=((2 * c, 4 * c), BF16), tinv=((c, 2 * c), BF16), akv=((c, PAIR), F32),
                      bk_bd=((4 * c, PAIR), BF16))
        for f in UNIT_COMMON + UNIT_FREE_OUT + UNIT_FREE_IN:
            shape, dtype = shapes[f]
            unit_scratch.append(pltpu.VMEM(((2 * n_g if f in UNIT_COMMON else n_g),) + shape, dtype))
    return pl.pallas_call(
        functools.partial(_wkv_kernel, bb_n=bb_n, c=c, n_ch=n_ch, n_grp=n_grp, pipelined=pipelined),
        name=f"wkv_c{c}",
        grid=(b // bb_n, n_t),
        in_specs=[first(SHIFT_WIDTH), first(R_WIDTH), ahead(SHIFT_WIDTH), ahead(R_WIDTH),
                  pl.BlockSpec((bb_n, 1, SHIFT_WIDTH), lambda i, j: (i, 0, 0)),
                  pl.BlockSpec((bb_n, R_HEADS, HEAD_DIM, HEAD_DIM), lambda i, j: (i, 0, 0, 0))]
                 + [full(p) for p in params],
        out_specs=[pl.BlockSpec((bb_n, tt, R_WIDTH), lambda i, j: (i, j, 0)),
                   pl.BlockSpec((bb_n, R_HEADS, HEAD_DIM, HEAD_DIM), lambda i, j: (i, 0, 0, 0))],
        out_shape=[jax.ShapeDtypeStruct((b, t, R_WIDTH), F32),
                   jax.ShapeDtypeStruct((b, R_HEADS, HEAD_DIM, HEAD_DIM), F32)],
        scratch_shapes=[pltpu.VMEM((bb_n, 1, SHIFT_WIDTH), F32),
                        pltpu.VMEM((bb_n, R_HEADS // 2, HEAD_DIM, PAIR), F32)] + unit_scratch,
        compiler_params=pltpu.CompilerParams(
            dimension_semantics=("arbitrary", "arbitrary"), vmem_limit_bytes=VMEM_LIMIT),
    )(zs, gr, zs, gr, prev, s0, *params)


def _fill_rel_bias(bias_ref, gtab_ref, offset):
    heads, nq, nk = bias_ref.shape
    for h in range(heads):
        g = jnp.broadcast_to(gtab_ref[h:h + 1, :], (nq, BIAS_L))
        bias_ref[h] = pltpu.roll(g, BIAS_L - offset, 1, stride=1, stride_axis=0)[:, :nk]


def _stack_heads(q2, left):
    zero = jnp.zeros((), q2.dtype)
    return jnp.concatenate([jnp.where(left, q2, zero), jnp.where(left, zero, q2)], axis=0)


def _attend_stages(jobs, left, write):
    def scores():
        for jb in jobs:
            s = _dot(jb["lhs"], jb["k"], NT)
            jb["s"] = s if jb.get("bias") is None else s + jb["bias"]

    def weights():
        for jb in jobs:
            m = jnp.max(jb["s"], axis=-1, keepdims=True)
            p = jnp.exp2(jb["s"] - m)
            jb["l"] = jnp.sum(p, axis=-1, keepdims=True)
            jb["p"] = p.astype(BF16)

    def values():
        for jb in jobs:
            o2 = _dot(jb["p"], jb["v"]) / jb["l"]
            n = o2.shape[0] // 2
            write(jb, jnp.where(left, o2[:n], o2[n:]))

    return [scores, weights, values]


def _run_staggered(groups):
    depth = max(len(g) for g in groups)
    for step in range(len(groups) + depth - 1):
        for g in range(len(groups) - 1, -1, -1):
            if 0 <= step - g < len(groups[g]):
                groups[g][step - g]()


def _layer_norm(h, g, b):
    mean = jnp.mean(h, axis=-1, keepdims=True)
    hc = h - mean
    var = jnp.mean(hc * hc, axis=-1, keepdims=True)
    return hc * lax.rsqrt(var + LN_EPS) * g + b


def _attn_prompt_kernel(q_ref, kp_ref, kc_ref, vp_ref, vc_ref, gb_ref, mq_ref, gm_ref, mk_ref, mv_ref, gtab_ref,
                        x_ref, mr_ref, w_ref, lng_ref, lnb_ref, y_ref,
                        kcat_ref, vcat_ref, bias_ref, biasv_ref, mix_ref, *, tq):
    j = pl.program_id(1)
    n_chunks = tq // CHUNK

    @pl.when((pl.program_id(0) == 0) & (j == 0))
    def _():
        _fill_rel_bias(bias_ref, gtab_ref, CHUNK - 1)
        kcol = lax.broadcasted_iota(jnp.int32, (1, BAND_LEN), 1)
        for h in range(B_HEADS):
            rows = slice((h % 2) * CHUNK, (h % 2 + 1) * CHUNK)
            scaled = bias_ref[h] * LOG2E
            biasv_ref[0, h // 2, rows, :] = scaled
            for i in range(n_chunks):
                biasv_ref[1 + i, h // 2, rows, :] = jnp.where(kcol >= BAND_WINDOW - i * CHUNK, scaled, -jnp.inf)

    kcat_ref[0:BAND_WINDOW] = kp_ref[0].astype(BF16)
    kcat_ref[BAND_WINDOW:] = kc_ref[0].astype(BF16)
    vcat_ref[0:BAND_WINDOW] = vp_ref[0].astype(BF16)
    vcat_ref[BAND_WINDOW:] = vc_ref[0].astype(BF16)
    q = (q_ref[0] * (ATT_SCALE * LOG2E)).astype(BF16)
    gate_b = _silu(gb_ref[0])
    left = lax.broadcasted_iota(jnp.int32, (1, PAIR), 1) < HEAD_DIM
    mq = (mq_ref[0] * (ATT_SCALE * LOG2E)).astype(BF16)
    gate_m = _silu(gm_ref[0])
    mk = mk_ref[0].astype(BF16)
    mv = mv_ref[0].astype(BF16)

    def write_band(jb, o):
        mix_ref[jb["rows"], jb["lanes"]] = (o * gate_b[jb["rows"], jb["lanes"]]).astype(BF16)

    def write_mem(jb, o):
        mix_ref[jb["rows"], B_WIDTH + jb["lanes"].start:B_WIDTH + jb["lanes"].stop] = (
            o * gate_m[jb["rows"], jb["lanes"]]).astype(BF16)

    groups = []
    for i0 in range(0, n_chunks, ATTN_GROUP_CHUNKS):
        jobs = []
        for i in range(i0, i0 + ATTN_GROUP_CHUNKS):
            rows = slice(i * CHUNK, (i + 1) * CHUNK)
            keys = slice(i * CHUNK, i * CHUNK + BAND_LEN)
            variant = jnp.where(j == 0, 1 + i, 0)
            for pr in range(B_HEADS // 2):
                lanes = slice(pr * PAIR, (pr + 1) * PAIR)
                jobs.append(dict(rows=rows, lanes=lanes, lhs=_stack_heads(q[rows, lanes], left),
                                 k=kcat_ref[keys, lanes], v=vcat_ref[keys, lanes],
                                 bias=biasv_ref[variant, pr]))
        groups.append(_attend_stages(jobs, left, write_band))
    for r0 in range(0, tq, MEM_ROWS * MEM_GROUP_BLOCKS):
        jobs = []
        for r in range(r0, r0 + MEM_ROWS * MEM_GROUP_BLOCKS, MEM_ROWS):
            rows = slice(r, r + MEM_ROWS)
            for pr in range(M_HEADS // 2):
                lanes = slice(pr * PAIR, (pr + 1) * PAIR)
                jobs.append(dict(rows=rows, lanes=lanes, lhs=_stack_heads(mq[rows, lanes], left),
                                 k=mk[:, lanes], v=mv[:, lanes]))
        groups.append(_attend_stages(jobs, left, write_mem))
    _run_staggered(groups)

    o = _dot(mr_ref[0].astype(BF16), w_ref[0:R_WIDTH, :]) + _dot(mix_ref[...], w_ref[R_WIDTH:, :])
    y_ref[0] = _layer_norm(ALPHA * x_ref[0] + o, lng_ref[...], lnb_ref[...])


def _attn_prompt(att, memkv, gtab, x, mix_r, w_out_bf16, ln_g, ln_b, *, tq):
    b, t, _ = att.shape
    col = lambda cidx: pl.BlockSpec((1, tq, B_WIDTH), lambda i, j: (i, j, cidx))
    prev = lambda cidx: pl.BlockSpec((1, tq, B_WIDTH), lambda i, j: (i, jnp.maximum(j - 1, 0), cidx))
    tile = lambda width: pl.BlockSpec((1, tq, width), lambda i, j: (i, j, 0))
    full = lambda arr: pl.BlockSpec(arr.shape, lambda i, j: (0, 0))
    return pl.pallas_call(
        functools.partial(_attn_prompt_kernel, tq=tq),
        name="attn_prompt",
        grid=(b, t // tq),
        in_specs=[col(0), prev(1), col(1), prev(2), col(2), col(3), col(4), col(5),
                  pl.BlockSpec((1, N_MEM, B_WIDTH), lambda i, j: (i, 0, 0)),
                  pl.BlockSpec((1, N_MEM, B_WIDTH), lambda i, j: (i, 0, 1)),
                  full(gtab), tile(D_MODEL), tile(R_WIDTH), full(w_out_bf16), full(ln_g), full(ln_b)],
        out_specs=tile(D_MODEL),
        out_shape=jax.ShapeDtypeStruct((b, t, D_MODEL), F32),
        scratch_shapes=[pltpu.VMEM((2 * BAND_WINDOW, B_WIDTH), BF16),
                        pltpu.VMEM((2 * BAND_WINDOW, B_WIDTH), BF16),
                        pltpu.VMEM((B_HEADS, CHUNK, BAND_LEN), F32),
                        pltpu.VMEM((1 + tq // CHUNK, B_HEADS // 2, 2 * CHUNK, BAND_LEN), F32),
                        pltpu.VMEM((tq, 2 * B_WIDTH), BF16)],
        compiler_params=pltpu.CompilerParams(
            dimension_semantics=("arbitrary", "arbitrary"), vmem_limit_bytes=VMEM_LIMIT),
    )(att, att, att, att, att, att, att, att, memkv, memkv, gtab, x, mix_r, w_out_bf16, ln_g, ln_b)


def _attn_sample_kernel(att_ref, ck_ref, cv_ref, mk_ref, mv_ref, gtab_ref, out_ref, bias_ref, bias2_ref):
    n_seq, n_new, _ = att_ref.shape

    @pl.when(pl.program_id(0) == 0)
    def _():
        _fill_rel_bias(bias_ref, gtab_ref, n_new - 1)
        for h in range(B_HEADS):
            bias2_ref[h // 2, (h % 2) * n_new:(h % 2 + 1) * n_new, :] = bias_ref[h] * LOG2E

    left = lax.broadcasted_iota(jnp.int32, (1, PAIR), 1) < HEAD_DIM
    jobs = []
    for b in range(n_seq):
        att = att_ref[b]
        q = (att[:, 0:B_WIDTH] * (ATT_SCALE * LOG2E)).astype(BF16)
        k_all = jnp.concatenate([ck_ref[b].astype(BF16), att[:, B_WIDTH:2 * B_WIDTH].astype(BF16)], axis=0)
        v_all = jnp.concatenate([cv_ref[b].astype(BF16), att[:, 2 * B_WIDTH:3 * B_WIDTH].astype(BF16)], axis=0)
        gate_b = _silu(att[:, 3 * B_WIDTH:4 * B_WIDTH])
        mq = (att[:, 4 * B_WIDTH:5 * B_WIDTH] * (ATT_SCALE * LOG2E)).astype(BF16)
        gate_m = _silu(att[:, 5 * B_WIDTH:6 * B_WIDTH])
        mk = mk_ref[b].astype(BF16)
        mv = mv_ref[b].astype(BF16)
        for pr in range(B_HEADS // 2):
            lanes = slice(pr * PAIR, (pr + 1) * PAIR)
            jobs.append(dict(b=b, out=lanes, gate=gate_b[:, lanes], lhs=_stack_heads(q[:, lanes], left),
                             k=k_all[:, lanes], v=v_all[:, lanes], bias=bias2_ref[pr]))
        for pr in range(M_HEADS // 2):
            lanes = slice(pr * PAIR, (pr + 1) * PAIR)
            jobs.append(dict(b=b, out=slice(B_WIDTH + lanes.start, B_WIDTH + lanes.stop), gate=gate_m[:, lanes],
                             lhs=_stack_heads(mq[:, lanes], left), k=mk[:, lanes], v=mv[:, lanes]))
    def write(jb, o):
        out_ref[jb["b"], :, jb["out"]] = o * jb["gate"]

    _run_staggered([_attend_stages(jobs, left, write)])


def _attn_sample(att, cache_k, cache_v, mem_k, mem_v, gtab, *, n_seq):
    b, s, _ = att.shape
    per_b = lambda arr: pl.BlockSpec((n_seq,) + arr.shape[1:], lambda i: (i, 0, 0))
    n_keys = cache_k.shape[1] + s
    return pl.pallas_call(
        _attn_sample_kernel,
        name="attn_sample",
        grid=(b // n_seq,),
        in_specs=[per_b(att), per_b(cache_k), per_b(cache_v), per_b(mem_k), per_b(mem_v),
                  pl.BlockSpec(gtab.shape, lambda i: (0, 0))],
        out_specs=pl.BlockSpec((n_seq, s, 2 * B_WIDTH), lambda i: (i, 0, 0)),
        out_shape=jax.ShapeDtypeStruct((b, s, 2 * B_WIDTH), F32),
        scratch_shapes=[pltpu.VMEM((B_HEADS, s, n_keys), F32),
                        pltpu.VMEM((B_HEADS // 2, 2 * s, n_keys), F32)],
        compiler_params=pltpu.CompilerParams(dimension_semantics=("arbitrary",)),
    )(att, cache_k, cache_v, mem_k, mem_v, gtab)


def _finish_kernel(x_ref, mr_ref, mbm_ref, w_ref, g_ref, b_ref, y_ref):
    o = _dot(mr_ref[...].astype(BF16), w_ref[0:R_WIDTH, :])
    o = o + _dot(mbm_ref[...].astype(BF16), w_ref[R_WIDTH:, :])
    y_ref[...] = _layer_norm(ALPHA * x_ref[...] + o, g_ref[...], b_ref[...])


def _finish(x, mix_r, mix_bm, w_out_bf16, ln_g, ln_b, tm):
    m = x.shape[0]
    rows = lambda width: pl.BlockSpec((tm, width), lambda i: (i, 0))
    full = lambda arr: pl.BlockSpec(arr.shape, lambda i: (0, 0))
    return pl.pallas_call(
        _finish_kernel,
        name=f"finish_{m}",
        grid=(m // tm,),
        in_specs=[rows(D_MODEL), rows(R_WIDTH), rows(2 * B_WIDTH), full(w_out_bf16), full(ln_g), full(ln_b)],
        out_specs=rows(D_MODEL),
        out_shape=jax.ShapeDtypeStruct((m, D_MODEL), F32),
        compiler_params=pltpu.CompilerParams(
            dimension_semantics=("arbitrary",), vmem_limit_bytes=VMEM_LIMIT),
    )(x, mix_r, mix_bm, w_out_bf16, ln_g, ln_b)


IN_SPLITS = ((0, SHIFT_WIDTH), (SHIFT_WIDTH, SHIFT_WIDTH + R_WIDTH), (SHIFT_WIDTH + R_WIDTH, IN_WIDTH))


def _rel_bias_row(table, rel0):
    n_hi = rel0 - REL_CLIP
    n_lo = BIAS_L - n_hi - (2 * REL_CLIP + 1)
    heads = table.shape[0]
    return jnp.concatenate([jnp.broadcast_to(table[:, 2 * REL_CLIP:], (heads, n_hi)), table[:, ::-1],
                            jnp.broadcast_to(table[:, 0:1], (heads, n_lo))], axis=1)


def kernel(x_prompt, x_sample, mem_prompt, state_shift, state_wkv, cache_band_k, cache_band_v,
           cache_mem_k, cache_mem_v, w_in, mu_shift, w0, w2, a0, a2, k_k, k_a, r_k, gn_g, gn_b,
           rel_bias, w_mem_kv, w_out, ln_g, ln_b):
    bp, t, _ = x_prompt.shape
    bs, s, _ = x_sample.shape
    depth = w_in.shape[0]
    assert depth == 1 and t % BAND_WINDOW == 0 and t % (WKV_CHUNKS_PER_STEP * CHUNK) == 0
    assert s <= CHUNK and s & (s - 1) == 0 and bp % WKV_PROMPT_SEQS == 0
    assert bs % WKV_SAMPLE_SEQS == 0 and bs % ATTN_SAMPLE_SEQS == 0 and (bp * t) % PROJ_ROWS == 0
    keep = min(BAND_WINDOW, t)
    l = 0

    w_in_b = w_in[l].astype(BF16)
    w_out_b = w_out[l].astype(BF16)
    w_mem_b = w_mem_kv[l].astype(BF16)
    row = lambda p: p.reshape(1, -1)
    wkv_params = (row(mu_shift[l]), row(w0[l]), w2[l], row(a0[l]), a2[l], row(k_k[l]), row(k_a[l]),
                  row(r_k[l]), row(gn_g[l]), row(gn_b[l]))
    table = rel_bias[l]
    r_rows = cache_band_k.shape[2]
    gtab_p = _rel_bias_row(table, BAND_WINDOW + CHUNK - 1)
    gtab_s = _rel_bias_row(table, r_rows + s - 1)

    xp = x_prompt.reshape(bp * t, D_MODEL)
    zs, gr, att = _proj(xp, w_in_b, IN_SPLITS, PROJ_ROWS)
    zs = zs.reshape(bp, t, SHIFT_WIDTH)
    att = att.reshape(bp, t, ATT_WIDTH)
    memkv, = _proj(mem_prompt.reshape(bp * N_MEM, D_MODEL), w_mem_b, ((0, 2 * B_WIDTH),), PROJ_ROWS)
    memkv = memkv.reshape(bp, N_MEM, 2 * B_WIDTH)
    mix_r, p_wkv = _wkv(zs, gr.reshape(bp, t, R_WIDTH), jnp.zeros((bp, 1, SHIFT_WIDTH), F32),
                        jnp.zeros((bp, R_HEADS, HEAD_DIM, HEAD_DIM), F32), wkv_params,
                        bb_n=WKV_PROMPT_SEQS, c=CHUNK, n_ch=WKV_CHUNKS_PER_STEP, n_grp=WKV_GROUP_CHUNKS)
    y_prompt = _attn_prompt(att, memkv, gtab_p, x_prompt, mix_r, w_out_b, row(ln_g[l]), row(ln_b[l]),
                            tq=BAND_WINDOW)
    p_shift = zs[:, -1]
    p_bk = att[:, t - keep:, B_WIDTH:2 * B_WIDTH].reshape(bp, keep, B_HEADS, HEAD_DIM)
    p_bv = att[:, t - keep:, 2 * B_WIDTH:3 * B_WIDTH].reshape(bp, keep, B_HEADS, HEAD_DIM)
    p_mk = memkv[:, :, :B_WIDTH].reshape(bp, N_MEM, M_HEADS, HEAD_DIM)
    p_mv = memkv[:, :, B_WIDTH:].reshape(bp, N_MEM, M_HEADS, HEAD_DIM)

    xs = x_sample.reshape(bs * s, D_MODEL)
    zs_s, gr_s, att_s = _proj(xs, w_in_b, IN_SPLITS, bs * s)
    zs_s = zs_s.reshape(bs, s, SHIFT_WIDTH)
    att_s = att_s.reshape(bs, s, ATT_WIDTH)
    mix_r_s, s_wkv = _wkv(zs_s, gr_s.reshape(bs, s, R_WIDTH), state_shift[l][:, None, :], state_wkv[l],
                          wkv_params, bb_n=WKV_SAMPLE_SEQS, c=s, n_ch=1)
    mix_bm_s = _attn_sample(att_s,
                            cache_band_k[l].reshape(bs, r_rows, B_WIDTH),
                            cache_band_v[l].reshape(bs, r_rows, B_WIDTH),
                            cache_mem_k[l].reshape(bs, N_MEM, B_WIDTH),
                            cache_mem_v[l].reshape(bs, N_MEM, B_WIDTH), gtab_s, n_seq=ATTN_SAMPLE_SEQS)
    y_sample = _finish(xs, mix_r_s.reshape(bs * s, R_WIDTH), mix_bm_s.reshape(bs * s, 2 * B_WIDTH),
                       w_out_b, row(ln_g[l]), row(ln_b[l]), bs * s).reshape(bs, s, D_MODEL)
    s_shift = zs_s[:, -1]
    s_bk = att_s[:, :, B_WIDTH:2 * B_WIDTH].reshape(bs, s, B_HEADS, HEAD_DIM)
    s_bv = att_s[:, :, 2 * B_WIDTH:3 * B_WIDTH].reshape(bs, s, B_HEADS, HEAD_DIM)

    st = lambda a: a[None]
    return (y_prompt, y_sample, st(p_shift), st(p_wkv), st(p_bk), st(p_bv), st(p_mk), st(p_mv),
            st(s_shift), st(s_wkv), st(s_bk), st(s_bv))
```

```python
import functools

import numpy as np
import jax
import jax.numpy as jnp
from jax import lax
from jax.experimental import pallas as pl
from jax.experimental.pallas import tpu as pltpu

F32 = jnp.float32
BF16 = jnp.bfloat16

D_MODEL = 1024
HEAD_DIM = 64
R_WIDTH = 512
R_HEADS = 8
LOW_RANK = 64
SHIFT_WIDTH = 3 * R_WIDTH + 2 * LOW_RANK
B_WIDTH = 256
B_HEADS = 4
M_HEADS = 4
N_MEM = 256
CHUNK = 64
BAND_CHUNKS = 8
BAND_WINDOW = BAND_CHUNKS * CHUNK
BAND_LEN = BAND_WINDOW + CHUNK
REL_CLIP = 128
ATT_WIDTH = 6 * B_WIDTH
IN_WIDTH = SHIFT_WIDTH + R_WIDTH + ATT_WIDTH
LN_EPS = 1e-5
GN_EPS = 64e-5
ALPHA = 2.0 ** 0.25
ATT_SCALE = HEAD_DIM ** -0.5
LOG2E = float(np.log2(np.e))
BIAS_L = 1024
PAIR = 2 * HEAD_DIM
ATTN_GROUP_CHUNKS = 2
MEM_ROWS = 128
MEM_GROUP_BLOCKS = 2
WKV_CHUNKS_PER_STEP = 4
WKV_GROUP_CHUNKS = 2
STAGE_SPLIT = 16
PROJ_ROWS = 256
WKV_PROMPT_SEQS = 2
WKV_SAMPLE_SEQS = 16
ATTN_SAMPLE_SEQS = 4

VMEM_LIMIT = 48 * 1024 * 1024

NN = ((1,), (0,))
NT = ((1,), (1,))
TN = ((0,), (0,))


def _dot(a, b, dims=NN):
    return lax.dot_general(a, b, (dims, ((), ())), preferred_element_type=F32)


def _split2(x):
    hi = x.astype(BF16)
    lo = (x - hi.astype(F32)).astype(BF16)
    return hi, lo


def _mm3(a, b, dims=NN):
    ah, al = _split2(a)
    bh, bl = _split2(b)
    return _dot(ah, bh, dims) + _dot(ah, bl, dims) + _dot(al, bh, dims)


def _mm_exact_lhs(lhs_bf16, x):
    x1 = x.astype(BF16)
    r1 = x - x1.astype(F32)
    x2 = r1.astype(BF16)
    x3 = (r1 - x2.astype(F32)).astype(BF16)
    return _dot(lhs_bf16, x1) + _dot(lhs_bf16, x2) + _dot(lhs_bf16, x3)


def _sigmoid(x):
    return 1.0 / (1.0 + jnp.exp(-x))


def _silu(x):
    return x * _sigmoid(x)


def _proj_kernel(x_ref, w_ref, *out_refs, splits):
    x = x_ref[...].astype(BF16)
    for o_ref, (lo, hi) in zip(out_refs, splits):
        o_ref[...] = _dot(x, w_ref[:, lo:hi])


def _proj(x, w_bf16, splits, tm):
    m, k = x.shape
    n = w_bf16.shape[1]
    return pl.pallas_call(
        functools.partial(_proj_kernel, splits=splits),
        name=f"proj_{m}x{n}",
        grid=(m // tm,),
        in_specs=[pl.BlockSpec((tm, k), lambda i: (i, 0)),
                  pl.BlockSpec((k, n), lambda i: (0, 0))],
        out_specs=[pl.BlockSpec((tm, hi - lo), lambda i: (i, 0)) for lo, hi in splits],
        out_shape=[jax.ShapeDtypeStruct((m, hi - lo), F32) for lo, hi in splits],
        compiler_params=pltpu.CompilerParams(
            dimension_semantics=("arbitrary",), vmem_limit_bytes=VMEM_LIMIT),
    )(x, w_bf16)


def _pair_blocks(x2, left):
    zero = jnp.zeros((), x2.dtype)
    return jnp.concatenate([jnp.where(left, x2, zero), jnp.where(left, zero, x2)], axis=0)


def _pair_sum(x2, left):
    s0 = jnp.sum(jnp.where(left, x2, 0.0), axis=-1, keepdims=True)
    s1 = jnp.sum(jnp.where(left, 0.0, x2), axis=-1, keepdims=True)
    return jnp.where(left, s0, s1)


UNIT_COMMON = ("ar", "bkp", "v16", "pc", "gate", "bonus")
UNIT_FREE_OUT = ("a4", "tinv", "akv")
UNIT_FREE_IN = ("bk_bd",)


def _wkv_kernel(z0_ref, g0_ref, zn_ref, gn_ref, prev_ref, s0_ref, mu_ref, w0_ref, w2_ref, a0_ref, a2_ref,
                kk_ref, ka_ref, rk_ref, gng_ref, gnb_ref, out_ref, sfin_ref, carry_ref, state_ref, *unit_refs,
                bb_n, c, n_ch, n_grp, pipelined):
    t = pl.program_id(1)
    tt = n_ch * c
    n_pairs = R_HEADS // 2
    per_chunk = bb_n * n_pairs

    row2 = lax.broadcasted_iota(jnp.int32, (c, 2 * c), 0)
    col2 = lax.broadcasted_iota(jnp.int32, (c, 2 * c), 1) & (c - 1)
    eye2 = (row2 == col2).astype(BF16)
    strict2 = col2 < row2
    incl4 = ((lax.broadcasted_iota(jnp.int32, (c, 4 * c), 1) & (c - 1))
             <= lax.broadcasted_iota(jnp.int32, (c, 4 * c), 0))
    level_masks = []
    shift = 0
    while (1 << shift) < c:
        rb = row2 >> shift
        level_masks.append(((rb & 1) == 1) & ((col2 >> shift) == rb - 1))
        shift += 1
    left_s = lax.broadcasted_iota(jnp.int32, (1, 2 * c), 1) < c
    left = lax.broadcasted_iota(jnp.int32, (1, PAIR), 1) < HEAD_DIM
    zero = jnp.zeros((), BF16)
    row_t = lax.broadcasted_iota(jnp.int32, (tt, tt), 0)
    col_t = lax.broadcasted_iota(jnp.int32, (tt, tt), 1)
    shift_c = c.bit_length() - 1
    ltri = ((col_t <= row_t) & ((col_t >> shift_c) == (row_t >> shift_c))).astype(BF16)
    first_row = lax.broadcasted_iota(jnp.int32, (tt, 1), 0) == 0
    mu = mu_ref[...]

    def field(un, f):
        if f not in un:
            un[f] = _pair_blocks(field(un, "v16"), left) if f == "v_bd" else un["stored"][f]()
        return un[f]

    def make_tile(zs_ref, gr_ref, prev_rows):
        tile = [dict() for _ in range(bb_n)]
        chunks = [[] for _ in range(n_ch)]

        def whole_tile(bb):
            zs = zs_ref[bb]
            zprev = jnp.where(first_row, prev_rows[bb], pltpu.roll(zs, 1, 0))
            xs = zs + (zprev - zs) * mu
            wd = xs[:, 3 * R_WIDTH:3 * R_WIDTH + LOW_RANK]
            ad = xs[:, 3 * R_WIDTH + LOW_RANK:]
            u = -(w0_ref[...] + _mm3(jnp.tanh(wd), w2_ref[...]))
            softplus = jnp.maximum(u, 0.0) + jnp.log(1.0 + jnp.exp(-jnp.abs(u)))
            ld = -jnp.exp(-softplus - 0.5)
            tile[bb].update(xs=xs, ld=ld, cum=_mm_exact_lhs(ltri, ld),
                            a_pre=a0_ref[...] + _mm3(ad, a2_ref[...]))

        def prep_tasks(ch):
            rows = slice(ch * c, (ch + 1) * c)
            tasks = []
            for bb in range(bb_n):
                wide = {}

                def full_width(bb=bb, wide=wide):
                    xs = tile[bb]["xs"][rows]
                    r = xs[:, 0:R_WIDTH]
                    k = xs[:, R_WIDTH:2 * R_WIDTH]
                    ld = tile[bb]["ld"][rows]
                    cum = tile[bb]["cum"][rows]
                    a = _sigmoid(tile[bb]["a_pre"][rows])
                    ecum = jnp.exp(cum)
                    k2 = k * (1.0 + (a - 1.0) * ka_ref[...])
                    wide.update(v=xs[:, 2 * R_WIDTH:3 * R_WIDTH], a=a, ecum=ecum, einv=jnp.exp(-cum),
                                eprev=jnp.exp(cum - ld), kkr=k * kk_ref[...], rt=r * ecum, k2=k2,
                                rk2=r * k2 * rk_ref[...], gate=_silu(gr_ref[bb, rows, :]))

                def pair(pr, bb=bb, wide=wide):
                    sl = slice(pr * PAIR, (pr + 1) * PAIR)
                    kkr_p = wide["kkr"][:, sl]
                    nrm = jnp.sqrt(_pair_sum(kkr_p * kkr_p, left))
                    kkn = kkr_p / jnp.maximum(nrm, 1e-12)
                    at = -kkn * wide["eprev"][:, sl]
                    bh = kkn * wide["a"][:, sl] * wide["einv"][:, sl]
                    kh = wide["k2"][:, sl] * wide["einv"][:, sl]
                    pc = wide["ecum"][c - 1:c, sl]
                    v_p = wide["v"][:, sl]
                    v16 = v_p.astype(BF16)
                    chunks[ch].append(dict(
                        bb=bb, pr=pr, sl=sl, rows=rows, pc=pc, gate=wide["gate"][:, sl],
                        bonus=_pair_sum(wide["rk2"][:, sl], left) * v_p, v_bd=_pair_blocks(v16, left), v16=v16,
                        ar=jnp.concatenate([at, wide["rt"][:, sl]], axis=0).astype(BF16),
                        bk_bd=jnp.concatenate([_pair_blocks(bh.astype(BF16), left),
                                               _pair_blocks(kh.astype(BF16), left)], axis=0),
                        bkp=jnp.concatenate([bh * pc, kh * pc], axis=0).astype(BF16)))

                tasks.append(full_width)
                tasks += [functools.partial(pair, pr) for pr in range(n_pairs)]
            return tasks

        return chunks, [functools.partial(whole_tile, bb) for bb in range(bb_n)], prep_tasks

    def st_a4(units):
        for un in units:
            un["a4"] = _dot(field(un, "ar"), field(un, "bk_bd"), NT).astype(BF16)
            un["tinv"] = jnp.where(level_masks[0], un["a4"][:c, :2 * c], eye2)

    def st_lt(mask):
        def run(units):
            for un in units:
                a_ab = un["a4"][:c, :2 * c]
                un["lt"] = _dot(jnp.where(mask, a_ab, zero), _pair_blocks(un["tinv"], left_s)).astype(BF16)
        return run

    def st_tinv(mask):
        def run(units):
            for un in units:
                new = _dot(un["tinv"], _pair_blocks(un["lt"], left_s)).astype(BF16)
                un["tinv"] = jnp.where(mask, new, un["tinv"])
        return run

    def st_akv(units):
        for un in units:
            un["akv"] = _dot(jnp.where(strict2, un["a4"][:c, 2 * c:], zero), field(un, "v_bd"))

    def st_ars(units):
        for un in units:
            un["s0"] = state[un["bb"], un["pr"]]
            un["ars"] = _dot(field(un, "ar"), _pair_blocks(un["s0"].astype(BF16), left), NT)

    def st_pm(units):
        for un in units:
            rhs = (un["ars"][:c] + field(un, "akv")).astype(BF16)
            un["pm"] = _dot(field(un, "tinv"), _pair_blocks(rhs, left)).astype(BF16)

    def st_state(units):
        for un in units:
            pv = jnp.concatenate([un["pm"], field(un, "v16")], axis=0)
            cross = _dot(pv, field(un, "bkp"), TN)
            state[un["bb"], un["pr"]] = (un["s0"] * field(un, "pc")
                                         + jnp.where(left, cross[:HEAD_DIM], cross[HEAD_DIM:]))

    def st_y(units):
        for un in units:
            pv_bd = jnp.concatenate([_pair_blocks(un["pm"], left), field(un, "v_bd")], axis=0)
            y = un["ars"][c:] + _dot(jnp.where(incl4, field(un, "a4")[c:], zero), pv_bd)
            mean = _pair_sum(y, left) * (1.0 / HEAD_DIM)
            yc = y - mean
            var = _pair_sum(yc * yc, left) * (1.0 / HEAD_DIM)
            yn = yc * lax.rsqrt(var + GN_EPS) * gng_ref[:, un["sl"]] + gnb_ref[:, un["sl"]]
            out_ref[un["bb"], un["rows"], un["sl"]] = (yn + field(un, "bonus")) * field(un, "gate")

    free_stages = [st_a4]
    for mask in level_masks[1:]:
        free_stages += [st_lt(mask), st_tinv(mask)]
    free_stages.append(st_akv)
    state_stages = [st_ars, st_pm, st_state, st_y]

    def stage_tasks(stages, chunks, chs):
        def run(stage, lo):
            units = [un for ch in chs for un in chunks[ch]]
            stage(units[lo:lo + STAGE_SPLIT])

        return [functools.partial(run, stage, lo) for stage in stages
                for lo in range(0, len(chs) * per_chunk, STAGE_SPLIT)]

    def run_interleaved(*task_lists):
        keyed = [((i + 0.5) / len(tasks), n, i, task)
                 for n, tasks in enumerate(task_lists) for i, task in enumerate(tasks)]
        for _, _, _, task in sorted(keyed, key=lambda e: e[:3]):
            task()

    groups = [list(range(g0, min(g0 + n_grp, n_ch))) for g0 in range(0, n_ch, n_grp)]

    if pipelined:
        refs = dict(zip(UNIT_COMMON + UNIT_FREE_OUT + UNIT_FREE_IN, unit_refs))
        n_g = n_grp * per_chunk

        def store_units(chunks, chs, fields, base):
            units = [un for ch in chs for un in chunks[ch]]
            for idx, un in enumerate(units):
                for f in fields:
                    refs[f][(base if f in UNIT_COMMON else 0) + idx] = un[f]

        def load_units(chs, fields, base):
            chunks = {}
            idx = 0
            for ch in chs:
                chunks[ch] = []
                for bb in range(bb_n):
                    for pr in range(n_pairs):
                        un = dict(bb=bb, pr=pr, sl=slice(pr * PAIR, (pr + 1) * PAIR),
                                  rows=slice(ch * c, (ch + 1) * c))
                        un["stored"] = {f: functools.partial(lambda f, i: refs[f][i], f,
                                                             (base if f in UNIT_COMMON else 0) + idx)
                                        for f in fields}
                        chunks[ch].append(un)
                        idx += 1
            return chunks

        def hand_over(chunks):
            store_units(chunks, groups[0], UNIT_COMMON + UNIT_FREE_OUT, 0)
            store_units(chunks, groups[1], UNIT_COMMON + UNIT_FREE_IN, n_g)

        @pl.when(t == 0)
        def _():
            for bb in range(bb_n):
                for h in range(R_HEADS):
                    state_ref[bb, h // 2, :, (h % 2) * HEAD_DIM:(h % 2 + 1) * HEAD_DIM] = s0_ref[bb, h]
            first, whole, prep = make_tile(z0_ref, g0_ref, [prev_ref[bb] for bb in range(bb_n)])
            run_interleaved(whole)
            run_interleaved([task for ch in range(n_ch) for task in prep(ch)])
            run_interleaved(stage_tasks(free_stages, first, groups[0]))
            hand_over(first)
            carry_ref[...] = z0_ref[:, tt - 1:tt, :]

        cur = load_units(groups[0], UNIT_COMMON + UNIT_FREE_OUT, 0)
        cur.update(load_units(groups[1], UNIT_COMMON + UNIT_FREE_IN, n_g))
        nxt, whole, prep = make_tile(zn_ref, gn_ref, [carry_ref[bb] for bb in range(bb_n)])
        state = {(bb, pr): state_ref[bb, pr] for bb in range(bb_n) for pr in range(n_pairs)}
        run_interleaved([task for ch in groups[0] for task in stage_tasks(state_stages, cur, [ch])],
                        stage_tasks(free_stages, cur, groups[1]),
                        whole + [task for ch in groups[0] for task in prep(ch)])
        run_interleaved([task for ch in groups[1] for task in stage_tasks(state_stages, cur, [ch])],
                        stage_tasks(free_stages, nxt, groups[0]),
                        [task for ch in groups[1] for task in prep(ch)])
        hand_over(nxt)
        carry_ref[...] = zn_ref[:, tt - 1:tt, :]
    else:
        @pl.when(t == 0)
        def _():
            carry_ref[...] = prev_ref[...]
            for bb in range(bb_n):
                for h in range(R_HEADS):
                    state_ref[bb, h // 2, :, (h % 2) * HEAD_DIM:(h % 2 + 1) * HEAD_DIM] = s0_ref[bb, h]

        cur, whole, prep = make_tile(z0_ref, g0_ref, [carry_ref[bb] for bb in range(bb_n)])
        state = {(bb, pr): state_ref[bb, pr] for bb in range(bb_n) for pr in range(n_pairs)}
        run_interleaved(whole)
        carry_ref[...] = z0_ref[:, tt - 1:tt, :]
        run_interleaved([task for ch in range(n_ch) for task in prep(ch)])
        run_interleaved(stage_tasks(free_stages, cur, groups[0]))
        for g, chs in enumerate(groups):
            run_interleaved([task for ch in chs for task in stage_tasks(state_stages, cur, [ch])],
                            stage_tasks(free_stages, cur, groups[g + 1]) if g + 1 < len(groups) else [])
    for (bb, pr), val in state.items():
        state_ref[bb, pr] = val

    @pl.when(t == pl.num_programs(1) - 1)
    def _():
        for bb in range(bb_n):
            for h in range(R_HEADS):
                sfin_ref[bb, h] = state_ref[bb, h // 2, :, (h % 2) * HEAD_DIM:(h % 2 + 1) * HEAD_DIM]


def _wkv(zs, gr, prev, s0, params, *, bb_n, c, n_ch, n_grp=1):
    b, t, _ = zs.shape
    tt = n_ch * c
    n_t = t // tt
    pipelined = n_ch == 2 * n_grp and n_t > 1
    full = lambda arr: pl.BlockSpec(arr.shape, lambda i, j: (0,) * arr.ndim)
    first = lambda width: pl.BlockSpec((bb_n, tt, width), lambda i, j: (i, 0 if pipelined else j, 0))
    ahead = lambda width: pl.BlockSpec((bb_n, tt, width), lambda i, j: (i, jnp.minimum(j + 1, n_t - 1), 0))
    unit_scratch = []
    if pipelined:
        n_g = n_grp * bb_n * (R_HEADS // 2)
        shapes = dict(ar=((2 * c, PAIR), BF16), bkp=((2 * c, PAIR), BF16), v16=((c, PAIR), BF16),
                      pc=((1, PAIR), F32), gate=((c, PAIR), F32), bonus=((c, PAIR), F32),
                      a4=((2 * c, 4 * c), BF16), tinv=((c, 2 * c), BF16), akv=((c, PAIR), F32),
                      bk_bd=((4 * c, PAIR), BF16))
        for f in UNIT_COMMON + UNIT_FREE_OUT + UNIT_FREE_IN:
            shape, dtype = shapes[f]
            unit_scratch.append(pltpu.VMEM(((2 * n_g if f in UNIT_COMMON else n_g),) + shape, dtype))
    return pl.pallas_call(
        functools.partial(_wkv_kernel, bb_n=bb_n, c=c, n_ch=n_ch, n_grp=n_grp, pipelined=pipelined),
        name=f"wkv_c{c}",
        grid=(b // bb_n, n_t),
        in_specs=[first(SHIFT_WIDTH), first(R_WIDTH), ahead(SHIFT_WIDTH), ahead(R_WIDTH),
                  pl.BlockSpec((bb_n, 1, SHIFT_WIDTH), lambda i, j: (i, 0, 0)),
                  pl.BlockSpec((bb_n, R_HEADS, HEAD_DIM, HEAD_DIM), lambda i, j: (i, 0, 0, 0))]
                 + [full(p) for p in params],
        out_specs=[pl.BlockSpec((bb_n, tt, R_WIDTH), lambda i, j: (i, j, 0)),
                   pl.BlockSpec((bb_n, R_HEADS, HEAD_DIM, HEAD_DIM), lambda i, j: (i, 0, 0, 0))],
        out_shape=[jax.ShapeDtypeStruct((b, t, R_WIDTH), F32),
                   jax.ShapeDtypeStruct((b, R_HEADS, HEAD_DIM, HEAD_DIM), F32)],
        scratch_shapes=[pltpu.VMEM((bb_n, 1, SHIFT_WIDTH), F32),
                        pltpu.VMEM((bb_n, R_HEADS // 2, HEAD_DIM, PAIR), F32)] + unit_scratch,
        compiler_params=pltpu.CompilerParams(
            dimension_semantics=("arbitrary", "arbitrary"), vmem_limit_bytes=VMEM_LIMIT),
    )(zs, gr, zs, gr, prev, s0, *params)


def _fill_rel_bias(bias_ref, gtab_ref, offset):
    heads, nq, nk = bias_ref.shape
    for h in range(heads):
        g = jnp.broadcast_to(gtab_ref[h:h + 1, :], (nq, BIAS_L))
        bias_ref[h] = pltpu.roll(g, BIAS_L - offset, 1, stride=1, stride_axis=0)[:, :nk]


def _stack_heads(q2, left):
    zero = jnp.zeros((), q2.dtype)
    return jnp.concatenate([jnp.where(left, q2, zero), jnp.where(left, zero, q2)], axis=0)


def _attend_stages(jobs, left, write):
    def scores():
        for jb in jobs:
            s = _dot(jb["lhs"], jb["k"], NT)
            jb["s"] = s if jb.get("bias") is None else s + jb["bias"]

    def weights():
        for jb in jobs:
            m = jnp.max(jb["s"], axis=-1, keepdims=True)
            p = jnp.exp2(jb["s"] - m)
            jb["l"] = jnp.sum(p, axis=-1, keepdims=True)
            jb["p"] = p.astype(BF16)

    def values():
        for jb in jobs:
            o2 = _dot(jb["p"], jb["v"]) / jb["l"]
            n = o2.shape[0] // 2
            write(jb, jnp.where(left, o2[:n], o2[n:]))

    return [scores, weights, values]


def _run_staggered(groups):
    depth = max(len(g) for g in groups)
    for step in range(len(groups) + depth - 1):
        for g in range(len(groups) - 1, -1, -1):
            if 0 <= step - g < len(groups[g]):
                groups[g][step - g]()


def _layer_norm(h, g, b):
    mean = jnp.mean(h, axis=-1, keepdims=True)
    hc = h - mean
    var = jnp.mean(hc * hc, axis=-1, keepdims=True)
    return hc * lax.rsqrt(var + LN_EPS) * g + b


def _attn_prompt_kernel(q_ref, kc_ref, vc_ref, gb_ref, mq_ref, gm_ref, mk_ref, mv_ref, gtab_ref,
                        x_ref, mr_ref, w_ref, lng_ref, lnb_ref, y_ref,
                        kcat_ref, vcat_ref, bias_ref, biasv_ref, mix_ref, *, tq):
    j = pl.program_id(1)
    n_chunks = tq // CHUNK

    @pl.when((pl.program_id(0) == 0) & (j == 0))
    def _():
        _fill_rel_bias(bias_ref, gtab_ref, CHUNK - 1)
        kcol = lax.broadcasted_iota(jnp.int32, (1, BAND_LEN), 1)
        for h in range(B_HEADS):
            rows = slice((h % 2) * CHUNK, (h % 2 + 1) * CHUNK)
            scaled = bias_ref[h] * LOG2E
            biasv_ref[0, h // 2, rows, :] = scaled
            for i in range(n_chunks):
                biasv_ref[1 + i, h // 2, rows, :] = jnp.where(kcol >= BAND_WINDOW - i * CHUNK, scaled, -jnp.inf)

    @pl.when(j == 0)
    def _():
        kcat_ref[0:BAND_WINDOW] = jnp.zeros((BAND_WINDOW, B_WIDTH), BF16)
        vcat_ref[0:BAND_WINDOW] = jnp.zeros((BAND_WINDOW, B_WIDTH), BF16)

    @pl.when(j > 0)
    def _():
        kcat_ref[0:BAND_WINDOW] = kcat_ref[BAND_WINDOW:]
        vcat_ref[0:BAND_WINDOW] = vcat_ref[BAND_WINDOW:]

    kcat_ref[BAND_WINDOW:] = kc_ref[0].astype(BF16)
    vcat_ref[BAND_WINDOW:] = vc_ref[0].astype(BF16)
    q = (q_ref[0] * (ATT_SCALE * LOG2E)).astype(BF16)
    gate_b = _silu(gb_ref[0])
    left = lax.broadcasted_iota(jnp.int32, (1, PAIR), 1) < HEAD_DIM
    mq = (mq_ref[0] * (ATT_SCALE * LOG2E)).astype(BF16)
    gate_m = _silu(gm_ref[0])
    mk = mk_ref[0].astype(BF16)
    mv = mv_ref[0].astype(BF16)

    def write_band(jb, o):
        mix_ref[jb["rows"], jb["lanes"]] = (o * gate_b[jb["rows"], jb["lanes"]]).astype(BF16)

    def write_mem(jb, o):
        mix_ref[jb["rows"], B_WIDTH + jb["lanes"].start:B_WIDTH + jb["lanes"].stop] = (
            o * gate_m[jb["rows"], jb["lanes"]]).astype(BF16)

    groups = []
    for i0 in range(0, n_chunks, ATTN_GROUP_CHUNKS):
        jobs = []
        for i in range(i0, i0 + ATTN_GROUP_CHUNKS):
            rows = slice(i * CHUNK, (i + 1) * CHUNK)
            keys = slice(i * CHUNK, i * CHUNK + BAND_LEN)
            variant = jnp.where(j == 0, 1 + i, 0)
            for pr in range(B_HEADS // 2):
                lanes = slice(pr * PAIR, (pr + 1) * PAIR)
                jobs.append(dict(rows=rows, lanes=lanes, lhs=_stack_heads(q[rows, lanes], left),
                                 k=kcat_ref[keys, lanes], v=vcat_ref[keys, lanes],
                                 bias=biasv_ref[variant, pr]))
        groups.append(_attend_stages(jobs, left, write_band))
    for r0 in range(0, tq, MEM_ROWS * MEM_GROUP_BLOCKS):
        jobs = []
        for r in range(r0, r0 + MEM_ROWS * MEM_GROUP_BLOCKS, MEM_ROWS):
            rows = slice(r, r + MEM_ROWS)
            for pr in range(M_HEADS // 2):
                lanes = slice(pr * PAIR, (pr + 1) * PAIR)
                jobs.append(dict(rows=rows, lanes=lanes, lhs=_stack_heads(mq[rows, lanes], left),
                                 k=mk[:, lanes], v=mv[:, lanes]))
        groups.append(_attend_stages(jobs, left, write_mem))
    _run_staggered(groups)

    o = _dot(mr_ref[0].astype(BF16), w_ref[0:R_WIDTH, :]) + _dot(mix_ref[...], w_ref[R_WIDTH:, :])
    y_ref[0] = _layer_norm(ALPHA * x_ref[0] + o, lng_ref[...], lnb_ref[...])


def _attn_prompt(att, memkv, gtab, x, mix_r, w_out_bf16, ln_g, ln_b, *, tq):
    b, t, _ = att.shape
    col = lambda cidx: pl.BlockSpec((1, tq, B_WIDTH), lambda i, j: (i, j, cidx))
    tile = lambda width: pl.BlockSpec((1, tq, width), lambda i, j: (i, j, 0))
    full = lambda arr: pl.BlockSpec(arr.shape, lambda i, j: (0, 0))
    return pl.pallas_call(
        functools.partial(_attn_prompt_kernel, tq=tq),
        name="attn_prompt",
        grid=(b, t // tq),
        in_specs=[col(0), col(1), col(2), col(3), col(4), col(5),
                  pl.BlockSpec((1, N_MEM, B_WIDTH), lambda i, j: (i, 0, 0)),
                  pl.BlockSpec((1, N_MEM, B_WIDTH), lambda i, j: (i, 0, 1)),
                  full(gtab), tile(D_MODEL), tile(R_WIDTH), full(w_out_bf16), full(ln_g), full(ln_b)],
        out_specs=tile(D_MODEL),
        out_shape=jax.ShapeDtypeStruct((b, t, D_MODEL), F32),
        scratch_shapes=[pltpu.VMEM((2 * BAND_WINDOW, B_WIDTH), BF16),
                        pltpu.VMEM((2 * BAND_WINDOW, B_WIDTH), BF16),
                        pltpu.VMEM((B_HEADS, CHUNK, BAND_LEN), F32),
                        pltpu.VMEM((1 + tq // CHUNK, B_HEADS // 2, 2 * CHUNK, BAND_LEN), F32),
                        pltpu.VMEM((tq, 2 * B_WIDTH), BF16)],
        compiler_params=pltpu.CompilerParams(
            dimension_semantics=("arbitrary", "arbitrary"), vmem_limit_bytes=VMEM_LIMIT),
    )(att, att, att, att, att, att, memkv, memkv, gtab, x, mix_r, w_out_bf16, ln_g, ln_b)


def _attn_sample_kernel(att_ref, ck_ref, cv_ref, mk_ref, mv_ref, gtab_ref, out_ref, bias_ref, bias2_ref):
    n_seq, n_new, _ = att_ref.shape

    @pl.when(pl.program_id(0) == 0)
    def _():
        _fill_rel_bias(bias_ref, gtab_ref, n_new - 1)
        for h in range(B_HEADS):
            bias2_ref[h // 2, (h % 2) * n_new:(h % 2 + 1) * n_new, :] = bias_ref[h] * LOG2E

    left = lax.broadcasted_iota(jnp.int32, (1, PAIR), 1) < HEAD_DIM
    jobs = []
    for b in range(n_seq):
        att = att_ref[b]
        q = (att[:, 0:B_WIDTH] * (ATT_SCALE * LOG2E)).astype(BF16)
        k_all = jnp.concatenate([ck_ref[b].astype(BF16), att[:, B_WIDTH:2 * B_WIDTH].astype(BF16)], axis=0)
        v_all = jnp.concatenate([cv_ref[b].astype(BF16), att[:, 2 * B_WIDTH:3 * B_WIDTH].astype(BF16)], axis=0)
        gate_b = _silu(att[:, 3 * B_WIDTH:4 * B_WIDTH])
        mq = (att[:, 4 * B_WIDTH:5 * B_WIDTH] * (ATT_SCALE * LOG2E)).astype(BF16)
        gate_m = _silu(att[:, 5 * B_WIDTH:6 * B_WIDTH])
        mk = mk_ref[b].astype(BF16)
        mv = mv_ref[b].astype(BF16)
        for pr in range(B_HEADS // 2):
            lanes = slice(pr * PAIR, (pr + 1) * PAIR)
            jobs.append(dict(b=b, out=lanes, gate=gate_b[:, lanes], lhs=_stack_heads(q[:, lanes], left),
                             k=k_all[:, lanes], v=v_all[:, lanes], bias=bias2_ref[pr]))
        for pr in range(M_HEADS // 2):
            lanes = slice(pr * PAIR, (pr + 1) * PAIR)
            jobs.append(dict(b=b, out=slice(B_WIDTH + lanes.start, B_WIDTH + lanes.stop), gate=gate_m[:, lanes],
                             lhs=_stack_heads(mq[:, lanes], left), k=mk[:, lanes], v=mv[:, lanes]))
    def write(jb, o):
        out_ref[jb["b"], :, jb["out"]] = o * jb["gate"]

    _run_staggered([_attend_stages(jobs, left, write)])


def _attn_sample(att, cache_k, cache_v, mem_k, mem_v, gtab, *, n_seq):
    b, s, _ = att.shape
    per_b = lambda arr: pl.BlockSpec((n_seq,) + arr.shape[1:], lambda i: (i, 0, 0))
    n_keys = cache_k.shape[1] + s
    return pl.pallas_call(
        _attn_sample_kernel,
        name="attn_sample",
        grid=(b // n_seq,),
        in_specs=[per_b(att), per_b(cache_k), per_b(cache_v), per_b(mem_k), per_b(mem_v),
                  pl.BlockSpec(gtab.shape, lambda i: (0, 0))],
        out_specs=pl.BlockSpec((n_seq, s, 2 * B_WIDTH), lambda i: (i, 0, 0)),
        out_shape=jax.ShapeDtypeStruct((b, s, 2 * B_WIDTH), F32),
        scratch_shapes=[pltpu.VMEM((B_HEADS, s, n_keys), F32),
                        pltpu.VMEM((B_HEADS // 2, 2 * s, n_keys), F32)],
        compiler_params=pltpu.CompilerParams(dimension_semantics=("arbitrary",)),
    )(att, cache_k, cache_v, mem_k, mem_v, gtab)


def _finish_kernel(x_ref, mr_ref, mbm_ref, w_ref, g_ref, b_ref, y_ref):
    o = _dot(mr_ref[...].astype(BF16), w_ref[0:R_WIDTH, :])
    o = o + _dot(mbm_ref[...].astype(BF16), w_ref[R_WIDTH:, :])
    y_ref[...] = _layer_norm(ALPHA * x_ref[...] + o, g_ref[...], b_ref[...])


def _finish(x, mix_r, mix_bm, w_out_bf16, ln_g, ln_b, tm):
    m = x.shape[0]
    rows = lambda width: pl.BlockSpec((tm, width), lambda i: (i, 0))
    full = lambda arr: pl.BlockSpec(arr.shape, lambda i: (0, 0))
    return pl.pallas_call(
        _finish_kernel,
        name=f"finish_{m}",
        grid=(m // tm,),
        in_specs=[rows(D_MODEL), rows(R_WIDTH), rows(2 * B_WIDTH), full(w_out_bf16), full(ln_g), full(ln_b)],
        out_specs=rows(D_MODEL),
        out_shape=jax.ShapeDtypeStruct((m, D_MODEL), F32),
        compiler_params=pltpu.CompilerParams(
            dimension_semantics=("arbitrary",), vmem_limit_bytes=VMEM_LIMIT),
    )(x, mix_r, mix_bm, w_out_bf16, ln_g, ln_b)


IN_SPLITS = ((0, SHIFT_WIDTH), (SHIFT_WIDTH, SHIFT_WIDTH + R_WIDTH), (SHIFT_WIDTH + R_WIDTH, IN_WIDTH))


def _rel_bias_row(table, rel0):
    n_hi = rel0 - REL_CLIP
    n_lo = BIAS_L - n_hi - (2 * REL_CLIP + 1)
    heads = table.shape[0]
    return jnp.concatenate([jnp.broadcast_to(table[:, 2 * REL_CLIP:], (heads, n_hi)), table[:, ::-1],
                            jnp.broadcast_to(table[:, 0:1], (heads, n_lo))], axis=1)


def kernel(x_prompt, x_sample, mem_prompt, state_shift, state_wkv, cache_band_k, cache_band_v,
           cache_mem_k, cache_mem_v, w_in, mu_shift, w0, w2, a0, a2, k_k, k_a, r_k, gn_g, gn_b,
           rel_bias, w_mem_kv, w_out, ln_g, ln_b):
    bp, t, _ = x_prompt.shape
    bs, s, _ = x_sample.shape
    depth = w_in.shape[0]
    assert depth == 1 and t % BAND_WINDOW == 0 and t % (WKV_CHUNKS_PER_STEP * CHUNK) == 0
    assert s <= CHUNK and s & (s - 1) == 0 and bp % WKV_PROMPT_SEQS == 0
    assert bs % WKV_SAMPLE_SEQS == 0 and bs % ATTN_SAMPLE_SEQS == 0 and (bp * t) % PROJ_ROWS == 0
    keep = min(BAND_WINDOW, t)
    l = 0

    w_in_b = w_in[l].astype(BF16)
    w_out_b = w_out[l].astype(BF16)
    w_mem_b = w_mem_kv[l].astype(BF16)
    row = lambda p: p.reshape(1, -1)
    wkv_params = (row(mu_shift[l]), row(w0[l]), w2[l], row(a0[l]), a2[l], row(k_k[l]), row(k_a[l]),
                  row(r_k[l]), row(gn_g[l]), row(gn_b[l]))
    table = rel_bias[l]
    r_rows = cache_band_k.shape[2]
    gtab_p = _rel_bias_row(table, BAND_WINDOW + CHUNK - 1)
    gtab_s = _rel_bias_row(table, r_rows + s - 1)

    xp = x_prompt.reshape(bp * t, D_MODEL)
    zs, gr, att = _proj(xp, w_in_b, IN_SPLITS, PROJ_ROWS)
    zs = zs.reshape(bp, t, SHIFT_WIDTH)
    att = att.reshape(bp, t, ATT_WIDTH)
    memkv, = _proj(mem_prompt.reshape(bp * N_MEM, D_MODEL), w_mem_b, ((0, 2 * B_WIDTH),), PROJ_ROWS)
    memkv = memkv.reshape(bp, N_MEM, 2 * B_WIDTH)
    mix_r, p_wkv = _wkv(zs, gr.reshape(bp, t, R_WIDTH), jnp.zeros((bp, 1, SHIFT_WIDTH), F32),
                        jnp.zeros((bp, R_HEADS, HEAD_DIM, HEAD_DIM), F32), wkv_params,
                        bb_n=WKV_PROMPT_SEQS, c=CHUNK, n_ch=WKV_CHUNKS_PER_STEP, n_grp=WKV_GROUP_CHUNKS)
    y_prompt = _attn_prompt(att, memkv, gtab_p, x_prompt, mix_r, w_out_b, row(ln_g[l]), row(ln_b[l]),
                            tq=BAND_WINDOW)
    p_shift = zs[:, -1]
    p_bk = att[:, t - keep:, B_WIDTH:2 * B_WIDTH].reshape(bp, keep, B_HEADS, HEAD_DIM)
    p_bv = att[:, t - keep:, 2 * B_WIDTH:3 * B_WIDTH].reshape(bp, keep, B_HEADS, HEAD_DIM)
    p_mk = memkv[:, :, :B_WIDTH].reshape(bp, N_MEM, M_HEADS, HEAD_DIM)
    p_mv = memkv[:, :, B_WIDTH:].reshape(bp, N_MEM, M_HEADS, HEAD_DIM)

    xs = x_sample.reshape(bs * s, D_MODEL)
    zs_s, gr_s, att_s = _proj(xs, w_in_b, IN_SPLITS, bs * s)
    zs_s = zs_s.reshape(bs, s, SHIFT_WIDTH)
    att_s = att_s.reshape(bs, s, ATT_WIDTH)
    mix_r_s, s_wkv = _wkv(zs_s, gr_s.reshape(bs, s, R_WIDTH), state_shift[l][:, None, :], state_wkv[l],
                          wkv_params, bb_n=WKV_SAMPLE_SEQS, c=s, n_ch=1)
    mix_bm_s = _attn_sample(att_s,
                            cache_band_k[l].reshape(bs, r_rows, B_WIDTH),
                            cache_band_v[l].reshape(bs, r_rows, B_WIDTH),
                            cache_mem_k[l].reshape(bs, N_MEM, B_WIDTH),
                            cache_mem_v[l].reshape(bs, N_MEM, B_WIDTH), gtab_s, n_seq=ATTN_SAMPLE_SEQS)
    y_sample = _finish(xs, mix_r_s.reshape(bs * s, R_WIDTH), mix_bm_s.reshape(bs * s, 2 * B_WIDTH),
                       w_out_b, row(ln_g[l]), row(ln_b[l]), bs * s).reshape(bs, s, D_MODEL)
    s_shift = zs_s[:, -1]
    s_bk = att_s[:, :, B_WIDTH:2 * B_WIDTH].reshape(bs, s, B_HEADS, HEAD_DIM)
    s_bv = att_s[:, :, 2 * B_WIDTH:3 * B_WIDTH].reshape(bs, s, B_HEADS, HEAD_DIM)

    st = lambda a: a[None]
    return (y_prompt, y_sample, st(p_shift), st(p_wkv), st(p_bk), st(p_bv), st(p_mk), st(p_mv),
            st(s_shift), st(s_wkv), st(s_bk), st(s_bv))
```

```python
import functools

import numpy as np
import jax
import jax.numpy as jnp
from jax import lax
from jax.experimental import pallas as pl
from jax.experimental.pallas import tpu as pltpu

F32 = jnp.float32
BF16 = jnp.bfloat16

D_MODEL = 1024
HEAD_DIM = 64
R_WIDTH = 512
R_HEADS = 8
LOW_RANK = 64
SHIFT_WIDTH = 3 * R_WIDTH + 2 * LOW_RANK
B_WIDTH = 256
B_HEADS = 4
M_HEADS = 4
N_MEM = 256
CHUNK = 64
BAND_CHUNKS = 8
BAND_WINDOW = BAND_CHUNKS * CHUNK
BAND_LEN = BAND_WINDOW + CHUNK
REL_CLIP = 128
ATT_WIDTH = 6 * B_WIDTH
IN_WIDTH = SHIFT_WIDTH + R_WIDTH + ATT_WIDTH
LN_EPS = 1e-5
GN_EPS = 64e-5
ALPHA = 2.0 ** 0.25
ATT_SCALE = HEAD_DIM ** -0.5
LOG2E = float(np.log2(np.e))
BIAS_L = 1024
PAIR = 2 * HEAD_DIM
ATTN_GROUP_CHUNKS = 2
MEM_ROWS = 128
MEM_GROUP_BLOCKS = 2
WKV_CHUNKS_PER_STEP = 4
WKV_GROUP_CHUNKS = 2
STAGE_SPLIT = 16
PROJ_ROWS = 256
WKV_PROMPT_SEQS = 2
WKV_SAMPLE_SEQS = 16
ATTN_SAMPLE_SEQS = 4

VMEM_LIMIT = 48 * 1024 * 1024

NN = ((1,), (0,))
NT = ((1,), (1,))
TN = ((0,), (0,))


def _dot(a, b, dims=NN):
    return lax.dot_general(a, b, (dims, ((), ())), preferred_element_type=F32)


def _split2(x):
    hi = x.astype(BF16)
    lo = (x - hi.astype(F32)).astype(BF16)
    return hi, lo


def _mm3(a, b, dims=NN):
    ah, al = _split2(a)
    bh, bl = _split2(b)
    return _dot(ah, bh, dims) + _dot(ah, bl, dims) + _dot(al, bh, dims)


def _mm_exact_lhs(lhs_bf16, x):
    x1 = x.astype(BF16)
    r1 = x - x1.astype(F32)
    x2 = r1.astype(BF16)
    x3 = (r1 - x2.astype(F32)).astype(BF16)
    return _dot(lhs_bf16, x1) + _dot(lhs_bf16, x2) + _dot(lhs_bf16, x3)


def _sigmoid(x):
    return 1.0 / (1.0 + jnp.exp(-x))


def _silu(x):
    return x * _sigmoid(x)


def _proj_kernel(x_ref, w_ref, *out_refs, splits):
    x = x_ref[...].astype(BF16)
    for o_ref, (lo, hi) in zip(out_refs, splits):
        o_ref[...] = _dot(x, w_ref[:, lo:hi])


def _proj(x, w_bf16, splits, tm):
    m, k = x.shape
    n = w_bf16.shape[1]
    return pl.pallas_call(
        functools.partial(_proj_kernel, splits=splits),
        name=f"proj_{m}x{n}",
        grid=(m // tm,),
        in_specs=[pl.BlockSpec((tm, k), lambda i: (i, 0)),
                  pl.BlockSpec((k, n), lambda i: (0, 0))],
        out_specs=[pl.BlockSpec((tm, hi - lo), lambda i: (i, 0)) for lo, hi in splits],
        out_shape=[jax.ShapeDtypeStruct((m, hi - lo), F32) for lo, hi in splits],
        compiler_params=pltpu.CompilerParams(
            dimension_semantics=("arbitrary",), vmem_limit_bytes=VMEM_LIMIT),
    )(x, w_bf16)


def _pair_blocks(x2, left):
    zero = jnp.zeros((), x2.dtype)
    return jnp.concatenate([jnp.where(left, x2, zero), jnp.where(left, zero, x2)], axis=0)


def _pair_sum(x2, left):
    s0 = jnp.sum(jnp.where(left, x2, 0.0), axis=-1, keepdims=True)
    s1 = jnp.sum(jnp.where(left, 0.0, x2), axis=-1, keepdims=True)
    return jnp.where(left, s0, s1)


UNIT_COMMON = ("ar", "bkp", "v16", "pc", "gate", "bonus")
UNIT_FREE_OUT = ("a4", "tinv", "akv")
UNIT_FREE_IN = ("bk_bd",)


def _wkv_kernel(z0_ref, g0_ref, zn_ref, gn_ref, prev_ref, s0_ref, mu_ref, w0_ref, w2_ref, a0_ref, a2_ref,
                kk_ref, ka_ref, rk_ref, gng_ref, gnb_ref, out_ref, sfin_ref, carry_ref, state_ref, *unit_refs,
                bb_n, c, n_ch, n_grp, pipelined):
    t = pl.program_id(1)
    tt = n_ch * c
    n_pairs = R_HEADS // 2
    per_chunk = bb_n * n_pairs

    row2 = lax.broadcasted_iota(jnp.int32, (c, 2 * c), 0)
    col2 = lax.broadcasted_iota(jnp.int32, (c, 2 * c), 1) & (c - 1)
    eye2 = (row2 == col2).astype(BF16)
    strict2 = col2 < row2
    incl4 = ((lax.broadcasted_iota(jnp.int32, (c, 4 * c), 1) & (c - 1))
             <= lax.broadcasted_iota(jnp.int32, (c, 4 * c), 0))
    level_masks = []
    shift = 0
    while (1 << shift) < c:
        rb = row2 >> shift
        level_masks.append(((rb & 1) == 1) & ((col2 >> shift) == rb - 1))
        shift += 1
    left_s = lax.broadcasted_iota(jnp.int32, (1, 2 * c), 1) < c
    left = lax.broadcasted_iota(jnp.int32, (1, PAIR), 1) < HEAD_DIM
    zero = jnp.zeros((), BF16)
    row_t = lax.broadcasted_iota(jnp.int32, (tt, tt), 0)
    col_t = lax.broadcasted_iota(jnp.int32, (tt, tt), 1)
    shift_c = c.bit_length() - 1
    ltri = ((col_t <= row_t) & ((col_t >> shift_c) == (row_t >> shift_c))).astype(BF16)
    first_row = lax.broadcasted_iota(jnp.int32, (tt, 1), 0) == 0
    mu = mu_ref[...]

    def field(un, f):
        if f not in un:
            un[f] = _pair_blocks(field(un, "v16"), left) if f == "v_bd" else un["stored"][f]()
        return un[f]

    def make_tile(zs_ref, gr_ref, prev_rows):
        tile = [dict() for _ in range(bb_n)]
        chunks = [[] for _ in range(n_ch)]

        def whole_tile(bb):
            zs = zs_ref[bb]
            zprev = jnp.where(first_row, prev_rows[bb], pltpu.roll(zs, 1, 0))
            xs = zs + (zprev - zs) * mu
            wd = xs[:, 3 * R_WIDTH:3 * R_WIDTH + LOW_RANK]
            ad = xs[:, 3 * R_WIDTH + LOW_RANK:]
            u = -(w0_ref[...] + _mm3(jnp.tanh(wd), w2_ref[...]))
            softplus = jnp.maximum(u, 0.0) + jnp.log(1.0 + jnp.exp(-jnp.abs(u)))
            ld = -jnp.exp(-softplus - 0.5)
            tile[bb].update(xs=xs, ld=ld, cum=_mm_exact_lhs(ltri, ld),
                            a_pre=a0_ref[...] + _mm3(ad, a2_ref[...]))

        def prep_tasks(ch):
            rows = slice(ch * c, (ch + 1) * c)
            tasks = []
            for bb in range(bb_n):
                wide = {}

                def full_width(bb=bb, wide=wide):
                    xs = tile[bb]["xs"][rows]
                    r = xs[:, 0:R_WIDTH]
                    k = xs[:, R_WIDTH:2 * R_WIDTH]
                    ld = tile[bb]["ld"][rows]
                    cum = tile[bb]["cum"][rows]
                    a = _sigmoid(tile[bb]["a_pre"][rows])
                    ecum = jnp.exp(cum)
                    k2 = k * (1.0 + (a - 1.0) * ka_ref[...])
                    wide.update(v=xs[:, 2 * R_WIDTH:3 * R_WIDTH], a=a, ecum=ecum, einv=jnp.exp(-cum),
                                eprev=jnp.exp(cum - ld), kkr=k * kk_ref[...], rt=r * ecum, k2=k2,
                                rk2=r * k2 * rk_ref[...], gate=_silu(gr_ref[bb, rows, :]))

                def pair(pr, bb=bb, wide=wide):
                    sl = slice(pr * PAIR, (pr + 1) * PAIR)
                    kkr_p = wide["kkr"][:, sl]
                    nrm = jnp.sqrt(_pair_sum(kkr_p * kkr_p, left))
                    kkn = kkr_p / jnp.maximum(nrm, 1e-12)
                    at = -kkn * wide["eprev"][:, sl]
                    bh = kkn * wide["a"][:, sl] * wide["einv"][:, sl]
                    kh = wide["k2"][:, sl] * wide["einv"][:, sl]
                    pc = wide["ecum"][c - 1:c, sl]
                    v_p = wide["v"][:, sl]
                    v16 = v_p.astype(BF16)
                    chunks[ch].append(dict(
                        bb=bb, pr=pr, sl=sl, rows=rows, pc=pc, gate=wide["gate"][:, sl],
                        bonus=_pair_sum(wide["rk2"][:, sl], left) * v_p, v_bd=_pair_blocks(v16, left), v16=v16,
                        ar=jnp.concatenate([at, wide["rt"][:, sl]], axis=0).astype(BF16),
                        bk_bd=jnp.concatenate([_pair_blocks(bh.astype(BF16), left),
                                               _pair_blocks(kh.astype(BF16), left)], axis=0),
                        bkp=jnp.concatenate([bh * pc, kh * pc], axis=0).astype(BF16)))

                tasks.append(full_width)
                tasks += [functools.partial(pair, pr) for pr in range(n_pairs)]
            return tasks

        return chunks, [functools.partial(whole_tile, bb) for bb in range(bb_n)], prep_tasks

    def st_a4(units):
        for un in units:
            un["a4"] = _dot(field(un, "ar"), field(un, "bk_bd"), NT).astype(BF16)
            un["tinv"] = jnp.where(level_masks[0], un["a4"][:c, :2 * c], eye2)

    def st_lt(mask):
        def run(units):
            for un in units:
                a_ab = un["a4"][:c, :2 * c]
                un["lt"] = _dot(jnp.where(mask, a_ab, zero), _pair_blocks(un["tinv"], left_s)).astype(BF16)
        return run

    def st_tinv(mask):
        def run(units):
            for un in units:
                new = _dot(un["tinv"], _pair_blocks(un["lt"], left_s)).astype(BF16)
                un["tinv"] = jnp.where(mask, new, un["tinv"])
        return run

    def st_akv(units):
        for un in units:
            un["akv"] = _dot(jnp.where(strict2, un["a4"][:c, 2 * c:], zero), field(un, "v_bd"))

    def st_ars(units):
        for un in units:
            un["s0"] = state[un["bb"], un["pr"]]
            un["ars"] = _dot(field(un, "ar"), _pair_blocks(un["s0"].astype(BF16), left), NT)

    def st_pm(units):
        for un in units:
            rhs = (un["ars"][:c] + field(un, "akv")).astype(BF16)
            un["pm"] = _dot(field(un, "tinv"), _pair_blocks(rhs, left)).astype(BF16)

    def st_state(units):
        for un in units:
            pv = jnp.concatenate([un["pm"], field(un, "v16")], axis=0)
            cross = _dot(pv, field(un, "bkp"), TN)
            state[un["bb"], un["pr"]] = (un["s0"] * field(un, "pc")
                                         + jnp.where(left, cross[:HEAD_DIM], cross[HEAD_DIM:]))

    def st_y(units):
        for un in units:
            pv_bd = jnp.concatenate([_pair_blocks(un["pm"], left), field(un, "v_bd")], axis=0)
            y = un["ars"][c:] + _dot(jnp.where(incl4, field(un, "a4")[c:], zero), pv_bd)
            mean = _pair_sum(y, left) * (1.0 / HEAD_DIM)
            yc = y - mean
            var = _pair_sum(yc * yc, left) * (1.0 / HEAD_DIM)
            yn = yc * lax.rsqrt(var + GN_EPS) * gng_ref[:, un["sl"]] + gnb_ref[:, un["sl"]]
            out_ref[un["bb"], un["rows"], un["sl"]] = (yn + field(un, "bonus")) * field(un, "gate")

    free_stages = [st_a4]
    for mask in level_masks[1:]:
        free_stages += [st_lt(mask), st_tinv(mask)]
    free_stages.append(st_akv)
    state_stages = [st_ars, st_pm, st_state, st_y]

    def stage_tasks(stages, chunks, chs):
        def run(stage, lo):
            units = [un for ch in chs for un in chunks[ch]]
            stage(units[lo:lo + STAGE_SPLIT])

        return [functools.partial(run, stage, lo) for stage in stages
                for lo in range(0, len(chs) * per_chunk, STAGE_SPLIT)]

    def run_interleaved(*task_lists):
        keyed = [((i + 0.5) / len(tasks), n, i, task)
                 for n, tasks in enumerate(task_lists) for i, task in enumerate(tasks)]
        for _, _, _, task in sorted(keyed, key=lambda e: e[:3]):
            task()

    groups = [list(range(g0, min(g0 + n_grp, n_ch))) for g0 in range(0, n_ch, n_grp)]

    if pipelined:
        refs = dict(zip(UNIT_COMMON + UNIT_FREE_OUT + UNIT_FREE_IN, unit_refs))
        n_g = n_grp * per_chunk

        def store_units(chunks, chs, fields, base):
            units = [un for ch in chs for un in chunks[ch]]
            for idx, un in enumerate(units):
                for f in fields:
                    refs[f][(base if f in UNIT_COMMON else 0) + idx] = un[f]

        def load_units(chs, fields, base):
            chunks = {}
            idx = 0
            for ch in chs:
                chunks[ch] = []
                for bb in range(bb_n):
                    for pr in range(n_pairs):
                        un = dict(bb=bb, pr=pr, sl=slice(pr * PAIR, (pr + 1) * PAIR),
                                  rows=slice(ch * c, (ch + 1) * c))
                        un["stored"] = {f: functools.partial(lambda f, i: refs[f][i], f,
                                                             (base if f in UNIT_COMMON else 0) + idx)
                                        for f in fields}
                        chunks[ch].append(un)
                        idx += 1
            return chunks

        def hand_over(chunks):
            store_units(chunks, groups[0], UNIT_COMMON + UNIT_FREE_OUT, 0)
            store_units(chunks, groups[1], UNIT_COMMON + UNIT_FREE_IN, n_g)

        @pl.when(t == 0)
        def _():
            for bb in range(bb_n):
                for h in range(R_HEADS):
                    state_ref[bb, h // 2, :, (h % 2) * HEAD_DIM:(h % 2 + 1) * HEAD_DIM] = s0_ref[bb, h]
            first, whole, prep = make_tile(z0_ref, g0_ref, [prev_ref[bb] for bb in range(bb_n)])
            run_interleaved(whole)
            run_interleaved([task for ch in range(n_ch) for task in prep(ch)])
            run_interleaved(stage_tasks(free_stages, first, groups[0]))
            hand_over(first)
            carry_ref[...] = z0_ref[:, tt - 1:tt, :]

        cur = load_units(groups[0], UNIT_COMMON + UNIT_FREE_OUT, 0)
        cur.update(load_units(groups[1], UNIT_COMMON + UNIT_FREE_IN, n_g))
        nxt, whole, prep = make_tile(zn_ref, gn_ref, [carry_ref[bb] for bb in range(bb_n)])
        state = {(bb, pr): state_ref[bb, pr] for bb in range(bb_n) for pr in range(n_pairs)}
        run_interleaved([task for ch in groups[0] for task in stage_tasks(state_stages, cur, [ch])],
                        stage_tasks(free_stages, cur, groups[1]),
                        whole + [task for ch in groups[0] for task in prep(ch)])
        run_interleaved([task for ch in groups[1] for task in stage_tasks(state_stages, cur, [ch])],
                        stage_tasks(free_stages, nxt, groups[0]),
                        [task for ch in groups[1] for task in prep(ch)])
        hand_over(nxt)
        carry_ref[...] = zn_ref[:, tt - 1:tt, :]
    else:
        @pl.when(t == 0)
        def _():
            carry_ref[...] = prev_ref[...]
            for bb in range(bb_n):
                for h in range(R_HEADS):
                    state_ref[bb, h // 2, :, (h % 2) * HEAD_DIM:(h % 2 + 1) * HEAD_DIM] = s0_ref[bb, h]

        cur, whole, prep = make_tile(z0_ref, g0_ref, [carry_ref[bb] for bb in range(bb_n)])
        state = {(bb, pr): state_ref[bb, pr] for bb in range(bb_n) for pr in range(n_pairs)}
        run_interleaved(whole)
        carry_ref[...] = z0_ref[:, tt - 1:tt, :]
        run_interleaved([task for ch in range(n_ch) for task in prep(ch)])
        run_interleaved(stage_tasks(free_stages, cur, groups[0]))
        for g, chs in enumerate(groups):
            run_interleaved([task for ch in chs for task in stage_tasks(state_stages, cur, [ch])],
                            stage_tasks(free_stages, cur, groups[g + 1]) if g + 1 < len(groups) else [])
    for (bb, pr), val in state.items():
        state_ref[bb, pr] = val

    @pl.when(t == pl.num_programs(1) - 1)
    def _():
        for bb in range(bb_n):
            for h in range(R_HEADS):
                sfin_ref[bb, h] = state_ref[bb, h // 2, :, (h % 2) * HEAD_DIM:(h % 2 + 1) * HEAD_DIM]


def _wkv(zs, gr, prev, s0, params, *, bb_n, c, n_ch, n_grp=1):
    b, t, _ = zs.shape
    tt = n_ch * c
    n_t = t // tt
    pipelined = n_ch == 2 * n_grp and n_t > 1
    full = lambda arr: pl.BlockSpec(arr.shape, lambda i, j: (0,) * arr.ndim)
    first = lambda width: pl.BlockSpec((bb_n, tt, width), lambda i, j: (i, 0 if pipelined else j, 0))
    ahead = lambda width: pl.BlockSpec((bb_n, tt, width), lambda i, j: (i, jnp.minimum(j + 1, n_t - 1), 0))
    unit_scratch = []
    if pipelined:
        n_g = n_grp * bb_n * (R_HEADS // 2)
        shapes = dict(ar=((2 * c, PAIR), BF16), bkp=((2 * c, PAIR), BF16), v16=((c, PAIR), BF16),
                      pc=((1, PAIR), F32), gate=((c, PAIR), F32), bonus=((c, PAIR), F32),
                      a4=((2 * c, 4 * c), BF16), tinv=((c, 2 * c), BF16), akv=((c, PAIR), F32),
                      bk_bd=((4 * c, PAIR), BF16))
        for f in UNIT_COMMON + UNIT_FREE_OUT + UNIT_FREE_IN:
            shape, dtype = shapes[f]
            unit_scratch.append(pltpu.VMEM(((2 * n_g if f in UNIT_COMMON else n_g),) + shape, dtype))
    return pl.pallas_call(
        functools.partial(_wkv_kernel, bb_n=bb_n, c=c, n_ch=n_ch, n_grp=n_grp, pipelined=pipelined),
        name=f"wkv_c{c}",
        grid=(b // bb_n, n_t),
        in_specs=[first(SHIFT_WIDTH), first(R_WIDTH), ahead(SHIFT_WIDTH), ahead(R_WIDTH),
                  pl.BlockSpec((bb_n, 1, SHIFT_WIDTH), lambda i, j: (i, 0, 0)),
                  pl.BlockSpec((bb_n, R_HEADS, HEAD_DIM, HEAD_DIM), lambda i, j: (i, 0, 0, 0))]
                 + [full(p) for p in params],
        out_specs=[pl.BlockSpec((bb_n, tt, R_WIDTH), lambda i, j: (i, j, 0)),
                   pl.BlockSpec((bb_n, R_HEADS, HEAD_DIM, HEAD_DIM), lambda i, j: (i, 0, 0, 0))],
        out_shape=[jax.ShapeDtypeStruct((b, t, R_WIDTH), F32),
                   jax.ShapeDtypeStruct((b, R_HEADS, HEAD_DIM, HEAD_DIM), F32)],
        scratch_shapes=[pltpu.VMEM((bb_n, 1, SHIFT_WIDTH), F32),
                        pltpu.VMEM((bb_n, R_HEADS // 2, HEAD_DIM, PAIR), F32)] + unit_scratch,
        compiler_params=pltpu.CompilerParams(
            dimension_semantics=("arbitrary", "arbitrary"), vmem_limit_bytes=VMEM_LIMIT),
    )(zs, gr, zs, gr, prev, s0, *params)


def _fill_rel_bias(bias_ref, gtab_ref, offset):
    heads, nq, nk = bias_ref.shape
    for h in range(heads):
        g = jnp.broadcast_to(gtab_ref[h:h + 1, :], (nq, BIAS_L))
        bias_ref[h] = pltpu.roll(g, BIAS_L - offset, 1, stride=1, stride_axis=0)[:, :nk]


def _stack_heads(q2, left):
    zero = jnp.zeros((), q2.dtype)
    return jnp.concatenate([jnp.where(left, q2, zero), jnp.where(left, zero, q2)], axis=0)


def _attend_stages(jobs, left, write):
    def scores():
        for jb in jobs:
            s = _dot(jb["lhs"], jb["k"], NT)
            jb["s"] = s if jb.get("bias") is None else s + jb["bias"]

    def weights():
        for jb in jobs:
            m = jnp.max(jb["s"], axis=-1, keepdims=True)
            p = jnp.exp2(jb["s"] - m)
            jb["l"] = jnp.sum(p, axis=-1, keepdims=True)
            jb["p"] = p.astype(BF16)

    def values():
        for jb in jobs:
            o2 = _dot(jb["p"], jb["v"]) / jb["l"]
            n = o2.shape[0] // 2
            write(jb, jnp.where(left, o2[:n], o2[n:]))

    return [scores, weights, values]


def _run_staggered(groups):
    depth = max(len(g) for g in groups)
    for step in range(len(groups) + depth - 1):
        for g in range(len(groups) - 1, -1, -1):
            if 0 <= step - g < len(groups[g]):
                groups[g][step - g]()


def _layer_norm(h, g, b):
    mean = jnp.mean(h, axis=-1, keepdims=True)
    hc = h - mean
    var = jnp.mean(hc * hc, axis=-1, keepdims=True)
    return hc * lax.rsqrt(var + LN_EPS) * g + b


def _attn_prompt_kernel(q_ref, kc_ref, vc_ref, gb_ref, mq_ref, gm_ref, mk_ref, mv_ref, gtab_ref,
                        x_ref, mr_ref, w_ref, lng_ref, lnb_ref, y_ref,
                        kcat_ref, vcat_ref, bias_ref, biasv_ref, mix_ref, *, tq):
    j = pl.program_id(1)
    n_chunks = tq // CHUNK

    @pl.when((pl.program_id(0) == 0) & (j == 0))
    def _():
        _fill_rel_bias(bias_ref, gtab_ref, CHUNK - 1)
        kcol = lax.broadcasted_iota(jnp.int32, (1, BAND_LEN), 1)
        for h in range(B_HEADS):
            rows = slice((h % 2) * CHUNK, (h % 2 + 1) * CHUNK)
            scaled = bias_ref[h] * LOG2E
            biasv_ref[0, h // 2, rows, :] = scaled
            for i in range(n_chunks):
                biasv_ref[1 + i, h // 2, rows, :] = jnp.where(kcol >= BAND_WINDOW - i * CHUNK, scaled, -jnp.inf)

    @pl.when(j == 0)
    def _():
        kcat_ref[0:BAND_WINDOW] = jnp.zeros((BAND_WINDOW, B_WIDTH), BF16)
        vcat_ref[0:BAND_WINDOW] = jnp.zeros((BAND_WINDOW, B_WIDTH), BF16)

    @pl.when(j > 0)
    def _():
        kcat_ref[0:BAND_WINDOW] = kcat_ref[BAND_WINDOW:]
        vcat_ref[0:BAND_WINDOW] = vcat_ref[BAND_WINDOW:]

    kcat_ref[BAND_WINDOW:] = kc_ref[0].astype(BF16)
    vcat_ref[BAND_WINDOW:] = vc_ref[0].astype(BF16)
    q = (q_ref[0] * (ATT_SCALE * LOG2E)).astype(BF16)
    gate_b = _silu(gb_ref[0])
    left = lax.broadcasted_iota(jnp.int32, (1, PAIR), 1) < HEAD_DIM
    mq = (mq_ref[0] * (ATT_SCALE * LOG2E)).astype(BF16)
    gate_m = _silu(gm_ref[0])
    mk = mk_ref[0].astype(BF16)
    mv = mv_ref[0].astype(BF16)

    def write_band(jb, o):
        mix_ref[jb["rows"], jb["lanes"]] = (o * gate_b[jb["rows"], jb["lanes"]]).astype(BF16)

    def write_mem(jb, o):
        mix_ref[jb["rows"], B_WIDTH + jb["lanes"].start:B_WIDTH + jb["lanes"].stop] = (
            o * gate_m[jb["rows"], jb["lanes"]]).astype(BF16)

    groups = []
    for i0 in range(0, n_chunks, ATTN_GROUP_CHUNKS):
        jobs = []
        for i in range(i0, i0 + ATTN_GROUP_CHUNKS):
            rows = slice(i * CHUNK, (i + 1) * CHUNK)
            keys = slice(i * CHUNK, i * CHUNK + BAND_LEN)
            variant = jnp.where(j == 0, 1 + i, 0)
            for pr in range(B_HEADS // 2):
                lanes = slice(pr * PAIR, (pr + 1) * PAIR)
                jobs.append(dict(rows=rows, lanes=lanes, lhs=_stack_heads(q[rows, lanes], left),
                                 k=kcat_ref[keys, lanes], v=vcat_ref[keys, lanes],
                                 bias=biasv_ref[variant, pr]))
        groups.append(_attend_stages(jobs, left, write_band))
    for r0 in range(0, tq, MEM_ROWS * MEM_GROUP_BLOCKS):
        jobs = []
        for r in range(r0, r0 + MEM_ROWS * MEM_GROUP_BLOCKS, MEM_ROWS):
            rows = slice(r, r + MEM_ROWS)
            for pr in range(M_HEADS // 2):
                lanes = slice(pr * PAIR, (pr + 1) * PAIR)
                jobs.append(dict(rows=rows, lanes=lanes, lhs=_stack_heads(mq[rows, lanes], left),
                                 k=mk[:, lanes], v=mv[:, lanes]))
        groups.append(_attend_stages(jobs, left, write_mem))
    _run_staggered(groups)

    o = _dot(mr_ref[0].astype(BF16), w_ref[0:R_WIDTH, :]) + _dot(mix_ref[...], w_ref[R_WIDTH:, :])
    y_ref[0] = _layer_norm(ALPHA * x_ref[0] + o, lng_ref[...], lnb_ref[...])


def _attn_prompt(att, memkv, gtab, x, mix_r, w_out_bf16, ln_g, ln_b, *, tq):
    b, t, _ = att.shape
    col = lambda cidx: pl.BlockSpec((1, tq, B_WIDTH), lambda i, j: (i, j, cidx))
    tile = lambda width: pl.BlockSpec((1, tq, width), lambda i, j: (i, j, 0))
    full = lambda arr: pl.BlockSpec(arr.shape, lambda i, j: (0, 0))
    return pl.pallas_call(
        functools.partial(_attn_prompt_kernel, tq=tq),
        name="attn_prompt",
        grid=(b, t // tq),
        in_specs=[col(0), col(1), col(2), col(3), col(4), col(5),
                  pl.BlockSpec((1, N_MEM, B_WIDTH), lambda i, j: (i, 0, 0)),
                  pl.BlockSpec((1, N_MEM, B_WIDTH), lambda i, j: (i, 0, 1)),
                  full(gtab), tile(D_MODEL), tile(R_WIDTH), full(w_out_bf16), full(ln_g), full(ln_b)],
        out_specs=tile(D_MODEL),
        out_shape=jax.ShapeDtypeStruct((b, t, D_MODEL), F32),
        scratch_shapes=[pltpu.VMEM((2 * BAND_WINDOW, B_WIDTH), BF16),
                        pltpu.VMEM((2 * BAND_WINDOW, B_WIDTH), BF16),
                        pltpu.VMEM((B_HEADS, CHUNK, BAND_LEN), F32),
                        pltpu.VMEM((1 + tq // CHUNK, B_HEADS // 2, 2 * CHUNK, BAND_LEN), F32),
                        pltpu.VMEM((tq, 2 * B_WIDTH), BF16)],
        compiler_params=pltpu.CompilerParams(
            dimension_semantics=("arbitrary", "arbitrary"), vmem_limit_bytes=VMEM_LIMIT),
    )(att, att, att, att, att, att, memkv, memkv, gtab, x, mix_r, w_out_bf16, ln_g, ln_b)


def _attn_sample_kernel(att_ref, ck_ref, cv_ref, mk_ref, mv_ref, gtab_ref, out_ref, bias_ref, bias2_ref):
    n_seq, n_new, _ = att_ref.shape

    @pl.when(pl.program_id(0) == 0)
    def _():
        _fill_rel_bias(bias_ref, gtab_ref, n_new - 1)
        for h in range(B_HEADS):
            bias2_ref[h // 2, (h % 2) * n_new:(h % 2 + 1) * n_new, :] = bias_ref[h] * LOG2E

    left = lax.broadcasted_iota(jnp.int32, (1, PAIR), 1) < HEAD_DIM
    jobs = []
    for b in range(n_seq):
        att = att_ref[b]
        q = (att[:, 0:B_WIDTH] * (ATT_SCALE * LOG2E)).astype(BF16)
        k_all = jnp.concatenate([ck_ref[b].astype(BF16), att[:, B_WIDTH:2 * B_WIDTH].astype(BF16)], axis=0)
        v_all = jnp.concatenate([cv_ref[b].astype(BF16), att[:, 2 * B_WIDTH:3 * B_WIDTH].astype(BF16)], axis=0)
        gate_b = _silu(att[:, 3 * B_WIDTH:4 * B_WIDTH])
        mq = (att[:, 4 * B_WIDTH:5 * B_WIDTH] * (ATT_SCALE * LOG2E)).astype(BF16)
        gate_m = _silu(att[:, 5 * B_WIDTH:6 * B_WIDTH])
        mk = mk_ref[b].astype(BF16)
        mv = mv_ref[b].astype(BF16)
        for pr in range(B_HEADS // 2):
            lanes = slice(pr * PAIR, (pr + 1) * PAIR)
            jobs.append(dict(b=b, out=lanes, gate=gate_b[:, lanes], lhs=_stack_heads(q[:, lanes], left),
                             k=k_all[:, lanes], v=v_all[:, lanes], bias=bias2_ref[pr]))
        for pr in range(M_HEADS // 2):
            lanes = slice(pr * PAIR, (pr + 1) * PAIR)
            jobs.append(dict(b=b, out=slice(B_WIDTH + lanes.start, B_WIDTH + lanes.stop), gate=gate_m[:, lanes],
                             lhs=_stack_heads(mq[:, lanes], left), k=mk[:, lanes], v=mv[:, lanes]))
    def write(jb, o):
        out_ref[jb["b"], :, jb["out"]] = o * jb["gate"]

    _run_staggered([_attend_stages(jobs, left, write)])


def _attn_sample(att, cache_k, cache_v, mem_k, mem_v, gtab, *, n_seq):
    b, s, _ = att.shape
    per_b = lambda arr: pl.BlockSpec((n_seq,) + arr.shape[1:], lambda i: (i, 0, 0))
    n_keys = cache_k.shape[1] + s
    return pl.pallas_call(
        _attn_sample_kernel,
        name="attn_sample",
        grid=(b // n_seq,),
        in_specs=[per_b(att), per_b(cache_k), per_b(cache_v), per_b(mem_k), per_b(mem_v),
                  pl.BlockSpec(gtab.shape, lambda i: (0, 0))],
        out_specs=pl.BlockSpec((n_seq, s, 2 * B_WIDTH), lambda i: (i, 0, 0)),
        out_shape=jax.ShapeDtypeStruct((b, s, 2 * B_WIDTH), F32),
        scratch_shapes=[pltpu.VMEM((B_HEADS, s, n_keys), F32),
                        pltpu.VMEM((B_HEADS // 2, 2 * s, n_keys), F32)],
        compiler_params=pltpu.CompilerParams(dimension_semantics=("arbitrary",)),
    )(att, cache_k, cache_v, mem_k, mem_v, gtab)


def _finish_kernel(x_ref, mr_ref, mbm_ref, w_ref, g_ref, b_ref, y_ref):
    o = _dot(mr_ref[...].astype(BF16), w_ref[0:R_WIDTH, :])
    o = o + _dot(mbm_ref[...].astype(BF16), w_ref[R_WIDTH:, :])
    y_ref[...] = _layer_norm(ALPHA * x_ref[...] + o, g_ref[...], b_ref[...])


def _finish(x, mix_r, mix_bm, w_out_bf16, ln_g, ln_b, tm):
    m = x.shape[0]
    rows = lambda width: pl.BlockSpec((tm, width), lambda i: (i, 0))
    full = lambda arr: pl.BlockSpec(arr.shape, lambda i: (0, 0))
    return pl.pallas_call(
        _finish_kernel,
        name=f"finish_{m}",
        grid=(m // tm,),
        in_specs=[rows(D_MODEL), rows(R_WIDTH), rows(2 * B_WIDTH), full(w_out_bf16), full(ln_g), full(ln_b)],
        out_specs=rows(D_MODEL),
        out_shape=jax.ShapeDtypeStruct((m, D_MODEL), F32),
        compiler_params=pltpu.CompilerParams(
            dimension_semantics=("arbitrary",), vmem_limit_bytes=VMEM_LIMIT),
    )(x, mix_r, mix_bm, w_out_bf16, ln_g, ln_b)


IN_SPLITS = ((0, SHIFT_WIDTH), (SHIFT_WIDTH, SHIFT_WIDTH + R_WIDTH), (SHIFT_WIDTH + R_WIDTH, IN_WIDTH))


def _rel_bias_row(table, rel0):
    n_hi = rel0 - REL_CLIP
    n_lo = BIAS_L - n_hi - (2 * REL_CLIP + 1)
    heads = table.shape[0]
    return jnp.concatenate([jnp.broadcast_to(table[:, 2 * REL_CLIP:], (heads, n_hi)), table[:, ::-1],
                            jnp.broadcast_to(table[:, 0:1], (heads, n_lo))], axis=1)


def kernel(x_prompt, x_sample, mem_prompt, state_shift, state_wkv, cache_band_k, cache_band_v,
           cache_mem_k, cache_mem_v, w_in, mu_shift, w0, w2, a0, a2, k_k, k_a, r_k, gn_g, gn_b,
           rel_bias, w_mem_kv, w_out, ln_g, ln_b):
    bp, t, _ = x_prompt.shape
    bs, s, _ = x_sample.shape
    depth = w_in.shape[0]
    assert depth == 1 and t % BAND_WINDOW == 0 and t % (WKV_CHUNKS_PER_STEP * CHUNK) == 0
    assert s <= CHUNK and s & (s - 1) == 0 and bp % WKV_PROMPT_SEQS == 0
    assert bs % WKV_SAMPLE_SEQS == 0 and bs % ATTN_SAMPLE_SEQS == 0 and (bp * t) % PROJ_ROWS == 0
    keep = min(BAND_WINDOW, t)
    l = 0

    w_in_b = w_in[l].astype(BF16)
    w_out_b = w_out[l].astype(BF16)
    w_mem_b = w_mem_kv[l].astype(BF16)
    row = lambda p: p.reshape(1, -1)
    wkv_params = (row(mu_shift[l]), row(w0[l]), w2[l], row(a0[l]), a2[l], row(k_k[l]), row(k_a[l]),
                  row(r_k[l]), row(gn_g[l]), row(gn_b[l]))
    table = rel_bias[l]
    r_rows = cache_band_k.shape[2]
    gtab_p = _rel_bias_row(table, BAND_WINDOW + CHUNK - 1)
    gtab_s = _rel_bias_row(table, r_rows + s - 1)

    xp = x_prompt.reshape(bp * t, D_MODEL)
    zs, gr, att = _proj(xp, w_in_b, IN_SPLITS, PROJ_ROWS)
    zs = zs.reshape(bp, t, SHIFT_WIDTH)
    att = att.reshape(bp, t, ATT_WIDTH)
    memkv, = _proj(mem_prompt.reshape(bp * N_MEM, D_MODEL), w_mem_b, ((0, 2 * B_WIDTH),), PROJ_ROWS)
    memkv = memkv.reshape(bp, N_MEM, 2 * B_WIDTH)
    mix_r, p_wkv = _wkv(zs, gr.reshape(bp, t, R_WIDTH), jnp.zeros((bp, 1, SHIFT_WIDTH), F32),
                        jnp.zeros((bp, R_HEADS, HEAD_DIM, HEAD_DIM), F32), wkv_params,
                        bb_n=WKV_PROMPT_SEQS, c=CHUNK, n_ch=WKV_CHUNKS_PER_STEP, n_grp=WKV_GROUP_CHUNKS)
    y_prompt = _attn_prompt(att, memkv, gtab_p, x_prompt, mix_r, w_out_b, row(ln_g[l]), row(ln_b[l]),
                            tq=BAND_WINDOW)
    p_shift = zs[:, -1]
    p_bk = att[:, t - keep:, B_WIDTH:2 * B_WIDTH].reshape(bp, keep, B_HEADS, HEAD_DIM)
    p_bv = att[:, t - keep:, 2 * B_WIDTH:3 * B_WIDTH].reshape(bp, keep, B_HEADS, HEAD_DIM)
    p_mk = memkv[:, :, :B_WIDTH].reshape(bp, N_MEM, M_HEADS, HEAD_DIM)
    p_mv = memkv[:, :, B_WIDTH:].reshape(bp, N_MEM, M_HEADS, HEAD_DIM)

    xs = x_sample.reshape(bs * s, D_MODEL)
    zs_s, gr_s, att_s = _proj(xs, w_in_b, IN_SPLITS, bs * s)
    zs_s = zs_s.reshape(bs, s, SHIFT_WIDTH)
    att_s = att_s.reshape(bs, s, ATT_WIDTH)
    mix_r_s, s_wkv = _wkv(zs_s, gr_s.reshape(bs, s, R_WIDTH), state_shift[l][:, None, :], state_wkv[l],
                          wkv_params, bb_n=WKV_SAMPLE_SEQS, c=s, n_ch=1)
    mix_bm_s = _attn_sample(att_s,
                            cache_band_k[l].reshape(bs, r_rows, B_WIDTH).astype(BF16),
                            cache_band_v[l].reshape(bs, r_rows, B_WIDTH).astype(BF16),
                            cache_mem_k[l].reshape(bs, N_MEM, B_WIDTH).astype(BF16),
                            cache_mem_v[l].reshape(bs, N_MEM, B_WIDTH).astype(BF16),
                            gtab_s, n_seq=ATTN_SAMPLE_SEQS)
    y_sample = _finish(xs, mix_r_s.reshape(bs * s, R_WIDTH), mix_bm_s.reshape(bs * s, 2 * B_WIDTH),
                       w_out_b, row(ln_g[l]), row(ln_b[l]), bs * s).reshape(bs, s, D_MODEL)
    s_shift = zs_s[:, -1]
    s_bk = att_s[:, :, B_WIDTH:2 * B_WIDTH].reshape(bs, s, B_HEADS, HEAD_DIM)
    s_bv = att_s[:, :, 2 * B_WIDTH:3 * B_WIDTH].reshape(bs, s, B_HEADS, HEAD_DIM)

    st = lambda a: a[None]
    return (y_prompt, y_sample, st(p_shift), st(p_wkv), st(p_bk), st(p_bv), st(p_mk), st(p_mv),
            st(s_shift), st(s_wkv), st(s_bk), st(s_bv))
```

```python
import functools

import numpy as np
import jax
import jax.numpy as jnp
from jax import lax
from jax.experimental import pallas as pl
from jax.experimental.pallas import tpu as pltpu

F32 = jnp.float32
BF16 = jnp.bfloat16

D_MODEL = 1024
HEAD_DIM = 64
R_WIDTH = 512
R_HEADS = 8
LOW_RANK = 64
SHIFT_WIDTH = 3 * R_WIDTH + 2 * LOW_RANK
B_WIDTH = 256
B_HEADS = 4
M_HEADS = 4
N_MEM = 256
CHUNK = 64
BAND_CHUNKS = 8
BAND_WINDOW = BAND_CHUNKS * CHUNK
BAND_LEN = BAND_WINDOW + CHUNK
REL_CLIP = 128
ATT_WIDTH = 6 * B_WIDTH
IN_WIDTH = SHIFT_WIDTH + R_WIDTH + ATT_WIDTH
LN_EPS = 1e-5
GN_EPS = 64e-5
ALPHA = 2.0 ** 0.25
ATT_SCALE = HEAD_DIM ** -0.5
LOG2E = float(np.log2(np.e))
BIAS_L = 1024
PAIR = 2 * HEAD_DIM
ATTN_GROUP_CHUNKS = 2
MEM_ROWS = 128
MEM_GROUP_BLOCKS = 2
WKV_CHUNKS_PER_STEP = 4
WKV_GROUP_CHUNKS = 2
STAGE_SPLIT = 16
PROJ_ROWS = 256
WKV_PROMPT_SEQS = 2
WKV_SAMPLE_SEQS = 16
ATTN_SAMPLE_SEQS = 4

VMEM_LIMIT = 48 * 1024 * 1024

NN = ((1,), (0,))
NT = ((1,), (1,))
TN = ((0,), (0,))


def _dot(a, b, dims=NN):
    return lax.dot_general(a, b, (dims, ((), ())), preferred_element_type=F32)


def _split2(x):
    hi = x.astype(BF16)
    lo = (x - hi.astype(F32)).astype(BF16)
    return hi, lo


def _mm1(a, b):
    return _dot(a.astype(BF16), b.astype(BF16))


def _mm_exact_lhs(lhs_bf16, x):
    hi, lo = _split2(x)
    return _dot(lhs_bf16, hi) + _dot(lhs_bf16, lo)


def _sigmoid(x):
    return 1.0 / (1.0 + jnp.exp(-x))


def _silu(x):
    return x * _sigmoid(x)


def _proj_kernel(x_ref, w_ref, *out_refs, splits):
    x = x_ref[...].astype(BF16)
    for o_ref, (lo, hi) in zip(out_refs, splits):
        o_ref[...] = _dot(x, w_ref[:, lo:hi])


def _proj(x, w_bf16, splits, tm):
    m, k = x.shape
    n = w_bf16.shape[1]
    return pl.pallas_call(
        functools.partial(_proj_kernel, splits=splits),
        name=f"proj_{m}x{n}",
        grid=(m // tm,),
        in_specs=[pl.BlockSpec((tm, k), lambda i: (i, 0)),
                  pl.BlockSpec((k, n), lambda i: (0, 0))],
        out_specs=[pl.BlockSpec((tm, hi - lo), lambda i: (i, 0)) for lo, hi in splits],
        out_shape=[jax.ShapeDtypeStruct((m, hi - lo), F32) for lo, hi in splits],
        compiler_params=pltpu.CompilerParams(
            dimension_semantics=("arbitrary",), vmem_limit_bytes=VMEM_LIMIT),
    )(x, w_bf16)


def _pair_blocks(x2, left):
    zero = jnp.zeros((), x2.dtype)
    return jnp.concatenate([jnp.where(left, x2, zero), jnp.where(left, zero, x2)], axis=0)


def _pair_sum(x2, left):
    s0 = jnp.sum(jnp.where(left, x2, 0.0), axis=-1, keepdims=True)
    s1 = jnp.sum(jnp.where(left, 0.0, x2), axis=-1, keepdims=True)
    return jnp.where(left, s0, s1)


UNIT_COMMON = ("ar", "bkp", "v16", "pc", "gate", "bonus")
UNIT_FREE_OUT = ("a4", "tinv", "akv")
UNIT_FREE_IN = ("bk_bd",)


def _wkv_kernel(z0_ref, g0_ref, zn_ref, gn_ref, prev_ref, s0_ref, mu_ref, w0_ref, w2_ref, a0_ref, a2_ref,
                kk_ref, ka_ref, rk_ref, gng_ref, gnb_ref, out_ref, sfin_ref, carry_ref, state_ref, *unit_refs,
                bb_n, c, n_ch, n_grp, pipelined):
    t = pl.program_id(1)
    tt = n_ch * c
    n_pairs = R_HEADS // 2
    per_chunk = bb_n * n_pairs

    row2 = lax.broadcasted_iota(jnp.int32, (c, 2 * c), 0)
    col2 = lax.broadcasted_iota(jnp.int32, (c, 2 * c), 1) & (c - 1)
    eye2 = (row2 == col2).astype(BF16)
    strict2 = col2 < row2
    incl4 = ((lax.broadcasted_iota(jnp.int32, (c, 4 * c), 1) & (c - 1))
             <= lax.broadcasted_iota(jnp.int32, (c, 4 * c), 0))
    level_masks = []
    shift = 0
    while (1 << shift) < c:
        rb = row2 >> shift
        level_masks.append(((rb & 1) == 1) & ((col2 >> shift) == rb - 1))
        shift += 1
    left_s = lax.broadcasted_iota(jnp.int32, (1, 2 * c), 1) < c
    left = lax.broadcasted_iota(jnp.int32, (1, PAIR), 1) < HEAD_DIM
    zero = jnp.zeros((), BF16)
    row_t = lax.broadcasted_iota(jnp.int32, (tt, tt), 0)
    col_t = lax.broadcasted_iota(jnp.int32, (tt, tt), 1)
    shift_c = c.bit_length() - 1
    ltri = ((col_t <= row_t) & ((col_t >> shift_c) == (row_t >> shift_c))).astype(BF16)
    first_row = lax.broadcasted_iota(jnp.int32, (tt, 1), 0) == 0
    mu = mu_ref[...]

    def field(un, f):
        if f not in un:
            un[f] = _pair_blocks(field(un, "v16"), left) if f == "v_bd" else un["stored"][f]()
        return un[f]

    def make_tile(zs_ref, gr_ref, prev_rows):
        tile = [dict() for _ in range(bb_n)]
        chunks = [[] for _ in range(n_ch)]

        def whole_tile(bb):
            zs = zs_ref[bb]
            zprev = jnp.where(first_row, prev_rows[bb], pltpu.roll(zs, 1, 0))
            xs = zs + (zprev - zs) * mu
            wd = xs[:, 3 * R_WIDTH:3 * R_WIDTH + LOW_RANK]
            ad = xs[:, 3 * R_WIDTH + LOW_RANK:]
            u = -(w0_ref[...] + _mm1(jnp.tanh(wd), w2_ref[...]))
            softplus = jnp.maximum(u, 0.0) + jnp.log(1.0 + jnp.exp(-jnp.abs(u)))
            ld = -jnp.exp(-softplus - 0.5)
            tile[bb].update(xs=xs, ld=ld, cum=_mm_exact_lhs(ltri, ld),
                            a_pre=a0_ref[...] + _mm1(ad, a2_ref[...]))

        def prep_tasks(ch):
            rows = slice(ch * c, (ch + 1) * c)
            tasks = []
            for bb in range(bb_n):
                wide = {}

                def full_width(bb=bb, wide=wide):
                    xs = tile[bb]["xs"][rows]
                    r = xs[:, 0:R_WIDTH]
                    k = xs[:, R_WIDTH:2 * R_WIDTH]
                    ld = tile[bb]["ld"][rows]
                    cum = tile[bb]["cum"][rows]
                    a = _sigmoid(tile[bb]["a_pre"][rows])
                    ecum = jnp.exp(cum)
                    k2 = k * (1.0 + (a - 1.0) * ka_ref[...])
                    wide.update(v=xs[:, 2 * R_WIDTH:3 * R_WIDTH], a=a, ecum=ecum, einv=jnp.exp(-cum),
                                eprev=jnp.exp(cum - ld), kkr=k * kk_ref[...], rt=r * ecum, k2=k2,
                                rk2=r * k2 * rk_ref[...], gate=_silu(gr_ref[bb, rows, :]))

                def pair(pr, bb=bb, wide=wide):
                    sl = slice(pr * PAIR, (pr + 1) * PAIR)
                    kkr_p = wide["kkr"][:, sl]
                    nrm = jnp.sqrt(_pair_sum(kkr_p * kkr_p, left))
                    kkn = kkr_p / jnp.maximum(nrm, 1e-12)
                    at = -kkn * wide["eprev"][:, sl]
                    bh = kkn * wide["a"][:, sl] * wide["einv"][:, sl]
                    kh = wide["k2"][:, sl] * wide["einv"][:, sl]
                    pc = wide["ecum"][c - 1:c, sl]
                    v_p = wide["v"][:, sl]
                    v16 = v_p.astype(BF16)
                    chunks[ch].append(dict(
                        bb=bb, pr=pr, sl=sl, rows=rows, pc=pc, gate=wide["gate"][:, sl],
                        bonus=_pair_sum(wide["rk2"][:, sl], left) * v_p, v_bd=_pair_blocks(v16, left), v16=v16,
                        ar=jnp.concatenate([at, wide["rt"][:, sl]], axis=0).astype(BF16),
                        bk_bd=jnp.concatenate([_pair_blocks(bh.astype(BF16), left),
                                               _pair_blocks(kh.astype(BF16), left)], axis=0),
                        bkp=jnp.concatenate([bh * pc, kh * pc], axis=0).astype(BF16)))

                tasks.append(full_width)
                tasks += [functools.partial(pair, pr) for pr in range(n_pairs)]
            return tasks

        return chunks, [functools.partial(whole_tile, bb) for bb in range(bb_n)], prep_tasks

    def st_a4(units):
        for un in units:
            un["a4"] = _dot(field(un, "ar"), field(un, "bk_bd"), NT).astype(BF16)
            un["tinv"] = jnp.where(level_masks[0], un["a4"][:c, :2 * c], eye2)

    def st_lt(mask):
        def run(units):
            for un in units:
                a_ab = un["a4"][:c, :2 * c]
                un["lt"] = _dot(jnp.where(mask, a_ab, zero), _pair_blocks(un["tinv"], left_s)).astype(BF16)
        return run

    def st_tinv(mask):
        def run(units):
            for un in units:
                new = _dot(un["tinv"], _pair_blocks(un["lt"], left_s)).astype(BF16)
                un["tinv"] = jnp.where(mask, new, un["tinv"])
        return run

    def st_akv(units):
        for un in units:
            un["akv"] = _dot(jnp.where(strict2, un["a4"][:c, 2 * c:], zero), field(un, "v_bd"))

    def st_ars(units):
        for un in units:
            un["s0"] = state[un["bb"], un["pr"]]
            un["ars"] = _dot(field(un, "ar"), _pair_blocks(un["s0"].astype(BF16), left), NT)

    def st_pm(units):
        for un in units:
            rhs = (un["ars"][:c] + field(un, "akv")).astype(BF16)
            un["pm"] = _dot(field(un, "tinv"), _pair_blocks(rhs, left)).astype(BF16)

    def st_state(units):
        for un in units:
            pv = jnp.concatenate([un["pm"], field(un, "v16")], axis=0)
            cross = _dot(pv, field(un, "bkp"), TN)
            state[un["bb"], un["pr"]] = (un["s0"] * field(un, "pc")
                                         + jnp.where(left, cross[:HEAD_DIM], cross[HEAD_DIM:]))

    def st_y(units):
        for un in units:
            pv_bd = jnp.concatenate([_pair_blocks(un["pm"], left), field(un, "v_bd")], axis=0)
            y = un["ars"][c:] + _dot(jnp.where(incl4, field(un, "a4")[c:], zero), pv_bd)
            mean = _pair_sum(y, left) * (1.0 / HEAD_DIM)
            yc = y - mean
            var = _pair_sum(yc * yc, left) * (1.0 / HEAD_DIM)
            yn = yc * lax.rsqrt(var + GN_EPS) * gng_ref[:, un["sl"]] + gnb_ref[:, un["sl"]]
            out_ref[un["bb"], un["rows"], un["sl"]] = (yn + field(un, "bonus")) * field(un, "gate")

    free_stages = [st_a4]
    for mask in level_masks[1:]:
        free_stages += [st_lt(mask), st_tinv(mask)]
    free_stages.append(st_akv)
    state_stages = [st_ars, st_pm, st_state, st_y]

    def stage_tasks(stages, chunks, chs):
        def run(stage, lo):
            units = [un for ch in chs for un in chunks[ch]]
            stage(units[lo:lo + STAGE_SPLIT])

        return [functools.partial(run, stage, lo) for stage in stages
                for lo in range(0, len(chs) * per_chunk, STAGE_SPLIT)]

    def run_interleaved(*task_lists):
        keyed = [((i + 0.5) / len(tasks), n, i, task)
                 for n, tasks in enumerate(task_lists) for i, task in enumerate(tasks)]
        for _, _, _, task in sorted(keyed, key=lambda e: e[:3]):
            task()

    groups = [list(range(g0, min(g0 + n_grp, n_ch))) for g0 in range(0, n_ch, n_grp)]

    if pipelined:
        refs = dict(zip(UNIT_COMMON + UNIT_FREE_OUT + UNIT_FREE_IN, unit_refs))
        n_g = n_grp * per_chunk

        def store_units(chunks, chs, fields, base):
            units = [un for ch in chs for un in chunks[ch]]
            for idx, un in enumerate(units):
                for f in fields:
                    refs[f][(base if f in UNIT_COMMON else 0) + idx] = un[f]

        def load_units(chs, fields, base):
            chunks = {}
            idx = 0
            for ch in chs:
                chunks[ch] = []
                for bb in range(bb_n):
                    for pr in range(n_pairs):
                        un = dict(bb=bb, pr=pr, sl=slice(pr * PAIR, (pr + 1) * PAIR),
                                  rows=slice(ch * c, (ch + 1) * c))
                        un["stored"] = {f: functools.partial(lambda f, i: refs[f][i], f,
                                                             (base if f in UNIT_COMMON else 0) + idx)
                                        for f in fields}
                        chunks[ch].append(un)
                        idx += 1
            return chunks

        def hand_over(chunks):
            store_units(chunks, groups[0], UNIT_COMMON + UNIT_FREE_OUT, 0)
            store_units(chunks, groups[1], UNIT_COMMON + UNIT_FREE_IN, n_g)

        @pl.when(t == 0)
        def _():
            for bb in range(bb_n):
                for h in range(R_HEADS):
                    state_ref[bb, h // 2, :, (h % 2) * HEAD_DIM:(h % 2 + 1) * HEAD_DIM] = s0_ref[bb, h]
            first, whole, prep = make_tile(z0_ref, g0_ref, [prev_ref[bb] for bb in range(bb_n)])
            run_interleaved(whole)
            run_interleaved([task for ch in range(n_ch) for task in prep(ch)])
            run_interleaved(stage_tasks(free_stages, first, groups[0]))
            hand_over(first)
            carry_ref[...] = z0_ref[:, tt - 1:tt, :]

        cur = load_units(groups[0], UNIT_COMMON + UNIT_FREE_OUT, 0)
        cur.update(load_units(groups[1], UNIT_COMMON + UNIT_FREE_IN, n_g))
        nxt, whole, prep = make_tile(zn_ref, gn_ref, [carry_ref[bb] for bb in range(bb_n)])
        state = {(bb, pr): state_ref[bb, pr] for bb in range(bb_n) for pr in range(n_pairs)}
        run_interleaved([task for ch in groups[0] for task in stage_tasks(state_stages, cur, [ch])],
                        stage_tasks(free_stages, cur, groups[1]),
                        whole + [task for ch in groups[0] for task in prep(ch)])
        run_interleaved([task for ch in groups[1] for task in stage_tasks(state_stages, cur, [ch])],
                        stage_tasks(free_stages, nxt, groups[0]),
                        [task for ch in groups[1] for task in prep(ch)])
        hand_over(nxt)
        carry_ref[...] = zn_ref[:, tt - 1:tt, :]
    else:
        @pl.when(t == 0)
        def _():
            carry_ref[...] = prev_ref[...]
            for bb in range(bb_n):
                for h in range(R_HEADS):
                    state_ref[bb, h // 2, :, (h % 2) * HEAD_DIM:(h % 2 + 1) * HEAD_DIM] = s0_ref[bb, h]

        cur, whole, prep = make_tile(z0_ref, g0_ref, [carry_ref[bb] for bb in range(bb_n)])
        state = {(bb, pr): state_ref[bb, pr] for bb in range(bb_n) for pr in range(n_pairs)}
        run_interleaved(whole)
        carry_ref[...] = z0_ref[:, tt - 1:tt, :]
        run_interleaved([task for ch in range(n_ch) for task in prep(ch)])
        run_interleaved(stage_tasks(free_stages, cur, groups[0]))
        for g, chs in enumerate(groups):
            run_interleaved([task for ch in chs for task in stage_tasks(state_stages, cur, [ch])],
                            stage_tasks(free_stages, cur, groups[g + 1]) if g + 1 < len(groups) else [])
    for (bb, pr), val in state.items():
        state_ref[bb, pr] = val

    @pl.when(t == pl.num_programs(1) - 1)
    def _():
        for bb in range(bb_n):
            for h in range(R_HEADS):
                sfin_ref[bb, h] = state_ref[bb, h // 2, :, (h % 2) * HEAD_DIM:(h % 2 + 1) * HEAD_DIM]


def _wkv(zs, gr, prev, s0, params, *, bb_n, c, n_ch, n_grp=1):
    b, t, _ = zs.shape
    tt = n_ch * c
    n_t = t // tt
    pipelined = n_ch == 2 * n_grp and n_t > 1
    full = lambda arr: pl.BlockSpec(arr.shape, lambda i, j: (0,) * arr.ndim)
    first = lambda width: pl.BlockSpec((bb_n, tt, width), lambda i, j: (i, 0 if pipelined else j, 0))
    ahead = lambda width: pl.BlockSpec((bb_n, tt, width), lambda i, j: (i, jnp.minimum(j + 1, n_t - 1), 0))
    unit_scratch = []
    if pipelined:
        n_g = n_grp * bb_n * (R_HEADS // 2)
        shapes = dict(ar=((2 * c, PAIR), BF16), bkp=((2 * c, PAIR), BF16), v16=((c, PAIR), BF16),
                      pc=((1, PAIR), F32), gate=((c, PAIR), F32), bonus=((c, PAIR), F32),
                      a4=((2 * c, 4 * c), BF16), tinv=((c, 2 * c), BF16), akv=((c, PAIR), F32),
                      bk_bd=((4 * c, PAIR), BF16))
        for f in UNIT_COMMON + UNIT_FREE_OUT + UNIT_FREE_IN:
            shape, dtype = shapes[f]
            unit_scratch.append(pltpu.VMEM(((2 * n_g if f in UNIT_COMMON else n_g),) + shape, dtype))
    return pl.pallas_call(
        functools.partial(_wkv_kernel, bb_n=bb_n, c=c, n_ch=n_ch, n_grp=n_grp, pipelined=pipelined),
        name=f"wkv_c{c}",
        grid=(b // bb_n, n_t),
        in_specs=[first(SHIFT_WIDTH), first(R_WIDTH), ahead(SHIFT_WIDTH), ahead(R_WIDTH),
                  pl.BlockSpec((bb_n, 1, SHIFT_WIDTH), lambda i, j: (i, 0, 0)),
                  pl.BlockSpec((bb_n, R_HEADS, HEAD_DIM, HEAD_DIM), lambda i, j: (i, 0, 0, 0))]
                 + [full(p) for p in params],
        out_specs=[pl.BlockSpec((bb_n, tt, R_WIDTH), lambda i, j: (i, j, 0)),
                   pl.BlockSpec((bb_n, R_HEADS, HEAD_DIM, HEAD_DIM), lambda i, j: (i, 0, 0, 0))],
        out_shape=[jax.ShapeDtypeStruct((b, t, R_WIDTH), F32),
                   jax.ShapeDtypeStruct((b, R_HEADS, HEAD_DIM, HEAD_DIM), F32)],
        scratch_shapes=[pltpu.VMEM((bb_n, 1, SHIFT_WIDTH), F32),
                        pltpu.VMEM((bb_n, R_HEADS // 2, HEAD_DIM, PAIR), F32)] + unit_scratch,
        compiler_params=pltpu.CompilerParams(
            dimension_semantics=("arbitrary", "arbitrary"), vmem_limit_bytes=VMEM_LIMIT),
    )(zs, gr, zs, gr, prev, s0, *params)


def _fill_rel_bias(bias_ref, gtab_ref, offset):
    heads, nq, nk = bias_ref.shape
    for h in range(heads):
        g = jnp.broadcast_to(gtab_ref[h:h + 1, :], (nq, BIAS_L))
        bias_ref[h] = pltpu.roll(g, BIAS_L - offset, 1, stride=1, stride_axis=0)[:, :nk]


def _stack_heads(q2, left):
    zero = jnp.zeros((), q2.dtype)
    return jnp.concatenate([jnp.where(left, q2, zero), jnp.where(left, zero, q2)], axis=0)


def _attend_stages(jobs, left, write):
    def scores():
        for jb in jobs:
            s = _dot(jb["lhs"], jb["k"], NT)
            jb["s"] = s if jb.get("bias") is None else s + jb["bias"]

    def weights():
        for jb in jobs:
            m = jnp.max(jb["s"], axis=-1, keepdims=True)
            p = jnp.exp2(jb["s"] - m)
            jb["l"] = jnp.sum(p, axis=-1, keepdims=True)
            jb["p"] = p.astype(BF16)

    def values():
        for jb in jobs:
            o2 = _dot(jb["p"], jb["v"]) / jb["l"]
            n = o2.shape[0] // 2
            write(jb, jnp.where(left, o2[:n], o2[n:]))

    return [scores, weights, values]


def _run_staggered(groups):
    depth = max(len(g) for g in groups)
    for step in range(len(groups) + depth - 1):
        for g in range(len(groups) - 1, -1, -1):
            if 0 <= step - g < len(groups[g]):
                groups[g][step - g]()


def _layer_norm(h, g, b):
    mean = jnp.mean(h, axis=-1, keepdims=True)
    hc = h - mean
    var = jnp.mean(hc * hc, axis=-1, keepdims=True)
    return hc * lax.rsqrt(var + LN_EPS) * g + b


def _attn_prompt_kernel(q_ref, kc_ref, vc_ref, gb_ref, mq_ref, gm_ref, mk_ref, mv_ref, gtab_ref,
                        x_ref, mr_ref, w_ref, lng_ref, lnb_ref, y_ref,
                        kcat_ref, vcat_ref, bias_ref, biasv_ref, mix_ref, *, tq):
    j = pl.program_id(1)
    n_chunks = tq // CHUNK

    @pl.when((pl.program_id(0) == 0) & (j == 0))
    def _():
        _fill_rel_bias(bias_ref, gtab_ref, CHUNK - 1)
        kcol = lax.broadcasted_iota(jnp.int32, (1, BAND_LEN), 1)
        for h in range(B_HEADS):
            rows = slice((h % 2) * CHUNK, (h % 2 + 1) * CHUNK)
            scaled = bias_ref[h] * LOG2E
            biasv_ref[0, h // 2, rows, :] = scaled
            for i in range(n_chunks):
                biasv_ref[1 + i, h // 2, rows, :] = jnp.where(kcol >= BAND_WINDOW - i * CHUNK, scaled, -jnp.inf)

    @pl.when(j == 0)
    def _():
        kcat_ref[0:BAND_WINDOW] = jnp.zeros((BAND_WINDOW, B_WIDTH), BF16)
        vcat_ref[0:BAND_WINDOW] = jnp.zeros((BAND_WINDOW, B_WIDTH), BF16)

    @pl.when(j > 0)
    def _():
        kcat_ref[0:BAND_WINDOW] = kcat_ref[BAND_WINDOW:]
        vcat_ref[0:BAND_WINDOW] = vcat_ref[BAND_WINDOW:]

    kcat_ref[BAND_WINDOW:] = kc_ref[0].astype(BF16)
    vcat_ref[BAND_WINDOW:] = vc_ref[0].astype(BF16)
    q = (q_ref[0] * (ATT_SCALE * LOG2E)).astype(BF16)
    gate_b = _silu(gb_ref[0])
    left = lax.broadcasted_iota(jnp.int32, (1, PAIR), 1) < HEAD_DIM
    mq = (mq_ref[0] * (ATT_SCALE * LOG2E)).astype(BF16)
    gate_m = _silu(gm_ref[0])
    mk = mk_ref[0].astype(BF16)
    mv = mv_ref[0].astype(BF16)

    def write_band(jb, o):
        mix_ref[jb["rows"], jb["lanes"]] = (o * gate_b[jb["rows"], jb["lanes"]]).astype(BF16)

    def write_mem(jb, o):
        mix_ref[jb["rows"], B_WIDTH + jb["lanes"].start:B_WIDTH + jb["lanes"].stop] = (
            o * gate_m[jb["rows"], jb["lanes"]]).astype(BF16)

    groups = []
    for i0 in range(0, n_chunks, ATTN_GROUP_CHUNKS):
        jobs = []
        for i in range(i0, i0 + ATTN_GROUP_CHUNKS):
            rows = slice(i * CHUNK, (i + 1) * CHUNK)
            keys = slice(i * CHUNK, i * CHUNK + BAND_LEN)
            variant = jnp.where(j == 0, 1 + i, 0)
            for pr in range(B_HEADS // 2):
                lanes = slice(pr * PAIR, (pr + 1) * PAIR)
                jobs.append(dict(rows=rows, lanes=lanes, lhs=_stack_heads(q[rows, lanes], left),
                                 k=kcat_ref[keys, lanes], v=vcat_ref[keys, lanes],
                                 bias=biasv_ref[variant, pr]))
        groups.append(_attend_stages(jobs, left, write_band))
    for r0 in range(0, tq, MEM_ROWS * MEM_GROUP_BLOCKS):
        jobs = []
        for r in range(r0, r0 + MEM_ROWS * MEM_GROUP_BLOCKS, MEM_ROWS):
            rows = slice(r, r + MEM_ROWS)
            for pr in range(M_HEADS // 2):
                lanes = slice(pr * PAIR, (pr + 1) * PAIR)
                jobs.append(dict(rows=rows, lanes=lanes, lhs=_stack_heads(mq[rows, lanes], left),
                                 k=mk[:, lanes], v=mv[:, lanes]))
        groups.append(_attend_stages(jobs, left, write_mem))
    _run_staggered(groups)

    o = _dot(mr_ref[0].astype(BF16), w_ref[0:R_WIDTH, :]) + _dot(mix_ref[...], w_ref[R_WIDTH:, :])
    y_ref[0] = _layer_norm(ALPHA * x_ref[0] + o, lng_ref[...], lnb_ref[...])


def _attn_prompt(att, memkv, gtab, x, mix_r, w_out_bf16, ln_g, ln_b, *, tq):
    b, t, _ = att.shape
    col = lambda cidx: pl.BlockSpec((1, tq, B_WIDTH), lambda i, j: (i, j, cidx))
    tile = lambda width: pl.BlockSpec((1, tq, width), lambda i, j: (i, j, 0))
    full = lambda arr: pl.BlockSpec(arr.shape, lambda i, j: (0, 0))
    return pl.pallas_call(
        functools.partial(_attn_prompt_kernel, tq=tq),
        name="attn_prompt",
        grid=(b, t // tq),
        in_specs=[col(0), col(1), col(2), col(3), col(4), col(5),
                  pl.BlockSpec((1, N_MEM, B_WIDTH), lambda i, j: (i, 0, 0)),
                  pl.BlockSpec((1, N_MEM, B_WIDTH), lambda i, j: (i, 0, 1)),
                  full(gtab), tile(D_MODEL), tile(R_WIDTH), full(w_out_bf16), full(ln_g), full(ln_b)],
        out_specs=tile(D_MODEL),
        out_shape=jax.ShapeDtypeStruct((b, t, D_MODEL), F32),
        scratch_shapes=[pltpu.VMEM((2 * BAND_WINDOW, B_WIDTH), BF16),
                        pltpu.VMEM((2 * BAND_WINDOW, B_WIDTH), BF16),
                        pltpu.VMEM((B_HEADS, CHUNK, BAND_LEN), F32),
                        pltpu.VMEM((1 + tq // CHUNK, B_HEADS // 2, 2 * CHUNK, BAND_LEN), F32),
                        pltpu.VMEM((tq, 2 * B_WIDTH), BF16)],
        compiler_params=pltpu.CompilerParams(
            dimension_semantics=("arbitrary", "arbitrary"), vmem_limit_bytes=VMEM_LIMIT),
    )(att, att, att, att, att, att, memkv, memkv, gtab, x, mix_r, w_out_bf16, ln_g, ln_b)


def _attn_sample_kernel(att_ref, ck_ref, cv_ref, mk_ref, mv_ref, gtab_ref, out_ref, bias_ref, bias2_ref):
    n_seq, n_new, _ = att_ref.shape

    @pl.when(pl.program_id(0) == 0)
    def _():
        _fill_rel_bias(bias_ref, gtab_ref, n_new - 1)
        for h in range(B_HEADS):
            bias2_ref[h // 2, (h % 2) * n_new:(h % 2 + 1) * n_new, :] = bias_ref[h] * LOG2E

    left = lax.broadcasted_iota(jnp.int32, (1, PAIR), 1) < HEAD_DIM
    jobs = []
    for b in range(n_seq):
        att = att_ref[b]
        q = (att[:, 0:B_WIDTH] * (ATT_SCALE * LOG2E)).astype(BF16)
        k_all = jnp.concatenate([ck_ref[b].astype(BF16), att[:, B_WIDTH:2 * B_WIDTH].astype(BF16)], axis=0)
        v_all = jnp.concatenate([cv_ref[b].astype(BF16), att[:, 2 * B_WIDTH:3 * B_WIDTH].astype(BF16)], axis=0)
        gate_b = _silu(att[:, 3 * B_WIDTH:4 * B_WIDTH])
        mq = (att[:, 4 * B_WIDTH:5 * B_WIDTH] * (ATT_SCALE * LOG2E)).astype(BF16)
        gate_m = _silu(att[:, 5 * B_WIDTH:6 * B_WIDTH])
        mk = mk_ref[b].astype(BF16)
        mv = mv_ref[b].astype(BF16)
        for pr in range(B_HEADS // 2):
            lanes = slice(pr * PAIR, (pr + 1) * PAIR)
            jobs.append(dict(b=b, out=lanes, gate=gate_b[:, lanes], lhs=_stack_heads(q[:, lanes], left),
                             k=k_all[:, lanes], v=v_all[:, lanes], bias=bias2_ref[pr]))
        for pr in range(M_HEADS // 2):
            lanes = slice(pr * PAIR, (pr + 1) * PAIR)
            jobs.append(dict(b=b, out=slice(B_WIDTH + lanes.start, B_WIDTH + lanes.stop), gate=gate_m[:, lanes],
                             lhs=_stack_heads(mq[:, lanes], left), k=mk[:, lanes], v=mv[:, lanes]))
    def write(jb, o):
        out_ref[jb["b"], :, jb["out"]] = o * jb["gate"]

    _run_staggered([_attend_stages(jobs, left, write)])


def _attn_sample(att, cache_k, cache_v, mem_k, mem_v, gtab, *, n_seq):
    b, s, _ = att.shape
    per_b = lambda arr: pl.BlockSpec((n_seq,) + arr.shape[1:], lambda i: (i, 0, 0))
    n_keys = cache_k.shape[1] + s
    return pl.pallas_call(
        _attn_sample_kernel,
        name="attn_sample",
        grid=(b // n_seq,),
        in_specs=[per_b(att), per_b(cache_k), per_b(cache_v), per_b(mem_k), per_b(mem_v),
                  pl.BlockSpec(gtab.shape, lambda i: (0, 0))],
        out_specs=pl.BlockSpec((n_seq, s, 2 * B_WIDTH), lambda i: (i, 0, 0)),
        out_shape=jax.ShapeDtypeStruct((b, s, 2 * B_WIDTH), F32),
        scratch_shapes=[pltpu.VMEM((B_HEADS, s, n_keys), F32),
                        pltpu.VMEM((B_HEADS // 2, 2 * s, n_keys), F32)],
        compiler_params=pltpu.CompilerParams(dimension_semantics=("arbitrary",)),
    )(att, cache_k, cache_v, mem_k, mem_v, gtab)


def _finish_kernel(x_ref, mr_ref, mbm_ref, w_ref, g_ref, b_ref, y_ref):
    o = _dot(mr_ref[...].astype(BF16), w_ref[0:R_WIDTH, :])
    o = o + _dot(mbm_ref[...].astype(BF16), w_ref[R_WIDTH:, :])
    y_ref[...] = _layer_norm(ALPHA * x_ref[...] + o, g_ref[...], b_ref[...])


def _finish(x, mix_r, mix_bm, w_out_bf16, ln_g, ln_b, tm):
    m = x.shape[0]
    rows = lambda width: pl.BlockSpec((tm, width), lambda i: (i, 0))
    full = lambda arr: pl.BlockSpec(arr.shape, lambda i: (0, 0))
    return pl.pallas_call(
        _finish_kernel,
        name=f"finish_{m}",
        grid=(m // tm,),
        in_specs=[rows(D_MODEL), rows(R_WIDTH), rows(2 * B_WIDTH), full(w_out_bf16), full(ln_g), full(ln_b)],
        out_specs=rows(D_MODEL),
        out_shape=jax.ShapeDtypeStruct((m, D_MODEL), F32),
        compiler_params=pltpu.CompilerParams(
            dimension_semantics=("arbitrary",), vmem_limit_bytes=VMEM_LIMIT),
    )(x, mix_r, mix_bm, w_out_bf16, ln_g, ln_b)


IN_SPLITS = ((0, SHIFT_WIDTH), (SHIFT_WIDTH, SHIFT_WIDTH + R_WIDTH), (SHIFT_WIDTH + R_WIDTH, IN_WIDTH))


def _rel_bias_row(table, rel0):
    n_hi = rel0 - REL_CLIP
    n_lo = BIAS_L - n_hi - (2 * REL_CLIP + 1)
    heads = table.shape[0]
    return jnp.concatenate([jnp.broadcast_to(table[:, 2 * REL_CLIP:], (heads, n_hi)), table[:, ::-1],
                            jnp.broadcast_to(table[:, 0:1], (heads, n_lo))], axis=1)


def kernel(x_prompt, x_sample, mem_prompt, state_shift, state_wkv, cache_band_k, cache_band_v,
           cache_mem_k, cache_mem_v, w_in, mu_shift, w0, w2, a0, a2, k_k, k_a, r_k, gn_g, gn_b,
           rel_bias, w_mem_kv, w_out, ln_g, ln_b):
    bp, t, _ = x_prompt.shape
    bs, s, _ = x_sample.shape
    depth = w_in.shape[0]
    assert depth == 1 and t % BAND_WINDOW == 0 and t % (WKV_CHUNKS_PER_STEP * CHUNK) == 0
    assert s <= CHUNK and s & (s - 1) == 0 and bp % WKV_PROMPT_SEQS == 0
    assert bs % WKV_SAMPLE_SEQS == 0 and bs % ATTN_SAMPLE_SEQS == 0 and (bp * t) % PROJ_ROWS == 0
    keep = min(BAND_WINDOW, t)
    l = 0

    w_in_b = w_in[l].astype(BF16)
    w_out_b = w_out[l].astype(BF16)
    w_mem_b = w_mem_kv[l].astype(BF16)
    row = lambda p: p.reshape(1, -1)
    wkv_params = (row(mu_shift[l]), row(w0[l]), w2[l], row(a0[l]), a2[l], row(k_k[l]), row(k_a[l]),
                  row(r_k[l]), row(gn_g[l]), row(gn_b[l]))
    table = rel_bias[l]
    r_rows = cache_band_k.shape[2]
    gtab_p = _rel_bias_row(table, BAND_WINDOW + CHUNK - 1)
    gtab_s = _rel_bias_row(table, r_rows + s - 1)

    xp = x_prompt.reshape(bp * t, D_MODEL)
    zs, gr, att = _proj(xp, w_in_b, IN_SPLITS, PROJ_ROWS)
    zs = zs.reshape(bp, t, SHIFT_WIDTH)
    att = att.reshape(bp, t, ATT_WIDTH)
    memkv, = _proj(mem_prompt.reshape(bp * N_MEM, D_MODEL), w_mem_b, ((0, 2 * B_WIDTH),), PROJ_ROWS)
    memkv = memkv.reshape(bp, N_MEM, 2 * B_WIDTH)
    mix_r, p_wkv = _wkv(zs, gr.reshape(bp, t, R_WIDTH), jnp.zeros((bp, 1, SHIFT_WIDTH), F32),
                        jnp.zeros((bp, R_HEADS, HEAD_DIM, HEAD_DIM), F32), wkv_params,
                        bb_n=WKV_PROMPT_SEQS, c=CHUNK, n_ch=WKV_CHUNKS_PER_STEP, n_grp=WKV_GROUP_CHUNKS)
    y_prompt = _attn_prompt(att, memkv, gtab_p, x_prompt, mix_r, w_out_b, row(ln_g[l]), row(ln_b[l]),
                            tq=BAND_WINDOW)
    p_shift = zs[:, -1]
    p_bk = att[:, t - keep:, B_WIDTH:2 * B_WIDTH].reshape(bp, keep, B_HEADS, HEAD_DIM)
    p_bv = att[:, t - keep:, 2 * B_WIDTH:3 * B_WIDTH].reshape(bp, keep, B_HEADS, HEAD_DIM)
    p_mk = memkv[:, :, :B_WIDTH].reshape(bp, N_MEM, M_HEADS, HEAD_DIM)
    p_mv = memkv[:, :, B_WIDTH:].reshape(bp, N_MEM, M_HEADS, HEAD_DIM)

    xs = x_sample.reshape(bs * s, D_MODEL)
    zs_s, gr_s, att_s = _proj(xs, w_in_b, IN_SPLITS, bs * s)
    zs_s = zs_s.reshape(bs, s, SHIFT_WIDTH)
    att_s = att_s.reshape(bs, s, ATT_WIDTH)
    mix_r_s, s_wkv = _wkv(zs_s, gr_s.reshape(bs, s, R_WIDTH), state_shift[l][:, None, :], state_wkv[l],
                          wkv_params, bb_n=WKV_SAMPLE_SEQS, c=s, n_ch=1)
    mix_bm_s = _attn_sample(att_s,
                            cache_band_k[l].reshape(bs, r_rows, B_WIDTH),
                            cache_band_v[l].reshape(bs, r_rows, B_WIDTH),
                            cache_mem_k[l].reshape(bs, N_MEM, B_WIDTH),
                            cache_mem_v[l].reshape(bs, N_MEM, B_WIDTH), gtab_s, n_seq=ATTN_SAMPLE_SEQS)
    y_sample = _finish(xs, mix_r_s.reshape(bs * s, R_WIDTH), mix_bm_s.reshape(bs * s, 2 * B_WIDTH),
                       w_out_b, row(ln_g[l]), row(ln_b[l]), bs * s).reshape(bs, s, D_MODEL)
    s_shift = zs_s[:, -1]
    s_bk = att_s[:, :, B_WIDTH:2 * B_WIDTH].reshape(bs, s, B_HEADS, HEAD_DIM)
    s_bv = att_s[:, :, 2 * B_WIDTH:3 * B_WIDTH].reshape(bs, s, B_HEADS, HEAD_DIM)

    st = lambda a: a[None]
    return (y_prompt, y_sample, st(p_shift), st(p_wkv), st(p_bk), st(p_bv), st(p_mk), st(p_mv),
            st(s_shift), st(s_wkv), st(s_bk), st(s_bv))
```

```python
import functools

import numpy as np
import jax
import jax.numpy as jnp
from jax import lax
from jax.experimental import pallas as pl
from jax.experimental.pallas import tpu as pltpu

F32 = jnp.float32
BF16 = jnp.bfloat16

D_MODEL = 1024
HEAD_DIM = 64
R_WIDTH = 512
R_HEADS = 8
LOW_RANK = 64
SHIFT_WIDTH = 3 * R_WIDTH + 2 * LOW_RANK
B_WIDTH = 256
B_HEADS = 4
M_HEADS = 4
N_MEM = 256
CHUNK = 64
BAND_CHUNKS = 8
BAND_WINDOW = BAND_CHUNKS * CHUNK
BAND_LEN = BAND_WINDOW + CHUNK
REL_CLIP = 128
ATT_WIDTH = 6 * B_WIDTH
IN_WIDTH = SHIFT_WIDTH + R_WIDTH + ATT_WIDTH
LN_EPS = 1e-5
GN_EPS = 64e-5
ALPHA = 2.0 ** 0.25
ATT_SCALE = HEAD_DIM ** -0.5
LOG2E = float(np.log2(np.e))
BIAS_L = 1024
PAIR = 2 * HEAD_DIM
MEM_ROWS = 128
WKV_CHUNKS_PER_STEP = 4
WKV_GROUP_CHUNKS = 2
STAGE_SPLIT = 16
PROJ_ROWS = 512
WKV_PROMPT_SEQS = 2
WKV_SAMPLE_SEQS = 16
ATTN_SAMPLE_SEQS = 4

VMEM_LIMIT = 48 * 1024 * 1024

NN = ((1,), (0,))
NT = ((1,), (1,))
TN = ((0,), (0,))


def _dot(a, b, dims=NN):
    return lax.dot_general(a, b, (dims, ((), ())), preferred_element_type=F32)


def _split2(x):
    hi = x.astype(BF16)
    lo = (x - hi.astype(F32)).astype(BF16)
    return hi, lo


def _mm1(a, b):
    return _dot(a.astype(BF16), b.astype(BF16))


def _mm_exact_lhs(lhs_bf16, x):
    hi, lo = _split2(x)
    return _dot(lhs_bf16, hi) + _dot(lhs_bf16, lo)


def _sigmoid(x):
    return 1.0 / (1.0 + jnp.exp(-x))


def _silu(x):
    return x * _sigmoid(x)


def _proj_kernel(x_ref, w_ref, *out_refs, splits):
    x = x_ref[...].astype(BF16)
    for o_ref, (lo, hi) in zip(out_refs, splits):
        o_ref[...] = _dot(x, w_ref[:, lo:hi])


def _proj(x, w_bf16, splits, tm):
    m, k = x.shape
    n = w_bf16.shape[1]
    return pl.pallas_call(
        functools.partial(_proj_kernel, splits=splits),
        name=f"proj_{m}x{n}",
        grid=(m // tm,),
        in_specs=[pl.BlockSpec((tm, k), lambda i: (i, 0)),
                  pl.BlockSpec((k, n), lambda i: (0, 0))],
        out_specs=[pl.BlockSpec((tm, hi - lo), lambda i: (i, 0)) for lo, hi in splits],
        out_shape=[jax.ShapeDtypeStruct((m, hi - lo), F32) for lo, hi in splits],
        compiler_params=pltpu.CompilerParams(
            dimension_semantics=("arbitrary",), vmem_limit_bytes=VMEM_LIMIT),
    )(x, w_bf16)


def _pair_blocks(x2, left):
    zero = jnp.zeros((), x2.dtype)
    return jnp.concatenate([jnp.where(left, x2, zero), jnp.where(left, zero, x2)], axis=0)


def _pair_sum(x2, left):
    s0 = jnp.sum(jnp.where(left, x2, 0.0), axis=-1, keepdims=True)
    s1 = jnp.sum(jnp.where(left, 0.0, x2), axis=-1, keepdims=True)
    return jnp.where(left, s0, s1)


UNIT_COMMON = ("ar", "bkp", "v16", "pc", "gate", "bonus")
UNIT_FREE_OUT = ("a4", "tinv", "akv")
UNIT_FREE_IN = ("bk_bd",)


def _wkv_kernel(z0_ref, g0_ref, zn_ref, gn_ref, prev_ref, s0_ref, mu_ref, w0_ref, w2_ref, a0_ref, a2_ref,
                kk_ref, ka_ref, rk_ref, gng_ref, gnb_ref, out_ref, sfin_ref, carry_ref, state_ref, *unit_refs,
                bb_n, c, n_ch, n_grp, pipelined):
    t = pl.program_id(1)
    tt = n_ch * c
    n_pairs = R_HEADS // 2
    per_chunk = bb_n * n_pairs

    row2 = lax.broadcasted_iota(jnp.int32, (c, 2 * c), 0)
    col2 = lax.broadcasted_iota(jnp.int32, (c, 2 * c), 1) & (c - 1)
    eye2 = (row2 == col2).astype(BF16)
    strict2 = col2 < row2
    incl4 = ((lax.broadcasted_iota(jnp.int32, (c, 4 * c), 1) & (c - 1))
             <= lax.broadcasted_iota(jnp.int32, (c, 4 * c), 0))
    level_masks = []
    shift = 0
    while (1 << shift) < c:
        rb = row2 >> shift
        level_masks.append(((rb & 1) == 1) & ((col2 >> shift) == rb - 1))
        shift += 1
    left_s = lax.broadcasted_iota(jnp.int32, (1, 2 * c), 1) < c
    left = lax.broadcasted_iota(jnp.int32, (1, PAIR), 1) < HEAD_DIM
    zero = jnp.zeros((), BF16)
    row_t = lax.broadcasted_iota(jnp.int32, (tt, tt), 0)
    col_t = lax.broadcasted_iota(jnp.int32, (tt, tt), 1)
    shift_c = c.bit_length() - 1
    ltri = ((col_t <= row_t) & ((col_t >> shift_c) == (row_t >> shift_c))).astype(BF16)
    first_row = lax.broadcasted_iota(jnp.int32, (tt, 1), 0) == 0
    mu = mu_ref[...]

    def field(un, f):
        if f not in un:
            un[f] = _pair_blocks(field(un, "v16"), left) if f == "v_bd" else un["stored"][f]()
        return un[f]

    def make_tile(zs_ref, gr_ref, prev_rows):
        tile = [dict() for _ in range(bb_n)]
        chunks = [[] for _ in range(n_ch)]

        def whole_tile(bb):
            zs = zs_ref[bb]
            zprev = jnp.where(first_row, prev_rows[bb], pltpu.roll(zs, 1, 0))
            xs = zs + (zprev - zs) * mu
            wd = xs[:, 3 * R_WIDTH:3 * R_WIDTH + LOW_RANK]
            ad = xs[:, 3 * R_WIDTH + LOW_RANK:]
            u = -(w0_ref[...] + _mm1(jnp.tanh(wd), w2_ref[...]))
            softplus = jnp.maximum(u, 0.0) + jnp.log(1.0 + jnp.exp(-jnp.abs(u)))
            ld = -jnp.exp(-softplus - 0.5)
            tile[bb].update(xs=xs, ld=ld, cum=_mm_exact_lhs(ltri, ld),
                            a_pre=a0_ref[...] + _mm1(ad, a2_ref[...]))

        def prep_tasks(ch):
            rows = slice(ch * c, (ch + 1) * c)
            tasks = []
            for bb in range(bb_n):
                wide = {}

                def full_width(bb=bb, wide=wide):
                    xs = tile[bb]["xs"][rows]
                    r = xs[:, 0:R_WIDTH]
                    k = xs[:, R_WIDTH:2 * R_WIDTH]
                    ld = tile[bb]["ld"][rows]
                    cum = tile[bb]["cum"][rows]
                    a = _sigmoid(tile[bb]["a_pre"][rows])
                    ecum = jnp.exp(cum)
                    k2 = k * (1.0 + (a - 1.0) * ka_ref[...])
                    wide.update(v=xs[:, 2 * R_WIDTH:3 * R_WIDTH], a=a, ecum=ecum, einv=jnp.exp(-cum),
                                eprev=jnp.exp(cum - ld), kkr=k * kk_ref[...], rt=r * ecum, k2=k2,
                                rk2=r * k2 * rk_ref[...], gate=_silu(gr_ref[bb, rows, :]))

                def pair(pr, bb=bb, wide=wide):
                    sl = slice(pr * PAIR, (pr + 1) * PAIR)
                    kkr_p = wide["kkr"][:, sl]
                    nrm = jnp.sqrt(_pair_sum(kkr_p * kkr_p, left))
                    kkn = kkr_p / jnp.maximum(nrm, 1e-12)
                    at = -kkn * wide["eprev"][:, sl]
                    bh = kkn * wide["a"][:, sl] * wide["einv"][:, sl]
                    kh = wide["k2"][:, sl] * wide["einv"][:, sl]
                    pc = wide["ecum"][c - 1:c, sl]
                    v_p = wide["v"][:, sl]
                    v16 = v_p.astype(BF16)
                    chunks[ch].append(dict(
                        bb=bb, pr=pr, sl=sl, rows=rows, pc=pc, gate=wide["gate"][:, sl],
                        bonus=_pair_sum(wide["rk2"][:, sl], left) * v_p, v_bd=_pair_blocks(v16, left), v16=v16,
                        ar=jnp.concatenate([at, wide["rt"][:, sl]], axis=0).astype(BF16),
                        bk_bd=jnp.concatenate([_pair_blocks(bh.astype(BF16), left),
                                               _pair_blocks(kh.astype(BF16), left)], axis=0),
                        bkp=jnp.concatenate([bh * pc, kh * pc], axis=0).astype(BF16)))

                tasks.append(full_width)
                tasks += [functools.partial(pair, pr) for pr in range(n_pairs)]
            return tasks

        return chunks, [functools.partial(whole_tile, bb) for bb in range(bb_n)], prep_tasks

    def st_a4(units):
        for un in units:
            un["a4"] = _dot(field(un, "ar"), field(un, "bk_bd"), NT).astype(BF16)
            un["tinv"] = jnp.where(level_masks[0], un["a4"][:c, :2 * c], eye2)

    def st_lt(mask):
        def run(units):
            for un in units:
                a_ab = un["a4"][:c, :2 * c]
                un["lt"] = _dot(jnp.where(mask, a_ab, zero), _pair_blocks(un["tinv"], left_s)).astype(BF16)
        return run

    def st_tinv(mask):
        def run(units):
            for un in units:
                new = _dot(un["tinv"], _pair_blocks(un["lt"], left_s)).astype(BF16)
                un["tinv"] = jnp.where(mask, new, un["tinv"])
        return run

    def st_akv(units):
        for un in units:
            un["akv"] = _dot(jnp.where(strict2, un["a4"][:c, 2 * c:], zero), field(un, "v_bd"))

    def st_ars(units):
        for un in units:
            un["s0"] = state[un["bb"], un["pr"]]
            un["ars"] = _dot(field(un, "ar"), _pair_blocks(un["s0"].astype(BF16), left), NT)

    def st_pm(units):
        for un in units:
            rhs = (un["ars"][:c] + field(un, "akv")).astype(BF16)
            un["pm"] = _dot(field(un, "tinv"), _pair_blocks(rhs, left)).astype(BF16)

    def st_state(units):
        for un in units:
            pv = jnp.concatenate([un["pm"], field(un, "v16")], axis=0)
            cross = _dot(pv, field(un, "bkp"), TN)
            state[un["bb"], un["pr"]] = (un["s0"] * field(un, "pc")
                                         + jnp.where(left, cross[:HEAD_DIM], cross[HEAD_DIM:]))

    def st_y(units):
        for un in units:
            pv_bd = jnp.concatenate([_pair_blocks(un["pm"], left), field(un, "v_bd")], axis=0)
            y = un["ars"][c:] + _dot(jnp.where(incl4, field(un, "a4")[c:], zero), pv_bd)
            mean = _pair_sum(y, left) * (1.0 / HEAD_DIM)
            yc = y - mean
            var = _pair_sum(yc * yc, left) * (1.0 / HEAD_DIM)
            yn = yc * lax.rsqrt(var + GN_EPS) * gng_ref[:, un["sl"]] + gnb_ref[:, un["sl"]]
            out_ref[un["bb"], un["rows"], un["sl"]] = (yn + field(un, "bonus")) * field(un, "gate")

    free_stages = [st_a4]
    for mask in level_masks[1:]:
        free_stages += [st_lt(mask), st_tinv(mask)]
    free_stages.append(st_akv)
    state_stages = [st_ars, st_pm, st_state, st_y]

    def stage_tasks(stages, chunks, chs):
        def run(stage, lo):
            units = [un for ch in chs for un in chunks[ch]]
            stage(units[lo:lo + STAGE_SPLIT])

        return [functools.partial(run, stage, lo) for stage in stages
                for lo in range(0, len(chs) * per_chunk, STAGE_SPLIT)]

    def run_interleaved(*task_lists):
        keyed = [((i + 0.5) / len(tasks), n, i, task)
                 for n, tasks in enumerate(task_lists) for i, task in enumerate(tasks)]
        for _, _, _, task in sorted(keyed, key=lambda e: e[:3]):
            task()

    groups = [list(range(g0, min(g0 + n_grp, n_ch))) for g0 in range(0, n_ch, n_grp)]

    if pipelined:
        refs = dict(zip(UNIT_COMMON + UNIT_FREE_OUT + UNIT_FREE_IN, unit_refs))
        n_g = n_grp * per_chunk

        def store_units(chunks, chs, fields, base):
            units = [un for ch in chs for un in chunks[ch]]
            for idx, un in enumerate(units):
                for f in fields:
                    refs[f][(base if f in UNIT_COMMON else 0) + idx] = un[f]

        def load_units(chs, fields, base):
            chunks = {}
            idx = 0
            for ch in chs:
                chunks[ch] = []
                for bb in range(bb_n):
                    for pr in range(n_pairs):
                        un = dict(bb=bb, pr=pr, sl=slice(pr * PAIR, (pr + 1) * PAIR),
                                  rows=slice(ch * c, (ch + 1) * c))
                        un["stored"] = {f: functools.partial(lambda f, i: refs[f][i], f,
                                                             (base if f in UNIT_COMMON else 0) + idx)
                                        for f in fields}
                        chunks[ch].append(un)
                        idx += 1
            return chunks

        def hand_over(chunks):
            store_units(chunks, groups[0], UNIT_COMMON + UNIT_FREE_OUT, 0)
            store_units(chunks, groups[1], UNIT_COMMON + UNIT_FREE_IN, n_g)

        @pl.when(t == 0)
        def _():
            for bb in range(bb_n):
                for h in range(R_HEADS):
                    state_ref[bb, h // 2, :, (h % 2) * HEAD_DIM:(h % 2 + 1) * HEAD_DIM] = s0_ref[bb, h]
            first, whole, prep = make_tile(z0_ref, g0_ref, [prev_ref[bb] for bb in range(bb_n)])
            run_interleaved(whole)
            run_interleaved([task for ch in range(n_ch) for task in prep(ch)])
            run_interleaved(stage_tasks(free_stages, first, groups[0]))
            hand_over(first)
            carry_ref[...] = z0_ref[:, tt - 1:tt, :]

        cur = load_units(groups[0], UNIT_COMMON + UNIT_FREE_OUT, 0)
        cur.update(load_units(groups[1], UNIT_COMMON + UNIT_FREE_IN, n_g))
        nxt, whole, prep = make_tile(zn_ref, gn_ref, [carry_ref[bb] for bb in range(bb_n)])
        state = {(bb, pr): state_ref[bb, pr] for bb in range(bb_n) for pr in range(n_pairs)}
        run_interleaved([task for ch in groups[0] for task in stage_tasks(state_stages, cur, [ch])],
                        stage_tasks(free_stages, cur, groups[1]),
                        whole + [task for ch in groups[0] for task in prep(ch)])
        run_interleaved([task for ch in groups[1] for task in stage_tasks(state_stages, cur, [ch])],
                        stage_tasks(free_stages, nxt, groups[0]),
                        [task for ch in groups[1] for task in prep(ch)])
        hand_over(nxt)
        carry_ref[...] = zn_ref[:, tt - 1:tt, :]
    else:
        @pl.when(t == 0)
        def _():
            carry_ref[...] = prev_ref[...]
            for bb in range(bb_n):
                for h in range(R_HEADS):
                    state_ref[bb, h // 2, :, (h % 2) * HEAD_DIM:(h % 2 + 1) * HEAD_DIM] = s0_ref[bb, h]

        cur, whole, prep = make_tile(z0_ref, g0_ref, [carry_ref[bb] for bb in range(bb_n)])
        state = {(bb, pr): state_ref[bb, pr] for bb in range(bb_n) for pr in range(n_pairs)}
        run_interleaved(whole)
        carry_ref[...] = z0_ref[:, tt - 1:tt, :]
        run_interleaved([task for ch in range(n_ch) for task in prep(ch)])
        run_interleaved(stage_tasks(free_stages, cur, groups[0]))
        for g, chs in enumerate(groups):
            run_interleaved([task for ch in chs for task in stage_tasks(state_stages, cur, [ch])],
                            stage_tasks(free_stages, cur, groups[g + 1]) if g + 1 < len(groups) else [])
    for (bb, pr), val in state.items():
        state_ref[bb, pr] = val

    @pl.when(t == pl.num_programs(1) - 1)
    def _():
        for bb in range(bb_n):
            for h in range(R_HEADS):
                sfin_ref[bb, h] = state_ref[bb, h // 2, :, (h % 2) * HEAD_DIM:(h % 2 + 1) * HEAD_DIM]


def _wkv(zs, gr, prev, s0, params, *, bb_n, c, n_ch, n_grp=1):
    b, t, _ = zs.shape
    tt = n_ch * c
    n_t = t // tt
    pipelined = n_ch == 2 * n_grp and n_t > 1
    full = lambda arr: pl.BlockSpec(arr.shape, lambda i, j: (0,) * arr.ndim)
    first = lambda width: pl.BlockSpec((bb_n, tt, width), lambda i, j: (i, 0 if pipelined else j, 0))
    ahead = lambda width: pl.BlockSpec((bb_n, tt, width), lambda i, j: (i, jnp.minimum(j + 1, n_t - 1), 0))
    unit_scratch = []
    if pipelined:
        n_g = n_grp * bb_n * (R_HEADS // 2)
        shapes = dict(ar=((2 * c, PAIR), BF16), bkp=((2 * c, PAIR), BF16), v16=((c, PAIR), BF16),
                      pc=((1, PAIR), F32), gate=((c, PAIR), F32), bonus=((c, PAIR), F32),
                      a4=((2 * c, 4 * c), BF16), tinv=((c, 2 * c), BF16), akv=((c, PAIR), F32),
                      bk_bd=((4 * c, PAIR), BF16))
        for f in UNIT_COMMON + UNIT_FREE_OUT + UNIT_FREE_IN:
            shape, dtype = shapes[f]
            unit_scratch.append(pltpu.VMEM(((2 * n_g if f in UNIT_COMMON else n_g),) + shape, dtype))
    return pl.pallas_call(
        functools.partial(_wkv_kernel, bb_n=bb_n, c=c, n_ch=n_ch, n_grp=n_grp, pipelined=pipelined),
        name=f"wkv_c{c}",
        grid=(b // bb_n, n_t),
        in_specs=[first(SHIFT_WIDTH), first(R_WIDTH), ahead(SHIFT_WIDTH), ahead(R_WIDTH),
                  pl.BlockSpec((bb_n, 1, SHIFT_WIDTH), lambda i, j: (i, 0, 0)),
                  pl.BlockSpec((bb_n, R_HEADS, HEAD_DIM, HEAD_DIM), lambda i, j: (i, 0, 0, 0))]
                 + [full(p) for p in params],
        out_specs=[pl.BlockSpec((bb_n, tt, R_WIDTH), lambda i, j: (i, j, 0)),
                   pl.BlockSpec((bb_n, R_HEADS, HEAD_DIM, HEAD_DIM), lambda i, j: (i, 0, 0, 0))],
        out_shape=[jax.ShapeDtypeStruct((b, t, R_WIDTH), F32),
                   jax.ShapeDtypeStruct((b, R_HEADS, HEAD_DIM, HEAD_DIM), F32)],
        scratch_shapes=[pltpu.VMEM((bb_n, 1, SHIFT_WIDTH), F32),
                        pltpu.VMEM((bb_n, R_HEADS // 2, HEAD_DIM, PAIR), F32)] + unit_scratch,
        compiler_params=pltpu.CompilerParams(
            dimension_semantics=("arbitrary", "arbitrary"), vmem_limit_bytes=VMEM_LIMIT),
    )(zs, gr, zs, gr, prev, s0, *params)


def _fill_rel_bias(bias_ref, gtab_ref, offset):
    heads, nq, nk = bias_ref.shape
    for h in range(heads):
        g = jnp.broadcast_to(gtab_ref[h:h + 1, :], (nq, BIAS_L))
        bias_ref[h] = pltpu.roll(g, BIAS_L - offset, 1, stride=1, stride_axis=0)[:, :nk]


def _stack_heads(q2, left):
    zero = jnp.zeros((), q2.dtype)
    return jnp.concatenate([jnp.where(left, q2, zero), jnp.where(left, zero, q2)], axis=0)


def _attend_stages(jobs, left, write):
    def scores():
        for jb in jobs:
            s = _dot(jb["lhs"], jb["k"], NT)
            jb["s"] = s if jb.get("bias") is None else s + jb["bias"]

    def weights():
        for jb in jobs:
            m = jnp.max(jb["s"], axis=-1, keepdims=True)
            p = jnp.exp2(jb["s"] - m)
            jb["l"] = jnp.sum(p, axis=-1, keepdims=True)
            jb["p"] = p.astype(BF16)

    def values():
        for jb in jobs:
            o2 = _dot(jb["p"], jb["v"]) / jb["l"]
            n = o2.shape[0] // 2
            write(jb, jnp.where(left, o2[:n], o2[n:]))

    return [scores, weights, values]


def _run_staggered(groups):
    depth = max(len(g) for g in groups)
    for step in range(len(groups) + depth - 1):
        for g in range(len(groups) - 1, -1, -1):
            if 0 <= step - g < len(groups[g]):
                groups[g][step - g]()


def _layer_norm(h, g, b):
    mean = jnp.mean(h, axis=-1, keepdims=True)
    hc = h - mean
    var = jnp.mean(hc * hc, axis=-1, keepdims=True)
    return hc * lax.rsqrt(var + LN_EPS) * g + b


def _attn_prompt_kernel(q_ref, kc_ref, vc_ref, gb_ref, mq_ref, gm_ref, mk_ref, mv_ref, gtab_ref,
                        x_ref, mr_ref, w_ref, lng_ref, lnb_ref, y_ref,
                        kcat_ref, vcat_ref, bias_ref, biasv_ref, mix_ref, *, tq):
    j = pl.program_id(1)
    n_chunks = tq // CHUNK

    @pl.when((pl.program_id(0) == 0) & (j == 0))
    def _():
        _fill_rel_bias(bias_ref, gtab_ref, CHUNK - 1)
        kcol = lax.broadcasted_iota(jnp.int32, (1, BAND_LEN), 1)
        for h in range(B_HEADS):
            rows = slice((h % 2) * CHUNK, (h % 2 + 1) * CHUNK)
            scaled = bias_ref[h] * LOG2E
            biasv_ref[0, h // 2, rows, :] = scaled
            for i in range(n_chunks):
                biasv_ref[1 + i, h // 2, rows, :] = jnp.where(kcol >= BAND_WINDOW - i * CHUNK, scaled, -jnp.inf)

    @pl.when(j == 0)
    def _():
        kcat_ref[0:BAND_WINDOW] = jnp.zeros((BAND_WINDOW, B_WIDTH), BF16)
        vcat_ref[0:BAND_WINDOW] = jnp.zeros((BAND_WINDOW, B_WIDTH), BF16)

    @pl.when(j > 0)
    def _():
        kcat_ref[0:BAND_WINDOW] = kcat_ref[BAND_WINDOW:]
        vcat_ref[0:BAND_WINDOW] = vcat_ref[BAND_WINDOW:]

    kcat_ref[BAND_WINDOW:] = kc_ref[0].astype(BF16)
    vcat_ref[BAND_WINDOW:] = vc_ref[0].astype(BF16)
    q = (q_ref[0] * (ATT_SCALE * LOG2E)).astype(BF16)
    gate_b = _silu(gb_ref[0])
    left = lax.broadcasted_iota(jnp.int32, (1, PAIR), 1) < HEAD_DIM
    mq = (mq_ref[0] * (ATT_SCALE * LOG2E)).astype(BF16)
    gate_m = _silu(gm_ref[0])
    mk = mk_ref[0].astype(BF16)
    mv = mv_ref[0].astype(BF16)

    def write_band(jb, o):
        mix_ref[jb["rows"], jb["lanes"]] = (o * gate_b[jb["rows"], jb["lanes"]]).astype(BF16)

    def write_mem(jb, o):
        mix_ref[jb["rows"], B_WIDTH + jb["lanes"].start:B_WIDTH + jb["lanes"].stop] = (
            o * gate_m[jb["rows"], jb["lanes"]]).astype(BF16)

    jobs = []
    for i in range(n_chunks):
        rows = slice(i * CHUNK, (i + 1) * CHUNK)
        keys = slice(i * CHUNK, i * CHUNK + BAND_LEN)
        variant = jnp.where(j == 0, 1 + i, 0)
        for pr in range(B_HEADS // 2):
            lanes = slice(pr * PAIR, (pr + 1) * PAIR)
            job = dict(rows=rows, lanes=lanes, lhs=_stack_heads(q[rows, lanes], left),
                       k=kcat_ref[keys, lanes], v=vcat_ref[keys, lanes], bias=biasv_ref[variant, pr])
            jobs.append(_attend_stages([job], left, write_band))
    for r in range(0, tq, MEM_ROWS):
        rows = slice(r, r + MEM_ROWS)
        for pr in range(M_HEADS // 2):
            lanes = slice(pr * PAIR, (pr + 1) * PAIR)
            job = dict(rows=rows, lanes=lanes, lhs=_stack_heads(mq[rows, lanes], left),
                       k=mk[:, lanes], v=mv[:, lanes])
            jobs.append(_attend_stages([job], left, write_mem))
    _run_staggered(jobs)

    o = _dot(mr_ref[0].astype(BF16), w_ref[0:R_WIDTH, :]) + _dot(mix_ref[...], w_ref[R_WIDTH:, :])
    y_ref[0] = _layer_norm(ALPHA * x_ref[0] + o, lng_ref[...], lnb_ref[...])


def _attn_prompt(att, memkv, gtab, x, mix_r, w_out_bf16, ln_g, ln_b, *, tq):
    b, t, _ = att.shape
    col = lambda cidx: pl.BlockSpec((1, tq, B_WIDTH), lambda i, j: (i, j, cidx))
    tile = lambda width: pl.BlockSpec((1, tq, width), lambda i, j: (i, j, 0))
    full = lambda arr: pl.BlockSpec(arr.shape, lambda i, j: (0, 0))
    return pl.pallas_call(
        functools.partial(_attn_prompt_kernel, tq=tq),
        name="attn_prompt",
        grid=(b, t // tq),
        in_specs=[col(0), col(1), col(2), col(3), col(4), col(5),
                  pl.BlockSpec((1, N_MEM, B_WIDTH), lambda i, j: (i, 0, 0)),
                  pl.BlockSpec((1, N_MEM, B_WIDTH), lambda i, j: (i, 0, 1)),
                  full(gtab), tile(D_MODEL), tile(R_WIDTH), full(w_out_bf16), full(ln_g), full(ln_b)],
        out_specs=tile(D_MODEL),
        out_shape=jax.ShapeDtypeStruct((b, t, D_MODEL), F32),
        scratch_shapes=[pltpu.VMEM((2 * BAND_WINDOW, B_WIDTH), BF16),
                        pltpu.VMEM((2 * BAND_WINDOW, B_WIDTH), BF16),
                        pltpu.VMEM((B_HEADS, CHUNK, BAND_LEN), F32),
                        pltpu.VMEM((1 + tq // CHUNK, B_HEADS // 2, 2 * CHUNK, BAND_LEN), F32),
                        pltpu.VMEM((tq, 2 * B_WIDTH), BF16)],
        compiler_params=pltpu.CompilerParams(
            dimension_semantics=("arbitrary", "arbitrary"), vmem_limit_bytes=VMEM_LIMIT),
    )(att, att, att, att, att, att, memkv, memkv, gtab, x, mix_r, w_out_bf16, ln_g, ln_b)


def _attn_sample_kernel(att_ref, ck_ref, cv_ref, mk_ref, mv_ref, gtab_ref, out_ref, bias_ref, bias2_ref):
    n_seq, n_new, _ = att_ref.shape

    @pl.when(pl.program_id(0) == 0)
    def _():
        _fill_rel_bias(bias_ref, gtab_ref, n_new - 1)
        for h in range(B_HEADS):
            bias2_ref[h // 2, (h % 2) * n_new:(h % 2 + 1) * n_new, :] = bias_ref[h] * LOG2E

    left = lax.broadcasted_iota(jnp.int32, (1, PAIR), 1) < HEAD_DIM
    jobs = []
    for b in range(n_seq):
        att = att_ref[b]
        q = (att[:, 0:B_WIDTH] * (ATT_SCALE * LOG2E)).astype(BF16)
        k_all = jnp.concatenate([ck_ref[b].astype(BF16), att[:, B_WIDTH:2 * B_WIDTH].astype(BF16)], axis=0)
        v_all = jnp.concatenate([cv_ref[b].astype(BF16), att[:, 2 * B_WIDTH:3 * B_WIDTH].astype(BF16)], axis=0)
        gate_b = _silu(att[:, 3 * B_WIDTH:4 * B_WIDTH])
        mq = (att[:, 4 * B_WIDTH:5 * B_WIDTH] * (ATT_SCALE * LOG2E)).astype(BF16)
        gate_m = _silu(att[:, 5 * B_WIDTH:6 * B_WIDTH])
        mk = mk_ref[b].astype(BF16)
        mv = mv_ref[b].astype(BF16)
        for pr in range(B_HEADS // 2):
            lanes = slice(pr * PAIR, (pr + 1) * PAIR)
            jobs.append(dict(b=b, out=lanes, gate=gate_b[:, lanes], lhs=_stack_heads(q[:, lanes], left),
                             k=k_all[:, lanes], v=v_all[:, lanes], bias=bias2_ref[pr]))
        for pr in range(M_HEADS // 2):
            lanes = slice(pr * PAIR, (pr + 1) * PAIR)
            jobs.append(dict(b=b, out=slice(B_WIDTH + lanes.start, B_WIDTH + lanes.stop), gate=gate_m[:, lanes],
                             lhs=_stack_heads(mq[:, lanes], left), k=mk[:, lanes], v=mv[:, lanes]))
    def write(jb, o):
        out_ref[jb["b"], :, jb["out"]] = o * jb["gate"]

    _run_staggered([_attend_stages(jobs, left, write)])


def _attn_sample(att, cache_k, cache_v, mem_k, mem_v, gtab, *, n_seq):
    b, s, _ = att.shape
    per_b = lambda arr: pl.BlockSpec((n_seq,) + arr.shape[1:], lambda i: (i, 0, 0))
    n_keys = cache_k.shape[1] + s
    return pl.pallas_call(
        _attn_sample_kernel,
        name="attn_sample",
        grid=(b // n_seq,),
        in_specs=[per_b(att), per_b(cache_k), per_b(cache_v), per_b(mem_k), per_b(mem_v),
                  pl.BlockSpec(gtab.shape, lambda i: (0, 0))],
        out_specs=pl.BlockSpec((n_seq, s, 2 * B_WIDTH), lambda i: (i, 0, 0)),
        out_shape=jax.ShapeDtypeStruct((b, s, 2 * B_WIDTH), F32),
        scratch_shapes=[pltpu.VMEM((B_HEADS, s, n_keys), F32),
                        pltpu.VMEM((B_HEADS // 2, 2 * s, n_keys), F32)],
        compiler_params=pltpu.CompilerParams(dimension_semantics=("arbitrary",)),
    )(att, cache_k, cache_v, mem_k, mem_v, gtab)


def _finish_kernel(x_ref, mr_ref, mbm_ref, w_ref, g_ref, b_ref, y_ref):
    o = _dot(mr_ref[...].astype(BF16), w_ref[0:R_WIDTH, :])
    o = o + _dot(mbm_ref[...].astype(BF16), w_ref[R_WIDTH:, :])
    y_ref[...] = _layer_norm(ALPHA * x_ref[...] + o, g_ref[...], b_ref[...])


def _finish(x, mix_r, mix_bm, w_out_bf16, ln_g, ln_b, tm):
    m = x.shape[0]
    rows = lambda width: pl.BlockSpec((tm, width), lambda i: (i, 0))
    full = lambda arr: pl.BlockSpec(arr.shape, lambda i: (0, 0))
    return pl.pallas_call(
        _finish_kernel,
        name=f"finish_{m}",
        grid=(m // tm,),
        in_specs=[rows(D_MODEL), rows(R_WIDTH), rows(2 * B_WIDTH), full(w_out_bf16), full(ln_g), full(ln_b)],
        out_specs=rows(D_MODEL),
        out_shape=jax.ShapeDtypeStruct((m, D_MODEL), F32),
        compiler_params=pltpu.CompilerParams(
            dimension_semantics=("arbitrary",), vmem_limit_bytes=VMEM_LIMIT),
    )(x, mix_r, mix_bm, w_out_bf16, ln_g, ln_b)


IN_SPLITS = ((0, SHIFT_WIDTH), (SHIFT_WIDTH, SHIFT_WIDTH + R_WIDTH), (SHIFT_WIDTH + R_WIDTH, IN_WIDTH))


def _rel_bias_row(table, rel0):
    n_hi = rel0 - REL_CLIP
    n_lo = BIAS_L - n_hi - (2 * REL_CLIP + 1)
    heads = table.shape[0]
    return jnp.concatenate([jnp.broadcast_to(table[:, 2 * REL_CLIP:], (heads, n_hi)), table[:, ::-1],
                            jnp.broadcast_to(table[:, 0:1], (heads, n_lo))], axis=1)


def kernel(x_prompt, x_sample, mem_prompt, state_shift, state_wkv, cache_band_k, cache_band_v,
           cache_mem_k, cache_mem_v, w_in, mu_shift, w0, w2, a0, a2, k_k, k_a, r_k, gn_g, gn_b,
           rel_bias, w_mem_kv, w_out, ln_g, ln_b):
    bp, t, _ = x_prompt.shape
    bs, s, _ = x_sample.shape
    depth = w_in.shape[0]
    assert depth == 1 and t % BAND_WINDOW == 0 and t % (WKV_CHUNKS_PER_STEP * CHUNK) == 0
    assert s <= CHUNK and s & (s - 1) == 0 and bp % WKV_PROMPT_SEQS == 0
    assert bs % WKV_SAMPLE_SEQS == 0 and bs % ATTN_SAMPLE_SEQS == 0 and (bp * t) % PROJ_ROWS == 0
    keep = min(BAND_WINDOW, t)
    l = 0

    w_in_b = w_in[l].astype(BF16)
    w_out_b = w_out[l].astype(BF16)
    w_mem_b = w_mem_kv[l].astype(BF16)
    row = lambda p: p.reshape(1, -1)
    wkv_params = (row(mu_shift[l]), row(w0[l]), w2[l], row(a0[l]), a2[l], row(k_k[l]), row(k_a[l]),
                  row(r_k[l]), row(gn_g[l]), row(gn_b[l]))
    table = rel_bias[l]
    r_rows = cache_band_k.shape[2]
    gtab_p = _rel_bias_row(table, BAND_WINDOW + CHUNK - 1)
    gtab_s = _rel_bias_row(table, r_rows + s - 1)

    xp = x_prompt.reshape(bp * t, D_MODEL)
    zs, gr, att = _proj(xp, w_in_b, IN_SPLITS, PROJ_ROWS)
    zs = zs.reshape(bp, t, SHIFT_WIDTH)
    att = att.reshape(bp, t, ATT_WIDTH)
    memkv, = _proj(mem_prompt.reshape(bp * N_MEM, D_MODEL), w_mem_b, ((0, 2 * B_WIDTH),), PROJ_ROWS)
    memkv = memkv.reshape(bp, N_MEM, 2 * B_WIDTH)
    mix_r, p_wkv = _wkv(zs, gr.reshape(bp, t, R_WIDTH), jnp.zeros((bp, 1, SHIFT_WIDTH), F32),
                        jnp.zeros((bp, R_HEADS, HEAD_DIM, HEAD_DIM), F32), wkv_params,
                        bb_n=WKV_PROMPT_SEQS, c=CHUNK, n_ch=WKV_CHUNKS_PER_STEP, n_grp=WKV_GROUP_CHUNKS)
    y_prompt = _attn_prompt(att, memkv, gtab_p, x_prompt, mix_r, w_out_b, row(ln_g[l]), row(ln_b[l]),
                            tq=BAND_WINDOW)
    p_shift = zs[:, -1]
    p_bk = att[:, t - keep:, B_WIDTH:2 * B_WIDTH].reshape(bp, keep, B_HEADS, HEAD_DIM)
    p_bv = att[:, t - keep:, 2 * B_WIDTH:3 * B_WIDTH].reshape(bp, keep, B_HEADS, HEAD_DIM)
    p_mk = memkv[:, :, :B_WIDTH].reshape(bp, N_MEM, M_HEADS, HEAD_DIM)
    p_mv = memkv[:, :, B_WIDTH:].reshape(bp, N_MEM, M_HEADS, HEAD_DIM)

    xs = x_sample.reshape(bs * s, D_MODEL)
    zs_s, gr_s, att_s = _proj(xs, w_in_b, IN_SPLITS, bs * s)
    zs_s = zs_s.reshape(bs, s, SHIFT_WIDTH)
    att_s = att_s.reshape(bs, s, ATT_WIDTH)
    mix_r_s, s_wkv = _wkv(zs_s, gr_s.reshape(bs, s, R_WIDTH), state_shift[l][:, None, :], state_wkv[l],
                          wkv_params, bb_n=WKV_SAMPLE_SEQS, c=s, n_ch=1)
    mix_bm_s = _attn_sample(att_s,
                            cache_band_k[l].reshape(bs, r_rows, B_WIDTH),
                            cache_band_v[l].reshape(bs, r_rows, B_WIDTH),
                            cache_mem_k[l].reshape(bs, N_MEM, B_WIDTH),
                            cache_mem_v[l].reshape(bs, N_MEM, B_WIDTH), gtab_s, n_seq=ATTN_SAMPLE_SEQS)
    y_sample = _finish(xs, mix_r_s.reshape(bs * s, R_WIDTH), mix_bm_s.reshape(bs * s, 2 * B_WIDTH),
                       w_out_b, row(ln_g[l]), row(ln_b[l]), bs * s).reshape(bs, s, D_MODEL)
    s_shift = zs_s[:, -1]
    s_bk = att_s[:, :, B_WIDTH:2 * B_WIDTH].reshape(bs, s, B_HEADS, HEAD_DIM)
    s_bv = att_s[:, :, 2 * B_WIDTH:3 * B_WIDTH].reshape(bs, s, B_HEADS, HEAD_DIM)

    st = lambda a: a[None]
    return (y_prompt, y_sample, st(p_shift), st(p_wkv), st(p_bk), st(p_bv), st(p_mk), st(p_mv),
            st(s_shift), st(s_wkv), st(s_bk), st(s_bv))
```

```python
import functools

import numpy as np
import jax
import jax.numpy as jnp
from jax import lax
from jax.experimental import pallas as pl
from jax.experimental.pallas import tpu as pltpu

F32 = jnp.float32
BF16 = jnp.bfloat16

D_MODEL = 1024
HEAD_DIM = 64
R_WIDTH = 512
R_HEADS = 8
LOW_RANK = 64
SHIFT_WIDTH = 3 * R_WIDTH + 2 * LOW_RANK
B_WIDTH = 256
B_HEADS = 4
M_HEADS = 4
N_MEM = 256
CHUNK = 64
BAND_CHUNKS = 8
BAND_WINDOW = BAND_CHUNKS * CHUNK
BAND_LEN = BAND_WINDOW + CHUNK
REL_CLIP = 128
ATT_WIDTH = 6 * B_WIDTH
IN_WIDTH = SHIFT_WIDTH + R_WIDTH + ATT_WIDTH
LN_EPS = 1e-5
GN_EPS = 64e-5
ALPHA = 2.0 ** 0.25
ATT_SCALE = HEAD_DIM ** -0.5
LOG2E = float(np.log2(np.e))
BIAS_L = 1024
PAIR = 2 * HEAD_DIM
MEM_ROWS = 128
WKV_CHUNKS_PER_STEP = 4
WKV_GROUP_CHUNKS = 2
STAGE_SPLIT = 16
PROJ_ROWS = 512
WKV_PROMPT_SEQS = 2
WKV_SAMPLE_SEQS = 16
ATTN_SAMPLE_SEQS = 8

VMEM_LIMIT = 48 * 1024 * 1024

NN = ((1,), (0,))
NT = ((1,), (1,))
TN = ((0,), (0,))


def _dot(a, b, dims=NN):
    return lax.dot_general(a, b, (dims, ((), ())), preferred_element_type=F32)


def _split2(x):
    hi = x.astype(BF16)
    lo = (x - hi.astype(F32)).astype(BF16)
    return hi, lo


def _mm1(a, b):
    return _dot(a.astype(BF16), b.astype(BF16))


def _mm_exact_lhs(lhs_bf16, x):
    hi, lo = _split2(x)
    return _dot(lhs_bf16, hi) + _dot(lhs_bf16, lo)


def _sigmoid(x):
    return 1.0 / (1.0 + jnp.exp(-x))


def _silu(x):
    return x * _sigmoid(x)


def _proj_kernel(x_ref, w_ref, *out_refs, splits):
    x = x_ref[...].astype(BF16)
    for o_ref, (lo, hi) in zip(out_refs, splits):
        o_ref[...] = _dot(x, w_ref[:, lo:hi])


def _proj(x, w_bf16, splits, tm):
    m, k = x.shape
    n = w_bf16.shape[1]
    return pl.pallas_call(
        functools.partial(_proj_kernel, splits=splits),
        name=f"proj_{m}x{n}",
        grid=(m // tm,),
        in_specs=[pl.BlockSpec((tm, k), lambda i: (i, 0)),
                  pl.BlockSpec((k, n), lambda i: (0, 0))],
        out_specs=[pl.BlockSpec((tm, hi - lo), lambda i: (i, 0)) for lo, hi in splits],
        out_shape=[jax.ShapeDtypeStruct((m, hi - lo), F32) for lo, hi in splits],
        compiler_params=pltpu.CompilerParams(
            dimension_semantics=("arbitrary",), vmem_limit_bytes=VMEM_LIMIT),
    )(x, w_bf16)


def _pair_blocks(x2, left):
    zero = jnp.zeros((), x2.dtype)
    return jnp.concatenate([jnp.where(left, x2, zero), jnp.where(left, zero, x2)], axis=0)


def _pair_sum(x2, left):
    s0 = jnp.sum(jnp.where(left, x2, 0.0), axis=-1, keepdims=True)
    s1 = jnp.sum(jnp.where(left, 0.0, x2), axis=-1, keepdims=True)
    return jnp.where(left, s0, s1)


UNIT_COMMON = ("ar", "bkp", "v16", "pc", "gate", "bonus")
UNIT_FREE_OUT = ("a4", "tinv", "akv")
UNIT_FREE_IN = ("bk_bd",)


def _wkv_kernel(z0_ref, g0_ref, zn_ref, gn_ref, prev_ref, s0_ref, mu_ref, w0_ref, w2_ref, a0_ref, a2_ref,
                kk_ref, ka_ref, rk_ref, gng_ref, gnb_ref, out_ref, sfin_ref, carry_ref, state_ref, *unit_refs,
                bb_n, c, n_ch, n_grp, pipelined):
    t = pl.program_id(1)
    tt = n_ch * c
    n_pairs = R_HEADS // 2
    per_chunk = bb_n * n_pairs

    row2 = lax.broadcasted_iota(jnp.int32, (c, 2 * c), 0)
    col2 = lax.broadcasted_iota(jnp.int32, (c, 2 * c), 1) & (c - 1)
    eye2 = (row2 == col2).astype(BF16)
    strict2 = col2 < row2
    incl4 = ((lax.broadcasted_iota(jnp.int32, (c, 4 * c), 1) & (c - 1))
             <= lax.broadcasted_iota(jnp.int32, (c, 4 * c), 0))
    level_masks = []
    shift = 0
    while (1 << shift) < c:
        rb = row2 >> shift
        level_masks.append(((rb & 1) == 1) & ((col2 >> shift) == rb - 1))
        shift += 1
    left_s = lax.broadcasted_iota(jnp.int32, (1, 2 * c), 1) < c
    left = lax.broadcasted_iota(jnp.int32, (1, PAIR), 1) < HEAD_DIM
    zero = jnp.zeros((), BF16)
    row_t = lax.broadcasted_iota(jnp.int32, (tt, tt), 0)
    col_t = lax.broadcasted_iota(jnp.int32, (tt, tt), 1)
    shift_c = c.bit_length() - 1
    ltri = ((col_t <= row_t) & ((col_t >> shift_c) == (row_t >> shift_c))).astype(BF16)
    first_row = lax.broadcasted_iota(jnp.int32, (tt, 1), 0) == 0
    mu = mu_ref[...]

    def field(un, f):
        if f not in un:
            un[f] = _pair_blocks(field(un, "v16"), left) if f == "v_bd" else un["stored"][f]()
        return un[f]

    def make_tile(zs_ref, gr_ref, prev_rows):
        tile = [dict() for _ in range(bb_n)]
        chunks = [[] for _ in range(n_ch)]

        def whole_tile(bb):
            zs = zs_ref[bb]
            zprev = jnp.where(first_row, prev_rows[bb], pltpu.roll(zs, 1, 0))
            xs = zs + (zprev - zs) * mu
            wd = xs[:, 3 * R_WIDTH:3 * R_WIDTH + LOW_RANK]
            ad = xs[:, 3 * R_WIDTH + LOW_RANK:]
            u = -(w0_ref[...] + _mm1(jnp.tanh(wd), w2_ref[...]))
            softplus = jnp.maximum(u, 0.0) + jnp.log(1.0 + jnp.exp(-jnp.abs(u)))
            ld = -jnp.exp(-softplus - 0.5)
            tile[bb].update(xs=xs, ld=ld, cum=_mm_exact_lhs(ltri, ld),
                            a_pre=a0_ref[...] + _mm1(ad, a2_ref[...]))

        def prep_tasks(ch):
            rows = slice(ch * c, (ch + 1) * c)
            tasks = []
            for bb in range(bb_n):
                wide = {}

                def full_width(bb=bb, wide=wide):
                    xs = tile[bb]["xs"][rows]
                    r = xs[:, 0:R_WIDTH]
                    k = xs[:, R_WIDTH:2 * R_WIDTH]
                    ld = tile[bb]["ld"][rows]
                    cum = tile[bb]["cum"][rows]
                    a = _sigmoid(tile[bb]["a_pre"][rows])
                    ecum = jnp.exp(cum)
                    k2 = k * (1.0 + (a - 1.0) * ka_ref[...])
                    wide.update(v=xs[:, 2 * R_WIDTH:3 * R_WIDTH], a=a, ecum=ecum, einv=jnp.exp(-cum),
                                eprev=jnp.exp(cum - ld), kkr=k * kk_ref[...], rt=r * ecum, k2=k2,
                                rk2=r * k2 * rk_ref[...], gate=_silu(gr_ref[bb, rows, :]))

                def pair(pr, bb=bb, wide=wide):
                    sl = slice(pr * PAIR, (pr + 1) * PAIR)
                    kkr_p = wide["kkr"][:, sl]
                    nrm = jnp.sqrt(_pair_sum(kkr_p * kkr_p, left))
                    kkn = kkr_p / jnp.maximum(nrm, 1e-12)
                    at = -kkn * wide["eprev"][:, sl]
                    bh = kkn * wide["a"][:, sl] * wide["einv"][:, sl]
                    kh = wide["k2"][:, sl] * wide["einv"][:, sl]
                    pc = wide["ecum"][c - 1:c, sl]
                    v_p = wide["v"][:, sl]
                    v16 = v_p.astype(BF16)
                    chunks[ch].append(dict(
                        bb=bb, pr=pr, sl=sl, rows=rows, pc=pc, gate=wide["gate"][:, sl],
                        bonus=_pair_sum(wide["rk2"][:, sl], left) * v_p, v_bd=_pair_blocks(v16, left), v16=v16,
                        ar=jnp.concatenate([at, wide["rt"][:, sl]], axis=0).astype(BF16),
                        bk_bd=jnp.concatenate([_pair_blocks(bh.astype(BF16), left),
                                               _pair_blocks(kh.astype(BF16), left)], axis=0),
                        bkp=jnp.concatenate([bh * pc, kh * pc], axis=0).astype(BF16)))

                tasks.append(full_width)
                tasks += [functools.partial(pair, pr) for pr in range(n_pairs)]
            return tasks

        return chunks, [functools.partial(whole_tile, bb) for bb in range(bb_n)], prep_tasks

    def st_a4(units):
        for un in units:
            un["a4"] = _dot(field(un, "ar"), field(un, "bk_bd"), NT).astype(BF16)
            un["tinv"] = jnp.where(level_masks[0], un["a4"][:c, :2 * c], eye2)

    def st_lt(mask):
        def run(units):
            for un in units:
                a_ab = un["a4"][:c, :2 * c]
                un["lt"] = _dot(jnp.where(mask, a_ab, zero), _pair_blocks(un["tinv"], left_s)).astype(BF16)
        return run

    def st_tinv(mask):
        def run(units):
            for un in units:
                new = _dot(un["tinv"], _pair_blocks(un["lt"], left_s)).astype(BF16)
                un["tinv"] = jnp.where(mask, new, un["tinv"])
        return run

    def st_akv(units):
        for un in units:
            un["akv"] = _dot(jnp.where(strict2, un["a4"][:c, 2 * c:], zero), field(un, "v_bd"))

    def st_ars(units):
        for un in units:
            un["s0"] = state[un["bb"], un["pr"]]
            un["ars"] = _dot(field(un, "ar"), _pair_blocks(un["s0"].astype(BF16), left), NT)

    def st_pm(units):
        for un in units:
            rhs = (un["ars"][:c] + field(un, "akv")).astype(BF16)
            un["pm"] = _dot(field(un, "tinv"), _pair_blocks(rhs, left)).astype(BF16)

    def st_state(units):
        for un in units:
            pv = jnp.concatenate([un["pm"], field(un, "v16")], axis=0)
            cross = _dot(pv, field(un, "bkp"), TN)
            state[un["bb"], un["pr"]] = (un["s0"] * field(un, "pc")
                                         + jnp.where(left, cross[:HEAD_DIM], cross[HEAD_DIM:]))

    def st_y(units):
        for un in units:
            pv_bd = jnp.concatenate([_pair_blocks(un["pm"], left), field(un, "v_bd")], axis=0)
            y = un["ars"][c:] + _dot(jnp.where(incl4, field(un, "a4")[c:], zero), pv_bd)
            mean = _pair_sum(y, left) * (1.0 / HEAD_DIM)
            yc = y - mean
            var = _pair_sum(yc * yc, left) * (1.0 / HEAD_DIM)
            yn = yc * lax.rsqrt(var + GN_EPS) * gng_ref[:, un["sl"]] + gnb_ref[:, un["sl"]]
            out_ref[un["bb"], un["rows"], un["sl"]] = (yn + field(un, "bonus")) * field(un, "gate")

    free_stages = [st_a4]
    for mask in level_masks[1:]:
        free_stages += [st_lt(mask), st_tinv(mask)]
    free_stages.append(st_akv)
    state_stages = [st_ars, st_pm, st_state, st_y]

    def stage_tasks(stages, chunks, chs):
        def run(stage, lo):
            units = [un for ch in chs for un in chunks[ch]]
            stage(units[lo:lo + STAGE_SPLIT])

        return [functools.partial(run, stage, lo) for stage in stages
                for lo in range(0, len(chs) * per_chunk, STAGE_SPLIT)]

    def run_interleaved(*task_lists):
        keyed = [((i + 0.5) / len(tasks), n, i, task)
                 for n, tasks in enumerate(task_lists) for i, task in enumerate(tasks)]
        for _, _, _, task in sorted(keyed, key=lambda e: e[:3]):
            task()

    groups = [list(range(g0, min(g0 + n_grp, n_ch))) for g0 in range(0, n_ch, n_grp)]

    if pipelined:
        refs = dict(zip(UNIT_COMMON + UNIT_FREE_OUT + UNIT_FREE_IN, unit_refs))
        n_g = n_grp * per_chunk

        def store_units(chunks, chs, fields, base):
            units = [un for ch in chs for un in chunks[ch]]
            for idx, un in enumerate(units):
                for f in fields:
                    refs[f][(base if f in UNIT_COMMON else 0) + idx] = un[f]

        def load_units(chs, fields, base):
            chunks = {}
            idx = 0
            for ch in chs:
                chunks[ch] = []
                for bb in range(bb_n):
                    for pr in range(n_pairs):
                        un = dict(bb=bb, pr=pr, sl=slice(pr * PAIR, (pr + 1) * PAIR),
                                  rows=slice(ch * c, (ch + 1) * c))
                        un["stored"] = {f: functools.partial(lambda f, i: refs[f][i], f,
                                                             (base if f in UNIT_COMMON else 0) + idx)
                                        for f in fields}
                        chunks[ch].append(un)
                        idx += 1
            return chunks

        def hand_over(chunks):
            store_units(chunks, groups[0], UNIT_COMMON + UNIT_FREE_OUT, 0)
            store_units(chunks, groups[1], UNIT_COMMON + UNIT_FREE_IN, n_g)

        @pl.when(t == 0)
        def _():
            for bb in range(bb_n):
                for h in range(R_HEADS):
                    state_ref[bb, h // 2, :, (h % 2) * HEAD_DIM:(h % 2 + 1) * HEAD_DIM] = s0_ref[bb, h]
            first, whole, prep = make_tile(z0_ref, g0_ref, [prev_ref[bb] for bb in range(bb_n)])
            run_interleaved(whole)
            run_interleaved([task for ch in range(n_ch) for task in prep(ch)])
            run_interleaved(stage_tasks(free_stages, first, groups[0]))
            hand_over(first)
            carry_ref[...] = z0_ref[:, tt - 1:tt, :]

        cur = load_units(groups[0], UNIT_COMMON + UNIT_FREE_OUT, 0)
        cur.update(load_units(groups[1], UNIT_COMMON + UNIT_FREE_IN, n_g))
        nxt, whole, prep = make_tile(zn_ref, gn_ref, [carry_ref[bb] for bb in range(bb_n)])
        state = {(bb, pr): state_ref[bb, pr] for bb in range(bb_n) for pr in range(n_pairs)}
        run_interleaved([task for ch in groups[0] for task in stage_tasks(state_stages, cur, [ch])],
                        stage_tasks(free_stages, cur, groups[1]),
                        whole + [task for ch in groups[0] for task in prep(ch)])
        run_interleaved([task for ch in groups[1] for task in stage_tasks(state_stages, cur, [ch])],
                        stage_tasks(free_stages, nxt, groups[0]),
                        [task for ch in groups[1] for task in prep(ch)])
        hand_over(nxt)
        carry_ref[...] = zn_ref[:, tt - 1:tt, :]
    else:
        @pl.when(t == 0)
        def _():
            carry_ref[...] = prev_ref[...]
            for bb in range(bb_n):
                for h in range(R_HEADS):
                    state_ref[bb, h // 2, :, (h % 2) * HEAD_DIM:(h % 2 + 1) * HEAD_DIM] = s0_ref[bb, h]

        cur, whole, prep = make_tile(z0_ref, g0_ref, [carry_ref[bb] for bb in range(bb_n)])
        state = {(bb, pr): state_ref[bb, pr] for bb in range(bb_n) for pr in range(n_pairs)}
        run_interleaved(whole)
        carry_ref[...] = z0_ref[:, tt - 1:tt, :]
        run_interleaved([task for ch in range(n_ch) for task in prep(ch)])
        run_interleaved(stage_tasks(free_stages, cur, groups[0]))
        for g, chs in enumerate(groups):
            run_interleaved([task for ch in chs for task in stage_tasks(state_stages, cur, [ch])],
                            stage_tasks(free_stages, cur, groups[g + 1]) if g + 1 < len(groups) else [])
    for (bb, pr), val in state.items():
        state_ref[bb, pr] = val

    @pl.when(t == pl.num_programs(1) - 1)
    def _():
        for bb in range(bb_n):
            for h in range(R_HEADS):
                sfin_ref[bb, h] = state_ref[bb, h // 2, :, (h % 2) * HEAD_DIM:(h % 2 + 1) * HEAD_DIM]


def _wkv(zs, gr, prev, s0, params, *, bb_n, c, n_ch, n_grp=1):
    b, t, _ = zs.shape
    tt = n_ch * c
    n_t = t // tt
    pipelined = n_ch == 2 * n_grp and n_t > 1
    full = lambda arr: pl.BlockSpec(arr.shape, lambda i, j: (0,) * arr.ndim)
    first = lambda width: pl.BlockSpec((bb_n, tt, width), lambda i, j: (i, 0 if pipelined else j, 0))
    ahead = lambda width: pl.BlockSpec((bb_n, tt, width), lambda i, j: (i, jnp.minimum(j + 1, n_t - 1), 0))
    unit_scratch = []
    if pipelined:
        n_g = n_grp * bb_n * (R_HEADS // 2)
        shapes = dict(ar=((2 * c, PAIR), BF16), bkp=((2 * c, PAIR), BF16), v16=((c, PAIR), BF16),
                      pc=((1, PAIR), F32), gate=((c, PAIR), F32), bonus=((c, PAIR), F32),
                      a4=((2 * c, 4 * c), BF16), tinv=((c, 2 * c), BF16), akv=((c, PAIR), F32),
                      bk_bd=((4 * c, PAIR), BF16))
        for f in UNIT_COMMON + UNIT_FREE_OUT + UNIT_FREE_IN:
            shape, dtype = shapes[f]
            unit_scratch.append(pltpu.VMEM(((2 * n_g if f in UNIT_COMMON else n_g),) + shape, dtype))
    return pl.pallas_call(
        functools.partial(_wkv_kernel, bb_n=bb_n, c=c, n_ch=n_ch, n_grp=n_grp, pipelined=pipelined),
        name=f"wkv_c{c}",
        grid=(b // bb_n, n_t),
        in_specs=[first(SHIFT_WIDTH), first(R_WIDTH), ahead(SHIFT_WIDTH), ahead(R_WIDTH),
                  pl.BlockSpec((bb_n, 1, SHIFT_WIDTH), lambda i, j: (i, 0, 0)),
                  pl.BlockSpec((bb_n, R_HEADS, HEAD_DIM, HEAD_DIM), lambda i, j: (i, 0, 0, 0))]
                 + [full(p) for p in params],
        out_specs=[pl.BlockSpec((bb_n, tt, R_WIDTH), lambda i, j: (i, j, 0)),
                   pl.BlockSpec((bb_n, R_HEADS, HEAD_DIM, HEAD_DIM), lambda i, j: (i, 0, 0, 0))],
        out_shape=[jax.ShapeDtypeStruct((b, t, R_WIDTH), F32),
                   jax.ShapeDtypeStruct((b, R_HEADS, HEAD_DIM, HEAD_DIM), F32)],
        scratch_shapes=[pltpu.VMEM((bb_n, 1, SHIFT_WIDTH), F32),
                        pltpu.VMEM((bb_n, R_HEADS // 2, HEAD_DIM, PAIR), F32)] + unit_scratch,
        compiler_params=pltpu.CompilerParams(
            dimension_semantics=("arbitrary", "arbitrary"), vmem_limit_bytes=VMEM_LIMIT),
    )(zs, gr, zs, gr, prev, s0, *params)


def _fill_rel_bias(bias_ref, gtab_ref, offset):
    heads, nq, nk = bias_ref.shape
    for h in range(heads):
        g = jnp.broadcast_to(gtab_ref[h:h + 1, :], (nq, BIAS_L))
        bias_ref[h] = pltpu.roll(g, BIAS_L - offset, 1, stride=1, stride_axis=0)[:, :nk]


def _stack_heads(q2, left):
    zero = jnp.zeros((), q2.dtype)
    return jnp.concatenate([jnp.where(left, q2, zero), jnp.where(left, zero, q2)], axis=0)


def _attend_stages(jobs, left, write):
    def scores():
        for jb in jobs:
            s = _dot(jb["lhs"], jb["k"], NT)
            jb["s"] = s if jb.get("bias") is None else s + jb["bias"]

    def weights():
        for jb in jobs:
            m = jnp.max(jb["s"], axis=-1, keepdims=True)
            p = jnp.exp2(jb["s"] - m)
            jb["l"] = jnp.sum(p, axis=-1, keepdims=True)
            jb["p"] = p.astype(BF16)

    def values():
        for jb in jobs:
            o2 = _dot(jb["p"], jb["v"]) / jb["l"]
            n = o2.shape[0] // 2
            write(jb, jnp.where(left, o2[:n], o2[n:]))

    return [scores, weights, values]


def _run_staggered(groups):
    depth = max(len(g) for g in groups)
    for step in range(len(groups) + depth - 1):
        for g in range(len(groups) - 1, -1, -1):
            if 0 <= step - g < len(groups[g]):
                groups[g][step - g]()


def _layer_norm(h, g, b):
    mean = jnp.mean(h, axis=-1, keepdims=True)
    hc = h - mean
    var = jnp.mean(hc * hc, axis=-1, keepdims=True)
    return hc * lax.rsqrt(var + LN_EPS) * g + b


def _attn_prompt_kernel(q_ref, kc_ref, vc_ref, gb_ref, mq_ref, gm_ref, mk_ref, mv_ref, gtab_ref,
                        x_ref, mr_ref, w_ref, lng_ref, lnb_ref, y_ref,
                        kcat_ref, vcat_ref, bias_ref, biasv_ref, mix_ref, *, tq):
    j = pl.program_id(1)
    n_chunks = tq // CHUNK

    @pl.when((pl.program_id(0) == 0) & (j == 0))
    def _():
        _fill_rel_bias(bias_ref, gtab_ref, CHUNK - 1)
        kcol = lax.broadcasted_iota(jnp.int32, (1, BAND_LEN), 1)
        for h in range(B_HEADS):
            rows = slice((h % 2) * CHUNK, (h % 2 + 1) * CHUNK)
            scaled = bias_ref[h] * LOG2E
            biasv_ref[0, h // 2, rows, :] = scaled
            for i in range(n_chunks):
                biasv_ref[1 + i, h // 2, rows, :] = jnp.where(kcol >= BAND_WINDOW - i * CHUNK, scaled, -jnp.inf)

    @pl.when(j == 0)
    def _():
        kcat_ref[0:BAND_WINDOW] = jnp.zeros((BAND_WINDOW, B_WIDTH), BF16)
        vcat_ref[0:BAND_WINDOW] = jnp.zeros((BAND_WINDOW, B_WIDTH), BF16)

    @pl.when(j > 0)
    def _():
        kcat_ref[0:BAND_WINDOW] = kcat_ref[BAND_WINDOW:]
        vcat_ref[0:BAND_WINDOW] = vcat_ref[BAND_WINDOW:]

    kcat_ref[BAND_WINDOW:] = kc_ref[0].astype(BF16)
    vcat_ref[BAND_WINDOW:] = vc_ref[0].astype(BF16)
    q = (q_ref[0] * (ATT_SCALE * LOG2E)).astype(BF16)
    gate_b = _silu(gb_ref[0])
    left = lax.broadcasted_iota(jnp.int32, (1, PAIR), 1) < HEAD_DIM
    mq = (mq_ref[0] * (ATT_SCALE * LOG2E)).astype(BF16)
    gate_m = _silu(gm_ref[0])
    mk = mk_ref[0].astype(BF16)
    mv = mv_ref[0].astype(BF16)

    def write_band(jb, o):
        mix_ref[jb["rows"], jb["lanes"]] = (o * gate_b[jb["rows"], jb["lanes"]]).astype(BF16)

    def write_mem(jb, o):
        mix_ref[jb["rows"], B_WIDTH + jb["lanes"].start:B_WIDTH + jb["lanes"].stop] = (
            o * gate_m[jb["rows"], jb["lanes"]]).astype(BF16)

    jobs = []
    for i in range(n_chunks):
        rows = slice(i * CHUNK, (i + 1) * CHUNK)
        keys = slice(i * CHUNK, i * CHUNK + BAND_LEN)
        variant = jnp.where(j == 0, 1 + i, 0)
        for pr in range(B_HEADS // 2):
            lanes = slice(pr * PAIR, (pr + 1) * PAIR)
            job = dict(rows=rows, lanes=lanes, lhs=_stack_heads(q[rows, lanes], left),
                       k=kcat_ref[keys, lanes], v=vcat_ref[keys, lanes], bias=biasv_ref[variant, pr])
            jobs.append(_attend_stages([job], left, write_band))
    for r in range(0, tq, MEM_ROWS):
        rows = slice(r, r + MEM_ROWS)
        for pr in range(M_HEADS // 2):
            lanes = slice(pr * PAIR, (pr + 1) * PAIR)
            job = dict(rows=rows, lanes=lanes, lhs=_stack_heads(mq[rows, lanes], left),
                       k=mk[:, lanes], v=mv[:, lanes])
            jobs.append(_attend_stages([job], left, write_mem))
    _run_staggered(jobs)

    o = _dot(mr_ref[0].astype(BF16), w_ref[0:R_WIDTH, :]) + _dot(mix_ref[...], w_ref[R_WIDTH:, :])
    y_ref[0] = _layer_norm(ALPHA * x_ref[0] + o, lng_ref[...], lnb_ref[...])


def _attn_prompt(att, memkv, gtab, x, mix_r, w_out_bf16, ln_g, ln_b, *, tq):
    b, t, _ = att.shape
    col = lambda cidx: pl.BlockSpec((1, tq, B_WIDTH), lambda i, j: (i, j, cidx))
    tile = lambda width: pl.BlockSpec((1, tq, width), lambda i, j: (i, j, 0))
    full = lambda arr: pl.BlockSpec(arr.shape, lambda i, j: (0, 0))
    return pl.pallas_call(
        functools.partial(_attn_prompt_kernel, tq=tq),
        name="attn_prompt",
        grid=(b, t // tq),
        in_specs=[col(0), col(1), col(2), col(3), col(4), col(5),
                  pl.BlockSpec((1, N_MEM, B_WIDTH), lambda i, j: (i, 0, 0)),
                  pl.BlockSpec((1, N_MEM, B_WIDTH), lambda i, j: (i, 0, 1)),
                  full(gtab), tile(D_MODEL), tile(R_WIDTH), full(w_out_bf16), full(ln_g), full(ln_b)],
        out_specs=tile(D_MODEL),
        out_shape=jax.ShapeDtypeStruct((b, t, D_MODEL), F32),
        scratch_shapes=[pltpu.VMEM((2 * BAND_WINDOW, B_WIDTH), BF16),
                        pltpu.VMEM((2 * BAND_WINDOW, B_WIDTH), BF16),
                        pltpu.VMEM((B_HEADS, CHUNK, BAND_LEN), F32),
                        pltpu.VMEM((1 + tq // CHUNK, B_HEADS // 2, 2 * CHUNK, BAND_LEN), F32),
                        pltpu.VMEM((tq, 2 * B_WIDTH), BF16)],
        compiler_params=pltpu.CompilerParams(
            dimension_semantics=("arbitrary", "arbitrary"), vmem_limit_bytes=VMEM_LIMIT),
    )(att, att, att, att, att, att, memkv, memkv, gtab, x, mix_r, w_out_bf16, ln_g, ln_b)


def _attn_sample_kernel(att_ref, ck_ref, cv_ref, mk_ref, mv_ref, gtab_ref, out_ref, bias_ref, bias2_ref):
    n_seq, n_new, _ = att_ref.shape

    @pl.when(pl.program_id(0) == 0)
    def _():
        _fill_rel_bias(bias_ref, gtab_ref, BAND_WINDOW + CHUNK - 1 - ck_ref.shape[1])
        for h in range(B_HEADS):
            bias2_ref[h // 2, (h % 2) * n_new:(h % 2 + 1) * n_new, :] = bias_ref[h] * LOG2E

    left = lax.broadcasted_iota(jnp.int32, (1, PAIR), 1) < HEAD_DIM
    jobs = []
    for b in range(n_seq):
        att = att_ref[b]
        q = (att[:, 0:B_WIDTH] * (ATT_SCALE * LOG2E)).astype(BF16)
        k_all = jnp.concatenate([ck_ref[b].astype(BF16), att[:, B_WIDTH:2 * B_WIDTH].astype(BF16)], axis=0)
        v_all = jnp.concatenate([cv_ref[b].astype(BF16), att[:, 2 * B_WIDTH:3 * B_WIDTH].astype(BF16)], axis=0)
        gate_b = _silu(att[:, 3 * B_WIDTH:4 * B_WIDTH])
        mq = (att[:, 4 * B_WIDTH:5 * B_WIDTH] * (ATT_SCALE * LOG2E)).astype(BF16)
        gate_m = _silu(att[:, 5 * B_WIDTH:6 * B_WIDTH])
        mk = mk_ref[b].astype(BF16)
        mv = mv_ref[b].astype(BF16)
        for pr in range(B_HEADS // 2):
            lanes = slice(pr * PAIR, (pr + 1) * PAIR)
            jobs.append(dict(b=b, out=lanes, gate=gate_b[:, lanes], lhs=_stack_heads(q[:, lanes], left),
                             k=k_all[:, lanes], v=v_all[:, lanes], bias=bias2_ref[pr]))
        for pr in range(M_HEADS // 2):
            lanes = slice(pr * PAIR, (pr + 1) * PAIR)
            jobs.append(dict(b=b, out=slice(B_WIDTH + lanes.start, B_WIDTH + lanes.stop), gate=gate_m[:, lanes],
                             lhs=_stack_heads(mq[:, lanes], left), k=mk[:, lanes], v=mv[:, lanes]))
    def write(jb, o):
        out_ref[jb["b"], :, jb["out"]] = o * jb["gate"]

    _run_staggered([_attend_stages(jobs, left, write)])


def _attn_sample(att, cache_k, cache_v, mem_k, mem_v, gtab, *, n_seq):
    b, s, _ = att.shape
    per_b = lambda arr: pl.BlockSpec((n_seq,) + arr.shape[1:], lambda i: (i, 0, 0))
    n_keys = cache_k.shape[1] + s
    return pl.pallas_call(
        _attn_sample_kernel,
        name="attn_sample",
        grid=(b // n_seq,),
        in_specs=[per_b(att), per_b(cache_k), per_b(cache_v), per_b(mem_k), per_b(mem_v),
                  pl.BlockSpec(gtab.shape, lambda i: (0, 0))],
        out_specs=pl.BlockSpec((n_seq, s, 2 * B_WIDTH), lambda i: (i, 0, 0)),
        out_shape=jax.ShapeDtypeStruct((b, s, 2 * B_WIDTH), F32),
        scratch_shapes=[pltpu.VMEM((B_HEADS, s, n_keys), F32),
                        pltpu.VMEM((B_HEADS // 2, 2 * s, n_keys), F32)],
        compiler_params=pltpu.CompilerParams(dimension_semantics=("arbitrary",)),
    )(att, cache_k, cache_v, mem_k, mem_v, gtab)


def _finish_kernel(x_ref, mr_ref, mbm_ref, w_ref, g_ref, b_ref, y_ref):
    o = _dot(mr_ref[...].astype(BF16), w_ref[0:R_WIDTH, :])
    o = o + _dot(mbm_ref[...].astype(BF16), w_ref[R_WIDTH:, :])
    y_ref[...] = _layer_norm(ALPHA * x_ref[...] + o, g_ref[...], b_ref[...])


def _finish(x, mix_r, mix_bm, w_out_bf16, ln_g, ln_b, tm):
    m = x.shape[0]
    rows = lambda width: pl.BlockSpec((tm, width), lambda i: (i, 0))
    full = lambda arr: pl.BlockSpec(arr.shape, lambda i: (0, 0))
    return pl.pallas_call(
        _finish_kernel,
        name=f"finish_{m}",
        grid=(m // tm,),
        in_specs=[rows(D_MODEL), rows(R_WIDTH), rows(2 * B_WIDTH), full(w_out_bf16), full(ln_g), full(ln_b)],
        out_specs=rows(D_MODEL),
        out_shape=jax.ShapeDtypeStruct((m, D_MODEL), F32),
        compiler_params=pltpu.CompilerParams(
            dimension_semantics=("arbitrary",), vmem_limit_bytes=VMEM_LIMIT),
    )(x, mix_r, mix_bm, w_out_bf16, ln_g, ln_b)


IN_SPLITS = ((0, SHIFT_WIDTH), (SHIFT_WIDTH, SHIFT_WIDTH + R_WIDTH), (SHIFT_WIDTH + R_WIDTH, IN_WIDTH))


def _rel_bias_row(table, rel0):
    n_hi = rel0 - REL_CLIP
    n_lo = BIAS_L - n_hi - (2 * REL_CLIP + 1)
    heads = table.shape[0]
    return jnp.concatenate([jnp.broadcast_to(table[:, 2 * REL_CLIP:], (heads, n_hi)), table[:, ::-1],
                            jnp.broadcast_to(table[:, 0:1], (heads, n_lo))], axis=1)


def kernel(x_prompt, x_sample, mem_prompt, state_shift, state_wkv, cache_band_k, cache_band_v,
           cache_mem_k, cache_mem_v, w_in, mu_shift, w0, w2, a0, a2, k_k, k_a, r_k, gn_g, gn_b,
           rel_bias, w_mem_kv, w_out, ln_g, ln_b):
    bp, t, _ = x_prompt.shape
    bs, s, _ = x_sample.shape
    depth = w_in.shape[0]
    assert depth == 1 and t % BAND_WINDOW == 0 and t % (WKV_CHUNKS_PER_STEP * CHUNK) == 0
    assert s <= CHUNK and s & (s - 1) == 0 and bp % WKV_PROMPT_SEQS == 0
    assert bs % WKV_SAMPLE_SEQS == 0 and bs % ATTN_SAMPLE_SEQS == 0 and (bp * t) % PROJ_ROWS == 0
    keep = min(BAND_WINDOW, t)
    l = 0

    w_in_b = w_in[l].astype(BF16)
    w_out_b = w_out[l].astype(BF16)
    w_mem_b = w_mem_kv[l].astype(BF16)
    row = lambda p: p.reshape(1, -1)
    wkv_params = (row(mu_shift[l]), row(w0[l]), w2[l], row(a0[l]), a2[l], row(k_k[l]), row(k_a[l]),
                  row(r_k[l]), row(gn_g[l]), row(gn_b[l]))
    table = rel_bias[l]
    r_rows = cache_band_k.shape[2]
    assert r_rows <= BAND_WINDOW + CHUNK - 1
    gtab = _rel_bias_row(table, BAND_WINDOW + CHUNK - 1)

    xp = x_prompt.reshape(bp * t, D_MODEL)
    zs, gr, att = _proj(xp, w_in_b, IN_SPLITS, PROJ_ROWS)
    zs = zs.reshape(bp, t, SHIFT_WIDTH)
    att = att.reshape(bp, t, ATT_WIDTH)
    memkv, = _proj(mem_prompt.reshape(bp * N_MEM, D_MODEL), w_mem_b, ((0, 2 * B_WIDTH),), PROJ_ROWS)
    memkv = memkv.reshape(bp, N_MEM, 2 * B_WIDTH)
    mix_r, p_wkv = _wkv(zs, gr.reshape(bp, t, R_WIDTH), jnp.zeros((bp, 1, SHIFT_WIDTH), F32),
                        jnp.zeros((bp, R_HEADS, HEAD_DIM, HEAD_DIM), F32), wkv_params,
                        bb_n=WKV_PROMPT_SEQS, c=CHUNK, n_ch=WKV_CHUNKS_PER_STEP, n_grp=WKV_GROUP_CHUNKS)
    y_prompt = _attn_prompt(att, memkv, gtab, x_prompt, mix_r, w_out_b, row(ln_g[l]), row(ln_b[l]),
                            tq=BAND_WINDOW)
    p_shift = zs[:, -1]
    p_bk = att[:, t - keep:, B_WIDTH:2 * B_WIDTH].reshape(bp, keep, B_HEADS, HEAD_DIM)
    p_bv = att[:, t - keep:, 2 * B_WIDTH:3 * B_WIDTH].reshape(bp, keep, B_HEADS, HEAD_DIM)
    p_mk = memkv[:, :, :B_WIDTH].reshape(bp, N_MEM, M_HEADS, HEAD_DIM)
    p_mv = memkv[:, :, B_WIDTH:].reshape(bp, N_MEM, M_HEADS, HEAD_DIM)

    xs = x_sample.reshape(bs * s, D_MODEL)
    zs_s, gr_s, att_s = _proj(xs, w_in_b, IN_SPLITS, bs * s)
    zs_s = zs_s.reshape(bs, s, SHIFT_WIDTH)
    att_s = att_s.reshape(bs, s, ATT_WIDTH)
    mix_r_s, s_wkv = _wkv(zs_s, gr_s.reshape(bs, s, R_WIDTH), state_shift[l][:, None, :], state_wkv[l],
                          wkv_params, bb_n=WKV_SAMPLE_SEQS, c=s, n_ch=1)
    mix_bm_s = _attn_sample(att_s,
                            cache_band_k[l].reshape(bs, r_rows, B_WIDTH),
                            cache_band_v[l].reshape(bs, r_rows, B_WIDTH),
                            cache_mem_k[l].reshape(bs, N_MEM, B_WIDTH),
                            cache_mem_v[l].reshape(bs, N_MEM, B_WIDTH), gtab, n_seq=ATTN_SAMPLE_SEQS)
    y_sample = _finish(xs, mix_r_s.reshape(bs * s, R_WIDTH), mix_bm_s.reshape(bs * s, 2 * B_WIDTH),
                       w_out_b, row(ln_g[l]), row(ln_b[l]), bs * s).reshape(bs, s, D_MODEL)
    s_shift = zs_s[:, -1]
    s_bk = att_s[:, :, B_WIDTH:2 * B_WIDTH].reshape(bs, s, B_HEADS, HEAD_DIM)
    s_bv = att_s[:, :, 2 * B_WIDTH:3 * B_WIDTH].reshape(bs, s, B_HEADS, HEAD_DIM)

    st = lambda a: a[None]
    return (y_prompt, y_sample, st(p_shift), st(p_wkv), st(p_bk), st(p_bv), st(p_mk), st(p_mv),
            st(s_shift), st(s_wkv), st(s_bk), st(s_bv))
```

```python
import functools

import numpy as np
import jax
import jax.numpy as jnp
from jax import lax
from jax.experimental import pallas as pl
from jax.experimental.pallas import tpu as pltpu

F32 = jnp.float32
BF16 = jnp.bfloat16

D_MODEL = 1024
HEAD_DIM = 64
R_WIDTH = 512
R_HEADS = 8
LOW_RANK = 64
SHIFT_WIDTH = 3 * R_WIDTH + 2 * LOW_RANK
B_WIDTH = 256
B_HEADS = 4
M_HEADS = 4
N_MEM = 256
CHUNK = 64
BAND_CHUNKS = 8
BAND_WINDOW = BAND_CHUNKS * CHUNK
BAND_LEN = BAND_WINDOW + CHUNK
REL_CLIP = 128
ATT_WIDTH = 6 * B_WIDTH
IN_WIDTH = SHIFT_WIDTH + R_WIDTH + ATT_WIDTH
LN_EPS = 1e-5
GN_EPS = 64e-5
ALPHA = 2.0 ** 0.25
ATT_SCALE = HEAD_DIM ** -0.5
LOG2E = float(np.log2(np.e))
BIAS_L = 1024
PAIR = 2 * HEAD_DIM
MEM_ROWS = 128
WKV_CHUNKS_PER_STEP = 4
WKV_GROUP_CHUNKS = 2
STAGE_SPLIT = 16
PROJ_ROWS = 512
WKV_PROMPT_SEQS = 2
WKV_SAMPLE_SEQS = 16
ATTN_SAMPLE_SEQS = 8

VMEM_LIMIT = 48 * 1024 * 1024

NN = ((1,), (0,))
NT = ((1,), (1,))
TN = ((0,), (0,))


def _dot(a, b, dims=NN):
    return lax.dot_general(a, b, (dims, ((), ())), preferred_element_type=F32)


def _split2(x):
    hi = x.astype(BF16)
    lo = (x - hi.astype(F32)).astype(BF16)
    return hi, lo


def _mm1(a, b):
    return _dot(a.astype(BF16), b.astype(BF16))


def _mm_exact_lhs(lhs_bf16, x):
    hi, lo = _split2(x)
    return _dot(lhs_bf16, hi) + _dot(lhs_bf16, lo)


def _sigmoid(x):
    return 1.0 / (1.0 + jnp.exp(-x))


def _silu(x):
    return x * _sigmoid(x)


def _proj_kernel(x_ref, w_ref, *out_refs, splits):
    x = x_ref[...].astype(BF16)
    for o_ref, (lo, hi) in zip(out_refs, splits):
        o_ref[...] = _dot(x, w_ref[:, lo:hi])


def _proj(x, w_bf16, splits, tm):
    m, k = x.shape
    n = w_bf16.shape[1]
    return pl.pallas_call(
        functools.partial(_proj_kernel, splits=splits),
        name=f"proj_{m}x{n}",
        grid=(m // tm,),
        in_specs=[pl.BlockSpec((tm, k), lambda i: (i, 0)),
                  pl.BlockSpec((k, n), lambda i: (0, 0))],
        out_specs=[pl.BlockSpec((tm, hi - lo), lambda i: (i, 0)) for lo, hi in splits],
        out_shape=[jax.ShapeDtypeStruct((m, hi - lo), F32) for lo, hi in splits],
        compiler_params=pltpu.CompilerParams(
            dimension_semantics=("arbitrary",), vmem_limit_bytes=VMEM_LIMIT),
    )(x, w_bf16)


def _pair_blocks(x2, left):
    zero = jnp.zeros((), x2.dtype)
    return jnp.concatenate([jnp.where(left, x2, zero), jnp.where(left, zero, x2)], axis=0)


def _pair_sum(x2, left):
    s0 = jnp.sum(jnp.where(left, x2, 0.0), axis=-1, keepdims=True)
    s1 = jnp.sum(jnp.where(left, 0.0, x2), axis=-1, keepdims=True)
    return jnp.where(left, s0, s1)


UNIT_COMMON = ("ar", "bkp", "v16", "pc", "gate", "bonus")
UNIT_FREE_OUT = ("a4", "tinv", "akv")
UNIT_FREE_IN = ("bk_bd",)


def _wkv_kernel(z0_ref, g0_ref, zn_ref, gn_ref, prev_ref, s0_ref, mu_ref, w0_ref, w2_ref, a0_ref, a2_ref,
                kk_ref, ka_ref, rk_ref, gng_ref, gnb_ref, out_ref, sfin_ref, carry_ref, state_ref, *unit_refs,
                bb_n, c, n_ch, n_grp, pipelined):
    t = pl.program_id(1)
    tt = n_ch * c
    n_pairs = R_HEADS // 2
    per_chunk = bb_n * n_pairs

    row2 = lax.broadcasted_iota(jnp.int32, (c, 2 * c), 0)
    col2 = lax.broadcasted_iota(jnp.int32, (c, 2 * c), 1) & (c - 1)
    eye2 = (row2 == col2).astype(BF16)
    strict2 = col2 < row2
    incl4 = ((lax.broadcasted_iota(jnp.int32, (c, 4 * c), 1) & (c - 1))
             <= lax.broadcasted_iota(jnp.int32, (c, 4 * c), 0))
    level_masks = []
    shift = 0
    while (1 << shift) < c:
        rb = row2 >> shift
        level_masks.append(((rb & 1) == 1) & ((col2 >> shift) == rb - 1))
        shift += 1
    left_s = lax.broadcasted_iota(jnp.int32, (1, 2 * c), 1) < c
    left = lax.broadcasted_iota(jnp.int32, (1, PAIR), 1) < HEAD_DIM
    zero = jnp.zeros((), BF16)
    row_t = lax.broadcasted_iota(jnp.int32, (tt, tt), 0)
    col_t = lax.broadcasted_iota(jnp.int32, (tt, tt), 1)
    shift_c = c.bit_length() - 1
    ltri = ((col_t <= row_t) & ((col_t >> shift_c) == (row_t >> shift_c))).astype(BF16)
    first_row = lax.broadcasted_iota(jnp.int32, (tt, 1), 0) == 0
    mu = mu_ref[...]

    def field(un, f):
        if f not in un:
            un[f] = _pair_blocks(field(un, "v16"), left) if f == "v_bd" else un["stored"][f]()
        return un[f]

    def make_tile(zs_ref, gr_ref, prev_rows):
        tile = [dict() for _ in range(bb_n)]
        chunks = [[] for _ in range(n_ch)]

        def whole_tile(bb):
            zs = zs_ref[bb]
            zprev = jnp.where(first_row, prev_rows[bb], pltpu.roll(zs, 1, 0))
            xs = zs + (zprev - zs) * mu
            wd = xs[:, 3 * R_WIDTH:3 * R_WIDTH + LOW_RANK]
            ad = xs[:, 3 * R_WIDTH + LOW_RANK:]
            u = -(w0_ref[...] + _mm1(jnp.tanh(wd), w2_ref[...]))
            softplus = jnp.maximum(u, 0.0) + jnp.log(1.0 + jnp.exp(-jnp.abs(u)))
            ld = -jnp.exp(-softplus - 0.5)
            tile[bb].update(xs=xs, ld=ld, cum=_mm_exact_lhs(ltri, ld),
                            a_pre=a0_ref[...] + _mm1(ad, a2_ref[...]))

        def prep_tasks(ch):
            rows = slice(ch * c, (ch + 1) * c)
            tasks = []
            for bb in range(bb_n):
                wide = {}

                def full_width(bb=bb, wide=wide):
                    xs = tile[bb]["xs"][rows]
                    r = xs[:, 0:R_WIDTH]
                    k = xs[:, R_WIDTH:2 * R_WIDTH]
                    ld = tile[bb]["ld"][rows]
                    cum = tile[bb]["cum"][rows]
                    a = _sigmoid(tile[bb]["a_pre"][rows])
                    ecum = jnp.exp(cum)
                    k2 = k * (1.0 + (a - 1.0) * ka_ref[...])
                    wide.update(v=xs[:, 2 * R_WIDTH:3 * R_WIDTH], a=a, ecum=ecum, einv=jnp.exp(-cum),
                                eprev=jnp.exp(cum - ld), kkr=k * kk_ref[...], rt=r * ecum, k2=k2,
                                rk2=r * k2 * rk_ref[...], gate=_silu(gr_ref[bb, rows, :]))

                def pair(pr, bb=bb, wide=wide):
                    sl = slice(pr * PAIR, (pr + 1) * PAIR)
                    kkr_p = wide["kkr"][:, sl]
                    nrm = jnp.sqrt(_pair_sum(kkr_p * kkr_p, left))
                    kkn = kkr_p / jnp.maximum(nrm, 1e-12)
                    at = -kkn * wide["eprev"][:, sl]
                    bh = kkn * wide["a"][:, sl] * wide["einv"][:, sl]
                    kh = wide["k2"][:, sl] * wide["einv"][:, sl]
                    pc = wide["ecum"][c - 1:c, sl]
                    v_p = wide["v"][:, sl]
                    v16 = v_p.astype(BF16)
                    chunks[ch].append(dict(
                        bb=bb, pr=pr, sl=sl, rows=rows, pc=pc, gate=wide["gate"][:, sl],
                        bonus=_pair_sum(wide["rk2"][:, sl], left) * v_p, v_bd=_pair_blocks(v16, left), v16=v16,
                        ar=jnp.concatenate([at, wide["rt"][:, sl]], axis=0).astype(BF16),
                        bk_bd=jnp.concatenate([_pair_blocks(bh.astype(BF16), left),
                                               _pair_blocks(kh.astype(BF16), left)], axis=0),
                        bkp=jnp.concatenate([bh * pc, kh * pc], axis=0).astype(BF16)))

                tasks.append(full_width)
                tasks += [functools.partial(pair, pr) for pr in range(n_pairs)]
            return tasks

        return chunks, [functools.partial(whole_tile, bb) for bb in range(bb_n)], prep_tasks

    def st_a4(units):
        for un in units:
            un["a4"] = _dot(field(un, "ar"), field(un, "bk_bd"), NT).astype(BF16)
            un["tinv"] = jnp.where(level_masks[0], un["a4"][:c, :2 * c], eye2)

    def st_lt(mask):
        def run(units):
            for un in units:
                a_ab = un["a4"][:c, :2 * c]
                un["lt"] = _dot(jnp.where(mask, a_ab, zero), _pair_blocks(un["tinv"], left_s)).astype(BF16)
        return run

    def st_tinv(mask):
        def run(units):
            for un in units:
                new = _dot(un["tinv"], _pair_blocks(un["lt"], left_s)).astype(BF16)
                un["tinv"] = jnp.where(mask, new, un["tinv"])
        return run

    def st_akv(units):
        for un in units:
            un["akv"] = _dot(jnp.where(strict2, un["a4"][:c, 2 * c:], zero), field(un, "v_bd"))

    def st_ars(units):
        for un in units:
            un["s0"] = state[un["bb"], un["pr"]]
            un["ars"] = _dot(field(un, "ar"), _pair_blocks(un["s0"].astype(BF16), left), NT)

    def st_pm(units):
        for un in units:
            rhs = (un["ars"][:c] + field(un, "akv")).astype(BF16)
            un["pm"] = _dot(field(un, "tinv"), _pair_blocks(rhs, left)).astype(BF16)

    def st_state(units):
        for un in units:
            pv = jnp.concatenate([un["pm"], field(un, "v16")], axis=0)
            cross = _dot(pv, field(un, "bkp"), TN)
            state[un["bb"], un["pr"]] = (un["s0"] * field(un, "pc")
                                         + jnp.where(left, cross[:HEAD_DIM], cross[HEAD_DIM:]))

    def st_y(units):
        for un in units:
            pv_bd = jnp.concatenate([_pair_blocks(un["pm"], left), field(un, "v_bd")], axis=0)
            y = un["ars"][c:] + _dot(jnp.where(incl4, field(un, "a4")[c:], zero), pv_bd)
            mean = _pair_sum(y, left) * (1.0 / HEAD_DIM)
            yc = y - mean
            var = _pair_sum(yc * yc, left) * (1.0 / HEAD_DIM)
            yn = yc * lax.rsqrt(var + GN_EPS) * gng_ref[:, un["sl"]] + gnb_ref[:, un["sl"]]
            out_ref[un["bb"], un["rows"], un["sl"]] = (yn + field(un, "bonus")) * field(un, "gate")

    free_stages = [st_a4]
    for mask in level_masks[1:]:
        free_stages += [st_lt(mask), st_tinv(mask)]
    free_stages.append(st_akv)
    state_stages = [st_ars, st_pm, st_state, st_y]

    def stage_tasks(stages, chunks, chs):
        def run(stage, lo):
            units = [un for ch in chs for un in chunks[ch]]
            stage(units[lo:lo + STAGE_SPLIT])

        return [functools.partial(run, stage, lo) for stage in stages
                for lo in range(0, len(chs) * per_chunk, STAGE_SPLIT)]

    def run_interleaved(*task_lists):
        keyed = [((i + 0.5) / len(tasks), n, i, task)
                 for n, tasks in enumerate(task_lists) for i, task in enumerate(tasks)]
        for _, _, _, task in sorted(keyed, key=lambda e: e[:3]):
            task()

    groups = [list(range(g0, min(g0 + n_grp, n_ch))) for g0 in range(0, n_ch, n_grp)]

    if pipelined:
        refs = dict(zip(UNIT_COMMON + UNIT_FREE_OUT + UNIT_FREE_IN, unit_refs))
        n_g = n_grp * per_chunk

        def store_units(chunks, chs, fields, base):
            units = [un for ch in chs for un in chunks[ch]]
            for idx, un in enumerate(units):
                for f in fields:
                    refs[f][(base if f in UNIT_COMMON else 0) + idx] = un[f]

        def load_units(chs, fields, base):
            chunks = {}
            idx = 0
            for ch in chs:
                chunks[ch] = []
                for bb in range(bb_n):
                    for pr in range(n_pairs):
                        un = dict(bb=bb, pr=pr, sl=slice(pr * PAIR, (pr + 1) * PAIR),
                                  rows=slice(ch * c, (ch + 1) * c))
                        un["stored"] = {f: functools.partial(lambda f, i: refs[f][i], f,
                                                             (base if f in UNIT_COMMON else 0) + idx)
                                        for f in fields}
                        chunks[ch].append(un)
                        idx += 1
            return chunks

        def hand_over(chunks):
            store_units(chunks, groups[0], UNIT_COMMON + UNIT_FREE_OUT, 0)
            store_units(chunks, groups[1], UNIT_COMMON + UNIT_FREE_IN, n_g)

        @pl.when(t == 0)
        def _():
            for bb in range(bb_n):
                for h in range(R_HEADS):
                    state_ref[bb, h // 2, :, (h % 2) * HEAD_DIM:(h % 2 + 1) * HEAD_DIM] = s0_ref[bb, h]
            first, whole, prep = make_tile(z0_ref, g0_ref, [prev_ref[bb] for bb in range(bb_n)])
            run_interleaved(whole)
            run_interleaved([task for ch in range(n_ch) for task in prep(ch)])
            run_interleaved(stage_tasks(free_stages, first, groups[0]))
            hand_over(first)
            carry_ref[...] = z0_ref[:, tt - 1:tt, :]

        cur = load_units(groups[0], UNIT_COMMON + UNIT_FREE_OUT, 0)
        cur.update(load_units(groups[1], UNIT_COMMON + UNIT_FREE_IN, n_g))
        nxt, whole, prep = make_tile(zn_ref, gn_ref, [carry_ref[bb] for bb in range(bb_n)])
        state = {(bb, pr): state_ref[bb, pr] for bb in range(bb_n) for pr in range(n_pairs)}
        run_interleaved([task for ch in groups[0] for task in stage_tasks(state_stages, cur, [ch])],
                        stage_tasks(free_stages, cur, groups[1]),
                        whole + [task for ch in groups[0] for task in prep(ch)])
        run_interleaved([task for ch in groups[1] for task in stage_tasks(state_stages, cur, [ch])],
                        stage_tasks(free_stages, nxt, groups[0]),
                        [task for ch in groups[1] for task in prep(ch)])
        hand_over(nxt)
        carry_ref[...] = zn_ref[:, tt - 1:tt, :]
    else:
        @pl.when(t == 0)
        def _():
            carry_ref[...] = prev_ref[...]
            for bb in range(bb_n):
                for h in range(R_HEADS):
                    state_ref[bb, h // 2, :, (h % 2) * HEAD_DIM:(h % 2 + 1) * HEAD_DIM] = s0_ref[bb, h]

        cur, whole, prep = make_tile(z0_ref, g0_ref, [carry_ref[bb] for bb in range(bb_n)])
        state = {(bb, pr): state_ref[bb, pr] for bb in range(bb_n) for pr in range(n_pairs)}
        run_interleaved(whole)
        carry_ref[...] = z0_ref[:, tt - 1:tt, :]
        run_interleaved([task for ch in range(n_ch) for task in prep(ch)])
        run_interleaved(stage_tasks(free_stages, cur, groups[0]))
        for g, chs in enumerate(groups):
            run_interleaved([task for ch in chs for task in stage_tasks(state_stages, cur, [ch])],
                            stage_tasks(free_stages, cur, groups[g + 1]) if g + 1 < len(groups) else [])
    for (bb, pr), val in state.items():
        state_ref[bb, pr] = val

    @pl.when(t == pl.num_programs(1) - 1)
    def _():
        for bb in range(bb_n):
            for h in range(R_HEADS):
                sfin_ref[bb, h] = state_ref[bb, h // 2, :, (h % 2) * HEAD_DIM:(h % 2 + 1) * HEAD_DIM]


def _wkv(zs, gr, prev, s0, params, *, bb_n, c, n_ch, n_grp=1):
    b, t, _ = zs.shape
    tt = n_ch * c
    n_t = t // tt
    pipelined = n_ch == 2 * n_grp and n_t > 1
    full = lambda arr: pl.BlockSpec(arr.shape, lambda i, j: (0,) * arr.ndim)
    first = lambda width: pl.BlockSpec((bb_n, tt, width), lambda i, j: (i, 0 if pipelined else j, 0))
    ahead = lambda width: pl.BlockSpec((bb_n, tt, width), lambda i, j: (i, jnp.minimum(j + 1, n_t - 1), 0))
    unit_scratch = []
    if pipelined:
        n_g = n_grp * bb_n * (R_HEADS // 2)
        shapes = dict(ar=((2 * c, PAIR), BF16), bkp=((2 * c, PAIR), BF16), v16=((c, PAIR), BF16),
                      pc=((1, PAIR), F32), gate=((c, PAIR), F32), bonus=((c, PAIR), F32),
                      a4=((2 * c, 4 * c), BF16), tinv=((c, 2 * c), BF16), akv=((c, PAIR), F32),
                      bk_bd=((4 * c, PAIR), BF16))
        for f in UNIT_COMMON + UNIT_FREE_OUT + UNIT_FREE_IN:
            shape, dtype = shapes[f]
            unit_scratch.append(pltpu.VMEM(((2 * n_g if f in UNIT_COMMON else n_g),) + shape, dtype))
    return pl.pallas_call(
        functools.partial(_wkv_kernel, bb_n=bb_n, c=c, n_ch=n_ch, n_grp=n_grp, pipelined=pipelined),
        name=f"wkv_c{c}",
        grid=(b // bb_n, n_t),
        in_specs=[first(SHIFT_WIDTH), first(R_WIDTH), ahead(SHIFT_WIDTH), ahead(R_WIDTH),
                  pl.BlockSpec((bb_n, 1, SHIFT_WIDTH), lambda i, j: (i, 0, 0)),
                  pl.BlockSpec((bb_n, R_HEADS, HEAD_DIM, HEAD_DIM), lambda i, j: (i, 0, 0, 0))]
                 + [full(p) for p in params],
        out_specs=[pl.BlockSpec((bb_n, tt, R_WIDTH), lambda i, j: (i, j, 0)),
                   pl.BlockSpec((bb_n, R_HEADS, HEAD_DIM, HEAD_DIM), lambda i, j: (i, 0, 0, 0))],
        out_shape=[jax.ShapeDtypeStruct((b, t, R_WIDTH), F32),
                   jax.ShapeDtypeStruct((b, R_HEADS, HEAD_DIM, HEAD_DIM), F32)],
        scratch_shapes=[pltpu.VMEM((bb_n, 1, SHIFT_WIDTH), F32),
                        pltpu.VMEM((bb_n, R_HEADS // 2, HEAD_DIM, PAIR), F32)] + unit_scratch,
        compiler_params=pltpu.CompilerParams(
            dimension_semantics=("arbitrary", "arbitrary"), vmem_limit_bytes=VMEM_LIMIT),
    )(zs, gr, zs, gr, prev, s0, *params)


def _fill_rel_bias(bias_ref, gtab_ref, offset):
    heads, nq, nk = bias_ref.shape
    for h in range(heads):
        g = jnp.broadcast_to(gtab_ref[h:h + 1, :], (nq, BIAS_L))
        bias_ref[h] = pltpu.roll(g, BIAS_L - offset, 1, stride=1, stride_axis=0)[:, :nk]


def _stack_heads(q2, left):
    zero = jnp.zeros((), q2.dtype)
    return jnp.concatenate([jnp.where(left, q2, zero), jnp.where(left, zero, q2)], axis=0)


def _attend_stages(jobs, left, write):
    def scores():
        for jb in jobs:
            s = _dot(jb["lhs"], jb["k"], NT)
            jb["s"] = s if jb.get("bias") is None else s + jb["bias"]

    def weights():
        for jb in jobs:
            m = jnp.max(jb["s"], axis=-1, keepdims=True)
            p = jnp.exp2(jb["s"] - m)
            jb["l"] = jnp.sum(p, axis=-1, keepdims=True)
            jb["p"] = p.astype(BF16)

    def values():
        for jb in jobs:
            o2 = _dot(jb["p"], jb["v"]) / jb["l"]
            n = o2.shape[0] // 2
            write(jb, jnp.where(left, o2[:n], o2[n:]))

    return [scores, weights, values]


def _run_staggered(groups):
    depth = max(len(g) for g in groups)
    for step in range(len(groups) + depth - 1):
        for g in range(len(groups) - 1, -1, -1):
            if 0 <= step - g < len(groups[g]):
                groups[g][step - g]()


def _layer_norm(h, g, b):
    mean = jnp.mean(h, axis=-1, keepdims=True)
    hc = h - mean
    var = jnp.mean(hc * hc, axis=-1, keepdims=True)
    return hc * lax.rsqrt(var + LN_EPS) * g + b


def _attn_prompt_kernel(q_ref, kc_ref, vc_ref, gb_ref, mq_ref, gm_ref, mk_ref, mv_ref, gtab_ref,
                        x_ref, mr_ref, w_ref, lng_ref, lnb_ref, y_ref,
                        kcat_ref, vcat_ref, bias_ref, biasv_ref, mix_ref, *, tq):
    j = pl.program_id(1)
    n_chunks = tq // CHUNK

    @pl.when((pl.program_id(0) == 0) & (j == 0))
    def _():
        _fill_rel_bias(bias_ref, gtab_ref, CHUNK - 1)
        kcol = lax.broadcasted_iota(jnp.int32, (1, BAND_LEN), 1)
        for h in range(B_HEADS):
            rows = slice((h % 2) * CHUNK, (h % 2 + 1) * CHUNK)
            scaled = bias_ref[h] * LOG2E
            biasv_ref[0, h // 2, rows, :] = scaled
            for i in range(n_chunks):
                biasv_ref[1 + i, h // 2, rows, :] = jnp.where(kcol >= BAND_WINDOW - i * CHUNK, scaled, -jnp.inf)

    @pl.when(j == 0)
    def _():
        kcat_ref[0:BAND_WINDOW] = jnp.zeros((BAND_WINDOW, B_WIDTH), BF16)
        vcat_ref[0:BAND_WINDOW] = jnp.zeros((BAND_WINDOW, B_WIDTH), BF16)

    @pl.when(j > 0)
    def _():
        kcat_ref[0:BAND_WINDOW] = kcat_ref[BAND_WINDOW:]
        vcat_ref[0:BAND_WINDOW] = vcat_ref[BAND_WINDOW:]

    kcat_ref[BAND_WINDOW:] = kc_ref[0].astype(BF16)
    vcat_ref[BAND_WINDOW:] = vc_ref[0].astype(BF16)
    q = (q_ref[0] * (ATT_SCALE * LOG2E)).astype(BF16)
    gate_b = _silu(gb_ref[0])
    left = lax.broadcasted_iota(jnp.int32, (1, PAIR), 1) < HEAD_DIM
    mq = (mq_ref[0] * (ATT_SCALE * LOG2E)).astype(BF16)
    gate_m = _silu(gm_ref[0])
    mk = mk_ref[0].astype(BF16)
    mv = mv_ref[0].astype(BF16)

    def write_band(jb, o):
        mix_ref[jb["rows"], jb["lanes"]] = (o * gate_b[jb["rows"], jb["lanes"]]).astype(BF16)

    def write_mem(jb, o):
        mix_ref[jb["rows"], B_WIDTH + jb["lanes"].start:B_WIDTH + jb["lanes"].stop] = (
            o * gate_m[jb["rows"], jb["lanes"]]).astype(BF16)

    jobs = []
    for i in range(n_chunks):
        rows = slice(i * CHUNK, (i + 1) * CHUNK)
        keys = slice(i * CHUNK, i * CHUNK + BAND_LEN)
        variant = jnp.where(j == 0, 1 + i, 0)
        for pr in range(B_HEADS // 2):
            lanes = slice(pr * PAIR, (pr + 1) * PAIR)
            job = dict(rows=rows, lanes=lanes, lhs=_stack_heads(q[rows, lanes], left),
                       k=kcat_ref[keys, lanes], v=vcat_ref[keys, lanes], bias=biasv_ref[variant, pr])
            jobs.append(_attend_stages([job], left, write_band))
    for r in range(0, tq, MEM_ROWS):
        rows = slice(r, r + MEM_ROWS)
        for pr in range(M_HEADS // 2):
            lanes = slice(pr * PAIR, (pr + 1) * PAIR)
            job = dict(rows=rows, lanes=lanes, lhs=_stack_heads(mq[rows, lanes], left),
                       k=mk[:, lanes], v=mv[:, lanes])
            jobs.append(_attend_stages([job], left, write_mem))
    _run_staggered(jobs)

    o = _dot(mr_ref[0].astype(BF16), w_ref[0:R_WIDTH, :]) + _dot(mix_ref[...], w_ref[R_WIDTH:, :])
    y_ref[0] = _layer_norm(ALPHA * x_ref[0] + o, lng_ref[...], lnb_ref[...])


def _attn_prompt(att, memkv, gtab, x, mix_r, w_out_bf16, ln_g, ln_b, *, tq):
    b, t, _ = att.shape
    col = lambda cidx: pl.BlockSpec((1, tq, B_WIDTH), lambda i, j: (i, j, cidx))
    tile = lambda width: pl.BlockSpec((1, tq, width), lambda i, j: (i, j, 0))
    full = lambda arr: pl.BlockSpec(arr.shape, lambda i, j: (0, 0))
    return pl.pallas_call(
        functools.partial(_attn_prompt_kernel, tq=tq),
        name="attn_prompt",
        grid=(b, t // tq),
        in_specs=[col(0), col(1), col(2), col(3), col(4), col(5),
                  pl.BlockSpec((1, N_MEM, B_WIDTH), lambda i, j: (i, 0, 0)),
                  pl.BlockSpec((1, N_MEM, B_WIDTH), lambda i, j: (i, 0, 1)),
                  full(gtab), tile(D_MODEL), tile(R_WIDTH), full(w_out_bf16), full(ln_g), full(ln_b)],
        out_specs=tile(D_MODEL),
        out_shape=jax.ShapeDtypeStruct((b, t, D_MODEL), F32),
        scratch_shapes=[pltpu.VMEM((2 * BAND_WINDOW, B_WIDTH), BF16),
                        pltpu.VMEM((2 * BAND_WINDOW, B_WIDTH), BF16),
                        pltpu.VMEM((B_HEADS, CHUNK, BAND_LEN), F32),
                        pltpu.VMEM((1 + tq // CHUNK, B_HEADS // 2, 2 * CHUNK, BAND_LEN), F32),
                        pltpu.VMEM((tq, 2 * B_WIDTH), BF16)],
        compiler_params=pltpu.CompilerParams(
            dimension_semantics=("arbitrary", "arbitrary"), vmem_limit_bytes=VMEM_LIMIT),
    )(att, att, att, att, att, att, memkv, memkv, gtab, x, mix_r, w_out_bf16, ln_g, ln_b)


def _attn_sample_kernel(att_ref, ck_ref, cv_ref, mk_ref, mv_ref, gtab_ref, x_ref, mr_ref, w_ref, lng_ref, lnb_ref,
                        y_ref, bias_ref, bias2_ref, mix_ref):
    n_seq, n_new, _ = att_ref.shape

    @pl.when(pl.program_id(0) == 0)
    def _():
        _fill_rel_bias(bias_ref, gtab_ref, BAND_WINDOW + CHUNK - 1 - ck_ref.shape[1])
        for h in range(B_HEADS):
            bias2_ref[h // 2, (h % 2) * n_new:(h % 2 + 1) * n_new, :] = bias_ref[h] * LOG2E

    left = lax.broadcasted_iota(jnp.int32, (1, PAIR), 1) < HEAD_DIM
    jobs = []
    for b in range(n_seq):
        att = att_ref[b]
        q = (att[:, 0:B_WIDTH] * (ATT_SCALE * LOG2E)).astype(BF16)
        k_all = jnp.concatenate([ck_ref[b].astype(BF16), att[:, B_WIDTH:2 * B_WIDTH].astype(BF16)], axis=0)
        v_all = jnp.concatenate([cv_ref[b].astype(BF16), att[:, 2 * B_WIDTH:3 * B_WIDTH].astype(BF16)], axis=0)
        gate_b = _silu(att[:, 3 * B_WIDTH:4 * B_WIDTH])
        mq = (att[:, 4 * B_WIDTH:5 * B_WIDTH] * (ATT_SCALE * LOG2E)).astype(BF16)
        gate_m = _silu(att[:, 5 * B_WIDTH:6 * B_WIDTH])
        mk = mk_ref[b].astype(BF16)
        mv = mv_ref[b].astype(BF16)
        for pr in range(B_HEADS // 2):
            lanes = slice(pr * PAIR, (pr + 1) * PAIR)
            jobs.append(dict(b=b, out=lanes, gate=gate_b[:, lanes], lhs=_stack_heads(q[:, lanes], left),
                             k=k_all[:, lanes], v=v_all[:, lanes], bias=bias2_ref[pr]))
        for pr in range(M_HEADS // 2):
            lanes = slice(pr * PAIR, (pr + 1) * PAIR)
            jobs.append(dict(b=b, out=slice(B_WIDTH + lanes.start, B_WIDTH + lanes.stop), gate=gate_m[:, lanes],
                             lhs=_stack_heads(mq[:, lanes], left), k=mk[:, lanes], v=mv[:, lanes]))
    def write(jb, o):
        mix_ref[jb["b"] * n_new:(jb["b"] + 1) * n_new, jb["out"]] = (o * jb["gate"]).astype(BF16)

    _run_staggered([_attend_stages(jobs, left, write)])

    rows = n_seq * n_new
    o = (_dot(mr_ref[...].reshape(rows, R_WIDTH).astype(BF16), w_ref[0:R_WIDTH, :])
         + _dot(mix_ref[...], w_ref[R_WIDTH:, :]))
    y = _layer_norm(ALPHA * x_ref[...].reshape(rows, D_MODEL) + o, lng_ref[...], lnb_ref[...])
    y_ref[...] = y.reshape(n_seq, n_new, D_MODEL)


def _attn_sample(att, cache_k, cache_v, mem_k, mem_v, gtab, x, mix_r, w_out_bf16, ln_g, ln_b, *, n_seq):
    b, s, _ = att.shape
    per_b = lambda arr: pl.BlockSpec((n_seq,) + arr.shape[1:], lambda i: (i, 0, 0))
    full = lambda arr: pl.BlockSpec(arr.shape, lambda i: (0, 0))
    n_keys = cache_k.shape[1] + s
    return pl.pallas_call(
        _attn_sample_kernel,
        name="attn_sample",
        grid=(b // n_seq,),
        in_specs=[per_b(att), per_b(cache_k), per_b(cache_v), per_b(mem_k), per_b(mem_v), full(gtab),
                  per_b(x), per_b(mix_r), full(w_out_bf16), full(ln_g), full(ln_b)],
        out_specs=pl.BlockSpec((n_seq, s, D_MODEL), lambda i: (i, 0, 0)),
        out_shape=jax.ShapeDtypeStruct((b, s, D_MODEL), F32),
        scratch_shapes=[pltpu.VMEM((B_HEADS, s, n_keys), F32),
                        pltpu.VMEM((B_HEADS // 2, 2 * s, n_keys), F32),
                        pltpu.VMEM((n_seq * s, 2 * B_WIDTH), BF16)],
        compiler_params=pltpu.CompilerParams(
            dimension_semantics=("arbitrary",), vmem_limit_bytes=VMEM_LIMIT),
    )(att, cache_k, cache_v, mem_k, mem_v, gtab, x, mix_r, w_out_bf16, ln_g, ln_b)


IN_SPLITS = ((0, SHIFT_WIDTH), (SHIFT_WIDTH, SHIFT_WIDTH + R_WIDTH), (SHIFT_WIDTH + R_WIDTH, IN_WIDTH))


def _rel_bias_row(table, rel0):
    n_hi = rel0 - REL_CLIP
    n_lo = BIAS_L - n_hi - (2 * REL_CLIP + 1)
    heads = table.shape[0]
    return jnp.concatenate([jnp.broadcast_to(table[:, 2 * REL_CLIP:], (heads, n_hi)), table[:, ::-1],
                            jnp.broadcast_to(table[:, 0:1], (heads, n_lo))], axis=1)


def kernel(x_prompt, x_sample, mem_prompt, state_shift, state_wkv, cache_band_k, cache_band_v,
           cache_mem_k, cache_mem_v, w_in, mu_shift, w0, w2, a0, a2, k_k, k_a, r_k, gn_g, gn_b,
           rel_bias, w_mem_kv, w_out, ln_g, ln_b):
    bp, t, _ = x_prompt.shape
    bs, s, _ = x_sample.shape
    depth = w_in.shape[0]
    assert depth == 1 and t % BAND_WINDOW == 0 and t % (WKV_CHUNKS_PER_STEP * CHUNK) == 0
    assert s <= CHUNK and s & (s - 1) == 0 and bp % WKV_PROMPT_SEQS == 0
    assert bs % WKV_SAMPLE_SEQS == 0 and bs % ATTN_SAMPLE_SEQS == 0 and (bp * t) % PROJ_ROWS == 0
    keep = min(BAND_WINDOW, t)
    l = 0

    w_in_b = w_in[l].astype(BF16)
    w_out_b = w_out[l].astype(BF16)
    w_mem_b = w_mem_kv[l].astype(BF16)
    row = lambda p: p.reshape(1, -1)
    wkv_params = (row(mu_shift[l]), row(w0[l]), w2[l], row(a0[l]), a2[l], row(k_k[l]), row(k_a[l]),
                  row(r_k[l]), row(gn_g[l]), row(gn_b[l]))
    table = rel_bias[l]
    r_rows = cache_band_k.shape[2]
    assert r_rows <= BAND_WINDOW + CHUNK - 1
    gtab = _rel_bias_row(table, BAND_WINDOW + CHUNK - 1)

    xp = x_prompt.reshape(bp * t, D_MODEL)
    zs, gr, att = _proj(xp, w_in_b, IN_SPLITS, PROJ_ROWS)
    zs = zs.reshape(bp, t, SHIFT_WIDTH)
    att = att.reshape(bp, t, ATT_WIDTH)
    memkv, = _proj(mem_prompt.reshape(bp * N_MEM, D_MODEL), w_mem_b, ((0, 2 * B_WIDTH),), PROJ_ROWS)
    memkv = memkv.reshape(bp, N_MEM, 2 * B_WIDTH)
    mix_r, p_wkv = _wkv(zs, gr.reshape(bp, t, R_WIDTH), jnp.zeros((bp, 1, SHIFT_WIDTH), F32),
                        jnp.zeros((bp, R_HEADS, HEAD_DIM, HEAD_DIM), F32), wkv_params,
                        bb_n=WKV_PROMPT_SEQS, c=CHUNK, n_ch=WKV_CHUNKS_PER_STEP, n_grp=WKV_GROUP_CHUNKS)
    y_prompt = _attn_prompt(att, memkv, gtab, x_prompt, mix_r, w_out_b, row(ln_g[l]), row(ln_b[l]),
                            tq=BAND_WINDOW)
    p_shift = zs[:, -1]
    p_bk = att[:, t - keep:, B_WIDTH:2 * B_WIDTH].reshape(bp, keep, B_HEADS, HEAD_DIM)
    p_bv = att[:, t - keep:, 2 * B_WIDTH:3 * B_WIDTH].reshape(bp, keep, B_HEADS, HEAD_DIM)
    p_mk = memkv[:, :, :B_WIDTH].reshape(bp, N_MEM, M_HEADS, HEAD_DIM)
    p_mv = memkv[:, :, B_WIDTH:].reshape(bp, N_MEM, M_HEADS, HEAD_DIM)

    xs = x_sample.reshape(bs * s, D_MODEL)
    zs_s, gr_s, att_s = _proj(xs, w_in_b, IN_SPLITS, bs * s)
    zs_s = zs_s.reshape(bs, s, SHIFT_WIDTH)
    att_s = att_s.reshape(bs, s, ATT_WIDTH)
    mix_r_s, s_wkv = _wkv(zs_s, gr_s.reshape(bs, s, R_WIDTH), state_shift[l][:, None, :], state_wkv[l],
                          wkv_params, bb_n=WKV_SAMPLE_SEQS, c=s, n_ch=1)
    y_sample = _attn_sample(att_s,
                            cache_band_k[l].reshape(bs, r_rows, B_WIDTH),
                            cache_band_v[l].reshape(bs, r_rows, B_WIDTH),
                            cache_mem_k[l].reshape(bs, N_MEM, B_WIDTH),
                            cache_mem_v[l].reshape(bs, N_MEM, B_WIDTH), gtab,
                            x_sample, mix_r_s, w_out_b, row(ln_g[l]), row(ln_b[l]), n_seq=ATTN_SAMPLE_SEQS)
    s_shift = zs_s[:, -1]
    s_bk = att_s[:, :, B_WIDTH:2 * B_WIDTH].reshape(bs, s, B_HEADS, HEAD_DIM)
    s_bv = att_s[:, :, 2 * B_WIDTH:3 * B_WIDTH].reshape(bs, s, B_HEADS, HEAD_DIM)

    st = lambda a: a[None]
    return (y_prompt, y_sample, st(p_shift), st(p_wkv), st(p_bk), st(p_bv), st(p_mk), st(p_mv),
            st(s_shift), st(s_wkv), st(s_bk), st(s_bv))
```
